```python
import math
import jax, jax.numpy as jnp
from jax import lax
import numpy as np

D_MODEL = 1024
BATCH = 16
SEQ = 256
DEPTH = 4
DEC_BATCH = 2
DEC_SEQ = 2048
PAST_LEN = 512

GRID_W = 64
N_MIXERS = 2
N_HYENA = (DEPTH + 1) // 2
N_MLA = DEPTH // 2
EPS = 1e-6
HY_WIDTH = D_MODEL
SHORT_CONV = 3
FILTER_BANDS = 16
FILTER_EMB = 1 + 2 * FILTER_BANDS
FILTER_HIDDEN = 64
FAST_DECAY_PCT = 0.3
SLOW_DECAY_PCT = 1.5
DECAY_TARGET = 1e-2
N_HEADS = 16
Q_LORA = 384
KV_LORA = 256
QK_NOPE = 64
QK_ROPE = 32
V_HEAD = 64
ROPE_THETA = 10000.0
Q_BLOCK = 128
MLA_IN = Q_LORA + KV_LORA + QK_ROPE + N_HEADS * V_HEAD

kernel_name = 'hybrid_hyena_mla_diffusion_step'


def rmsnorm(x, g):
    xf = x.astype(jnp.float32)
    y = xf * lax.rsqrt(jnp.mean(xf * xf, axis=-1, keepdims=True) + EPS)
    return (y * g.astype(jnp.float32)).astype(x.dtype)


def adaln(cond, w, b):
    m = (jax.nn.silu(cond) @ w + b)[:, None, :]
    return jnp.split(m, 3, axis=-1)


def short_conv(u, w, b):
    L = u.shape[1]
    pad = SHORT_CONV // 2
    up = jnp.pad(u, ((0, 0), (pad, pad), (0, 0)))
    out = b
    for k in range(SHORT_CONV):
        out = out + up[:, k:k + L] * w[k]
    return out


def hyena_filters(L, w1, b1, freq, w2, b2, w3):
    f32 = jnp.float32
    t = jnp.linspace(0.0, 1.0, L, dtype=f32)[:, None]
    w = (2.0 * math.pi / L) * jnp.arange(L, dtype=f32)[:, None]
    bands = jnp.linspace(1e-4, FILTER_BANDS - 1, FILTER_BANDS, dtype=f32)[None, :]
    z = jnp.concatenate([t, jnp.cos(bands * w), -jnp.sin(bands * w)], axis=-1)
    fr = freq.astype(f32)
    hdn = jnp.sin(fr * (z @ w1.astype(f32) + b1.astype(f32)))
    hdn = jnp.sin(fr * (hdn @ w2.astype(f32) + b2.astype(f32)))
    h = hdn @ w3.astype(f32)
    max_decay = math.log(DECAY_TARGET) / FAST_DECAY_PCT
    min_decay = math.log(DECAY_TARGET) / SLOW_DECAY_PCT
    deltas = jnp.abs(jnp.linspace(min_decay, max_decay, HY_WIDTH, dtype=f32))
    deltas = jnp.concatenate([deltas, deltas])
    h = h * jnp.exp(-t * deltas)
    return h / jnp.sum(jnp.abs(h), axis=0, keepdims=True)


def hyena_mix(h, w_in, conv_w, conv_b, f_w1, f_b1, f_freq, f_w2, f_b2, f_w3, f_bias, w_out):
    L = h.shape[1]
    n = 2 * L
    proj = h @ w_in
    u = short_conv(proj[..., :3 * HY_WIDTH], conv_w, conv_b)
    gate = proj[..., 3 * HY_WIDTH:]
    x0, x1, v = jnp.split(u, 3, axis=-1)
    z = (v * x1).astype(jnp.float32)
    filt = hyena_filters(L, f_w1, f_b1, f_freq, f_w2, f_b2, f_w3)
    hf = jnp.fft.rfft(filt, n=n, axis=0)
    hf = hf[:, :HY_WIDTH] + jnp.conj(hf[:, HY_WIDTH:])
    y = jnp.fft.irfft(jnp.fft.rfft(z, n=n, axis=1) * hf, n=n, axis=1)[:, :L]
    y = (y + z * f_bias.astype(jnp.float32)).astype(h.dtype) * x0
    return (y * jax.nn.silu(gate)) @ w_out


def axial_rope_angles(L):
    rows = L // GRID_W
    axis_dim = QK_ROPE // 2
    inv = ROPE_THETA ** (-jnp.arange(0, axis_dim, 2, dtype=jnp.float32) / axis_dim)
    row = jnp.repeat(jnp.arange(rows, dtype=jnp.float32), GRID_W)
    col = jnp.tile(jnp.arange(GRID_W, dtype=jnp.float32), rows)
    return row[:, None] * inv, col[:, None] * inv


def rope_1d(x, ang):
    half = x.shape[-1] // 2
    cos = jnp.cos(ang).astype(x.dtype)
    sin = jnp.sin(ang).astype(x.dtype)
    x1, x2 = x[..., :half], x[..., half:]
    return jnp.concatenate([x1 * cos - x2 * sin, x2 * cos + x1 * sin], axis=-1)


def apply_axial_rope(x, ang_r, ang_c):
    half = QK_ROPE // 2
    return jnp.concatenate([rope_1d(x[..., :half], ang_r), rope_1d(x[..., half:], ang_c)], axis=-1)


def mla_project(h, w_in, q_norm, w_qb, kv_norm):
    B, L, _ = h.shape
    proj = h @ w_in
    q_a, kv_a, k_pe, gate = jnp.split(
        proj, [Q_LORA, Q_LORA + KV_LORA, Q_LORA + KV_LORA + QK_ROPE], axis=-1)
    q = (rmsnorm(q_a, q_norm) @ w_qb).reshape(B, L, N_HEADS, QK_NOPE + QK_ROPE)
    ckv = rmsnorm(kv_a, kv_norm)
    return q[..., :QK_NOPE], q[..., QK_NOPE:], ckv, k_pe, gate


def mla_expand(ckv, w_kvb):
    B, L, _ = ckv.shape
    kv = (ckv @ w_kvb).reshape(B, L, N_HEADS, QK_NOPE + V_HEAD)
    return kv[..., :QK_NOPE], kv[..., QK_NOPE:]


def mla_attention(q_nope, q_pe, k_nope, k_pe, v):
    B, Lq = q_nope.shape[:2]
    nb = Lq // Q_BLOCK
    scale = 1.0 / math.sqrt(QK_NOPE + QK_ROPE)

    def to_blocks(t):
        return jnp.moveaxis(t.reshape(B, nb, Q_BLOCK, *t.shape[2:]), 1, 0)

    def block(args):
        qn, qp = args
        s = jnp.einsum('bqhd,bkhd->bhqk', qn, k_nope) + jnp.einsum('bqhr,bkr->bhqk', qp, k_pe)
        p = jax.nn.softmax(s.astype(jnp.float32) * scale, axis=-1).astype(v.dtype)
        return jnp.einsum('bhqk,bkhd->bqhd', p, v)

    o = lax.map(block, (to_blocks(q_nope), to_blocks(q_pe)))
    return jnp.moveaxis(o, 0, 1).reshape(B, Lq, N_HEADS * V_HEAD)


def mla_context(h, w_in, q_norm, w_qb, kv_norm, w_kvb, w_o):
    q_nope, q_pe, ckv, k_pe, gate = mla_project(h, w_in, q_norm, w_qb, kv_norm)
    k_nope, v = mla_expand(ckv, w_kvb)
    o = mla_attention(q_nope, q_pe, k_nope, k_pe, v)
    return (o * jax.nn.silu(gate)) @ w_o, ckv, k_pe


def mla_latent(h, ckv_ctx, kpe_ctx, w_in, q_norm, w_qb, kv_norm, w_kvb, w_o):
    L = h.shape[1]
    q_nope, q_pe, ckv, k_pe, gate = mla_project(h, w_in, q_norm, w_qb, kv_norm)
    ang_r, ang_c = axial_rope_angles(L)
    q_pe = apply_axial_rope(q_pe, ang_r[:, None, :], ang_c[:, None, :])
    k_pe = apply_axial_rope(k_pe, ang_r, ang_c)
    k_nope_l, v_l = mla_expand(ckv, w_kvb)
    k_nope_c, v_c = mla_expand(ckv_ctx, w_kvb)
    k_nope = jnp.concatenate([k_nope_l, k_nope_c], axis=1)
    k_pe_all = jnp.concatenate([k_pe, kpe_ctx], axis=1)
    v = jnp.concatenate([v_l, v_c], axis=1)
    o = mla_attention(q_nope, q_pe, k_nope, k_pe_all, v)
    return (o * jax.nn.silu(gate)) @ w_o


def setup_inputs(seed: int = 0) -> dict:
    key = jax.random.key(seed)
    ks = iter(jax.random.split(key, 40))
    f32 = jnp.float32

    def nrm(shape, s=1.0):
        return jax.random.normal(next(ks), shape, f32) * s

    D = D_MODEL
    return {
        'x_prompt': nrm((BATCH, SEQ, D)),
        'x_sample': nrm((DEC_BATCH, DEC_SEQ, D)),
        'cache_ckv': nrm((DEC_BATCH, N_MLA, PAST_LEN, KV_LORA)),
        'cache_kpe': nrm((DEC_BATCH, N_MLA, PAST_LEN, QK_ROPE)),
        'c': nrm((DEC_BATCH, D)),
        'c_ctx': nrm((D,)),
        'norm_w': 1.0 + nrm((DEPTH, D), 0.02),
        'ada_w': nrm((DEPTH, D, 3 * D), 0.5 * D ** -0.5),
        'ada_b': nrm((DEPTH, 3 * D), 0.02),
        'hy_w_in': nrm((N_HYENA, D, 4 * HY_WIDTH), D ** -0.5),
        'hy_conv_w': nrm((N_HYENA, SHORT_CONV, 3 * HY_WIDTH), 0.5),
        'hy_conv_b': nrm((N_HYENA, 3 * HY_WIDTH), 0.02),
        'hy_f_w1': nrm((N_HYENA, FILTER_EMB, FILTER_HIDDEN), FILTER_EMB ** -0.5),
        'hy_f_b1': nrm((N_HYENA, FILTER_HIDDEN), 0.02),
        'hy_f_freq': 1.0 + nrm((N_HYENA, FILTER_HIDDEN), 0.1),
        'hy_f_w2': nrm((N_HYENA, FILTER_HIDDEN, FILTER_HIDDEN), FILTER_HIDDEN ** -0.5),
        'hy_f_b2': nrm((N_HYENA, FILTER_HIDDEN), 0.02),
        'hy_f_w3': nrm((N_HYENA, FILTER_HIDDEN, 2 * HY_WIDTH), FILTER_HIDDEN ** -0.5),
        'hy_f_bias': nrm((N_HYENA, HY_WIDTH), 0.1),
        'hy_w_out': nrm((N_HYENA, HY_WIDTH, D), HY_WIDTH ** -0.5),
        'mla_w_in': nrm((N_MLA, D, MLA_IN), D ** -0.5),
        'mla_q_norm': 1.0 + nrm((N_MLA, Q_LORA), 0.02),
        'mla_w_qb': nrm((N_MLA, Q_LORA, N_HEADS * (QK_NOPE + QK_ROPE)), Q_LORA ** -0.5),
        'mla_kv_norm': 1.0 + nrm((N_MLA, KV_LORA), 0.02),
        'mla_w_kvb': nrm((N_MLA, KV_LORA, N_HEADS * (QK_NOPE + V_HEAD)), KV_LORA ** -0.5),
        'mla_w_o': nrm((N_MLA, N_HEADS * V_HEAD, D), (N_HEADS * V_HEAD) ** -0.5),
        'final_norm': 1.0 + nrm((D,), 0.02),
    }


def reference(x_prompt, x_sample, cache_ckv, cache_kpe, c, c_ctx, norm_w, ada_w, ada_b,
              hy_w_in, hy_conv_w, hy_conv_b, hy_f_w1, hy_f_b1, hy_f_freq, hy_f_w2, hy_f_b2,
              hy_f_w3, hy_f_bias, hy_w_out, mla_w_in, mla_q_norm, mla_w_qb, mla_kv_norm,
              mla_w_kvb, mla_w_o, final_norm):
    xp, xs = x_prompt, x_sample
    new_ckv, new_kpe = [], []
    for i in range(DEPTH):
        sh_p, sc_p, g_p = adaln(c_ctx[None, :], ada_w[i], ada_b[i])
        sh_s, sc_s, g_s = adaln(c, ada_w[i], ada_b[i])
        hp = rmsnorm(xp, norm_w[i]) * (1.0 + sc_p) + sh_p
        hs = rmsnorm(xs, norm_w[i]) * (1.0 + sc_s) + sh_s
        j = i // N_MIXERS
        if i % N_MIXERS == 0:
            hy = (hy_w_in[j], hy_conv_w[j], hy_conv_b[j], hy_f_w1[j], hy_f_b1[j], hy_f_freq[j],
                  hy_f_w2[j], hy_f_b2[j], hy_f_w3[j], hy_f_bias[j], hy_w_out[j])
            op = hyena_mix(hp, *hy)
            os_ = hyena_mix(hs, *hy)
        else:
            mla = (mla_w_in[j], mla_q_norm[j], mla_w_qb[j], mla_kv_norm[j], mla_w_kvb[j], mla_w_o[j])
            op, ckv, kpe = mla_context(hp, *mla)
            os_ = mla_latent(hs, cache_ckv[:, j], cache_kpe[:, j], *mla)
            new_ckv.append(ckv)
            new_kpe.append(kpe)
        xp = xp + g_p * op
        xs = xs + g_s * os_
    y_prompt = rmsnorm(xp, final_norm)
    y_sample = rmsnorm(xs, final_norm)
    state_ckv = jnp.stack(new_ckv, axis=1)
    state_kpe = jnp.stack(new_kpe, axis=1)
    return (y_prompt, y_sample, state_ckv, state_kpe)
```

```python
import functools
import math

import numpy as np
import jax
import jax.numpy as jnp
from jax import lax
from jax.experimental import pallas as pl
from jax.experimental.pallas import tpu as pltpu

F32 = jnp.float32
BF16 = jnp.bfloat16

D_MODEL = 1024
BATCH = 16
SEQ = 256
DEPTH = 4
DEC_BATCH = 2
DEC_SEQ = 2048
PAST_LEN = 512
GRID_W = 64
EPS = 1e-6
HY_WIDTH = D_MODEL
FILTER_BANDS = 16
FILTER_EMB = 1 + 2 * FILTER_BANDS
FILTER_HIDDEN = 64
FAST_DECAY_PCT = 0.3
SLOW_DECAY_PCT = 1.5
DECAY_TARGET = 1e-2
N_HEADS = 16
Q_LORA = 384
KV_LORA = 256
QK_NOPE = 64
QK_ROPE = 32
V_HEAD = 64
ROPE_THETA = 10000.0

LANES = 128
HEAD_PAD = LANES
ROWS_P = BATCH * SEQ
ROWS_S = DEC_BATCH * DEC_SEQ
ROWS = ROWS_P + ROWS_S
TM = 512
N_TILES = ROWS // TM
TILES_P = ROWS_P // TM
TILES_PER_DEC_SEQ = DEC_SEQ // TM
N_COND = 8
SEQ_BLOCK = 2048
TK = 256
VMEM_CAP = 56 * 1024 * 1024


def _cparams(sem, vmem_bytes):
    return pltpu.CompilerParams(dimension_semantics=sem, vmem_limit_bytes=min(int(vmem_bytes), VMEM_CAP))


def _resident(shape, index_map):
    return pl.BlockSpec(shape, index_map, pipeline_mode=pl.Buffered(1))


def _cond_row(i):
    return jnp.where(i < TILES_P, 0, 1 + (i - TILES_P) // TILES_PER_DEC_SEQ)


def _silu(x):
    return x * jax.nn.sigmoid(x)


def _rms(x, g):
    return x * lax.rsqrt(jnp.mean(x * x, axis=-1, keepdims=True) + EPS) * g


def _ada_kernel(cond_ref, w_ref, b_ref, o_ref):
    s = _silu(cond_ref[...]).astype(BF16)
    o_ref[...] = jnp.dot(s, w_ref[...].astype(BF16), preferred_element_type=F32) + b_ref[...]


def _ada_all(cond, ada_w, ada_b):
    tn = 1024
    return pl.pallas_call(
        _ada_kernel,
        grid=(DEPTH, 3 * D_MODEL // tn),
        in_specs=[
            pl.BlockSpec((N_COND, D_MODEL), lambda l, j: (0, 0)),
            pl.BlockSpec((None, D_MODEL, tn), lambda l, j: (l, 0, j)),
            pl.BlockSpec((None, 1, tn), lambda l, j: (l, 0, j)),
        ],
        out_specs=pl.BlockSpec((None, N_COND, tn), lambda l, j: (l, 0, j)),
        out_shape=jax.ShapeDtypeStruct((DEPTH, N_COND, 3 * D_MODEL), F32),
        compiler_params=_cparams(("arbitrary", "arbitrary"), 32 << 20),
    )(cond, ada_w, ada_b.reshape(DEPTH, 1, 3 * D_MODEL))


def _mod_spec(layer, part):
    return pl.BlockSpec((None, N_COND, D_MODEL), lambda i, *_: (layer, 0, part))


def _modnorm_tile(i, x_ref, nw_ref, sh_ref, sc_ref):
    c = _cond_row(i)
    h = _rms(x_ref[...], nw_ref[...])
    return h * (1.0 + sc_ref[pl.ds(c, 1), :]) + sh_ref[pl.ds(c, 1), :]


def _hy_in_kernel(x_ref, nw_ref, sh_ref, sc_ref, w_ref, o_ref, h_scr, *, tn):
    i = pl.program_id(0)
    j = pl.program_id(1)

    @pl.when(j == 0)
    def _():
        h_scr[...] = _modnorm_tile(i, x_ref, nw_ref, sh_ref, sc_ref).astype(BF16)

    col = pl.multiple_of(j * tn, tn)
    o_ref[...] = jnp.dot(h_scr[...], w_ref[:, pl.ds(col, tn)], preferred_element_type=F32)


def _hy_in(x, norm_w, mod, layer, w_in):
    n = w_in.shape[1]
    tn = 1024
    return pl.pallas_call(
        functools.partial(_hy_in_kernel, tn=tn),
        grid=(N_TILES, n // tn),
        in_specs=[
            pl.BlockSpec((TM, D_MODEL), lambda i, j: (i, 0)),
            pl.BlockSpec((1, D_MODEL), lambda i, j: (0, 0)),
            _mod_spec(layer, 0),
            _mod_spec(layer, 1),
            _resident((D_MODEL, n), lambda i, j: (0, 0)),
        ],
        out_specs=pl.BlockSpec((TM, tn), lambda i, j: (i, j)),
        out_shape=jax.ShapeDtypeStruct((ROWS, n), F32),
        scratch_shapes=[pltpu.VMEM((TM, D_MODEL), BF16)],
        compiler_params=_cparams(("arbitrary", "arbitrary"), 40 << 20),
    )(x, norm_w, mod, mod, w_in)


def _hy_conv_kernel(x0_ref, x1_ref, v_ref, g_ref, w0_ref, w1_ref, wv_ref, b0_ref, b1_ref, bv_ref,
                    z_ref, m_ref):
    r = pl.program_id(0)
    rows, tc = x0_ref.shape
    seq_mask = jnp.where(r < ROWS_P // SEQ_BLOCK, SEQ - 1, DEC_SEQ - 1)
    pos = lax.broadcasted_iota(jnp.int32, (rows, tc), 0) & seq_mask
    first = pos == 0
    last = pos == seq_mask

    def conv(u_ref, w_ref, b_ref):
        u = u_ref[...]
        prev = jnp.where(first, 0.0, pltpu.roll(u, 1, axis=0))
        nxt = jnp.where(last, 0.0, pltpu.roll(u, rows - 1, axis=0))
        return b_ref[...] + prev * w_ref[0:1, :] + u * w_ref[1:2, :] + nxt * w_ref[2:3, :]

    x1 = conv(x1_ref, w1_ref, b1_ref)
    v = conv(v_ref, wv_ref, bv_ref)
    z_ref[...] = v * x1
    x0 = conv(x0_ref, w0_ref, b0_ref)
    m_ref[...] = x0 * _silu(g_ref[...])


def _hy_conv(proj, conv_w, conv_b):
    tc = 128
    nc = HY_WIDTH // tc
    blk = lambda k: pl.BlockSpec((SEQ_BLOCK, tc), lambda r, j: (r, j + k * nc))
    wblk = lambda k: pl.BlockSpec((3, tc), lambda r, j: (0, j + k * nc))
    bblk = lambda k: pl.BlockSpec((1, tc), lambda r, j: (0, j + k * nc))
    out = pl.BlockSpec((SEQ_BLOCK, tc), lambda r, j: (r, j))
    return pl.pallas_call(
        _hy_conv_kernel,
        grid=(ROWS // SEQ_BLOCK, nc),
        in_specs=[blk(0), blk(1), blk(2), blk(3), wblk(0), wblk(1), wblk(2), bblk(0), bblk(1), bblk(2)],
        out_specs=[out, out],
        out_shape=[jax.ShapeDtypeStruct((ROWS, HY_WIDTH), F32)] * 2,
        compiler_params=_cparams(("arbitrary", "arbitrary"), 48 << 20),
    )(proj, proj, proj, proj, conv_w, conv_w, conv_w, conv_b, conv_b, conv_b)


@functools.lru_cache(maxsize=None)
def _filter_consts(L):
    t = np.linspace(0.0, 1.0, L)[:, None]
    w = (2.0 * math.pi / L) * np.arange(L)[:, None]
    bands = np.linspace(1e-4, FILTER_BANDS - 1, FILTER_BANDS)[None, :]
    emb = np.concatenate([t, np.cos(bands * w), -np.sin(bands * w)], axis=-1)
    emb = np.pad(emb, ((0, 0), (0, FILTER_HIDDEN - FILTER_EMB)))
    max_decay = math.log(DECAY_TARGET) / FAST_DECAY_PCT
    min_decay = math.log(DECAY_TARGET) / SLOW_DECAY_PCT
    deltas = np.abs(np.linspace(min_decay, max_decay, HY_WIDTH))
    deltas = np.concatenate([deltas, deltas])[None, :]
    return emb.astype(np.float32), t.astype(np.float32), deltas.astype(np.float32)


def _filter_kernel(emb_ref, t_ref, w1_ref, b1_ref, fr_ref, w2_ref, b2_ref, w3_ref, dl_ref, o_ref):
    hi = lax.Precision.HIGHEST
    fr = fr_ref[...]
    h = jnp.sin(fr * (jnp.dot(emb_ref[...], w1_ref[...], precision=hi, preferred_element_type=F32) + b1_ref[...]))
    h = jnp.sin(fr * (jnp.dot(h, w2_ref[...], precision=hi, preferred_element_type=F32) + b2_ref[...]))
    h = jnp.dot(h, w3_ref[...], precision=hi, preferred_element_type=F32)
    h = h * jnp.exp(-t_ref[...] * dl_ref[...])
    o_ref[...] = h / jnp.sum(jnp.abs(h), axis=0, keepdims=True)


def _hy_filter(L, w1, b1, freq, w2, b2, w3):
    emb, t, deltas = _filter_consts(L)
    tcf = 512
    w1p = jnp.pad(w1, ((0, FILTER_HIDDEN - FILTER_EMB), (0, 0)))
    full = lambda shape: pl.BlockSpec(shape, lambda j: (0, 0))
    return pl.pallas_call(
        _filter_kernel,
        grid=(2 * HY_WIDTH // tcf,),
        in_specs=[
            full((L, FILTER_HIDDEN)), full((L, 1)), full((FILTER_HIDDEN, FILTER_HIDDEN)),
            full((1, FILTER_HIDDEN)), full((1, FILTER_HIDDEN)), full((FILTER_HIDDEN, FILTER_HIDDEN)),
            full((1, FILTER_HIDDEN)),
            pl.BlockSpec((FILTER_HIDDEN, tcf), lambda j: (0, j)),
            pl.BlockSpec((1, tcf), lambda j: (0, j)),
        ],
        out_specs=pl.BlockSpec((L, tcf), lambda j: (0, j)),
        out_shape=jax.ShapeDtypeStruct((L, 2 * HY_WIDTH), F32),
        compiler_params=_cparams(("arbitrary",), 40 << 20),
    )(jnp.asarray(emb), jnp.asarray(t), w1p, b1[None, :], freq[None, :], w2, b2[None, :], w3,
      jnp.asarray(deltas))


@functools.lru_cache(maxsize=None)
def _dft_consts(L):
    tk = min(TK, L)
    k = np.arange(L, dtype=np.int64)
    ang = (np.outer(k, k) % (2 * L)).astype(np.float64) * (math.pi / L)
    c = np.cos(ang)
    s = np.sin(ang)
    s[0, :] = 1.0 - 2.0 * (k % 2)
    nk = L // tk
    fwd = np.concatenate([c.reshape(nk, tk, L), s.reshape(nk, tk, L)], axis=1)
    inv = np.transpose(fwd, (0, 2, 1))
    return fwd.astype(np.float32), np.ascontiguousarray(inv).astype(np.float32)


def _dft_mats(L):
    fwd, inv = _dft_consts(L)
    return jnp.asarray(fwd).astype(BF16), jnp.asarray(inv).astype(BF16)


def _spec_kernel(ff_ref, fb_ref, fw_ref, a_ref, b_ref, d_ref, f_scr, d_scr, *, L):
    kk = pl.program_id(1)
    tk = a_ref.shape[0]

    @pl.when(kk == 0)
    def _():
        ff = ff_ref[...]
        fb = fb_ref[...]
        f_scr[...] = (ff + fb).astype(BF16)
        d_scr[...] = (fb - ff).astype(BF16)

    hre = jnp.dot(fw_ref[0:tk, :], f_scr[...], preferred_element_type=F32)
    him = jnp.dot(fw_ref[tk:2 * tk, :], d_scr[...], preferred_element_type=F32)
    nyq = jnp.dot(fw_ref[tk:tk + 8, :], f_scr[...], preferred_element_type=F32)[0:1, :]
    n = 2.0 * L
    row0 = (lax.broadcasted_iota(jnp.int32, hre.shape, 0) == 0) & (kk == 0)
    a = jnp.where(row0, hre * (1.0 / n), hre * (2.0 / n))
    a_ref[...] = a
    b_ref[...] = jnp.where(row0, 0.0, him * (2.0 / n))
    d_ref[...] = jnp.where(row0, nyq * (1.0 / n), a)


def _hy_spectrum(L, filt, fwd):
    tk = min(TK, L)
    tc = 512
    nc = HY_WIDTH // tc
    out = pl.BlockSpec((tk, tc), lambda j, kk: (kk, j))
    return pl.pallas_call(
        functools.partial(_spec_kernel, L=L),
        grid=(nc, L // tk),
        in_specs=[
            pl.BlockSpec((L, tc), lambda j, kk: (0, j)),
            pl.BlockSpec((L, tc), lambda j, kk: (0, j + nc)),
            pl.BlockSpec((None, 2 * tk, L), lambda j, kk: (kk, 0, 0)),
        ],
        out_specs=[out, out, out],
        out_shape=[jax.ShapeDtypeStruct((L, HY_WIDTH), F32)] * 3,
        scratch_shapes=[pltpu.VMEM((L, tc), BF16), pltpu.VMEM((L, tc), BF16)],
        compiler_params=_cparams(("arbitrary", "arbitrary"), 40 << 20),
    )(filt, filt, fwd)


def _lconv_kernel(z_ref, fw_ref, iv_ref, a_ref, b_ref, d_ref, y_ref, zb_scr, *, L):
    kk = pl.program_id(2)
    tk = a_ref.shape[0]
    n_seq = z_ref.shape[0] // L

    @pl.when(kk == 0)
    def _():
        zb_scr[...] = z_ref[...].astype(BF16)
        y_ref[...] = jnp.zeros_like(y_ref)

    a = a_ref[...]
    b = b_ref[...]
    d = d_ref[...]
    for s in range(n_seq):
        rows = pl.ds(s * L, L)
        zz = jnp.dot(fw_ref[...], zb_scr[rows, :], preferred_element_type=F32)
        zr = zz[0:tk]
        zi = zz[tk:2 * tk]
        yy = jnp.concatenate([zr * a + zi * b, zi * d - zr * b], axis=0).astype(BF16)
        y_ref[rows, :] += jnp.dot(iv_ref[...], yy, preferred_element_type=F32)


def _hy_lconv(L, row_block0, n_row_blocks, z, fwd, inv, a, b, d):
    tk = min(TK, L)
    tc = 512
    nc = HY_WIDTH // tc
    coef = pl.BlockSpec((tk, tc), lambda r, j, kk: (kk, j))
    return pl.pallas_call(
        functools.partial(_lconv_kernel, L=L),
        grid=(n_row_blocks, nc, L // tk),
        in_specs=[
            pl.BlockSpec((SEQ_BLOCK, tc), lambda r, j, kk: (r + row_block0, j)),
            pl.BlockSpec((None, 2 * tk, L), lambda r, j, kk: (kk, 0, 0)),
            pl.BlockSpec((None, L, 2 * tk), lambda r, j, kk: (kk, 0, 0)),
            coef, coef, coef,
        ],
        out_specs=pl.BlockSpec((SEQ_BLOCK, tc), lambda r, j, kk: (r, j)),
        out_shape=jax.ShapeDtypeStruct((n_row_blocks * SEQ_BLOCK, HY_WIDTH), F32),
        scratch_shapes=[pltpu.VMEM((SEQ_BLOCK, tc), BF16)],
        compiler_params=_cparams(("arbitrary", "arbitrary", "arbitrary"), 48 << 20),
    )(z, fwd, inv, a, b, d)


def _hy_out_kernel(y_ref, z_ref, m_ref, fb_ref, w_ref, x_ref, g_ref, o_ref):
    i = pl.program_id(0)
    t = ((y_ref[...] + z_ref[...] * fb_ref[...]) * m_ref[...]).astype(BF16)
    o = jnp.dot(t, w_ref[...], preferred_element_type=F32)
    o_ref[...] = x_ref[...] + g_ref[pl.ds(_cond_row(i), 1), :] * o


def _hy_out(y, z, m, f_bias, w_out, x, mod, layer):
    tile = pl.BlockSpec((TM, D_MODEL), lambda i: (i, 0))
    return pl.pallas_call(
        _hy_out_kernel,
        grid=(N_TILES,),
        in_specs=[tile, tile, tile,
                  pl.BlockSpec((1, HY_WIDTH), lambda i: (0, 0)),
                  _resident((HY_WIDTH, D_MODEL), lambda i: (0, 0)),
                  tile, _mod_spec(layer, 2)],
        out_specs=tile,
        out_shape=jax.ShapeDtypeStruct((ROWS, D_MODEL), F32),
        compiler_params=_cparams(("arbitrary",), 40 << 20),
    )(y, z, m, f_bias, w_out, x, mod)


@functools.lru_cache(maxsize=None)
def _rope_consts():
    axis_dim = QK_ROPE // 2
    nf = axis_dim // 2
    inv = ROPE_THETA ** (-np.arange(0, axis_dim, 2, dtype=np.float64) / axis_dim)
    t = np.arange(DEC_SEQ)
    ang_r = (t // GRID_W)[:, None] * inv
    ang_c = (t % GRID_W)[:, None] * inv
    cos = np.ones((DEC_SEQ, HEAD_PAD))
    sin_up = np.zeros((DEC_SEQ, HEAD_PAD))
    sin_dn = np.zeros((DEC_SEQ, HEAD_PAD))
    for base, ang in ((QK_NOPE, ang_r), (QK_NOPE + axis_dim, ang_c)):
        cos[:, base:base + nf] = np.cos(ang)
        cos[:, base + nf:base + 2 * nf] = np.cos(ang)
        sin_up[:, base:base + nf] = -np.sin(ang)
        sin_dn[:, base + nf:base + 2 * nf] = np.sin(ang)
    return cos.astype(np.float32), sin_up.astype(np.float32), sin_dn.astype(np.float32)


ROPE_HALF = QK_ROPE // 4


def _mla_proj_kernel(x_ref, nw_ref, sh_ref, sc_ref, win_ref, qn_ref, kvn_ref, wqb_ref, wk_ref, wv_ref,
                     cos_ref, sup_ref, sdn_ref,
                     q_ref, k_ref, v_ref, sg_ref, ckv_ref, kpe_ref):
    i = pl.program_id(0)
    h = _modnorm_tile(i, x_ref, nw_ref, sh_ref, sc_ref).astype(BF16)
    proj = jnp.dot(h, win_ref[...], preferred_element_type=F32)
    o_kv = Q_LORA
    o_gate = Q_LORA + KV_LORA
    o_pe = o_gate + N_HEADS * V_HEAD
    qn = _rms(proj[:, 0:o_kv], qn_ref[...]).astype(BF16)
    ckv = _rms(proj[:, o_kv:o_gate], kvn_ref[...])
    gate = proj[:, o_gate:o_pe]
    kpe = proj[:, o_pe:o_pe + HEAD_PAD]
    ckv_ref[...] = ckv
    kpe_ref[...] = kpe
    sg_ref[...] = _silu(gate)
    ckv_b = ckv.astype(BF16)
    v_ref[...] = jnp.dot(ckv_b, wv_ref[...], preferred_element_type=F32).astype(BF16)
    q = jnp.dot(qn, wqb_ref[...], preferred_element_type=F32)
    kn = jnp.dot(ckv_b, wk_ref[...], preferred_element_type=F32)

    latent = i >= TILES_P
    cos = jnp.where(latent, cos_ref[...], 1.0)
    sup = jnp.where(latent, sup_ref[...], 0.0)
    sdn = jnp.where(latent, sdn_ref[...], 0.0)

    def rope(u):
        return (u * cos + pltpu.roll(u, HEAD_PAD - ROPE_HALF, axis=1) * sup
                + pltpu.roll(u, ROPE_HALF, axis=1) * sdn)

    kpe_r = rope(kpe)
    for hd in range(N_HEADS):
        cols = slice(hd * HEAD_PAD, (hd + 1) * HEAD_PAD)
        q_ref[:, cols] = rope(q[:, cols]).astype(BF16)
        k_ref[:, cols] = (kn[:, cols] + kpe_r).astype(BF16)


def _mla_proj(x, norm_w, mod, layer, w):
    cos, sup, sdn = (jnp.asarray(c) for c in _rope_consts())
    n_in = w["w_in"].shape[1]
    hp = N_HEADS * HEAD_PAD
    rope_blk = pl.BlockSpec(
        (TM, HEAD_PAD), lambda i: (jnp.where(i >= TILES_P, (i - TILES_P) % TILES_PER_DEC_SEQ, 0), 0))
    tile = lambda n: pl.BlockSpec((TM, n), lambda i: (i, 0))
    const = lambda shape: _resident(shape, lambda i: (0, 0))
    return pl.pallas_call(
        _mla_proj_kernel,
        grid=(N_TILES,),
        in_specs=[
            tile(D_MODEL),
            pl.BlockSpec((1, D_MODEL), lambda i: (0, 0)),
            _mod_spec(layer, 0), _mod_spec(layer, 1),
            const((D_MODEL, n_in)), const((1, Q_LORA)), const((1, KV_LORA)),
            const((Q_LORA, hp)), const((KV_LORA, hp)), const((KV_LORA, N_HEADS * V_HEAD)),
            rope_blk, rope_blk, rope_blk,
        ],
        out_specs=[tile(hp), tile(hp), tile(N_HEADS * V_HEAD), tile(N_HEADS * V_HEAD),
                   tile(KV_LORA), tile(HEAD_PAD)],
        out_shape=[
            jax.ShapeDtypeStruct((ROWS, hp), BF16),
            jax.ShapeDtypeStruct((ROWS, hp), BF16),
            jax.ShapeDtypeStruct((ROWS, N_HEADS * V_HEAD), BF16),
            jax.ShapeDtypeStruct((ROWS, N_HEADS * V_HEAD), F32),
            jax.ShapeDtypeStruct((ROWS, KV_LORA), F32),
            jax.ShapeDtypeStruct((ROWS, HEAD_PAD), F32),
        ],
        compiler_params=_cparams(("arbitrary",), 56 << 20),
    )(x, norm_w, mod, mod, w["w_in"], w["q_norm"], w["kv_norm"], w["w_qb"], w["w_k"], w["w_v"],
      cos, sup, sdn)


def _mla_ctx_kernel(ckv_ref, kpe_ref, wk_ref, wv_ref, k_ref, v_ref):
    ckv_b = ckv_ref[...].astype(BF16)
    kn = jnp.dot(ckv_b, wk_ref[...], preferred_element_type=F32)
    v_ref[...] = jnp.dot(ckv_b, wv_ref[...], preferred_element_type=F32).astype(BF16)
    kpe = kpe_ref[...]
    for hd in range(N_HEADS):
        cols = slice(hd * HEAD_PAD, (hd + 1) * HEAD_PAD)
        k_ref[:, cols] = (kn[:, cols] + kpe).astype(BF16)


def _mla_ctx(ckv_ctx, kpe_ctx, w):
    hp = N_HEADS * HEAD_PAD
    rows = DEC_BATCH * PAST_LEN
    tile = lambda n: pl.BlockSpec((PAST_LEN, n), lambda i: (i, 0))
    const = lambda shape: _resident(shape, lambda i: (0, 0))
    return pl.pallas_call(
        _mla_ctx_kernel,
        grid=(DEC_BATCH,),
        in_specs=[tile(KV_LORA), tile(HEAD_PAD), const((KV_LORA, hp)), const((KV_LORA, N_HEADS * V_HEAD))],
        out_specs=[tile(hp), tile(N_HEADS * V_HEAD)],
        out_shape=[jax.ShapeDtypeStruct((rows, hp), BF16),
                   jax.ShapeDtypeStruct((rows, N_HEADS * V_HEAD), BF16)],
        compiler_params=_cparams(("arbitrary",), 32 << 20),
    )(ckv_ctx, kpe_ctx, w["w_k"], w["w_v"])


ATTN_SCALE = 1.0 / math.sqrt(QK_NOPE + QK_ROPE)
NT_DIMS = (((1,), (1,)), ((), ()))


def _attn_kernel(*refs, n_pairs, has_ctx):
    if has_ctx:
        q_ref, k_ref, v_ref, kc_ref, vc_ref, o_ref = refs
    else:
        q_ref, k_ref, v_ref, o_ref = refs
    tq = q_ref.shape[0]
    low_half = lax.broadcasted_iota(jnp.int32, (tq, LANES), 1) < V_HEAD
    for p in range(n_pairs):
        vcols = slice(p * LANES, (p + 1) * LANES)
        vp = v_ref[:, vcols]
        outs = []
        for hh in range(2):
            cols = slice((2 * p + hh) * HEAD_PAD, (2 * p + hh + 1) * HEAD_PAD)
            q = q_ref[:, cols]
            s = lax.dot_general(q, k_ref[:, cols], NT_DIMS, preferred_element_type=F32) * ATTN_SCALE
            mx = jnp.max(s, axis=-1, keepdims=True)
            if has_ctx:
                sc = lax.dot_general(q, kc_ref[:, cols], NT_DIMS, preferred_element_type=F32) * ATTN_SCALE
                mx = jnp.maximum(mx, jnp.max(sc, axis=-1, keepdims=True))
            e = jnp.exp(s - mx)
            den = jnp.sum(e, axis=-1, keepdims=True)
            pv = jnp.dot(e.astype(BF16), vp, preferred_element_type=F32)
            if has_ctx:
                ec = jnp.exp(sc - mx)
                den = den + jnp.sum(ec, axis=-1, keepdims=True)
                pv = pv + jnp.dot(ec.astype(BF16), vc_ref[:, vcols], preferred_element_type=F32)
            outs.append(pv / den)
        o_ref[:, vcols] = jnp.where(low_half, outs[0], outs[1])


def _attn_prompt(q, k, v):
    hp = N_HEADS * HEAD_PAD
    nv = N_HEADS * V_HEAD
    return pl.pallas_call(
        functools.partial(_attn_kernel, n_pairs=N_HEADS // 2, has_ctx=False),
        grid=(BATCH,),
        in_specs=[pl.BlockSpec((SEQ, hp), lambda b: (b, 0)),
                  pl.BlockSpec((SEQ, hp), lambda b: (b, 0)),
                  pl.BlockSpec((SEQ, nv), lambda b: (b, 0))],
        out_specs=pl.BlockSpec((SEQ, nv), lambda b: (b, 0)),
        out_shape=jax.ShapeDtypeStruct((ROWS_P, nv), F32),
        compiler_params=_cparams(("arbitrary",), 40 << 20),
    )(q, k, v)


def _attn_latent(q, k, v, kc, vc):
    tq = 256
    nv = N_HEADS * V_HEAD
    q0 = ROWS_P // tq
    s0 = ROWS_P // DEC_SEQ
    pair = 2 * HEAD_PAD
    return pl.pallas_call(
        functools.partial(_attn_kernel, n_pairs=1, has_ctx=True),
        grid=(DEC_BATCH, N_HEADS // 2, DEC_SEQ // tq),
        in_specs=[pl.BlockSpec((tq, pair), lambda b, p, t: (q0 + b * (DEC_SEQ // tq) + t, p)),
                  pl.BlockSpec((DEC_SEQ, pair), lambda b, p, t: (s0 + b, p)),
                  pl.BlockSpec((DEC_SEQ, LANES), lambda b, p, t: (s0 + b, p)),
                  pl.BlockSpec((PAST_LEN, pair), lambda b, p, t: (b, p)),
                  pl.BlockSpec((PAST_LEN, LANES), lambda b, p, t: (b, p))],
        out_specs=pl.BlockSpec((tq, LANES), lambda b, p, t: (b * (DEC_SEQ // tq) + t, p)),
        out_shape=jax.ShapeDtypeStruct((ROWS_S, nv), F32),
        compiler_params=_cparams(("arbitrary", "arbitrary", "arbitrary"), 40 << 20),
    )(q, k, v, kc, vc)


def _mla_out_kernel(op_ref, os_ref, sg_ref, w_ref, x_ref, g_ref, o_ref):
    i = pl.program_id(0)
    o = jnp.where(i < TILES_P, op_ref[...], os_ref[...])
    t = (o * sg_ref[...]).astype(BF16)
    u = jnp.dot(t, w_ref[...], preferred_element_type=F32)
    o_ref[...] = x_ref[...] + g_ref[pl.ds(_cond_row(i), 1), :] * u


def _mla_out(o_p, o_s, sg, w_o, x, mod, layer):
    tile = pl.BlockSpec((TM, D_MODEL), lambda i: (i, 0))
    return pl.pallas_call(
        _mla_out_kernel,
        grid=(N_TILES,),
        in_specs=[pl.BlockSpec((TM, D_MODEL), lambda i: (jnp.minimum(i, TILES_P - 1), 0)),
                  pl.BlockSpec((TM, D_MODEL), lambda i: (jnp.maximum(i - TILES_P, 0), 0)),
                  tile,
                  _resident((N_HEADS * V_HEAD, D_MODEL), lambda i: (0, 0)),
                  tile, _mod_spec(layer, 2)],
        out_specs=tile,
        out_shape=jax.ShapeDtypeStruct((ROWS, D_MODEL), F32),
        compiler_params=_cparams(("arbitrary",), 40 << 20),
    )(o_p, o_s, sg, w_o, x, mod)


def _final_kernel(x_ref, g_ref, o_ref):
    o_ref[...] = _rms(x_ref[...], g_ref[...])


def _final_norm(x, g, tile0, n_tiles):
    return pl.pallas_call(
        _final_kernel,
        grid=(n_tiles,),
        in_specs=[pl.BlockSpec((TM, D_MODEL), lambda i: (i + tile0, 0)),
                  pl.BlockSpec((1, D_MODEL), lambda i: (0, 0))],
        out_specs=pl.BlockSpec((TM, D_MODEL), lambda i: (i, 0)),
        out_shape=jax.ShapeDtypeStruct((n_tiles * TM, D_MODEL), F32),
        compiler_params=_cparams(("arbitrary",), 24 << 20),
    )(x, g)


def _mla_weights(w_in, q_norm, w_qb, kv_norm, w_kvb, w_o):
    o_pe = Q_LORA + KV_LORA
    o_gate = o_pe + QK_ROPE
    zeros = lambda n: jnp.zeros((D_MODEL, n), F32)
    w_in_r = jnp.concatenate(
        [w_in[:, :o_pe], w_in[:, o_gate:], zeros(QK_NOPE), w_in[:, o_pe:o_gate],
         zeros(HEAD_PAD - QK_NOPE - QK_ROPE)], axis=1)
    qb = w_qb.reshape(Q_LORA, N_HEADS, QK_NOPE + QK_ROPE)
    qb = jnp.pad(qb, ((0, 0), (0, 0), (0, HEAD_PAD - QK_NOPE - QK_ROPE)))
    kvb = w_kvb.reshape(KV_LORA, N_HEADS, QK_NOPE + V_HEAD)
    wk = jnp.pad(kvb[:, :, :QK_NOPE], ((0, 0), (0, 0), (0, HEAD_PAD - QK_NOPE)))
    wv = kvb[:, :, QK_NOPE:]
    return {
        "w_in": w_in_r.astype(BF16),
        "q_norm": q_norm[None, :],
        "kv_norm": kv_norm[None, :],
        "w_qb": qb.reshape(Q_LORA, N_HEADS * HEAD_PAD).astype(BF16),
        "w_k": wk.reshape(KV_LORA, N_HEADS * HEAD_PAD).astype(BF16),
        "w_v": wv.reshape(KV_LORA, N_HEADS * V_HEAD).astype(BF16),
        "w_o": w_o.astype(BF16),
    }


def kernel(x_prompt, x_sample, cache_ckv, cache_kpe, c, c_ctx, norm_w, ada_w, ada_b, hy_w_in, hy_conv_w, hy_conv_b, hy_f_w1, hy_f_b1, hy_f_freq, hy_f_w2, hy_f_b2, hy_f_w3, hy_f_bias, hy_w_out, mla_w_in, mla_q_norm, mla_w_qb, mla_kv_norm, mla_w_kvb, mla_w_o, final_norm):
    x = jnp.concatenate([x_prompt.reshape(ROWS_P, D_MODEL), x_sample.reshape(ROWS_S, D_MODEL)], axis=0)
    cond = jnp.concatenate([c_ctx[None, :], c, jnp.zeros((N_COND - 1 - DEC_BATCH, D_MODEL), F32)], axis=0)
    mod = _ada_all(cond, ada_w, ada_b)

    new_ckv, new_kpe = [], []
    for layer in range(DEPTH):
        j = layer // 2
        if layer % 2 == 0:
            proj = _hy_in(x, norm_w[layer][None, :], mod, layer, hy_w_in[j].astype(BF16))
            z, m = _hy_conv(proj, hy_conv_w[j], hy_conv_b[j][None, :])
            ys = []
            for L, blk0, nblk in ((SEQ, 0, ROWS_P // SEQ_BLOCK), (DEC_SEQ, ROWS_P // SEQ_BLOCK, ROWS_S // SEQ_BLOCK)):
                filt = _hy_filter(L, hy_f_w1[j], hy_f_b1[j], hy_f_freq[j], hy_f_w2[j], hy_f_b2[j], hy_f_w3[j])
                fwd, inv = _dft_mats(L)
                a, b, d = _hy_spectrum(L, filt, fwd)
                ys.append(_hy_lconv(L, blk0, nblk, z, fwd, inv, a, b, d))
            y = jnp.concatenate(ys, axis=0)
            x = _hy_out(y, z, m, hy_f_bias[j][None, :], hy_w_out[j].astype(BF16), x, mod, layer)
        else:
            w = _mla_weights(mla_w_in[j], mla_q_norm[j], mla_w_qb[j], mla_kv_norm[j], mla_w_kvb[j], mla_w_o[j])
            q, k, v, sg, ckv, kpe = _mla_proj(x, norm_w[layer][None, :], mod, layer, w)
            kpe_ctx = jnp.pad(cache_kpe[:, j].reshape(DEC_BATCH * PAST_LEN, QK_ROPE),
                              ((0, 0), (QK_NOPE, HEAD_PAD - QK_NOPE - QK_ROPE)))
            kc, vc = _mla_ctx(cache_ckv[:, j].reshape(DEC_BATCH * PAST_LEN, KV_LORA), kpe_ctx, w)
            o_p = _attn_prompt(q, k, v)
            o_s = _attn_latent(q, k, v, kc, vc)
            x = _mla_out(o_p, o_s, sg, w["w_o"], x, mod, layer)
            new_ckv.append(ckv[:ROWS_P].reshape(BATCH, SEQ, KV_LORA))
            new_kpe.append(kpe[:ROWS_P, QK_NOPE:QK_NOPE + QK_ROPE].reshape(BATCH, SEQ, QK_ROPE))

    y_prompt = _final_norm(x, final_norm[None, :], 0, TILES_P).reshape(BATCH, SEQ, D_MODEL)
    y_sample = _final_norm(x, final_norm[None, :], TILES_P, N_TILES - TILES_P).reshape(DEC_BATCH, DEC_SEQ, D_MODEL)
    return (y_prompt, y_sample, jnp.stack(new_ckv, axis=1), jnp.stack(new_kpe, axis=1))
```

```python
import functools
import math

import numpy as np
import jax
import jax.numpy as jnp
from jax import lax
from jax.experimental import pallas as pl
from jax.experimental.pallas import tpu as pltpu

F32 = jnp.float32
BF16 = jnp.bfloat16

D_MODEL = 1024
BATCH = 16
SEQ = 256
DEPTH = 4
DEC_BATCH = 2
DEC_SEQ = 2048
PAST_LEN = 512
GRID_W = 64
EPS = 1e-6
HY_WIDTH = D_MODEL
FILTER_BANDS = 16
FILTER_EMB = 1 + 2 * FILTER_BANDS
FILTER_HIDDEN = 64
FAST_DECAY_PCT = 0.3
SLOW_DECAY_PCT = 1.5
DECAY_TARGET = 1e-2
N_HEADS = 16
Q_LORA = 384
KV_LORA = 256
QK_NOPE = 64
QK_ROPE = 32
V_HEAD = 64
ROPE_THETA = 10000.0

LANES = 128
HEAD_PAD = LANES
ROWS_P = BATCH * SEQ
ROWS_S = DEC_BATCH * DEC_SEQ
ROWS = ROWS_P + ROWS_S
TM = 512
N_TILES = ROWS // TM
TILES_P = ROWS_P // TM
TILES_PER_DEC_SEQ = DEC_SEQ // TM
N_COND = 8
SEQ_BLOCK = 2048
CONV_ROWS = 256
TK = 512
LCONV_TC = 256
VMEM_CAP = 56 * 1024 * 1024


def _cparams(sem, vmem_bytes):
    return pltpu.CompilerParams(dimension_semantics=sem, vmem_limit_bytes=min(int(vmem_bytes), VMEM_CAP))


def _resident(shape, index_map):
    return pl.BlockSpec(shape, index_map, pipeline_mode=pl.Buffered(1))


def _cond_row(i):
    return jnp.where(i < TILES_P, 0, 1 + (i - TILES_P) // TILES_PER_DEC_SEQ)


def _silu(x):
    return x * jax.nn.sigmoid(x)


def _rms(x, g):
    return x * lax.rsqrt(jnp.mean(x * x, axis=-1, keepdims=True) + EPS) * g


def _row_pair(x):
    if isinstance(x, tuple):
        return x[0], x[1], 0
    return x, x, TILES_P


def _row_pair_specs(x, n):
    _, _, base = _row_pair(x)
    lo = pl.BlockSpec((TM, n), lambda i, *_: (jnp.minimum(i, TILES_P - 1), 0))
    hi = pl.BlockSpec((TM, n), lambda i, *_: (jnp.maximum(i - TILES_P, 0) + base, 0))
    return lo, hi


def _pick_rows(i, lo_ref, hi_ref):
    return jnp.where(i < TILES_P, lo_ref[...], hi_ref[...])


def _ada_kernel(cond_ref, w_ref, b_ref, o_ref):
    s = _silu(cond_ref[...]).astype(BF16)
    o_ref[...] = jnp.dot(s, w_ref[...].astype(BF16), preferred_element_type=F32) + b_ref[...]


def _ada_all(cond, ada_w, ada_b):
    tn = 1024
    return pl.pallas_call(
        _ada_kernel,
        name="ada",
        grid=(DEPTH, 3 * D_MODEL // tn),
        in_specs=[
            pl.BlockSpec((N_COND, D_MODEL), lambda l, j: (0, 0)),
            pl.BlockSpec((None, D_MODEL, tn), lambda l, j: (l, 0, j)),
            pl.BlockSpec((None, 1, tn), lambda l, j: (l, 0, j)),
        ],
        out_specs=pl.BlockSpec((None, N_COND, tn), lambda l, j: (l, 0, j)),
        out_shape=jax.ShapeDtypeStruct((DEPTH, N_COND, 3 * D_MODEL), F32),
        compiler_params=_cparams(("arbitrary", "arbitrary"), 32 << 20),
    )(cond, ada_w, ada_b.reshape(DEPTH, 1, 3 * D_MODEL))


def _mod_spec(layer, part):
    return pl.BlockSpec((None, N_COND, D_MODEL), lambda i, *_: (layer, 0, part))


def _modnorm(i, x, nw_ref, sh_ref, sc_ref):
    c = _cond_row(i)
    return _rms(x, nw_ref[...]) * (1.0 + sc_ref[pl.ds(c, 1), :]) + sh_ref[pl.ds(c, 1), :]


def _hy_in_kernel(xlo_ref, xhi_ref, nw_ref, sh_ref, sc_ref, w_ref, o_ref, h_scr, *, tn):
    i = pl.program_id(0)
    j = pl.program_id(1)

    @pl.when(j == 0)
    def _():
        h_scr[...] = _modnorm(i, _pick_rows(i, xlo_ref, xhi_ref), nw_ref, sh_ref, sc_ref).astype(BF16)

    col = pl.multiple_of(j * tn, tn)
    o_ref[...] = jnp.dot(h_scr[...], w_ref[:, pl.ds(col, tn)], preferred_element_type=F32)


def _hy_in(x, norm_w, mod, layer, w_in):
    n = w_in.shape[1]
    tn = 2048
    xlo, xhi, _ = _row_pair(x)
    lo, hi = _row_pair_specs(x, D_MODEL)
    return pl.pallas_call(
        functools.partial(_hy_in_kernel, tn=tn),
        name="hy_in",
        grid=(N_TILES, n // tn),
        in_specs=[
            lo, hi,
            pl.BlockSpec((1, D_MODEL), lambda i, j: (0, 0)),
            _mod_spec(layer, 0),
            _mod_spec(layer, 1),
            _resident((D_MODEL, n), lambda i, j: (0, 0)),
        ],
        out_specs=pl.BlockSpec((TM, tn), lambda i, j: (i, j)),
        out_shape=jax.ShapeDtypeStruct((ROWS, n), F32),
        scratch_shapes=[pltpu.VMEM((TM, D_MODEL), BF16)],
        compiler_params=_cparams(("arbitrary", "arbitrary"), 44 << 20),
    )(xlo, xhi, norm_w, mod, mod, w_in)


@functools.lru_cache(maxsize=None)
def _filter_consts(L):
    t = np.linspace(0.0, 1.0, L)[:, None]
    w = (2.0 * math.pi / L) * np.arange(L)[:, None]
    bands = np.linspace(1e-4, FILTER_BANDS - 1, FILTER_BANDS)[None, :]
    emb = np.concatenate([t, np.cos(bands * w), -np.sin(bands * w)], axis=-1)
    emb = np.pad(emb, ((0, 0), (0, FILTER_HIDDEN - FILTER_EMB)))
    max_decay = math.log(DECAY_TARGET) / FAST_DECAY_PCT
    min_decay = math.log(DECAY_TARGET) / SLOW_DECAY_PCT
    deltas = np.abs(np.linspace(min_decay, max_decay, HY_WIDTH))
    deltas = np.concatenate([deltas, deltas])[None, :]
    return emb.astype(np.float32), t.astype(np.float32), deltas.astype(np.float32)


def _filter_kernel(emb_ref, t_ref, w1_ref, b1_ref, fr_ref, w2_ref, b2_ref, w3_ref, dl_ref, o_ref, h_scr):
    @pl.when(pl.program_id(0) == 0)
    def _():
        hi = lax.Precision.HIGHEST
        fr = fr_ref[...]
        h = jnp.sin(fr * (jnp.dot(emb_ref[...], w1_ref[...], precision=hi, preferred_element_type=F32) + b1_ref[...]))
        h = jnp.sin(fr * (jnp.dot(h, w2_ref[...], precision=hi, preferred_element_type=F32) + b2_ref[...]))
        h_scr[...] = h.astype(BF16)

    h = jnp.dot(h_scr[...], w3_ref[...].astype(BF16), preferred_element_type=F32)
    h = h * jnp.exp(-t_ref[...] * dl_ref[...])
    o_ref[...] = h / jnp.sum(jnp.abs(h), axis=0, keepdims=True)


def _hy_filter(L, w1, b1, freq, w2, b2, w3):
    emb, t, deltas = _filter_consts(L)
    tcf = 512
    w1p = jnp.pad(w1, ((0, FILTER_HIDDEN - FILTER_EMB), (0, 0)))
    full = lambda shape: pl.BlockSpec(shape, lambda j: (0, 0))
    return pl.pallas_call(
        _filter_kernel,
        name=f"hy_filter{L}",
        grid=(2 * HY_WIDTH // tcf,),
        in_specs=[
            full((L, FILTER_HIDDEN)), full((L, 1)), full((FILTER_HIDDEN, FILTER_HIDDEN)),
            full((1, FILTER_HIDDEN)), full((1, FILTER_HIDDEN)), full((FILTER_HIDDEN, FILTER_HIDDEN)),
            full((1, FILTER_HIDDEN)),
            pl.BlockSpec((FILTER_HIDDEN, tcf), lambda j: (0, j)),
            pl.BlockSpec((1, tcf), lambda j: (0, j)),
        ],
        out_specs=pl.BlockSpec((L, tcf), lambda j: (0, j)),
        out_shape=jax.ShapeDtypeStruct((L, 2 * HY_WIDTH), F32),
        scratch_shapes=[pltpu.VMEM((L, FILTER_HIDDEN), BF16)],
        compiler_params=_cparams(("arbitrary",), 40 << 20),
    )(jnp.asarray(emb), jnp.asarray(t), w1p, b1[None, :], freq[None, :], w2, b2[None, :], w3,
      jnp.asarray(deltas))


@functools.lru_cache(maxsize=None)
def _dft_consts(L):
    k = np.arange(L, dtype=np.int64)
    ang = (np.outer(k, k) % (2 * L)).astype(np.float64) * (math.pi / L)
    return np.cos(ang).astype(np.float32), np.sin(ang).astype(np.float32)


def _dft_mats(L):
    c, s = _dft_consts(L)
    return jnp.asarray(c).astype(BF16), jnp.asarray(s).astype(BF16)


def _alt_sign(shape):
    return (1 - 2 * (lax.broadcasted_iota(jnp.int32, shape, 0) & 1)).astype(F32)


def _spec_kernel(ff_ref, fb_ref, c_ref, s_ref, a_ref, b_ref, nq_ref, f_scr, d_scr, *, L, tk):
    kk = pl.program_id(1)
    inv_n = 1.0 / (2.0 * L)

    @pl.when(kk == 0)
    def _():
        ff = ff_ref[...]
        fb = fb_ref[...]
        f = ff + fb
        f_scr[...] = f.astype(BF16)
        d_scr[...] = (fb - ff).astype(BF16)
        nyq = jnp.sum(f * _alt_sign(f.shape), axis=0, keepdims=True) * inv_n
        nq_ref[...] = jnp.broadcast_to(nyq, nq_ref.shape)

    k0 = pl.multiple_of(kk * tk, tk)
    hre = jnp.dot(c_ref[pl.ds(k0, tk), :], f_scr[...], preferred_element_type=F32)
    him = jnp.dot(s_ref[pl.ds(k0, tk), :], d_scr[...], preferred_element_type=F32)
    dc_row = (lax.broadcasted_iota(jnp.int32, hre.shape, 0) == 0) & (kk == 0)
    a_ref[...] = jnp.where(dc_row, hre * inv_n, hre * (2.0 * inv_n))
    b_ref[...] = him * (2.0 * inv_n)


def _hy_spectrum(L, filt, c, s):
    tk = min(TK, L)
    tc = LCONV_TC
    nc = HY_WIDTH // tc
    out = pl.BlockSpec((tk, tc), lambda j, kk: (kk, j))
    mat = _resident((L, L), lambda j, kk: (0, 0))
    return pl.pallas_call(
        functools.partial(_spec_kernel, L=L, tk=tk),
        name=f"hy_spectrum{L}",
        grid=(nc, L // tk),
        in_specs=[
            pl.BlockSpec((L, tc), lambda j, kk: (0, j)),
            pl.BlockSpec((L, tc), lambda j, kk: (0, j + nc)),
            mat, mat,
        ],
        out_specs=[out, out, pl.BlockSpec((8, tc), lambda j, kk: (0, j))],
        out_shape=[jax.ShapeDtypeStruct((L, HY_WIDTH), F32)] * 2 + [jax.ShapeDtypeStruct((8, HY_WIDTH), F32)],
        scratch_shapes=[pltpu.VMEM((L, tc), BF16), pltpu.VMEM((L, tc), BF16)],
        compiler_params=_cparams(("arbitrary", "arbitrary"), 40 << 20),
    )(filt, filt, c, s)


def _conv_chunk(u_ref, w_ref, b_ref, c0, L):
    r = CONV_ROWS
    u = u_ref[pl.ds(c0, r), :]
    row = lax.broadcasted_iota(jnp.int32, u.shape, 0)
    if c0 % L == 0:
        prev = jnp.where(row == 0, 0.0, pltpu.roll(u, 1, axis=0))
    else:
        prev = u_ref[pl.ds(c0 - 1, r), :]
    if (c0 + r) % L == 0:
        nxt = jnp.where(row == r - 1, 0.0, pltpu.roll(u, r - 1, axis=0))
    else:
        nxt = u_ref[pl.ds(c0 + 1, r), :]
    return b_ref[...] + prev * w_ref[0:1, :] + u * w_ref[1:2, :] + nxt * w_ref[2:3, :]


def _lconv_kernel(x0_ref, x1_ref, v_ref, g_ref, w0_ref, w1_ref, wv_ref, b0_ref, b1_ref, bv_ref, fb_ref,
                  c_ref, s_ref, a_ref, b_ref, nq_ref, t_ref, z_scr, zb_scr, y_scr, *, L, tk):
    kk = pl.program_id(2)
    rows, tc = z_scr.shape
    chunks = [pl.ds(c0, CONV_ROWS) for c0 in range(0, rows, CONV_ROWS)]
    seqs = [pl.ds(s0, L) for s0 in range(0, rows, L)]

    @pl.when(kk == 0)
    def _():
        for ch in chunks:
            z = _conv_chunk(v_ref, wv_ref, bv_ref, ch.start, L) * _conv_chunk(x1_ref, w1_ref, b1_ref, ch.start, L)
            z_scr[ch, :] = z
            zb_scr[ch, :] = z.astype(BF16)
        alt = _alt_sign((L, tc))
        for sq in seqs:
            z_nyq = jnp.sum(z_scr[sq, :] * alt, axis=0, keepdims=True)
            y_scr[sq, :] = alt * (z_nyq * nq_ref[0:1, :])

    k0 = pl.multiple_of(kk * tk, tk)
    a = a_ref[...]
    b = b_ref[...]
    for sq in seqs:
        zb = zb_scr[sq, :]
        zr = jnp.dot(c_ref[pl.ds(k0, tk), :], zb, preferred_element_type=F32)
        zi = jnp.dot(s_ref[pl.ds(k0, tk), :], zb, preferred_element_type=F32)
        yr = (zr * a + zi * b).astype(BF16)
        yw = (zi * a - zr * b).astype(BF16)
        y_scr[sq, :] += (jnp.dot(c_ref[:, pl.ds(k0, tk)], yr, preferred_element_type=F32)
                         + jnp.dot(s_ref[:, pl.ds(k0, tk)], yw, preferred_element_type=F32))

    @pl.when(kk == pl.num_programs(2) - 1)
    def _():
        for ch in chunks:
            x0 = _conv_chunk(x0_ref, w0_ref, b0_ref, ch.start, L)
            y = (y_scr[ch, :] + z_scr[ch, :] * fb_ref[...]) * x0
            t_ref[ch, :] = (y * _silu(g_ref[ch, :])).astype(BF16)


def _hy_lconv(L, row_block0, n_row_blocks, proj, conv_w, conv_b, f_bias, c, s, a, b, nq):
    tk = min(TK, L)
    tc = LCONV_TC
    nc = HY_WIDTH // tc
    blk = lambda k: pl.BlockSpec((SEQ_BLOCK, tc), lambda r, j, kk: (r + row_block0, j + k * nc))
    wblk = lambda k: pl.BlockSpec((3, tc), lambda r, j, kk: (0, j + k * nc))
    bblk = lambda k: pl.BlockSpec((1, tc), lambda r, j, kk: (0, j + k * nc))
    coef = pl.BlockSpec((tk, tc), lambda r, j, kk: (kk, j))
    mat = _resident((L, L), lambda r, j, kk: (0, 0))
    return pl.pallas_call(
        functools.partial(_lconv_kernel, L=L, tk=tk),
        name=f"hy_lconv{L}",
        grid=(n_row_blocks, nc, L // tk),
        in_specs=[blk(0), blk(1), blk(2), blk(3), wblk(0), wblk(1), wblk(2), bblk(0), bblk(1), bblk(2),
                  pl.BlockSpec((1, tc), lambda r, j, kk: (0, j)),
                  mat, mat, coef, coef,
                  pl.BlockSpec((8, tc), lambda r, j, kk: (0, j))],
        out_specs=pl.BlockSpec((SEQ_BLOCK, tc), lambda r, j, kk: (r, j)),
        out_shape=jax.ShapeDtypeStruct((n_row_blocks * SEQ_BLOCK, HY_WIDTH), BF16),
        scratch_shapes=[pltpu.VMEM((SEQ_BLOCK, tc), F32), pltpu.VMEM((SEQ_BLOCK, tc), BF16),
                        pltpu.VMEM((SEQ_BLOCK, tc), F32)],
        compiler_params=_cparams(("arbitrary", "arbitrary", "arbitrary"), 56 << 20),
    )(proj, proj, proj, proj, conv_w, conv_w, conv_w, conv_b, conv_b, conv_b, f_bias, c, s, a, b, nq)


def _out_kernel(tlo_ref, thi_ref, w_ref, xlo_ref, xhi_ref, g_ref, o_ref):
    i = pl.program_id(0)
    u = jnp.dot(_pick_rows(i, tlo_ref, thi_ref), w_ref[...], preferred_element_type=F32)
    o_ref[...] = _pick_rows(i, xlo_ref, xhi_ref) + g_ref[pl.ds(_cond_row(i), 1), :] * u


def _out_proj(name, t, w_out, x, mod, layer):
    tlo, thi, _ = _row_pair(t)
    xlo, xhi, _ = _row_pair(x)
    return pl.pallas_call(
        _out_kernel,
        name=name,
        grid=(N_TILES,),
        in_specs=[*_row_pair_specs(t, w_out.shape[0]),
                  _resident(w_out.shape, lambda i: (0, 0)),
                  *_row_pair_specs(x, D_MODEL),
                  _mod_spec(layer, 2)],
        out_specs=pl.BlockSpec((TM, D_MODEL), lambda i: (i, 0)),
        out_shape=jax.ShapeDtypeStruct((ROWS, D_MODEL), F32),
        compiler_params=_cparams(("arbitrary",), 32 << 20),
    )(tlo, thi, w_out, xlo, xhi, mod)


@functools.lru_cache(maxsize=None)
def _rope_consts():
    axis_dim = QK_ROPE // 2
    nf = axis_dim // 2
    inv = ROPE_THETA ** (-np.arange(0, axis_dim, 2, dtype=np.float64) / axis_dim)
    t = np.arange(DEC_SEQ)
    ang_r = (t // GRID_W)[:, None] * inv
    ang_c = (t % GRID_W)[:, None] * inv
    cos = np.ones((DEC_SEQ, HEAD_PAD))
    sin_up = np.zeros((DEC_SEQ, HEAD_PAD))
    sin_dn = np.zeros((DEC_SEQ, HEAD_PAD))
    for base, ang in ((QK_NOPE, ang_r), (QK_NOPE + axis_dim, ang_c)):
        cos[:, base:base + nf] = np.cos(ang)
        cos[:, base + nf:base + 2 * nf] = np.cos(ang)
        sin_up[:, base:base + nf] = -np.sin(ang)
        sin_dn[:, base + nf:base + 2 * nf] = np.sin(ang)
    return cos.astype(np.float32), sin_up.astype(np.float32), sin_dn.astype(np.float32)


ROPE_HALF = QK_ROPE // 4
Q_SCALE = math.log2(math.e) / math.sqrt(QK_NOPE + QK_ROPE)


def _mla_proj_kernel(x_ref, nw_ref, sh_ref, sc_ref, win_ref, qn_ref, kvn_ref, wqb_ref, wk_ref, wv_ref,
                     cos_ref, sup_ref, sdn_ref,
                     q_ref, k_ref, v_ref, sg_ref, ckv_ref, kpe_ref):
    i = pl.program_id(0)
    h = _modnorm(i, x_ref[...], nw_ref, sh_ref, sc_ref).astype(BF16)
    proj = jnp.dot(h, win_ref[...], preferred_element_type=F32)
    o_kv = Q_LORA
    o_gate = Q_LORA + KV_LORA
    o_pe = o_gate + N_HEADS * V_HEAD
    qn = _rms(proj[:, 0:o_kv], qn_ref[...]).astype(BF16)
    ckv = _rms(proj[:, o_kv:o_gate], kvn_ref[...])
    gate = proj[:, o_gate:o_pe]
    kpe = proj[:, o_pe:o_pe + HEAD_PAD]
    ckv_ref[...] = ckv
    kpe_ref[...] = kpe
    sg_ref[...] = _silu(gate)
    ckv_b = ckv.astype(BF16)
    v_ref[...] = jnp.dot(ckv_b, wv_ref[...], preferred_element_type=F32).astype(BF16)
    q = jnp.dot(qn, wqb_ref[...], preferred_element_type=F32)
    kn = jnp.dot(ckv_b, wk_ref[...], preferred_element_type=F32)

    latent = i >= TILES_P
    cos = jnp.where(latent, cos_ref[...], 1.0)
    sup = jnp.where(latent, sup_ref[...], 0.0)
    sdn = jnp.where(latent, sdn_ref[...], 0.0)

    def rope(u, scale):
        return (u * (cos * scale) + pltpu.roll(u, HEAD_PAD - ROPE_HALF, axis=1) * (sup * scale)
                + pltpu.roll(u, ROPE_HALF, axis=1) * (sdn * scale))

    kpe_r = rope(kpe, 1.0)
    for hd in range(N_HEADS):
        cols = slice(hd * HEAD_PAD, (hd + 1) * HEAD_PAD)
        q_ref[:, cols] = rope(q[:, cols], Q_SCALE).astype(BF16)
        k_ref[:, cols] = (kn[:, cols] + kpe_r).astype(BF16)


def _mla_proj(x, norm_w, mod, layer, w):
    cos, sup, sdn = (jnp.asarray(c) for c in _rope_consts())
    n_in = w["w_in"].shape[1]
    hp = N_HEADS * HEAD_PAD
    rope_blk = pl.BlockSpec(
        (TM, HEAD_PAD), lambda i: (jnp.where(i >= TILES_P, (i - TILES_P) % TILES_PER_DEC_SEQ, 0), 0))
    tile = lambda n: pl.BlockSpec((TM, n), lambda i: (i, 0))
    const = lambda shape: _resident(shape, lambda i: (0, 0))
    return pl.pallas_call(
        _mla_proj_kernel,
        name="mla_proj",
        grid=(N_TILES,),
        in_specs=[
            tile(D_MODEL),
            pl.BlockSpec((1, D_MODEL), lambda i: (0, 0)),
            _mod_spec(layer, 0), _mod_spec(layer, 1),
            const((D_MODEL, n_in)), const((1, Q_LORA)), const((1, KV_LORA)),
            const((Q_LORA, hp)), const((KV_LORA, hp)), const((KV_LORA, N_HEADS * V_HEAD)),
            rope_blk, rope_blk, rope_blk,
        ],
        out_specs=[tile(hp), tile(hp), tile(N_HEADS * V_HEAD), tile(N_HEADS * V_HEAD),
                   tile(KV_LORA), tile(HEAD_PAD)],
        out_shape=[
            jax.ShapeDtypeStruct((ROWS, hp), BF16),
            jax.ShapeDtypeStruct((ROWS, hp), BF16),
            jax.ShapeDtypeStruct((ROWS, N_HEADS * V_HEAD), BF16),
            jax.ShapeDtypeStruct((ROWS, N_HEADS * V_HEAD), F32),
            jax.ShapeDtypeStruct((ROWS, KV_LORA), F32),
            jax.ShapeDtypeStruct((ROWS, HEAD_PAD), F32),
        ],
        compiler_params=_cparams(("arbitrary",), 56 << 20),
    )(x, norm_w, mod, mod, w["w_in"], w["q_norm"], w["kv_norm"], w["w_qb"], w["w_k"], w["w_v"],
      cos, sup, sdn)


def _mla_ctx_kernel(ckv_ref, kpe_ref, wk_ref, wv_ref, k_ref, v_ref):
    ckv_b = ckv_ref[...].astype(BF16)
    kn = jnp.dot(ckv_b, wk_ref[...], preferred_element_type=F32)
    v_ref[...] = jnp.dot(ckv_b, wv_ref[...], preferred_element_type=F32).astype(BF16)
    kpe = kpe_ref[...]
    for hd in range(N_HEADS):
        cols = slice(hd * HEAD_PAD, (hd + 1) * HEAD_PAD)
        k_ref[:, cols] = (kn[:, cols] + kpe).astype(BF16)


def _mla_ctx(ckv_ctx, kpe_ctx, w):
    hp = N_HEADS * HEAD_PAD
    rows = DEC_BATCH * PAST_LEN
    tile = lambda n: pl.BlockSpec((PAST_LEN, n), lambda i: (i, 0))
    const = lambda shape: _resident(shape, lambda i: (0, 0))
    return pl.pallas_call(
        _mla_ctx_kernel,
        name="mla_ctx",
        grid=(DEC_BATCH,),
        in_specs=[tile(KV_LORA), tile(HEAD_PAD), const((KV_LORA, hp)), const((KV_LORA, N_HEADS * V_HEAD))],
        out_specs=[tile(hp), tile(N_HEADS * V_HEAD)],
        out_shape=[jax.ShapeDtypeStruct((rows, hp), BF16),
                   jax.ShapeDtypeStruct((rows, N_HEADS * V_HEAD), BF16)],
        compiler_params=_cparams(("arbitrary",), 32 << 20),
    )(ckv_ctx, kpe_ctx, w["w_k"], w["w_v"])


NT_DIMS = (((1,), (1,)), ((), ()))


def _attn_kernel(*refs, n_pairs, has_ctx):
    if has_ctx:
        q_ref, k_ref, v_ref, kc_ref, vc_ref, sg_ref, o_ref = refs
    else:
        q_ref, k_ref, v_ref, sg_ref, o_ref = refs
    tq = q_ref.shape[0]
    low_half = lax.broadcasted_iota(jnp.int32, (tq, LANES), 1) < V_HEAD
    for p in range(n_pairs):
        vcols = slice(p * LANES, (p + 1) * LANES)
        vp = v_ref[:, vcols]
        outs = []
        for hh in range(2):
            cols = slice((2 * p + hh) * HEAD_PAD, (2 * p + hh + 1) * HEAD_PAD)
            q = q_ref[:, cols]
            s = lax.dot_general(q, k_ref[:, cols], NT_DIMS, preferred_element_type=F32)
            mx = jnp.max(s, axis=-1, keepdims=True)
            if has_ctx:
                sc = lax.dot_general(q, kc_ref[:, cols], NT_DIMS, preferred_element_type=F32)
                mx = jnp.maximum(mx, jnp.max(sc, axis=-1, keepdims=True))
            e = jnp.exp2(s - mx)
            den = jnp.sum(e, axis=-1, keepdims=True)
            pv = jnp.dot(e.astype(BF16), vp, preferred_element_type=F32)
            if has_ctx:
                ec = jnp.exp2(sc - mx)
                den = den + jnp.sum(ec, axis=-1, keepdims=True)
                pv = pv + jnp.dot(ec.astype(BF16), vc_ref[:, vcols], preferred_element_type=F32)
            outs.append(pv / den)
        o = jnp.where(low_half, outs[0], outs[1])
        o_ref[:, vcols] = (o * sg_ref[:, vcols]).astype(BF16)


def _attn_prompt(q, k, v, sg):
    hp = N_HEADS * HEAD_PAD
    nv = N_HEADS * V_HEAD
    return pl.pallas_call(
        functools.partial(_attn_kernel, n_pairs=N_HEADS // 2, has_ctx=False),
        name="attn_prompt",
        grid=(BATCH,),
        in_specs=[pl.BlockSpec((SEQ, hp), lambda b: (b, 0)),
                  pl.BlockSpec((SEQ, hp), lambda b: (b, 0)),
                  pl.BlockSpec((SEQ, nv), lambda b: (b, 0)),
                  pl.BlockSpec((SEQ, nv), lambda b: (b, 0))],
        out_specs=pl.BlockSpec((SEQ, nv), lambda b: (b, 0)),
        out_shape=jax.ShapeDtypeStruct((ROWS_P, nv), BF16),
        compiler_params=_cparams(("arbitrary",), 40 << 20),
    )(q, k, v, sg)


def _attn_latent(q, k, v, kc, vc, sg):
    tq = 256
    nv = N_HEADS * V_HEAD
    q0 = ROWS_P // tq
    s0 = ROWS_P // DEC_SEQ
    pair = 2 * HEAD_PAD
    qrow = lambda b, p, t: (q0 + b * (DEC_SEQ // tq) + t, p)
    return pl.pallas_call(
        functools.partial(_attn_kernel, n_pairs=1, has_ctx=True),
        name="attn_latent",
        grid=(DEC_BATCH, N_HEADS // 2, DEC_SEQ // tq),
        in_specs=[pl.BlockSpec((tq, pair), qrow),
                  pl.BlockSpec((DEC_SEQ, pair), lambda b, p, t: (s0 + b, p)),
                  pl.BlockSpec((DEC_SEQ, LANES), lambda b, p, t: (s0 + b, p)),
                  pl.BlockSpec((PAST_LEN, pair), lambda b, p, t: (b, p)),
                  pl.BlockSpec((PAST_LEN, LANES), lambda b, p, t: (b, p)),
                  pl.BlockSpec((tq, LANES), qrow)],
        out_specs=pl.BlockSpec((tq, LANES), lambda b, p, t: (b * (DEC_SEQ // tq) + t, p)),
        out_shape=jax.ShapeDtypeStruct((ROWS_S, nv), BF16),
        compiler_params=_cparams(("arbitrary", "arbitrary", "arbitrary"), 40 << 20),
    )(q, k, v, kc, vc, sg)


def _final_kernel(x_ref, g_ref, o_ref):
    o_ref[...] = _rms(x_ref[...], g_ref[...])


def _final_norm(name, x, g, tile0, n_tiles):
    return pl.pallas_call(
        _final_kernel,
        name=name,
        grid=(n_tiles,),
        in_specs=[pl.BlockSpec((TM, D_MODEL), lambda i: (i + tile0, 0)),
                  pl.BlockSpec((1, D_MODEL), lambda i: (0, 0))],
        out_specs=pl.BlockSpec((TM, D_MODEL), lambda i: (i, 0)),
        out_shape=jax.ShapeDtypeStruct((n_tiles * TM, D_MODEL), F32),
        compiler_params=_cparams(("arbitrary",), 24 << 20),
    )(x, g)


def _mla_weights(w_in, q_norm, w_qb, kv_norm, w_kvb, w_o):
    o_pe = Q_LORA + KV_LORA
    o_gate = o_pe + QK_ROPE
    zeros = lambda n: jnp.zeros((D_MODEL, n), F32)
    w_in_r = jnp.concatenate(
        [w_in[:, :o_pe], w_in[:, o_gate:], zeros(QK_NOPE), w_in[:, o_pe:o_gate],
         zeros(HEAD_PAD - QK_NOPE - QK_ROPE)], axis=1)
    qb = w_qb.reshape(Q_LORA, N_HEADS, QK_NOPE + QK_ROPE)
    qb = jnp.pad(qb, ((0, 0), (0, 0), (0, HEAD_PAD - QK_NOPE - QK_ROPE)))
    kvb = w_kvb.reshape(KV_LORA, N_HEADS, QK_NOPE + V_HEAD)
    wk = jnp.pad(kvb[:, :, :QK_NOPE], ((0, 0), (0, 0), (0, HEAD_PAD - QK_NOPE)))
    wv = kvb[:, :, QK_NOPE:]
    return {
        "w_in": w_in_r.astype(BF16),
        "q_norm": q_norm[None, :],
        "kv_norm": kv_norm[None, :],
        "w_qb": qb.reshape(Q_LORA, N_HEADS * HEAD_PAD).astype(BF16),
        "w_k": wk.reshape(KV_LORA, N_HEADS * HEAD_PAD).astype(BF16),
        "w_v": wv.reshape(KV_LORA, N_HEADS * V_HEAD).astype(BF16),
        "w_o": w_o.astype(BF16),
    }


def kernel(x_prompt, x_sample, cache_ckv, cache_kpe, c, c_ctx, norm_w, ada_w, ada_b, hy_w_in, hy_conv_w, hy_conv_b, hy_f_w1, hy_f_b1, hy_f_freq, hy_f_w2, hy_f_b2, hy_f_w3, hy_f_bias, hy_w_out, mla_w_in, mla_q_norm, mla_w_qb, mla_kv_norm, mla_w_kvb, mla_w_o, final_norm):
    x = (x_prompt.reshape(ROWS_P, D_MODEL), x_sample.reshape(ROWS_S, D_MODEL))
    cond = jnp.concatenate([c_ctx[None, :], c, jnp.zeros((N_COND - 1 - DEC_BATCH, D_MODEL), F32)], axis=0)
    mod = _ada_all(cond, ada_w, ada_b)

    new_ckv, new_kpe = [], []
    for layer in range(DEPTH):
        j = layer // 2
        nw = norm_w[layer][None, :]
        if layer % 2 == 0:
            proj = _hy_in(x, nw, mod, layer, hy_w_in[j].astype(BF16))
            t = []
            for L, blk0, nblk in ((SEQ, 0, ROWS_P // SEQ_BLOCK), (DEC_SEQ, ROWS_P // SEQ_BLOCK, ROWS_S // SEQ_BLOCK)):
                filt = _hy_filter(L, hy_f_w1[j], hy_f_b1[j], hy_f_freq[j], hy_f_w2[j], hy_f_b2[j], hy_f_w3[j])
                cm, sm = _dft_mats(L)
                a, b, nq = _hy_spectrum(L, filt, cm, sm)
                t.append(_hy_lconv(L, blk0, nblk, proj, hy_conv_w[j], hy_conv_b[j][None, :],
                                   hy_f_bias[j][None, :], cm, sm, a, b, nq))
            x = _out_proj("hy_out", tuple(t), hy_w_out[j].astype(BF16), x, mod, layer)
        else:
            w = _mla_weights(mla_w_in[j], mla_q_norm[j], mla_w_qb[j], mla_kv_norm[j], mla_w_kvb[j], mla_w_o[j])
            q, k, v, sg, ckv, kpe = _mla_proj(x, nw, mod, layer, w)
            kpe_ctx = jnp.pad(cache_kpe[:, j].reshape(DEC_BATCH * PAST_LEN, QK_ROPE),
                              ((0, 0), (QK_NOPE, HEAD_PAD - QK_NOPE - QK_ROPE)))
            kc, vc = _mla_ctx(cache_ckv[:, j].reshape(DEC_BATCH * PAST_LEN, KV_LORA), kpe_ctx, w)
            o_p = _attn_prompt(q, k, v, sg)
            o_s = _attn_latent(q, k, v, kc, vc, sg)
            x = _out_proj("mla_out", (o_p, o_s), w["w_o"], x, mod, layer)
            new_ckv.append(ckv[:ROWS_P].reshape(BATCH, SEQ, KV_LORA))
            new_kpe.append(kpe[:ROWS_P, QK_NOPE:QK_NOPE + QK_ROPE].reshape(BATCH, SEQ, QK_ROPE))

    y_prompt = _final_norm("final_prompt", x, final_norm[None, :], 0, TILES_P)
    y_sample = _final_norm("final_latent", x, final_norm[None, :], TILES_P, N_TILES - TILES_P)
    return (y_prompt.reshape(BATCH, SEQ, D_MODEL), y_sample.reshape(DEC_BATCH, DEC_SEQ, D_MODEL),
            jnp.stack(new_ckv, axis=1), jnp.stack(new_kpe, axis=1))
```

```python
import functools
import math

import numpy as np
import jax
import jax.numpy as jnp
from jax import lax
from jax.experimental import pallas as pl
from jax.experimental.pallas import tpu as pltpu

F32 = jnp.float32
BF16 = jnp.bfloat16

D_MODEL = 1024
BATCH = 16
SEQ = 256
DEPTH = 4
DEC_BATCH = 2
DEC_SEQ = 2048
PAST_LEN = 512
GRID_W = 64
EPS = 1e-6
HY_WIDTH = D_MODEL
FILTER_BANDS = 16
FILTER_EMB = 1 + 2 * FILTER_BANDS
FILTER_HIDDEN = 64
FAST_DECAY_PCT = 0.3
SLOW_DECAY_PCT = 1.5
DECAY_TARGET = 1e-2
N_HEADS = 16
Q_LORA = 384
KV_LORA = 256
QK_NOPE = 64
QK_ROPE = 32
V_HEAD = 64
ROPE_THETA = 10000.0

LANES = 128
HEAD_PAD = LANES
ROWS_P = BATCH * SEQ
ROWS_S = DEC_BATCH * DEC_SEQ
ROWS = ROWS_P + ROWS_S
TM = 512
N_TILES = ROWS // TM
TILES_P = ROWS_P // TM
TILES_PER_DEC_SEQ = DEC_SEQ // TM
N_COND = 8
SEQ_BLOCK = 2048
CONV_ROWS = 256
TK = 512
LCONV_TC = 256
VMEM_CAP = 56 * 1024 * 1024


def _cparams(sem, vmem_bytes):
    return pltpu.CompilerParams(dimension_semantics=sem, vmem_limit_bytes=min(int(vmem_bytes), VMEM_CAP))


def _resident(shape, index_map):
    return pl.BlockSpec(shape, index_map, pipeline_mode=pl.Buffered(1))


def _cond_row(i):
    return jnp.where(i < TILES_P, 0, 1 + (i - TILES_P) // TILES_PER_DEC_SEQ)


def _silu(x):
    return x * jax.nn.sigmoid(x)


def _rms(x, g):
    return x * lax.rsqrt(jnp.mean(x * x, axis=-1, keepdims=True) + EPS) * g


def _row_pair(x):
    if isinstance(x, tuple):
        return x[0], x[1], 0
    return x, x, TILES_P


def _row_pair_specs(x, n):
    _, _, base = _row_pair(x)
    lo = pl.BlockSpec((TM, n), lambda i, *_: (jnp.minimum(i, TILES_P - 1), 0))
    hi = pl.BlockSpec((TM, n), lambda i, *_: (jnp.maximum(i - TILES_P, 0) + base, 0))
    return lo, hi


def _pick_rows(i, lo_ref, hi_ref):
    return jnp.where(i < TILES_P, lo_ref[...], hi_ref[...])


def _ada_kernel(cond_ref, w_ref, b_ref, o_ref):
    s = _silu(cond_ref[...]).astype(BF16)
    o_ref[...] = jnp.dot(s, w_ref[...].astype(BF16), preferred_element_type=F32) + b_ref[...]


def _ada_all(cond, ada_w, ada_b):
    tn = 1024
    return pl.pallas_call(
        _ada_kernel,
        name="ada",
        grid=(DEPTH, 3 * D_MODEL // tn),
        in_specs=[
            pl.BlockSpec((N_COND, D_MODEL), lambda l, j: (0, 0)),
            pl.BlockSpec((None, D_MODEL, tn), lambda l, j: (l, 0, j)),
            pl.BlockSpec((None, 1, tn), lambda l, j: (l, 0, j)),
        ],
        out_specs=pl.BlockSpec((None, N_COND, tn), lambda l, j: (l, 0, j)),
        out_shape=jax.ShapeDtypeStruct((DEPTH, N_COND, 3 * D_MODEL), F32),
        compiler_params=_cparams(("arbitrary", "arbitrary"), 32 << 20),
    )(cond, ada_w, ada_b.reshape(DEPTH, 1, 3 * D_MODEL))


def _mod_spec(layer, part):
    return pl.BlockSpec((None, N_COND, D_MODEL), lambda i, *_: (layer, 0, part))


def _modnorm(i, x, nw_ref, sh_ref, sc_ref):
    c = _cond_row(i)
    return _rms(x, nw_ref[...]) * (1.0 + sc_ref[pl.ds(c, 1), :]) + sh_ref[pl.ds(c, 1), :]


def _hy_in_kernel(xlo_ref, xhi_ref, nw_ref, sh_ref, sc_ref, w_ref, o_ref, h_scr, *, tn):
    i = pl.program_id(0)
    j = pl.program_id(1)

    @pl.when(j == 0)
    def _():
        h_scr[...] = _modnorm(i, _pick_rows(i, xlo_ref, xhi_ref), nw_ref, sh_ref, sc_ref).astype(BF16)

    col = pl.multiple_of(j * tn, tn)
    o_ref[...] = jnp.dot(h_scr[...], w_ref[:, pl.ds(col, tn)], preferred_element_type=F32)


def _hy_in(x, norm_w, mod, layer, w_in):
    n = w_in.shape[1]
    tn = 2048
    xlo, xhi, _ = _row_pair(x)
    lo, hi = _row_pair_specs(x, D_MODEL)
    return pl.pallas_call(
        functools.partial(_hy_in_kernel, tn=tn),
        name="hy_in",
        grid=(N_TILES, n // tn),
        in_specs=[
            lo, hi,
            pl.BlockSpec((1, D_MODEL), lambda i, j: (0, 0)),
            _mod_spec(layer, 0),
            _mod_spec(layer, 1),
            _resident((D_MODEL, n), lambda i, j: (0, 0)),
        ],
        out_specs=pl.BlockSpec((TM, tn), lambda i, j: (i, j)),
        out_shape=jax.ShapeDtypeStruct((ROWS, n), F32),
        scratch_shapes=[pltpu.VMEM((TM, D_MODEL), BF16)],
        compiler_params=_cparams(("arbitrary", "arbitrary"), 44 << 20),
    )(xlo, xhi, norm_w, mod, mod, w_in)


@functools.lru_cache(maxsize=None)
def _filter_consts(L):
    t = np.linspace(0.0, 1.0, L)[:, None]
    w = (2.0 * math.pi / L) * np.arange(L)[:, None]
    bands = np.linspace(1e-4, FILTER_BANDS - 1, FILTER_BANDS)[None, :]
    emb = np.concatenate([t, np.cos(bands * w), -np.sin(bands * w)], axis=-1)
    emb = np.pad(emb, ((0, 0), (0, FILTER_HIDDEN - FILTER_EMB)))
    max_decay = math.log(DECAY_TARGET) / FAST_DECAY_PCT
    min_decay = math.log(DECAY_TARGET) / SLOW_DECAY_PCT
    deltas = np.abs(np.linspace(min_decay, max_decay, HY_WIDTH))
    deltas = np.concatenate([deltas, deltas])[None, :]
    return emb.astype(np.float32), t.astype(np.float32), deltas.astype(np.float32)


def _filter_kernel(emb_ref, t_ref, w1_ref, b1_ref, fr_ref, w2_ref, b2_ref, w3_ref, dl_ref, o_ref, h_scr):
    @pl.when(pl.program_id(0) == 0)
    def _():
        hi = lax.Precision.HIGHEST
        fr = fr_ref[...]
        h = jnp.sin(fr * (jnp.dot(emb_ref[...], w1_ref[...], precision=hi, preferred_element_type=F32) + b1_ref[...]))
        h = jnp.sin(fr * (jnp.dot(h, w2_ref[...], precision=hi, preferred_element_type=F32) + b2_ref[...]))
        h_scr[...] = h.astype(BF16)

    h = jnp.dot(h_scr[...], w3_ref[...].astype(BF16), preferred_element_type=F32)
    h = h * jnp.exp(-t_ref[...] * dl_ref[...])
    o_ref[...] = h / jnp.sum(jnp.abs(h), axis=0, keepdims=True)


def _hy_filter(L, w1, b1, freq, w2, b2, w3):
    emb, t, deltas = _filter_consts(L)
    tcf = 512
    w1p = jnp.pad(w1, ((0, FILTER_HIDDEN - FILTER_EMB), (0, 0)))
    full = lambda shape: pl.BlockSpec(shape, lambda j: (0, 0))
    return pl.pallas_call(
        _filter_kernel,
        name=f"hy_filter{L}",
        grid=(2 * HY_WIDTH // tcf,),
        in_specs=[
            full((L, FILTER_HIDDEN)), full((L, 1)), full((FILTER_HIDDEN, FILTER_HIDDEN)),
            full((1, FILTER_HIDDEN)), full((1, FILTER_HIDDEN)), full((FILTER_HIDDEN, FILTER_HIDDEN)),
            full((1, FILTER_HIDDEN)),
            pl.BlockSpec((FILTER_HIDDEN, tcf), lambda j: (0, j)),
            pl.BlockSpec((1, tcf), lambda j: (0, j)),
        ],
        out_specs=pl.BlockSpec((L, tcf), lambda j: (0, j)),
        out_shape=jax.ShapeDtypeStruct((L, 2 * HY_WIDTH), F32),
        scratch_shapes=[pltpu.VMEM((L, FILTER_HIDDEN), BF16)],
        compiler_params=_cparams(("arbitrary",), 40 << 20),
    )(jnp.asarray(emb), jnp.asarray(t), w1p, b1[None, :], freq[None, :], w2, b2[None, :], w3,
      jnp.asarray(deltas))


@functools.lru_cache(maxsize=None)
def _dft_consts(L):
    k = np.arange(L, dtype=np.int64)
    ang = (np.outer(k, k) % (2 * L)).astype(np.float64) * (math.pi / L)
    return np.cos(ang).astype(np.float32), np.sin(ang).astype(np.float32)


def _dft_mats(L):
    c, s = _dft_consts(L)
    return jnp.asarray(c).astype(BF16), jnp.asarray(s).astype(BF16)


def _alt_sign(shape):
    return (1 - 2 * (lax.broadcasted_iota(jnp.int32, shape, 0) & 1)).astype(F32)


def _spec_kernel(ff_ref, fb_ref, c_ref, s_ref, a_ref, b_ref, nq_ref, f_scr, d_scr, *, L, tk):
    kk = pl.program_id(1)
    inv_n = 1.0 / (2.0 * L)

    @pl.when(kk == 0)
    def _():
        ff = ff_ref[...]
        fb = fb_ref[...]
        f = ff + fb
        f_scr[...] = f.astype(BF16)
        d_scr[...] = (fb - ff).astype(BF16)
        nyq = jnp.sum(f * _alt_sign(f.shape), axis=0, keepdims=True) * inv_n
        nq_ref[...] = jnp.broadcast_to(nyq, nq_ref.shape)

    k0 = pl.multiple_of(kk * tk, tk)
    hre = jnp.dot(c_ref[pl.ds(k0, tk), :], f_scr[...], preferred_element_type=F32)
    him = jnp.dot(s_ref[pl.ds(k0, tk), :], d_scr[...], preferred_element_type=F32)
    dc_row = (lax.broadcasted_iota(jnp.int32, hre.shape, 0) == 0) & (kk == 0)
    a_ref[...] = jnp.where(dc_row, hre * inv_n, hre * (2.0 * inv_n))
    b_ref[...] = him * (2.0 * inv_n)


def _hy_spectrum(L, filt, c, s):
    tk = min(TK, L)
    tc = LCONV_TC
    nc = HY_WIDTH // tc
    out = pl.BlockSpec((tk, tc), lambda j, kk: (kk, j))
    mat = _resident((L, L), lambda j, kk: (0, 0))
    return pl.pallas_call(
        functools.partial(_spec_kernel, L=L, tk=tk),
        name=f"hy_spectrum{L}",
        grid=(nc, L // tk),
        in_specs=[
            pl.BlockSpec((L, tc), lambda j, kk: (0, j)),
            pl.BlockSpec((L, tc), lambda j, kk: (0, j + nc)),
            mat, mat,
        ],
        out_specs=[out, out, pl.BlockSpec((8, tc), lambda j, kk: (0, j))],
        out_shape=[jax.ShapeDtypeStruct((L, HY_WIDTH), F32)] * 2 + [jax.ShapeDtypeStruct((8, HY_WIDTH), F32)],
        scratch_shapes=[pltpu.VMEM((L, tc), BF16), pltpu.VMEM((L, tc), BF16)],
        compiler_params=_cparams(("arbitrary", "arbitrary"), 40 << 20),
    )(filt, filt, c, s)


def _conv_chunk(u_ref, w_ref, b_ref, c0, L):
    r = CONV_ROWS
    u = u_ref[pl.ds(c0, r), :]
    row = lax.broadcasted_iota(jnp.int32, u.shape, 0)
    if c0 % L == 0:
        prev = jnp.where(row == 0, 0.0, pltpu.roll(u, 1, axis=0))
    else:
        prev = u_ref[pl.ds(c0 - 1, r), :]
    if (c0 + r) % L == 0:
        nxt = jnp.where(row == r - 1, 0.0, pltpu.roll(u, r - 1, axis=0))
    else:
        nxt = u_ref[pl.ds(c0 + 1, r), :]
    return b_ref[...] + prev * w_ref[0:1, :] + u * w_ref[1:2, :] + nxt * w_ref[2:3, :]


def _lconv_kernel(x0_ref, x1_ref, v_ref, g_ref, w0_ref, w1_ref, wv_ref, b0_ref, b1_ref, bv_ref, fb_ref,
                  c_ref, s_ref, a_ref, b_ref, nq_ref, t_ref, z_scr, zb_scr, y_scr, *, L, tk):
    kk = pl.program_id(2)
    rows, tc = z_scr.shape
    chunks = [pl.ds(c0, CONV_ROWS) for c0 in range(0, rows, CONV_ROWS)]
    seqs = [pl.ds(s0, L) for s0 in range(0, rows, L)]

    @pl.when(kk == 0)
    def _():
        for ch in chunks:
            z = _conv_chunk(v_ref, wv_ref, bv_ref, ch.start, L) * _conv_chunk(x1_ref, w1_ref, b1_ref, ch.start, L)
            z_scr[ch, :] = z
            zb_scr[ch, :] = z.astype(BF16)
        alt = _alt_sign((L, tc))
        for sq in seqs:
            z_nyq = jnp.sum(z_scr[sq, :] * alt, axis=0, keepdims=True)
            y_scr[sq, :] = alt * (z_nyq * nq_ref[0:1, :])

    k0 = pl.multiple_of(kk * tk, tk)
    a = a_ref[...]
    b = b_ref[...]
    for sq in seqs:
        zb = zb_scr[sq, :]
        zr = jnp.dot(c_ref[pl.ds(k0, tk), :], zb, preferred_element_type=F32)
        zi = jnp.dot(s_ref[pl.ds(k0, tk), :], zb, preferred_element_type=F32)
        yr = (zr * a + zi * b).astype(BF16)
        yw = (zi * a - zr * b).astype(BF16)
        y_scr[sq, :] += (jnp.dot(c_ref[:, pl.ds(k0, tk)], yr, preferred_element_type=F32)
                         + jnp.dot(s_ref[:, pl.ds(k0, tk)], yw, preferred_element_type=F32))

    @pl.when(kk == pl.num_programs(2) - 1)
    def _():
        for ch in chunks:
            x0 = _conv_chunk(x0_ref, w0_ref, b0_ref, ch.start, L)
            y = (y_scr[ch, :] + z_scr[ch, :] * fb_ref[...]) * x0
            t_ref[ch, :] = (y * _silu(g_ref[ch, :])).astype(BF16)


def _hy_lconv(L, row_block0, n_row_blocks, proj, conv_w, conv_b, f_bias, c, s, a, b, nq):
    tk = min(TK, L)
    tc = LCONV_TC
    nc = HY_WIDTH // tc
    blk = lambda k: pl.BlockSpec((SEQ_BLOCK, tc), lambda r, j, kk: (r + row_block0, j + k * nc))
    wblk = lambda k: pl.BlockSpec((3, tc), lambda r, j, kk: (0, j + k * nc))
    bblk = lambda k: pl.BlockSpec((1, tc), lambda r, j, kk: (0, j + k * nc))
    coef = pl.BlockSpec((tk, tc), lambda r, j, kk: (kk, j))
    mat = _resident((L, L), lambda r, j, kk: (0, 0))
    return pl.pallas_call(
        functools.partial(_lconv_kernel, L=L, tk=tk),
        name=f"hy_lconv{L}",
        grid=(n_row_blocks, nc, L // tk),
        in_specs=[blk(0), blk(1), blk(2), blk(3), wblk(0), wblk(1), wblk(2), bblk(0), bblk(1), bblk(2),
                  pl.BlockSpec((1, tc), lambda r, j, kk: (0, j)),
                  mat, mat, coef, coef,
                  pl.BlockSpec((8, tc), lambda r, j, kk: (0, j))],
        out_specs=pl.BlockSpec((SEQ_BLOCK, tc), lambda r, j, kk: (r, j)),
        out_shape=jax.ShapeDtypeStruct((n_row_blocks * SEQ_BLOCK, HY_WIDTH), BF16),
        scratch_shapes=[pltpu.VMEM((SEQ_BLOCK, tc), F32), pltpu.VMEM((SEQ_BLOCK, tc), BF16),
                        pltpu.VMEM((SEQ_BLOCK, tc), F32)],
        compiler_params=_cparams(("arbitrary", "arbitrary", "arbitrary"), 56 << 20),
    )(proj, proj, proj, proj, conv_w, conv_w, conv_w, conv_b, conv_b, conv_b, f_bias, c, s, a, b, nq)


def _out_kernel(tlo_ref, thi_ref, w_ref, xlo_ref, xhi_ref, g_ref, o_ref):
    i = pl.program_id(0)
    u = jnp.dot(_pick_rows(i, tlo_ref, thi_ref), w_ref[...], preferred_element_type=F32)
    o_ref[...] = _pick_rows(i, xlo_ref, xhi_ref) + g_ref[pl.ds(_cond_row(i), 1), :] * u


def _out_proj(name, t, w_out, x, mod, layer):
    tlo, thi, _ = _row_pair(t)
    xlo, xhi, _ = _row_pair(x)
    return pl.pallas_call(
        _out_kernel,
        name=name,
        grid=(N_TILES,),
        in_specs=[*_row_pair_specs(t, w_out.shape[0]),
                  _resident(w_out.shape, lambda i: (0, 0)),
                  *_row_pair_specs(x, D_MODEL),
                  _mod_spec(layer, 2)],
        out_specs=pl.BlockSpec((TM, D_MODEL), lambda i: (i, 0)),
        out_shape=jax.ShapeDtypeStruct((ROWS, D_MODEL), F32),
        compiler_params=_cparams(("arbitrary",), 32 << 20),
    )(tlo, thi, w_out, xlo, xhi, mod)


@functools.lru_cache(maxsize=None)
def _rope_consts():
    axis_dim = QK_ROPE // 2
    nf = axis_dim // 2
    inv = ROPE_THETA ** (-np.arange(0, axis_dim, 2, dtype=np.float64) / axis_dim)
    t = np.arange(DEC_SEQ)
    ang_r = (t // GRID_W)[:, None] * inv
    ang_c = (t % GRID_W)[:, None] * inv
    cos = np.ones((DEC_SEQ, HEAD_PAD))
    sin_up = np.zeros((DEC_SEQ, HEAD_PAD))
    sin_dn = np.zeros((DEC_SEQ, HEAD_PAD))
    for base, ang in ((QK_NOPE, ang_r), (QK_NOPE + axis_dim, ang_c)):
        cos[:, base:base + nf] = np.cos(ang)
        cos[:, base + nf:base + 2 * nf] = np.cos(ang)
        sin_up[:, base:base + nf] = -np.sin(ang)
        sin_dn[:, base + nf:base + 2 * nf] = np.sin(ang)
    return cos.astype(np.float32), sin_up.astype(np.float32), sin_dn.astype(np.float32)


ROPE_HALF = QK_ROPE // 4
Q_SCALE = math.log2(math.e) / math.sqrt(QK_NOPE + QK_ROPE)


def _mla_proj_kernel(x_ref, nw_ref, sh_ref, sc_ref, win_ref, qn_ref, kvn_ref, wqb_ref, wk_ref, wv_ref,
                     cos_ref, sup_ref, sdn_ref,
                     q_ref, k_ref, v_ref, sg_ref, ckv_ref, kpe_ref):
    i = pl.program_id(0)
    h = _modnorm(i, x_ref[...], nw_ref, sh_ref, sc_ref).astype(BF16)
    proj = jnp.dot(h, win_ref[...], preferred_element_type=F32)
    o_kv = Q_LORA
    o_gate = Q_LORA + KV_LORA
    o_pe = o_gate + N_HEADS * V_HEAD
    qn = _rms(proj[:, 0:o_kv], qn_ref[...]).astype(BF16)
    ckv = _rms(proj[:, o_kv:o_gate], kvn_ref[...])
    gate = proj[:, o_gate:o_pe]
    kpe = proj[:, o_pe:o_pe + HEAD_PAD]
    ckv_ref[...] = ckv
    kpe_ref[...] = kpe
    sg_ref[...] = _silu(gate)
    ckv_b = ckv.astype(BF16)
    v_ref[...] = jnp.dot(ckv_b, wv_ref[...], preferred_element_type=F32).astype(BF16)
    q = jnp.dot(qn, wqb_ref[...], preferred_element_type=F32)
    kn = jnp.dot(ckv_b, wk_ref[...], preferred_element_type=F32)

    latent = i >= TILES_P
    cos = jnp.where(latent, cos_ref[...], 1.0)
    sup = jnp.where(latent, sup_ref[...], 0.0)
    sdn = jnp.where(latent, sdn_ref[...], 0.0)

    def rope(u, scale):
        return (u * (cos * scale) + pltpu.roll(u, HEAD_PAD - ROPE_HALF, axis=1) * (sup * scale)
                + pltpu.roll(u, ROPE_HALF, axis=1) * (sdn * scale))

    kpe_r = rope(kpe, 1.0)
    for hd in range(N_HEADS):
        cols = slice(hd * HEAD_PAD, (hd + 1) * HEAD_PAD)
        q_ref[:, cols] = rope(q[:, cols], Q_SCALE).astype(BF16)
        k_ref[:, cols] = (kn[:, cols] + kpe_r).astype(BF16)


def _mla_proj(x, norm_w, mod, layer, w):
    cos, sup, sdn = (jnp.asarray(c) for c in _rope_consts())
    n_in = w["w_in"].shape[1]
    hp = N_HEADS * HEAD_PAD
    rope_blk = pl.BlockSpec(
        (TM, HEAD_PAD), lambda i: (jnp.where(i >= TILES_P, (i - TILES_P) % TILES_PER_DEC_SEQ, 0), 0))
    tile = lambda n: pl.BlockSpec((TM, n), lambda i: (i, 0))
    const = lambda shape: _resident(shape, lambda i: (0, 0))
    return pl.pallas_call(
        _mla_proj_kernel,
        name="mla_proj",
        grid=(N_TILES,),
        in_specs=[
            tile(D_MODEL),
            pl.BlockSpec((1, D_MODEL), lambda i: (0, 0)),
            _mod_spec(layer, 0), _mod_spec(layer, 1),
            const((D_MODEL, n_in)), const((1, Q_LORA)), const((1, KV_LORA)),
            const((Q_LORA, hp)), const((KV_LORA, hp)), const((KV_LORA, N_HEADS * V_HEAD)),
            rope_blk, rope_blk, rope_blk,
        ],
        out_specs=[tile(hp), tile(hp), tile(N_HEADS * V_HEAD),
                   tile(N_HEADS * V_HEAD), tile(KV_LORA), tile(HEAD_PAD)],
        out_shape=[
            jax.ShapeDtypeStruct((ROWS, hp), BF16),
            jax.ShapeDtypeStruct((ROWS, hp), BF16),
            jax.ShapeDtypeStruct((ROWS, N_HEADS * V_HEAD), BF16),
            jax.ShapeDtypeStruct((ROWS, N_HEADS * V_HEAD), F32),
            jax.ShapeDtypeStruct((ROWS, KV_LORA), F32),
            jax.ShapeDtypeStruct((ROWS, HEAD_PAD), F32),
        ],
        compiler_params=_cparams(("arbitrary",), 56 << 20),
    )(x, norm_w, mod, mod, w["w_in"], w["q_norm"], w["kv_norm"], w["w_qb"], w["w_k"], w["w_v"],
      cos, sup, sdn)


def _mla_ctx_kernel(ckv_ref, kpe_ref, wk_ref, wv_ref, k_ref, v_ref):
    ckv_b = ckv_ref[...].astype(BF16)
    kn = jnp.dot(ckv_b, wk_ref[...], preferred_element_type=F32)
    v_ref[...] = jnp.dot(ckv_b, wv_ref[...], preferred_element_type=F32).astype(BF16)
    kpe = kpe_ref[...]
    for hd in range(N_HEADS):
        cols = slice(hd * HEAD_PAD, (hd + 1) * HEAD_PAD)
        k_ref[:, cols] = (kn[:, cols] + kpe).astype(BF16)


def _mla_ctx(ckv_ctx, kpe_ctx, w):
    hp = N_HEADS * HEAD_PAD
    rows = DEC_BATCH * PAST_LEN
    tile = lambda n: pl.BlockSpec((PAST_LEN, n), lambda i: (i, 0))
    const = lambda shape: _resident(shape, lambda i: (0, 0))
    return pl.pallas_call(
        _mla_ctx_kernel,
        name="mla_ctx",
        grid=(DEC_BATCH,),
        in_specs=[tile(KV_LORA), tile(HEAD_PAD), const((KV_LORA, hp)), const((KV_LORA, N_HEADS * V_HEAD))],
        out_specs=[tile(hp), tile(N_HEADS * V_HEAD)],
        out_shape=[jax.ShapeDtypeStruct((rows, hp), BF16),
                   jax.ShapeDtypeStruct((rows, N_HEADS * V_HEAD), BF16)],
        compiler_params=_cparams(("arbitrary",), 32 << 20),
    )(ckv_ctx, kpe_ctx, w["w_k"], w["w_v"])


NT_DIMS = (((1,), (1,)), ((), ()))
ATTN_PAIRS = 4
ATTN_TQ = 256


def _attn_kernel(*refs, n_pairs, has_ctx):
    if has_ctx:
        q_ref, k_ref, v_ref, kc_ref, vc_ref, sg_ref, o_ref = refs
    else:
        q_ref, k_ref, v_ref, sg_ref, o_ref = refs
    tq = q_ref.shape[0]
    low_half = lax.broadcasted_iota(jnp.int32, (tq, LANES), 1) < V_HEAD
    for p in range(n_pairs):
        vcols = slice(p * LANES, (p + 1) * LANES)
        vp = v_ref[:, vcols]
        outs = []
        for hh in range(2):
            cols = slice((2 * p + hh) * HEAD_PAD, (2 * p + hh + 1) * HEAD_PAD)
            q = q_ref[:, cols]
            s = lax.dot_general(q, k_ref[:, cols], NT_DIMS, preferred_element_type=F32)
            mx = jnp.max(s, axis=-1, keepdims=True)
            if has_ctx:
                sc = lax.dot_general(q, kc_ref[:, cols], NT_DIMS, preferred_element_type=F32)
                mx = jnp.maximum(mx, jnp.max(sc, axis=-1, keepdims=True))
            e = jnp.exp2(s - mx)
            den = jnp.sum(e, axis=-1, keepdims=True)
            pv = jnp.dot(e.astype(BF16), vp, preferred_element_type=F32)
            if has_ctx:
                ec = jnp.exp2(sc - mx)
                den = den + jnp.sum(ec, axis=-1, keepdims=True)
                pv = pv + jnp.dot(ec.astype(BF16), vc_ref[:, vcols], preferred_element_type=F32)
            outs.append(pv / den)
        o = jnp.where(low_half, outs[0], outs[1])
        o_ref[:, vcols] = (o * sg_ref[:, vcols]).astype(BF16)


def _attn_prompt(q, k, v, sg):
    hp = N_HEADS * HEAD_PAD
    nv = N_HEADS * V_HEAD
    return pl.pallas_call(
        functools.partial(_attn_kernel, n_pairs=N_HEADS // 2, has_ctx=False),
        name="attn_prompt",
        grid=(BATCH,),
        in_specs=[pl.BlockSpec((SEQ, hp), lambda b: (b, 0)),
                  pl.BlockSpec((SEQ, hp), lambda b: (b, 0)),
                  pl.BlockSpec((SEQ, nv), lambda b: (b, 0)),
                  pl.BlockSpec((SEQ, nv), lambda b: (b, 0))],
        out_specs=pl.BlockSpec((SEQ, nv), lambda b: (b, 0)),
        out_shape=jax.ShapeDtypeStruct((ROWS_P, nv), BF16),
        compiler_params=_cparams(("arbitrary",), 40 << 20),
    )(q, k, v, sg)


def _attn_latent(q, k, v, kc, vc, sg):
    tq = ATTN_TQ
    npair = ATTN_PAIRS
    nv = N_HEADS * V_HEAD
    q0 = ROWS_P // tq
    s0 = ROWS_P // DEC_SEQ
    wide = npair * 2 * HEAD_PAD
    narrow = npair * LANES
    qrow = lambda b, p, t: (q0 + b * (DEC_SEQ // tq) + t, p)
    return pl.pallas_call(
        functools.partial(_attn_kernel, n_pairs=npair, has_ctx=True),
        name="attn_latent",
        grid=(DEC_BATCH, N_HEADS // 2 // npair, DEC_SEQ // tq),
        in_specs=[pl.BlockSpec((tq, wide), qrow),
                  pl.BlockSpec((DEC_SEQ, wide), lambda b, p, t: (s0 + b, p)),
                  pl.BlockSpec((DEC_SEQ, narrow), lambda b, p, t: (s0 + b, p)),
                  pl.BlockSpec((PAST_LEN, wide), lambda b, p, t: (b, p)),
                  pl.BlockSpec((PAST_LEN, narrow), lambda b, p, t: (b, p)),
                  pl.BlockSpec((tq, narrow), qrow)],
        out_specs=pl.BlockSpec((tq, narrow), lambda b, p, t: (b * (DEC_SEQ // tq) + t, p)),
        out_shape=jax.ShapeDtypeStruct((ROWS_S, nv), BF16),
        compiler_params=_cparams(("arbitrary", "arbitrary", "arbitrary"), 48 << 20),
    )(q, k, v, kc, vc, sg)


def _final_kernel(x_ref, g_ref, o_ref):
    o_ref[...] = _rms(x_ref[...], g_ref[...])


def _final_norm(name, x, g, tile0, n_tiles):
    return pl.pallas_call(
        _final_kernel,
        name=name,
        grid=(n_tiles,),
        in_specs=[pl.BlockSpec((TM, D_MODEL), lambda i: (i + tile0, 0)),
                  pl.BlockSpec((1, D_MODEL), lambda i: (0, 0))],
        out_specs=pl.BlockSpec((TM, D_MODEL), lambda i: (i, 0)),
        out_shape=jax.ShapeDtypeStruct((n_tiles * TM, D_MODEL), F32),
        compiler_params=_cparams(("arbitrary",), 24 << 20),
    )(x, g)


def _mla_weights(w_in, q_norm, w_qb, kv_norm, w_kvb, w_o):
    o_pe = Q_LORA + KV_LORA
    o_gate = o_pe + QK_ROPE
    zeros = lambda n: jnp.zeros((D_MODEL, n), F32)
    w_in_r = jnp.concatenate(
        [w_in[:, :o_pe], w_in[:, o_gate:], zeros(QK_NOPE), w_in[:, o_pe:o_gate],
         zeros(HEAD_PAD - QK_NOPE - QK_ROPE)], axis=1)
    qb = w_qb.reshape(Q_LORA, N_HEADS, QK_NOPE + QK_ROPE)
    qb = jnp.pad(qb, ((0, 0), (0, 0), (0, HEAD_PAD - QK_NOPE - QK_ROPE)))
    kvb = w_kvb.reshape(KV_LORA, N_HEADS, QK_NOPE + V_HEAD)
    wk = jnp.pad(kvb[:, :, :QK_NOPE], ((0, 0), (0, 0), (0, HEAD_PAD - QK_NOPE)))
    wv = kvb[:, :, QK_NOPE:]
    return {
        "w_in": w_in_r.astype(BF16),
        "q_norm": q_norm[None, :],
        "kv_norm": kv_norm[None, :],
        "w_qb": qb.reshape(Q_LORA, N_HEADS * HEAD_PAD).astype(BF16),
        "w_k": wk.reshape(KV_LORA, N_HEADS * HEAD_PAD).astype(BF16),
        "w_v": wv.reshape(KV_LORA, N_HEADS * V_HEAD).astype(BF16),
        "w_o": w_o.astype(BF16),
    }


def kernel(x_prompt, x_sample, cache_ckv, cache_kpe, c, c_ctx, norm_w, ada_w, ada_b, hy_w_in, hy_conv_w, hy_conv_b, hy_f_w1, hy_f_b1, hy_f_freq, hy_f_w2, hy_f_b2, hy_f_w3, hy_f_bias, hy_w_out, mla_w_in, mla_q_norm, mla_w_qb, mla_kv_norm, mla_w_kvb, mla_w_o, final_norm):
    x = (x_prompt.reshape(ROWS_P, D_MODEL), x_sample.reshape(ROWS_S, D_MODEL))
    cond = jnp.concatenate([c_ctx[None, :], c, jnp.zeros((N_COND - 1 - DEC_BATCH, D_MODEL), F32)], axis=0)
    mod = _ada_all(cond, ada_w, ada_b)

    new_ckv, new_kpe = [], []
    for layer in range(DEPTH):
        j = layer // 2
        nw = norm_w[layer][None, :]
        if layer % 2 == 0:
            proj = _hy_in(x, nw, mod, layer, hy_w_in[j].astype(BF16))
            t = []
            for L, blk0, nblk in ((SEQ, 0, ROWS_P // SEQ_BLOCK), (DEC_SEQ, ROWS_P // SEQ_BLOCK, ROWS_S // SEQ_BLOCK)):
                filt = _hy_filter(L, hy_f_w1[j], hy_f_b1[j], hy_f_freq[j], hy_f_w2[j], hy_f_b2[j], hy_f_w3[j])
                cm, sm = _dft_mats(L)
                a, b, nq = _hy_spectrum(L, filt, cm, sm)
                t.append(_hy_lconv(L, blk0, nblk, proj, hy_conv_w[j], hy_conv_b[j][None, :],
                                   hy_f_bias[j][None, :], cm, sm, a, b, nq))
            x = _out_proj("hy_out", tuple(t), hy_w_out[j].astype(BF16), x, mod, layer)
        else:
            w = _mla_weights(mla_w_in[j], mla_q_norm[j], mla_w_qb[j], mla_kv_norm[j], mla_w_kvb[j], mla_w_o[j])
            q, k, v, sg, ckv, kpe = _mla_proj(x, nw, mod, layer, w)
            kpe_ctx = jnp.pad(cache_kpe[:, j].reshape(DEC_BATCH * PAST_LEN, QK_ROPE),
                              ((0, 0), (QK_NOPE, HEAD_PAD - QK_NOPE - QK_ROPE)))
            kc, vc = _mla_ctx(cache_ckv[:, j].reshape(DEC_BATCH * PAST_LEN, KV_LORA), kpe_ctx, w)
            o_p = _attn_prompt(q, k, v, sg)
            o_s = _attn_latent(q, k, v, kc, vc, sg)
            x = _out_proj("mla_out", (o_p, o_s), w["w_o"], x, mod, layer)
            new_ckv.append(ckv[:ROWS_P].reshape(BATCH, SEQ, KV_LORA))
            new_kpe.append(kpe[:ROWS_P, QK_NOPE:QK_NOPE + QK_ROPE].reshape(BATCH, SEQ, QK_ROPE))

    y_prompt = _final_norm("final_prompt", x, final_norm[None, :], 0, TILES_P)
    y_sample = _final_norm("final_latent", x, final_norm[None, :], TILES_P, N_TILES - TILES_P)
    return (y_prompt.reshape(BATCH, SEQ, D_MODEL), y_sample.reshape(DEC_BATCH, DEC_SEQ, D_MODEL),
            jnp.stack(new_ckv, axis=1), jnp.stack(new_kpe, axis=1))
```

```python
import functools
import math

import numpy as np
import jax
import jax.numpy as jnp
from jax import lax
from jax.experimental import pallas as pl
from jax.experimental.pallas import tpu as pltpu

F32 = jnp.float32
BF16 = jnp.bfloat16

D_MODEL = 1024
BATCH = 16
SEQ = 256
DEPTH = 4
DEC_BATCH = 2
DEC_SEQ = 2048
PAST_LEN = 512
GRID_W = 64
EPS = 1e-6
HY_WIDTH = D_MODEL
FILTER_BANDS = 16
FILTER_EMB = 1 + 2 * FILTER_BANDS
FILTER_HIDDEN = 64
FAST_DECAY_PCT = 0.3
SLOW_DECAY_PCT = 1.5
DECAY_TARGET = 1e-2
N_HEADS = 16
Q_LORA = 384
KV_LORA = 256
QK_NOPE = 64
QK_ROPE = 32
V_HEAD = 64
ROPE_THETA = 10000.0

LANES = 128
HEAD_PAD = LANES
ROWS_P = BATCH * SEQ
ROWS_S = DEC_BATCH * DEC_SEQ
ROWS = ROWS_P + ROWS_S
TM = 512
N_TILES = ROWS // TM
TILES_P = ROWS_P // TM
TILES_PER_DEC_SEQ = DEC_SEQ // TM
N_COND = 8
SEQ_BLOCK = 2048
TK = 512
LCONV_TC = 512
VMEM_CAP = 56 * 1024 * 1024


def _cparams(sem, vmem_bytes):
    return pltpu.CompilerParams(dimension_semantics=sem, vmem_limit_bytes=min(int(vmem_bytes), VMEM_CAP))


def _resident(shape, index_map):
    return pl.BlockSpec(shape, index_map, pipeline_mode=pl.Buffered(1))


def _cond_row(i):
    return jnp.where(i < TILES_P, 0, 1 + (i - TILES_P) // TILES_PER_DEC_SEQ)


def _silu(x):
    return x * jax.nn.sigmoid(x)


def _rms(x, g):
    return x * lax.rsqrt(jnp.mean(x * x, axis=-1, keepdims=True) + EPS) * g


def _row_pair(x):
    if isinstance(x, tuple):
        return x[0], x[1], 0
    return x, x, TILES_P


def _row_pair_specs(x, n):
    _, _, base = _row_pair(x)
    lo = pl.BlockSpec((TM, n), lambda i, *_: (jnp.minimum(i, TILES_P - 1), 0))
    hi = pl.BlockSpec((TM, n), lambda i, *_: (jnp.maximum(i - TILES_P, 0) + base, 0))
    return lo, hi


def _pick_rows(i, lo_ref, hi_ref):
    return jnp.where(i < TILES_P, lo_ref[...], hi_ref[...])


def _ada_kernel(cond_ref, w_ref, b_ref, o_ref):
    s = _silu(cond_ref[...]).astype(BF16)
    o_ref[...] = jnp.dot(s, w_ref[...].astype(BF16), preferred_element_type=F32) + b_ref[...]


def _ada_all(cond, ada_w, ada_b):
    tn = 1024
    return pl.pallas_call(
        _ada_kernel,
        name="ada",
        grid=(DEPTH, 3 * D_MODEL // tn),
        in_specs=[
            pl.BlockSpec((N_COND, D_MODEL), lambda l, j: (0, 0)),
            pl.BlockSpec((None, D_MODEL, tn), lambda l, j: (l, 0, j)),
            pl.BlockSpec((None, 1, tn), lambda l, j: (l, 0, j)),
        ],
        out_specs=pl.BlockSpec((None, N_COND, tn), lambda l, j: (l, 0, j)),
        out_shape=jax.ShapeDtypeStruct((DEPTH, N_COND, 3 * D_MODEL), F32),
        compiler_params=_cparams(("arbitrary", "arbitrary"), 32 << 20),
    )(cond, ada_w, ada_b.reshape(DEPTH, 1, 3 * D_MODEL))


def _mod_spec(layer, part):
    return pl.BlockSpec((None, N_COND, D_MODEL), lambda i, *_: (layer, 0, part))


def _modnorm(i, x, nw_ref, sh_ref, sc_ref):
    c = _cond_row(i)
    return _rms(x, nw_ref[...]) * (1.0 + sc_ref[pl.ds(c, 1), :]) + sh_ref[pl.ds(c, 1), :]


HALO = 16
HALO_PER_TILE = TM // HALO


def _halo_specs(x):
    _, _, base = _row_pair(x)
    r = HALO_PER_TILE
    lo_last = TILES_P * r - 1
    hi_last = (N_TILES - TILES_P) * r - 1
    lo_tile = lambda i: jnp.minimum(i, TILES_P - 1)
    hi_tile = lambda i: jnp.maximum(i - TILES_P, 0)
    blk = lambda f: pl.BlockSpec((HALO, D_MODEL), lambda i: (f(i), 0))
    return [
        blk(lambda i: jnp.maximum(lo_tile(i) * r - 1, 0)),
        blk(lambda i: jnp.maximum(hi_tile(i) * r - 1, 0) + base * r),
        blk(lambda i: jnp.minimum((lo_tile(i) + 1) * r, lo_last)),
        blk(lambda i: jnp.minimum((hi_tile(i) + 1) * r, hi_last) + base * r),
    ]


def _hy_in_kernel(xlo_ref, xhi_ref, plo_ref, phi_ref, nlo_ref, nhi_ref, nw_ref, sh_ref, sc_ref, w_ref,
                  cw_ref, cb_ref, z_ref, m_ref, h_scr, u0_scr, u1_scr, u2_scr):
    i = pl.program_id(0)
    width = z_ref.shape[1]
    norm = lambda lo_ref, hi_ref: _modnorm(i, _pick_rows(i, lo_ref, hi_ref), nw_ref, sh_ref, sc_ref).astype(BF16)
    h_scr[0:HALO, :] = norm(plo_ref, phi_ref)
    h_scr[HALO:HALO + TM, :] = norm(xlo_ref, xhi_ref)
    h_scr[HALO + TM:HALO + TM + HALO, :] = norm(nlo_ref, nhi_ref)

    seq_mask = jnp.where(i < TILES_P, SEQ - 1, DEC_SEQ - 1)
    pos = (lax.broadcasted_iota(jnp.int32, (TM, width), 0) + i * TM) & seq_mask
    first = pos == 0
    last = pos == seq_mask

    def conv(g, u_scr):
        cols = slice(g * width, (g + 1) * width)
        u_scr[...] = jnp.dot(h_scr[...], w_ref[:, cols], preferred_element_type=F32)
        prev = jnp.where(first, 0.0, u_scr[HALO - 1:HALO - 1 + TM, :])
        nxt = jnp.where(last, 0.0, u_scr[HALO + 1:HALO + 1 + TM, :])
        return (cb_ref[:, cols] + prev * cw_ref[0:1, cols] + u_scr[HALO:HALO + TM, :] * cw_ref[1:2, cols]
                + nxt * cw_ref[2:3, cols])

    z_ref[...] = (conv(2, u2_scr) * conv(1, u1_scr)).astype(BF16)
    gate = jnp.dot(h_scr[HALO:HALO + TM, :], w_ref[:, 3 * width:4 * width], preferred_element_type=F32)
    m_ref[...] = (conv(0, u0_scr) * _silu(gate)).astype(BF16)


def _hy_in(x, norm_w, mod, layer, w_in, conv_w, conv_b):
    xlo, xhi, _ = _row_pair(x)
    rows_h = TM + 2 * HALO
    out = pl.BlockSpec((TM, HY_WIDTH), lambda i: (i, 0))
    return pl.pallas_call(
        _hy_in_kernel,
        name="hy_in",
        grid=(N_TILES,),
        in_specs=[
            *_row_pair_specs(x, D_MODEL),
            *_halo_specs(x),
            pl.BlockSpec((1, D_MODEL), lambda i: (0, 0)),
            _mod_spec(layer, 0),
            _mod_spec(layer, 1),
            _resident(w_in.shape, lambda i: (0, 0)),
            pl.BlockSpec(conv_w.shape, lambda i: (0, 0)),
            pl.BlockSpec(conv_b.shape, lambda i: (0, 0)),
        ],
        out_specs=[out, out],
        out_shape=[jax.ShapeDtypeStruct((ROWS, HY_WIDTH), BF16)] * 2,
        scratch_shapes=[pltpu.VMEM((rows_h, D_MODEL), BF16)] + [pltpu.VMEM((rows_h, HY_WIDTH), F32)] * 3,
        compiler_params=_cparams(("arbitrary",), 48 << 20),
    )(xlo, xhi, xlo, xhi, xlo, xhi, norm_w, mod, mod, w_in, conv_w, conv_b)


@functools.lru_cache(maxsize=None)
def _filter_consts(L):
    t = np.linspace(0.0, 1.0, L)[:, None]
    w = (2.0 * math.pi / L) * np.arange(L)[:, None]
    bands = np.linspace(1e-4, FILTER_BANDS - 1, FILTER_BANDS)[None, :]
    emb = np.concatenate([t, np.cos(bands * w), -np.sin(bands * w)], axis=-1)
    emb = np.pad(emb, ((0, 0), (0, FILTER_HIDDEN - FILTER_EMB)))
    max_decay = math.log(DECAY_TARGET) / FAST_DECAY_PCT
    min_decay = math.log(DECAY_TARGET) / SLOW_DECAY_PCT
    deltas = np.abs(np.linspace(min_decay, max_decay, HY_WIDTH))
    deltas = np.concatenate([deltas, deltas])[None, :]
    return emb.astype(np.float32), t.astype(np.float32), deltas.astype(np.float32)


def _filter_kernel(emb_ref, t_ref, w1_ref, b1_ref, fr_ref, w2_ref, b2_ref, w3_ref, dl_ref, o_ref, h_scr):
    @pl.when(pl.program_id(0) == 0)
    def _():
        hi = lax.Precision.HIGHEST
        fr = fr_ref[...]
        h = jnp.sin(fr * (jnp.dot(emb_ref[...], w1_ref[...], precision=hi, preferred_element_type=F32) + b1_ref[...]))
        h = jnp.sin(fr * (jnp.dot(h, w2_ref[...], precision=hi, preferred_element_type=F32) + b2_ref[...]))
        h_scr[...] = h.astype(BF16)

    h = jnp.dot(h_scr[...], w3_ref[...].astype(BF16), preferred_element_type=F32)
    h = h * jnp.exp(-t_ref[...] * dl_ref[...])
    o_ref[...] = h / jnp.sum(jnp.abs(h), axis=0, keepdims=True)


def _hy_filter(L, w1, b1, freq, w2, b2, w3):
    emb, t, deltas = _filter_consts(L)
    tcf = 512
    w1p = jnp.pad(w1, ((0, FILTER_HIDDEN - FILTER_EMB), (0, 0)))
    full = lambda shape: pl.BlockSpec(shape, lambda j: (0, 0))
    return pl.pallas_call(
        _filter_kernel,
        name=f"hy_filter{L}",
        grid=(2 * HY_WIDTH // tcf,),
        in_specs=[
            full((L, FILTER_HIDDEN)), full((L, 1)), full((FILTER_HIDDEN, FILTER_HIDDEN)),
            full((1, FILTER_HIDDEN)), full((1, FILTER_HIDDEN)), full((FILTER_HIDDEN, FILTER_HIDDEN)),
            full((1, FILTER_HIDDEN)),
            pl.BlockSpec((FILTER_HIDDEN, tcf), lambda j: (0, j)),
            pl.BlockSpec((1, tcf), lambda j: (0, j)),
        ],
        out_specs=pl.BlockSpec((L, tcf), lambda j: (0, j)),
        out_shape=jax.ShapeDtypeStruct((L, 2 * HY_WIDTH), F32),
        scratch_shapes=[pltpu.VMEM((L, FILTER_HIDDEN), BF16)],
        compiler_params=_cparams(("arbitrary",), 40 << 20),
    )(jnp.asarray(emb), jnp.asarray(t), w1p, b1[None, :], freq[None, :], w2, b2[None, :], w3,
      jnp.asarray(deltas))


@functools.lru_cache(maxsize=None)
def _dft_consts(L):
    k = np.arange(L, dtype=np.int64)
    ang = (np.outer(k, k) % (2 * L)).astype(np.float64) * (math.pi / L)
    return np.cos(ang).astype(np.float32), np.sin(ang).astype(np.float32)


def _dft_mats(L):
    c, s = _dft_consts(L)
    return jnp.asarray(c).astype(BF16), jnp.asarray(s).astype(BF16)


def _alt_sign(shape):
    return (1 - 2 * (lax.broadcasted_iota(jnp.int32, shape, 0) & 1)).astype(F32)


def _spec_kernel(ff_ref, fb_ref, bias_ref, c_ref, s_ref, a_ref, b_ref, nq_ref, f_scr, d_scr, *, L, tk):
    kk = pl.program_id(1)
    inv_n = 1.0 / (2.0 * L)
    bias = bias_ref[...]

    @pl.when(kk == 0)
    def _():
        ff = ff_ref[...]
        fb = fb_ref[...]
        f = ff + fb
        f_scr[...] = f.astype(BF16)
        d_scr[...] = (fb - ff).astype(BF16)
        nyq = (jnp.sum(f * _alt_sign(f.shape), axis=0, keepdims=True) + bias) * inv_n
        nq_ref[...] = jnp.broadcast_to(nyq, nq_ref.shape)

    k0 = pl.multiple_of(kk * tk, tk)
    hre = jnp.dot(c_ref[pl.ds(k0, tk), :], f_scr[...], preferred_element_type=F32) + bias
    him = jnp.dot(s_ref[pl.ds(k0, tk), :], d_scr[...], preferred_element_type=F32)
    dc_row = (lax.broadcasted_iota(jnp.int32, hre.shape, 0) == 0) & (kk == 0)
    a_ref[...] = jnp.where(dc_row, hre * inv_n, hre * (2.0 * inv_n))
    b_ref[...] = him * (2.0 * inv_n)


def _hy_spectrum(L, filt, f_bias, c, s):
    tk = min(TK, L)
    tc = LCONV_TC
    nc = HY_WIDTH // tc
    out = pl.BlockSpec((tk, tc), lambda j, kk: (kk, j))
    mat = _resident((L, L), lambda j, kk: (0, 0))
    return pl.pallas_call(
        functools.partial(_spec_kernel, L=L, tk=tk),
        name=f"hy_spectrum{L}",
        grid=(nc, L // tk),
        in_specs=[
            pl.BlockSpec((L, tc), lambda j, kk: (0, j)),
            pl.BlockSpec((L, tc), lambda j, kk: (0, j + nc)),
            pl.BlockSpec((1, tc), lambda j, kk: (0, j)),
            mat, mat,
        ],
        out_specs=[out, out, pl.BlockSpec((8, tc), lambda j, kk: (0, j))],
        out_shape=[jax.ShapeDtypeStruct((L, HY_WIDTH), F32)] * 2 + [jax.ShapeDtypeStruct((8, HY_WIDTH), F32)],
        scratch_shapes=[pltpu.VMEM((L, tc), BF16), pltpu.VMEM((L, tc), BF16)],
        compiler_params=_cparams(("arbitrary", "arbitrary"), 48 << 20),
    )(filt, filt, f_bias, c, s)


def _lconv_kernel(z_ref, m_ref, c_ref, s_ref, a_ref, b_ref, nq_ref, t_ref, y_scr, *, L, tk):
    kk = pl.program_id(2)
    rows, tc = y_scr.shape
    seqs = [pl.ds(s0, L) for s0 in range(0, rows, L)]

    @pl.when(kk == 0)
    def _():
        alt = _alt_sign((L, tc))
        for sq in seqs:
            z_nyq = jnp.sum(z_ref[sq, :].astype(F32) * alt, axis=0, keepdims=True)
            y_scr[sq, :] = alt * (z_nyq * nq_ref[0:1, :])

    k0 = pl.multiple_of(kk * tk, tk)
    a = a_ref[...]
    b = b_ref[...]
    for sq in seqs:
        zb = z_ref[sq, :]
        zr = jnp.dot(c_ref[pl.ds(k0, tk), :], zb, preferred_element_type=F32)
        zi = jnp.dot(s_ref[pl.ds(k0, tk), :], zb, preferred_element_type=F32)
        yr = (zr * a + zi * b).astype(BF16)
        yw = (zi * a - zr * b).astype(BF16)
        y_scr[sq, :] += (jnp.dot(c_ref[:, pl.ds(k0, tk)], yr, preferred_element_type=F32)
                         + jnp.dot(s_ref[:, pl.ds(k0, tk)], yw, preferred_element_type=F32))

    @pl.when(kk == pl.num_programs(2) - 1)
    def _():
        t_ref[...] = (y_scr[...] * m_ref[...].astype(F32)).astype(BF16)


def _hy_lconv(L, row_block0, n_row_blocks, z, m, c, s, a, b, nq):
    tk = min(TK, L)
    tc = LCONV_TC
    nc = HY_WIDTH // tc
    blk = pl.BlockSpec((SEQ_BLOCK, tc), lambda r, j, kk: (r + row_block0, j))
    coef = pl.BlockSpec((tk, tc), lambda r, j, kk: (kk, j))
    mat = _resident((L, L), lambda r, j, kk: (0, 0))
    return pl.pallas_call(
        functools.partial(_lconv_kernel, L=L, tk=tk),
        name=f"hy_lconv{L}",
        grid=(n_row_blocks, nc, L // tk),
        in_specs=[blk, blk, mat, mat, coef, coef, pl.BlockSpec((8, tc), lambda r, j, kk: (0, j))],
        out_specs=pl.BlockSpec((SEQ_BLOCK, tc), lambda r, j, kk: (r, j)),
        out_shape=jax.ShapeDtypeStruct((n_row_blocks * SEQ_BLOCK, HY_WIDTH), BF16),
        scratch_shapes=[pltpu.VMEM((SEQ_BLOCK, tc), F32)],
        compiler_params=_cparams(("arbitrary", "arbitrary", "arbitrary"), 56 << 20),
    )(z, m, c, s, a, b, nq)


def _out_kernel(tlo_ref, thi_ref, w_ref, xlo_ref, xhi_ref, g_ref, o_ref):
    i = pl.program_id(0)
    u = jnp.dot(_pick_rows(i, tlo_ref, thi_ref), w_ref[...], preferred_element_type=F32)
    o_ref[...] = _pick_rows(i, xlo_ref, xhi_ref) + g_ref[pl.ds(_cond_row(i), 1), :] * u


def _out_proj(name, t, w_out, x, mod, layer):
    tlo, thi, _ = _row_pair(t)
    xlo, xhi, _ = _row_pair(x)
    return pl.pallas_call(
        _out_kernel,
        name=name,
        grid=(N_TILES,),
        in_specs=[*_row_pair_specs(t, w_out.shape[0]),
                  _resident(w_out.shape, lambda i: (0, 0)),
                  *_row_pair_specs(x, D_MODEL),
                  _mod_spec(layer, 2)],
        out_specs=pl.BlockSpec((TM, D_MODEL), lambda i: (i, 0)),
        out_shape=jax.ShapeDtypeStruct((ROWS, D_MODEL), F32),
        compiler_params=_cparams(("arbitrary",), 32 << 20),
    )(tlo, thi, w_out, xlo, xhi, mod)


@functools.lru_cache(maxsize=None)
def _rope_consts():
    axis_dim = QK_ROPE // 2
    nf = axis_dim // 2
    inv = ROPE_THETA ** (-np.arange(0, axis_dim, 2, dtype=np.float64) / axis_dim)
    t = np.arange(DEC_SEQ)
    ang_r = (t // GRID_W)[:, None] * inv
    ang_c = (t % GRID_W)[:, None] * inv
    cos = np.ones((DEC_SEQ, HEAD_PAD))
    sin_up = np.zeros((DEC_SEQ, HEAD_PAD))
    sin_dn = np.zeros((DEC_SEQ, HEAD_PAD))
    for base, ang in ((QK_NOPE, ang_r), (QK_NOPE + axis_dim, ang_c)):
        cos[:, base:base + nf] = np.cos(ang)
        cos[:, base + nf:base + 2 * nf] = np.cos(ang)
        sin_up[:, base:base + nf] = -np.sin(ang)
        sin_dn[:, base + nf:base + 2 * nf] = np.sin(ang)
    return cos.astype(np.float32), sin_up.astype(np.float32), sin_dn.astype(np.float32)


ROPE_HALF = QK_ROPE // 4
Q_SCALE = math.log2(math.e) / math.sqrt(QK_NOPE + QK_ROPE)


def _mla_proj_kernel(x_ref, nw_ref, sh_ref, sc_ref, win_ref, qn_ref, kvn_ref, wqb_ref, wk_ref, wv_ref,
                     cos_ref, sup_ref, sdn_ref,
                     q_ref, k_ref, v_ref, sg_ref, ckv_ref, kpe_ref):
    i = pl.program_id(0)
    h = _modnorm(i, x_ref[...], nw_ref, sh_ref, sc_ref).astype(BF16)
    proj = jnp.dot(h, win_ref[...], preferred_element_type=F32)
    o_kv = Q_LORA
    o_gate = Q_LORA + KV_LORA
    o_pe = o_gate + N_HEADS * V_HEAD
    qn = _rms(proj[:, 0:o_kv], qn_ref[...]).astype(BF16)
    ckv = _rms(proj[:, o_kv:o_gate], kvn_ref[...])
    gate = proj[:, o_gate:o_pe]
    kpe = proj[:, o_pe:o_pe + HEAD_PAD]
    ckv_ref[...] = ckv
    kpe_ref[...] = kpe
    sg_ref[...] = _silu(gate)
    ckv_b = ckv.astype(BF16)
    v_ref[...] = jnp.dot(ckv_b, wv_ref[...], preferred_element_type=F32).astype(BF16)
    q = jnp.dot(qn, wqb_ref[...], preferred_element_type=F32)
    kn = jnp.dot(ckv_b, wk_ref[...], preferred_element_type=F32)

    latent = i >= TILES_P
    cos = jnp.where(latent, cos_ref[...], 1.0)
    sup = jnp.where(latent, sup_ref[...], 0.0)
    sdn = jnp.where(latent, sdn_ref[...], 0.0)

    def rope(u, scale):
        return (u * (cos * scale) + pltpu.roll(u, HEAD_PAD - ROPE_HALF, axis=1) * (sup * scale)
                + pltpu.roll(u, ROPE_HALF, axis=1) * (sdn * scale))

    kpe_r = rope(kpe, 1.0)
    for hd in range(N_HEADS):
        cols = slice(hd * HEAD_PAD, (hd + 1) * HEAD_PAD)
        q_ref[:, cols] = rope(q[:, cols], Q_SCALE).astype(BF16)
        k_ref[:, cols] = (kn[:, cols] + kpe_r).astype(BF16)


def _mla_proj(x, norm_w, mod, layer, w):
    cos, sup, sdn = (jnp.asarray(c) for c in _rope_consts())
    n_in = w["w_in"].shape[1]
    hp = N_HEADS * HEAD_PAD
    rope_blk = pl.BlockSpec(
        (TM, HEAD_PAD), lambda i: (jnp.where(i >= TILES_P, (i - TILES_P) % TILES_PER_DEC_SEQ, 0), 0))
    tile = lambda n: pl.BlockSpec((TM, n), lambda i: (i, 0))
    const = lambda shape: _resident(shape, lambda i: (0, 0))
    return pl.pallas_call(
        _mla_proj_kernel,
        name="mla_proj",
        grid=(N_TILES,),
        in_specs=[
            tile(D_MODEL),
            pl.BlockSpec((1, D_MODEL), lambda i: (0, 0)),
            _mod_spec(layer, 0), _mod_spec(layer, 1),
            const((D_MODEL, n_in)), const((1, Q_LORA)), const((1, KV_LORA)),
            const((Q_LORA, hp)), const((KV_LORA, hp)), const((KV_LORA, N_HEADS * V_HEAD)),
            rope_blk, rope_blk, rope_blk,
        ],
        out_specs=[tile(hp), tile(hp), tile(N_HEADS * V_HEAD),
                   tile(N_HEADS * V_HEAD), tile(KV_LORA), tile(HEAD_PAD)],
        out_shape=[
            jax.ShapeDtypeStruct((ROWS, hp), BF16),
            jax.ShapeDtypeStruct((ROWS, hp), BF16),
            jax.ShapeDtypeStruct((ROWS, N_HEADS * V_HEAD), BF16),
            jax.ShapeDtypeStruct((ROWS, N_HEADS * V_HEAD), F32),
            jax.ShapeDtypeStruct((ROWS, KV_LORA), F32),
            jax.ShapeDtypeStruct((ROWS, HEAD_PAD), F32),
        ],
        compiler_params=_cparams(("arbitrary",), 56 << 20),
    )(x, norm_w, mod, mod, w["w_in"], w["q_norm"], w["kv_norm"], w["w_qb"], w["w_k"], w["w_v"],
      cos, sup, sdn)


def _mla_ctx_kernel(ckv_ref, kpe_ref, wk_ref, wv_ref, k_ref, v_ref):
    ckv_b = ckv_ref[...].astype(BF16)
    kn = jnp.dot(ckv_b, wk_ref[...], preferred_element_type=F32)
    v_ref[...] = jnp.dot(ckv_b, wv_ref[...], preferred_element_type=F32).astype(BF16)
    kpe = kpe_ref[...]
    for hd in range(N_HEADS):
        cols = slice(hd * HEAD_PAD, (hd + 1) * HEAD_PAD)
        k_ref[:, cols] = (kn[:, cols] + kpe).astype(BF16)


def _mla_ctx(ckv_ctx, kpe_ctx, w):
    hp = N_HEADS * HEAD_PAD
    rows = DEC_BATCH * PAST_LEN
    tile = lambda n: pl.BlockSpec((PAST_LEN, n), lambda i: (i, 0))
    const = lambda shape: _resident(shape, lambda i: (0, 0))
    return pl.pallas_call(
        _mla_ctx_kernel,
        name="mla_ctx",
        grid=(DEC_BATCH,),
        in_specs=[tile(KV_LORA), tile(HEAD_PAD), const((KV_LORA, hp)), const((KV_LORA, N_HEADS * V_HEAD))],
        out_specs=[tile(hp), tile(N_HEADS * V_HEAD)],
        out_shape=[jax.ShapeDtypeStruct((rows, hp), BF16),
                   jax.ShapeDtypeStruct((rows, N_HEADS * V_HEAD), BF16)],
        compiler_params=_cparams(("arbitrary",), 32 << 20),
    )(ckv_ctx, kpe_ctx, w["w_k"], w["w_v"])


NT_DIMS = (((1,), (1,)), ((), ()))
ATTN_PAIRS = 4
ATTN_TQ = 256


def _attn_kernel(*refs, n_pairs, has_ctx):
    if has_ctx:
        q_ref, k_ref, v_ref, kc_ref, vc_ref, sg_ref, o_ref = refs
    else:
        q_ref, k_ref, v_ref, sg_ref, o_ref = refs
    tq = q_ref.shape[0]
    low_half = lax.broadcasted_iota(jnp.int32, (tq, LANES), 1) < V_HEAD
    for p in range(n_pairs):
        vcols = slice(p * LANES, (p + 1) * LANES)
        vp = v_ref[:, vcols]
        outs = []
        for hh in range(2):
            cols = slice((2 * p + hh) * HEAD_PAD, (2 * p + hh + 1) * HEAD_PAD)
            q = q_ref[:, cols]
            s = lax.dot_general(q, k_ref[:, cols], NT_DIMS, preferred_element_type=F32)
            mx = jnp.max(s, axis=-1, keepdims=True)
            if has_ctx:
                sc = lax.dot_general(q, kc_ref[:, cols], NT_DIMS, preferred_element_type=F32)
                mx = jnp.maximum(mx, jnp.max(sc, axis=-1, keepdims=True))
            e = jnp.exp2(s - mx)
            den = jnp.sum(e, axis=-1, keepdims=True)
            pv = jnp.dot(e.astype(BF16), vp, preferred_element_type=F32)
            if has_ctx:
                ec = jnp.exp2(sc - mx)
                den = den + jnp.sum(ec, axis=-1, keepdims=True)
                pv = pv + jnp.dot(ec.astype(BF16), vc_ref[:, vcols], preferred_element_type=F32)
            outs.append(pv / den)
        o = jnp.where(low_half, outs[0], outs[1])
        o_ref[:, vcols] = (o * sg_ref[:, vcols]).astype(BF16)


def _attn_prompt(q, k, v, sg):
    hp = N_HEADS * HEAD_PAD
    nv = N_HEADS * V_HEAD
    return pl.pallas_call(
        functools.partial(_attn_kernel, n_pairs=N_HEADS // 2, has_ctx=False),
        name="attn_prompt",
        grid=(BATCH,),
        in_specs=[pl.BlockSpec((SEQ, hp), lambda b: (b, 0)),
                  pl.BlockSpec((SEQ, hp), lambda b: (b, 0)),
                  pl.BlockSpec((SEQ, nv), lambda b: (b, 0)),
                  pl.BlockSpec((SEQ, nv), lambda b: (b, 0))],
        out_specs=pl.BlockSpec((SEQ, nv), lambda b: (b, 0)),
        out_shape=jax.ShapeDtypeStruct((ROWS_P, nv), BF16),
        compiler_params=_cparams(("arbitrary",), 40 << 20),
    )(q, k, v, sg)


def _attn_latent(q, k, v, kc, vc, sg):
    tq = ATTN_TQ
    npair = ATTN_PAIRS
    nv = N_HEADS * V_HEAD
    q0 = ROWS_P // tq
    s0 = ROWS_P // DEC_SEQ
    wide = npair * 2 * HEAD_PAD
    narrow = npair * LANES
    qrow = lambda b, p, t: (q0 + b * (DEC_SEQ // tq) + t, p)
    return pl.pallas_call(
        functools.partial(_attn_kernel, n_pairs=npair, has_ctx=True),
        name="attn_latent",
        grid=(DEC_BATCH, N_HEADS // 2 // npair, DEC_SEQ // tq),
        in_specs=[pl.BlockSpec((tq, wide), qrow),
                  pl.BlockSpec((DEC_SEQ, wide), lambda b, p, t: (s0 + b, p)),
                  pl.BlockSpec((DEC_SEQ, narrow), lambda b, p, t: (s0 + b, p)),
                  pl.BlockSpec((PAST_LEN, wide), lambda b, p, t: (b, p)),
                  pl.BlockSpec((PAST_LEN, narrow), lambda b, p, t: (b, p)),
                  pl.BlockSpec((tq, narrow), qrow)],
        out_specs=pl.BlockSpec((tq, narrow), lambda b, p, t: (b * (DEC_SEQ // tq) + t, p)),
        out_shape=jax.ShapeDtypeStruct((ROWS_S, nv), BF16),
        compiler_params=_cparams(("arbitrary", "arbitrary", "arbitrary"), 48 << 20),
    )(q, k, v, kc, vc, sg)


def _final_kernel(x_ref, g_ref, o_ref):
    o_ref[...] = _rms(x_ref[...], g_ref[...])


def _final_norm(name, x, g, tile0, n_tiles):
    return pl.pallas_call(
        _final_kernel,
        name=name,
        grid=(n_tiles,),
        in_specs=[pl.BlockSpec((TM, D_MODEL), lambda i: (i + tile0, 0)),
                  pl.BlockSpec((1, D_MODEL), lambda i: (0, 0))],
        out_specs=pl.BlockSpec((TM, D_MODEL), lambda i: (i, 0)),
        out_shape=jax.ShapeDtypeStruct((n_tiles * TM, D_MODEL), F32),
        compiler_params=_cparams(("arbitrary",), 24 << 20),
    )(x, g)


def _mla_weights(w_in, q_norm, w_qb, kv_norm, w_kvb, w_o):
    o_pe = Q_LORA + KV_LORA
    o_gate = o_pe + QK_ROPE
    zeros = lambda n: jnp.zeros((D_MODEL, n), F32)
    w_in_r = jnp.concatenate(
        [w_in[:, :o_pe], w_in[:, o_gate:], zeros(QK_NOPE), w_in[:, o_pe:o_gate],
         zeros(HEAD_PAD - QK_NOPE - QK_ROPE)], axis=1)
    qb = w_qb.reshape(Q_LORA, N_HEADS, QK_NOPE + QK_ROPE)
    qb = jnp.pad(qb, ((0, 0), (0, 0), (0, HEAD_PAD - QK_NOPE - QK_ROPE)))
    kvb = w_kvb.reshape(KV_LORA, N_HEADS, QK_NOPE + V_HEAD)
    wk = jnp.pad(kvb[:, :, :QK_NOPE], ((0, 0), (0, 0), (0, HEAD_PAD - QK_NOPE)))
    wv = kvb[:, :, QK_NOPE:]
    return {
        "w_in": w_in_r.astype(BF16),
        "q_norm": q_norm[None, :],
        "kv_norm": kv_norm[None, :],
        "w_qb": qb.reshape(Q_LORA, N_HEADS * HEAD_PAD).astype(BF16),
        "w_k": wk.reshape(KV_LORA, N_HEADS * HEAD_PAD).astype(BF16),
        "w_v": wv.reshape(KV_LORA, N_HEADS * V_HEAD).astype(BF16),
        "w_o": w_o.astype(BF16),
    }


def kernel(x_prompt, x_sample, cache_ckv, cache_kpe, c, c_ctx, norm_w, ada_w, ada_b, hy_w_in, hy_conv_w, hy_conv_b, hy_f_w1, hy_f_b1, hy_f_freq, hy_f_w2, hy_f_b2, hy_f_w3, hy_f_bias, hy_w_out, mla_w_in, mla_q_norm, mla_w_qb, mla_kv_norm, mla_w_kvb, mla_w_o, final_norm):
    x = (x_prompt.reshape(ROWS_P, D_MODEL), x_sample.reshape(ROWS_S, D_MODEL))
    cond = jnp.concatenate([c_ctx[None, :], c, jnp.zeros((N_COND - 1 - DEC_BATCH, D_MODEL), F32)], axis=0)
    mod = _ada_all(cond, ada_w, ada_b)

    new_ckv, new_kpe = [], []
    for layer in range(DEPTH):
        j = layer // 2
        nw = norm_w[layer][None, :]
        if layer % 2 == 0:
            z, m = _hy_in(x, nw, mod, layer, hy_w_in[j].astype(BF16), hy_conv_w[j], hy_conv_b[j][None, :])
            t = []
            for L, blk0, nblk in ((SEQ, 0, ROWS_P // SEQ_BLOCK), (DEC_SEQ, ROWS_P // SEQ_BLOCK, ROWS_S // SEQ_BLOCK)):
                filt = _hy_filter(L, hy_f_w1[j], hy_f_b1[j], hy_f_freq[j], hy_f_w2[j], hy_f_b2[j], hy_f_w3[j])
                cm, sm = _dft_mats(L)
                a, b, nq = _hy_spectrum(L, filt, hy_f_bias[j][None, :], cm, sm)
                t.append(_hy_lconv(L, blk0, nblk, z, m, cm, sm, a, b, nq))
            x = _out_proj("hy_out", tuple(t), hy_w_out[j].astype(BF16), x, mod, layer)
        else:
            w = _mla_weights(mla_w_in[j], mla_q_norm[j], mla_w_qb[j], mla_kv_norm[j], mla_w_kvb[j], mla_w_o[j])
            q, k, v, sg, ckv, kpe = _mla_proj(x, nw, mod, layer, w)
            kpe_ctx = jnp.pad(cache_kpe[:, j].reshape(DEC_BATCH * PAST_LEN, QK_ROPE),
                              ((0, 0), (QK_NOPE, HEAD_PAD - QK_NOPE - QK_ROPE)))
            kc, vc = _mla_ctx(cache_ckv[:, j].reshape(DEC_BATCH * PAST_LEN, KV_LORA), kpe_ctx, w)
            o_p = _attn_prompt(q, k, v, sg)
            o_s = _attn_latent(q, k, v, kc, vc, sg)
            x = _out_proj("mla_out", (o_p, o_s), w["w_o"], x, mod, layer)
            new_ckv.append(ckv[:ROWS_P].reshape(BATCH, SEQ, KV_LORA))
            new_kpe.append(kpe[:ROWS_P, QK_NOPE:QK_NOPE + QK_ROPE].reshape(BATCH, SEQ, QK_ROPE))

    y_prompt = _final_norm("final_prompt", x, final_norm[None, :], 0, TILES_P)
    y_sample = _final_norm("final_latent", x, final_norm[None, :], TILES_P, N_TILES - TILES_P)
    return (y_prompt.reshape(BATCH, SEQ, D_MODEL), y_sample.reshape(DEC_BATCH, DEC_SEQ, D_MODEL),
            jnp.stack(new_ckv, axis=1), jnp.stack(new_kpe, axis=1))
```

```python
import functools
import math

import numpy as np
import jax
import jax.numpy as jnp
from jax import lax
from jax.experimental import pallas as pl
from jax.experimental.pallas import tpu as pltpu

F32 = jnp.float32
BF16 = jnp.bfloat16

D_MODEL = 1024
BATCH = 16
SEQ = 256
DEPTH = 4
DEC_BATCH = 2
DEC_SEQ = 2048
PAST_LEN = 512
GRID_W = 64
EPS = 1e-6
HY_WIDTH = D_MODEL
FILTER_BANDS = 16
FILTER_EMB = 1 + 2 * FILTER_BANDS
FILTER_HIDDEN = 64
FAST_DECAY_PCT = 0.3
SLOW_DECAY_PCT = 1.5
DECAY_TARGET = 1e-2
N_HEADS = 16
Q_LORA = 384
KV_LORA = 256
QK_NOPE = 64
QK_ROPE = 32
V_HEAD = 64
ROPE_THETA = 10000.0

LANES = 128
MXU_TILE = 256
HEAD_PAD = LANES
ROWS_P = BATCH * SEQ
ROWS_S = DEC_BATCH * DEC_SEQ
ROWS = ROWS_P + ROWS_S
TM = 512
N_TILES = ROWS // TM
TILES_P = ROWS_P // TM
TILES_PER_DEC_SEQ = DEC_SEQ // TM
N_COND = 8
SEQ_BLOCK = 2048
TK = 512
LCONV_TC = 512
VMEM_CAP = 56 * 1024 * 1024


def _cparams(sem, vmem_bytes):
    return pltpu.CompilerParams(dimension_semantics=sem, vmem_limit_bytes=min(int(vmem_bytes), VMEM_CAP))


def _resident(shape, index_map):
    return pl.BlockSpec(shape, index_map, pipeline_mode=pl.Buffered(1))


def _entry(arr, idx, resident=False):
    zeros = (0,) * (arr.ndim - 1)
    return pl.BlockSpec((None,) + arr.shape[1:], lambda *_: (idx,) + zeros,
                        pipeline_mode=pl.Buffered(1) if resident else None)


def _cond_row(i):
    return jnp.where(i < TILES_P, 0, 1 + (i - TILES_P) // TILES_PER_DEC_SEQ)


def _silu(x):
    return x * jax.nn.sigmoid(x)


def _rms(x, g):
    return x * lax.rsqrt(jnp.mean(x * x, axis=-1, keepdims=True) + EPS) * g


def _row_pair(x):
    if isinstance(x, tuple):
        return x[0], x[1], 0
    return x, x, TILES_P


def _row_pair_specs(x, n):
    _, _, base = _row_pair(x)
    lo = pl.BlockSpec((TM, n), lambda i, *_: (jnp.minimum(i, TILES_P - 1), 0))
    hi = pl.BlockSpec((TM, n), lambda i, *_: (jnp.maximum(i - TILES_P, 0) + base, 0))
    return lo, hi


def _pick_rows(i, lo_ref, hi_ref):
    return jnp.where(i < TILES_P, lo_ref[...], hi_ref[...])


def _ada_kernel(cond_ref, w_ref, b_ref, o_ref):
    s = _silu(cond_ref[...]).astype(BF16)
    o_ref[...] = jnp.dot(s, w_ref[...].astype(BF16), preferred_element_type=F32) + b_ref[...]


def _ada_all(cond, ada_w, ada_b):
    tn = 1024
    return pl.pallas_call(
        _ada_kernel,
        name="ada",
        grid=(DEPTH, 3 * D_MODEL // tn),
        in_specs=[
            pl.BlockSpec((N_COND, D_MODEL), lambda l, j: (0, 0)),
            pl.BlockSpec((None, D_MODEL, tn), lambda l, j: (l, 0, j)),
            pl.BlockSpec((None, 1, tn), lambda l, j: (l, 0, j)),
        ],
        out_specs=pl.BlockSpec((None, N_COND, tn), lambda l, j: (l, 0, j)),
        out_shape=jax.ShapeDtypeStruct((DEPTH, N_COND, 3 * D_MODEL), F32),
        compiler_params=_cparams(("arbitrary", "arbitrary"), 32 << 20),
    )(cond, ada_w, ada_b.reshape(DEPTH, 1, 3 * D_MODEL))


def _mod_spec(layer, part):
    return pl.BlockSpec((None, N_COND, D_MODEL), lambda i, *_: (layer, 0, part))


def _modnorm(i, x, nw_ref, sh_ref, sc_ref):
    c = _cond_row(i)
    return _rms(x, nw_ref[...]) * (1.0 + sc_ref[pl.ds(c, 1), :]) + sh_ref[pl.ds(c, 1), :]


HALO = 16
HALO_PER_TILE = TM // HALO


def _halo_specs(x):
    _, _, base = _row_pair(x)
    r = HALO_PER_TILE
    lo_last = TILES_P * r - 1
    hi_last = (N_TILES - TILES_P) * r - 1
    lo_tile = lambda i: jnp.minimum(i, TILES_P - 1)
    hi_tile = lambda i: jnp.maximum(i - TILES_P, 0)
    blk = lambda f: pl.BlockSpec((HALO, D_MODEL), lambda i: (f(i), 0))
    return [
        blk(lambda i: jnp.maximum(lo_tile(i) * r - 1, 0)),
        blk(lambda i: jnp.maximum(hi_tile(i) * r - 1, 0) + base * r),
        blk(lambda i: jnp.minimum((lo_tile(i) + 1) * r, lo_last)),
        blk(lambda i: jnp.minimum((hi_tile(i) + 1) * r, hi_last) + base * r),
    ]


def _hy_in_kernel(xlo_ref, xhi_ref, plo_ref, phi_ref, nlo_ref, nhi_ref, nw_ref, sh_ref, sc_ref, w_ref,
                  cw_ref, cb_ref, z_ref, m_ref, h_scr, u0_scr, u1_scr, u2_scr):
    i = pl.program_id(0)
    width = z_ref.shape[1]
    norm = lambda lo_ref, hi_ref: _modnorm(i, _pick_rows(i, lo_ref, hi_ref), nw_ref, sh_ref, sc_ref).astype(BF16)
    h_scr[0:HALO, :] = norm(plo_ref, phi_ref)
    h_scr[HALO:HALO + TM, :] = norm(xlo_ref, xhi_ref)
    h_scr[HALO + TM:HALO + TM + HALO, :] = norm(nlo_ref, nhi_ref)

    seq_mask = jnp.where(i < TILES_P, SEQ - 1, DEC_SEQ - 1)
    pos = (lax.broadcasted_iota(jnp.int32, (TM, width), 0) + i * TM) & seq_mask
    first = pos == 0
    last = pos == seq_mask

    def conv(g, u_scr):
        cols = slice(g * width, (g + 1) * width)
        for c0 in range(0, width, MXU_TILE):
            u_scr[:, c0:c0 + MXU_TILE] = jnp.dot(
                h_scr[...], w_ref[:, g * width + c0:g * width + c0 + MXU_TILE], preferred_element_type=F32)
        prev = jnp.where(first, 0.0, u_scr[HALO - 1:HALO - 1 + TM, :])
        nxt = jnp.where(last, 0.0, u_scr[HALO + 1:HALO + 1 + TM, :])
        return (cb_ref[:, cols] + prev * cw_ref[0:1, cols] + u_scr[HALO:HALO + TM, :] * cw_ref[1:2, cols]
                + nxt * cw_ref[2:3, cols])

    z_ref[...] = (conv(2, u2_scr) * conv(1, u1_scr)).astype(BF16)
    gate = jnp.dot(h_scr[HALO:HALO + TM, :], w_ref[:, 3 * width:4 * width], preferred_element_type=F32)
    m_ref[...] = (conv(0, u0_scr) * _silu(gate)).astype(BF16)


def _hy_in(x, norm_w, mod, layer, w_in, conv_w, conv_b):
    j = layer // 2
    xlo, xhi, _ = _row_pair(x)
    rows_h = TM + 2 * HALO
    out = pl.BlockSpec((TM, HY_WIDTH), lambda i: (i, 0))
    return pl.pallas_call(
        _hy_in_kernel,
        name="hy_in",
        grid=(N_TILES,),
        in_specs=[
            *_row_pair_specs(x, D_MODEL),
            *_halo_specs(x),
            _entry(norm_w, layer),
            _mod_spec(layer, 0),
            _mod_spec(layer, 1),
            _entry(w_in, j, resident=True),
            _entry(conv_w, j),
            _entry(conv_b, j),
        ],
        out_specs=[out, out],
        out_shape=[jax.ShapeDtypeStruct((ROWS, HY_WIDTH), BF16)] * 2,
        scratch_shapes=[pltpu.VMEM((rows_h, D_MODEL), BF16)] + [pltpu.VMEM((rows_h, HY_WIDTH), F32)] * 3,
        compiler_params=_cparams(("arbitrary",), 48 << 20),
    )(xlo, xhi, xlo, xhi, xlo, xhi, norm_w, mod, mod, w_in, conv_w, conv_b)


@functools.lru_cache(maxsize=None)
def _filter_consts(L):
    t = np.linspace(0.0, 1.0, L)[:, None]
    w = (2.0 * math.pi / L) * np.arange(L)[:, None]
    bands = np.linspace(1e-4, FILTER_BANDS - 1, FILTER_BANDS)[None, :]
    emb = np.concatenate([t, np.cos(bands * w), -np.sin(bands * w)], axis=-1)
    emb = np.pad(emb, ((0, 0), (0, FILTER_HIDDEN - FILTER_EMB)))
    max_decay = math.log(DECAY_TARGET) / FAST_DECAY_PCT
    min_decay = math.log(DECAY_TARGET) / SLOW_DECAY_PCT
    deltas = np.abs(np.linspace(min_decay, max_decay, HY_WIDTH))
    deltas = np.concatenate([deltas, deltas])[None, :]
    return emb.astype(np.float32), t.astype(np.float32), deltas.astype(np.float32)


def _filter_kernel(emb_ref, t_ref, w1_ref, b1_ref, fr_ref, w2_ref, b2_ref, w3_ref, dl_ref, o_ref, h_scr):
    @pl.when(pl.program_id(0) == 0)
    def _():
        hi = lax.Precision.HIGHEST
        fr = fr_ref[...]
        h = jnp.sin(fr * (jnp.dot(emb_ref[...], w1_ref[...], precision=hi, preferred_element_type=F32) + b1_ref[...]))
        h = jnp.sin(fr * (jnp.dot(h, w2_ref[...], precision=hi, preferred_element_type=F32) + b2_ref[...]))
        h_scr[...] = h.astype(BF16)

    h = jnp.dot(h_scr[...], w3_ref[...].astype(BF16), preferred_element_type=F32)
    h = h * jnp.exp(-t_ref[...] * dl_ref[...])
    o_ref[...] = h / jnp.sum(jnp.abs(h), axis=0, keepdims=True)


def _hy_filter(L, hy, w1, b1, freq, w2, b2, w3):
    emb, t, deltas = _filter_consts(L)
    tcf = 512
    full = lambda shape: pl.BlockSpec(shape, lambda c: (0, 0))
    return pl.pallas_call(
        _filter_kernel,
        name=f"hy_filter{L}",
        grid=(2 * HY_WIDTH // tcf,),
        in_specs=[
            full((L, FILTER_HIDDEN)), full((L, 1)), _entry(w1, hy), _entry(b1, hy), _entry(freq, hy),
            _entry(w2, hy), _entry(b2, hy),
            pl.BlockSpec((None, FILTER_HIDDEN, tcf), lambda c: (hy, 0, c)),
            pl.BlockSpec((1, tcf), lambda c: (0, c)),
        ],
        out_specs=pl.BlockSpec((L, tcf), lambda c: (0, c)),
        out_shape=jax.ShapeDtypeStruct((L, 2 * HY_WIDTH), F32),
        scratch_shapes=[pltpu.VMEM((L, FILTER_HIDDEN), BF16)],
        compiler_params=_cparams(("arbitrary",), 40 << 20),
    )(jnp.asarray(emb), jnp.asarray(t), w1, b1, freq, w2, b2, w3, jnp.asarray(deltas))


@functools.lru_cache(maxsize=None)
def _dft_consts(L):
    k = np.arange(L, dtype=np.int64)
    ang = (np.outer(k, k) % (2 * L)).astype(np.float64) * (math.pi / L)
    return np.cos(ang).astype(np.float32), np.sin(ang).astype(np.float32)


def _dft_mats(L):
    c, s = _dft_consts(L)
    return jnp.asarray(c).astype(BF16), jnp.asarray(s).astype(BF16)


def _alt_sign(shape):
    return (1 - 2 * (lax.broadcasted_iota(jnp.int32, shape, 0) & 1)).astype(F32)


def _spec_kernel(ff_ref, fb_ref, bias_ref, c_ref, s_ref, a_ref, b_ref, nq_ref, f_scr, d_scr, *, L, tk):
    kk = pl.program_id(1)
    inv_n = 1.0 / (2.0 * L)
    bias = bias_ref[...]

    @pl.when(kk == 0)
    def _():
        ff = ff_ref[...]
        fb = fb_ref[...]
        f = ff + fb
        f_scr[...] = f.astype(BF16)
        d_scr[...] = (fb - ff).astype(BF16)
        nyq = (jnp.sum(f * _alt_sign(f.shape), axis=0, keepdims=True) + bias) * inv_n
        nq_ref[...] = jnp.broadcast_to(nyq, nq_ref.shape)

    k0 = pl.multiple_of(kk * tk, tk)
    hre = jnp.dot(c_ref[pl.ds(k0, tk), :], f_scr[...], preferred_element_type=F32) + bias
    him = jnp.dot(s_ref[pl.ds(k0, tk), :], d_scr[...], preferred_element_type=F32)
    dc_row = (lax.broadcasted_iota(jnp.int32, hre.shape, 0) == 0) & (kk == 0)
    a_ref[...] = jnp.where(dc_row, hre * inv_n, hre * (2.0 * inv_n))
    b_ref[...] = him * (2.0 * inv_n)


def _hy_spectrum(L, hy, filt, f_bias, c, s):
    tk = min(TK, L)
    tc = LCONV_TC
    nc = HY_WIDTH // tc
    out = pl.BlockSpec((tk, tc), lambda j, kk: (kk, j))
    mat = _resident((L, L), lambda j, kk: (0, 0))
    return pl.pallas_call(
        functools.partial(_spec_kernel, L=L, tk=tk),
        name=f"hy_spectrum{L}",
        grid=(nc, L // tk),
        in_specs=[
            pl.BlockSpec((L, tc), lambda j, kk: (0, j)),
            pl.BlockSpec((L, tc), lambda j, kk: (0, j + nc)),
            pl.BlockSpec((None, 1, tc), lambda j, kk: (hy, 0, j)),
            mat, mat,
        ],
        out_specs=[out, out, pl.BlockSpec((8, tc), lambda j, kk: (0, j))],
        out_shape=[jax.ShapeDtypeStruct((L, HY_WIDTH), F32)] * 2 + [jax.ShapeDtypeStruct((8, HY_WIDTH), F32)],
        scratch_shapes=[pltpu.VMEM((L, tc), BF16), pltpu.VMEM((L, tc), BF16)],
        compiler_params=_cparams(("arbitrary", "arbitrary"), 48 << 20),
    )(filt, filt, f_bias, c, s)


def _lconv_kernel(z_ref, m_ref, c_ref, s_ref, a_ref, b_ref, nq_ref, t_ref, y_scr, *, L, tk):
    kk = pl.program_id(2)
    rows, tc = y_scr.shape
    seqs = [pl.ds(s0, L) for s0 in range(0, rows, L)]

    @pl.when(kk == 0)
    def _():
        alt = _alt_sign((L, tc))
        for sq in seqs:
            z_nyq = jnp.sum(z_ref[sq, :].astype(F32) * alt, axis=0, keepdims=True)
            y_scr[sq, :] = alt * (z_nyq * nq_ref[0:1, :])

    k0 = pl.multiple_of(kk * tk, tk)
    a = a_ref[...]
    b = b_ref[...]
    for sq in seqs:
        zb = z_ref[sq, :]
        zr = jnp.dot(c_ref[pl.ds(k0, tk), :], zb, preferred_element_type=F32)
        zi = jnp.dot(s_ref[pl.ds(k0, tk), :], zb, preferred_element_type=F32)
        yr = (zr * a + zi * b).astype(BF16)
        yw = (zi * a - zr * b).astype(BF16)
        y_scr[sq, :] += (jnp.dot(c_ref[:, pl.ds(k0, tk)], yr, preferred_element_type=F32)
                         + jnp.dot(s_ref[:, pl.ds(k0, tk)], yw, preferred_element_type=F32))

    @pl.when(kk == pl.num_programs(2) - 1)
    def _():
        t_ref[...] = (y_scr[...] * m_ref[...].astype(F32)).astype(BF16)


def _hy_lconv(L, row_block0, n_row_blocks, z, m, c, s, a, b, nq):
    tk = min(TK, L)
    tc = LCONV_TC
    nc = HY_WIDTH // tc
    blk = pl.BlockSpec((SEQ_BLOCK, tc), lambda r, j, kk: (r + row_block0, j))
    coef = pl.BlockSpec((tk, tc), lambda r, j, kk: (kk, j))
    mat = _resident((L, L), lambda r, j, kk: (0, 0))
    return pl.pallas_call(
        functools.partial(_lconv_kernel, L=L, tk=tk),
        name=f"hy_lconv{L}",
        grid=(n_row_blocks, nc, L // tk),
        in_specs=[blk, blk, mat, mat, coef, coef, pl.BlockSpec((8, tc), lambda r, j, kk: (0, j))],
        out_specs=pl.BlockSpec((SEQ_BLOCK, tc), lambda r, j, kk: (r, j)),
        out_shape=jax.ShapeDtypeStruct((n_row_blocks * SEQ_BLOCK, HY_WIDTH), BF16),
        scratch_shapes=[pltpu.VMEM((SEQ_BLOCK, tc), F32)],
        compiler_params=_cparams(("arbitrary", "arbitrary", "arbitrary"), 56 << 20),
    )(z, m, c, s, a, b, nq)


def _out_kernel(tlo_ref, thi_ref, w_ref, xlo_ref, xhi_ref, g_ref, o_ref):
    i = pl.program_id(0)
    u = jnp.dot(_pick_rows(i, tlo_ref, thi_ref), w_ref[...], preferred_element_type=F32)
    o_ref[...] = _pick_rows(i, xlo_ref, xhi_ref) + g_ref[pl.ds(_cond_row(i), 1), :] * u


def _out_proj(name, t, w_out, x, mod, layer):
    tlo, thi, _ = _row_pair(t)
    xlo, xhi, _ = _row_pair(x)
    return pl.pallas_call(
        _out_kernel,
        name=name,
        grid=(N_TILES,),
        in_specs=[*_row_pair_specs(t, w_out.shape[1]),
                  _entry(w_out, layer // 2, resident=True),
                  *_row_pair_specs(x, D_MODEL),
                  _mod_spec(layer, 2)],
        out_specs=pl.BlockSpec((TM, D_MODEL), lambda i: (i, 0)),
        out_shape=jax.ShapeDtypeStruct((ROWS, D_MODEL), F32),
        compiler_params=_cparams(("arbitrary",), 32 << 20),
    )(tlo, thi, w_out, xlo, xhi, mod)


@functools.lru_cache(maxsize=None)
def _rope_consts():
    axis_dim = QK_ROPE // 2
    nf = axis_dim // 2
    inv = ROPE_THETA ** (-np.arange(0, axis_dim, 2, dtype=np.float64) / axis_dim)
    t = np.arange(DEC_SEQ)
    ang_r = (t // GRID_W)[:, None] * inv
    ang_c = (t % GRID_W)[:, None] * inv
    cos = np.ones((DEC_SEQ, HEAD_PAD))
    sin_up = np.zeros((DEC_SEQ, HEAD_PAD))
    sin_dn = np.zeros((DEC_SEQ, HEAD_PAD))
    for base, ang in ((QK_NOPE, ang_r), (QK_NOPE + axis_dim, ang_c)):
        cos[:, base:base + nf] = np.cos(ang)
        cos[:, base + nf:base + 2 * nf] = np.cos(ang)
        sin_up[:, base:base + nf] = -np.sin(ang)
        sin_dn[:, base + nf:base + 2 * nf] = np.sin(ang)
    return cos.astype(np.float32), sin_up.astype(np.float32), sin_dn.astype(np.float32)


ROPE_HALF = QK_ROPE // 4
Q_SCALE = math.log2(math.e) / math.sqrt(QK_NOPE + QK_ROPE)


def _mla_proj_kernel(x_ref, nw_ref, sh_ref, sc_ref, win_ref, qn_ref, kvn_ref, wqb_ref, wk_ref, wv_ref,
                     cos_ref, sup_ref, sdn_ref,
                     q_ref, k_ref, v_ref, sg_ref, ckv_ref, kpe_ref):
    i = pl.program_id(0)
    h = _modnorm(i, x_ref[...], nw_ref, sh_ref, sc_ref).astype(BF16)
    proj = jnp.dot(h, win_ref[...], preferred_element_type=F32)
    o_kv = Q_LORA
    o_gate = Q_LORA + KV_LORA
    o_pe = o_gate + N_HEADS * V_HEAD
    qn = _rms(proj[:, 0:o_kv], qn_ref[...]).astype(BF16)
    ckv = _rms(proj[:, o_kv:o_gate], kvn_ref[...])
    gate = proj[:, o_gate:o_pe]
    kpe = proj[:, o_pe:o_pe + HEAD_PAD]
    ckv_ref[...] = ckv
    kpe_ref[...] = kpe
    sg_ref[...] = _silu(gate)
    ckv_b = ckv.astype(BF16)
    v_ref[...] = jnp.dot(ckv_b, wv_ref[...], preferred_element_type=F32).astype(BF16)
    q = jnp.dot(qn, wqb_ref[...], preferred_element_type=F32)
    kn = jnp.dot(ckv_b, wk_ref[...], preferred_element_type=F32)

    latent = i >= TILES_P
    cos = jnp.where(latent, cos_ref[...], 1.0)
    sup = jnp.where(latent, sup_ref[...], 0.0)
    sdn = jnp.where(latent, sdn_ref[...], 0.0)

    def rope(u, scale):
        return (u * (cos * scale) + pltpu.roll(u, HEAD_PAD - ROPE_HALF, axis=1) * (sup * scale)
                + pltpu.roll(u, ROPE_HALF, axis=1) * (sdn * scale))

    kpe_r = rope(kpe, 1.0)
    for hd in range(N_HEADS):
        cols = slice(hd * HEAD_PAD, (hd + 1) * HEAD_PAD)
        q_ref[:, cols] = rope(q[:, cols], Q_SCALE).astype(BF16)
        k_ref[:, cols] = (kn[:, cols] + kpe_r).astype(BF16)


def _mla_proj(x, norm_w, mod, layer, w):
    cos, sup, sdn = (jnp.asarray(c) for c in _rope_consts())
    j = layer // 2
    hp = N_HEADS * HEAD_PAD
    rope_blk = pl.BlockSpec(
        (TM, HEAD_PAD), lambda i: (jnp.where(i >= TILES_P, (i - TILES_P) % TILES_PER_DEC_SEQ, 0), 0))
    tile = lambda n: pl.BlockSpec((TM, n), lambda i: (i, 0))
    return pl.pallas_call(
        _mla_proj_kernel,
        name="mla_proj",
        grid=(N_TILES,),
        in_specs=[
            tile(D_MODEL),
            _entry(norm_w, layer),
            _mod_spec(layer, 0), _mod_spec(layer, 1),
            _entry(w["w_in"], j, True), _entry(w["q_norm"], j), _entry(w["kv_norm"], j),
            _entry(w["w_qb"], j, True), _entry(w["w_k"], j, True), _entry(w["w_v"], j, True),
            rope_blk, rope_blk, rope_blk,
        ],
        out_specs=[tile(hp), tile(hp), tile(N_HEADS * V_HEAD),
                   tile(N_HEADS * V_HEAD), tile(KV_LORA), tile(HEAD_PAD)],
        out_shape=[
            jax.ShapeDtypeStruct((ROWS, hp), BF16),
            jax.ShapeDtypeStruct((ROWS, hp), BF16),
            jax.ShapeDtypeStruct((ROWS, N_HEADS * V_HEAD), BF16),
            jax.ShapeDtypeStruct((ROWS, N_HEADS * V_HEAD), F32),
            jax.ShapeDtypeStruct((ROWS, KV_LORA), F32),
            jax.ShapeDtypeStruct((ROWS, HEAD_PAD), F32),
        ],
        compiler_params=_cparams(("arbitrary",), 56 << 20),
    )(x, norm_w, mod, mod, w["w_in"], w["q_norm"], w["kv_norm"], w["w_qb"], w["w_k"], w["w_v"],
      cos, sup, sdn)


def _mla_ctx_kernel(ckv_ref, kpe_ref, wk_ref, wv_ref, k_ref, v_ref):
    ckv_b = ckv_ref[...].astype(BF16)
    kn = jnp.dot(ckv_b, wk_ref[...], preferred_element_type=F32)
    v_ref[...] = jnp.dot(ckv_b, wv_ref[...], preferred_element_type=F32).astype(BF16)
    kpe = kpe_ref[...]
    for hd in range(N_HEADS):
        cols = slice(hd * HEAD_PAD, (hd + 1) * HEAD_PAD)
        k_ref[:, cols] = (kn[:, cols] + kpe).astype(BF16)


def _mla_ctx(j, ckv_ctx, kpe_ctx, w):
    hp = N_HEADS * HEAD_PAD
    rows = DEC_BATCH * PAST_LEN
    tile = lambda n: pl.BlockSpec((PAST_LEN, n), lambda i: (i, 0))
    cache = lambda n: pl.BlockSpec((None, None, PAST_LEN, n), lambda i: (i, j, 0, 0))
    return pl.pallas_call(
        _mla_ctx_kernel,
        name="mla_ctx",
        grid=(DEC_BATCH,),
        in_specs=[cache(KV_LORA), cache(HEAD_PAD), _entry(w["w_k"], j, True), _entry(w["w_v"], j, True)],
        out_specs=[tile(hp), tile(N_HEADS * V_HEAD)],
        out_shape=[jax.ShapeDtypeStruct((rows, hp), BF16),
                   jax.ShapeDtypeStruct((rows, N_HEADS * V_HEAD), BF16)],
        compiler_params=_cparams(("arbitrary",), 32 << 20),
    )(ckv_ctx, kpe_ctx, w["w_k"], w["w_v"])


NT_DIMS = (((1,), (1,)), ((), ()))
ATTN_PAIRS = 4
ATTN_TQ = 256


def _attn_kernel(*refs, n_pairs, has_ctx):
    if has_ctx:
        q_ref, k_ref, v_ref, kc_ref, vc_ref, sg_ref, o_ref = refs
    else:
        q_ref, k_ref, v_ref, sg_ref, o_ref = refs
    tq = q_ref.shape[0]
    low_half = lax.broadcasted_iota(jnp.int32, (tq, LANES), 1) < V_HEAD
    for p in range(n_pairs):
        vcols = slice(p * LANES, (p + 1) * LANES)
        vp = v_ref[:, vcols]
        outs = []
        for hh in range(2):
            cols = slice((2 * p + hh) * HEAD_PAD, (2 * p + hh + 1) * HEAD_PAD)
            q = q_ref[:, cols]
            s = lax.dot_general(q, k_ref[:, cols], NT_DIMS, preferred_element_type=F32)
            mx = jnp.max(s, axis=-1, keepdims=True)
            if has_ctx:
                sc = lax.dot_general(q, kc_ref[:, cols], NT_DIMS, preferred_element_type=F32)
                mx = jnp.maximum(mx, jnp.max(sc, axis=-1, keepdims=True))
            e = jnp.exp2(s - mx)
            den = jnp.sum(e, axis=-1, keepdims=True)
            pv = jnp.dot(e.astype(BF16), vp, preferred_element_type=F32)
            if has_ctx:
                ec = jnp.exp2(sc - mx)
                den = den + jnp.sum(ec, axis=-1, keepdims=True)
                pv = pv + jnp.dot(ec.astype(BF16), vc_ref[:, vcols], preferred_element_type=F32)
            outs.append(pv / den)
        o = jnp.where(low_half, outs[0], outs[1])
        o_ref[:, vcols] = (o * sg_ref[:, vcols]).astype(BF16)


def _attn_prompt(q, k, v, sg):
    hp = N_HEADS * HEAD_PAD
    nv = N_HEADS * V_HEAD
    return pl.pallas_call(
        functools.partial(_attn_kernel, n_pairs=N_HEADS // 2, has_ctx=False),
        name="attn_prompt",
        grid=(BATCH,),
        in_specs=[pl.BlockSpec((SEQ, hp), lambda b: (b, 0)),
                  pl.BlockSpec((SEQ, hp), lambda b: (b, 0)),
                  pl.BlockSpec((SEQ, nv), lambda b: (b, 0)),
                  pl.BlockSpec((SEQ, nv), lambda b: (b, 0))],
        out_specs=pl.BlockSpec((SEQ, nv), lambda b: (b, 0)),
        out_shape=jax.ShapeDtypeStruct((ROWS_P, nv), BF16),
        compiler_params=_cparams(("arbitrary",), 40 << 20),
    )(q, k, v, sg)


def _attn_latent(q, k, v, kc, vc, sg):
    tq = ATTN_TQ
    npair = ATTN_PAIRS
    nv = N_HEADS * V_HEAD
    q0 = ROWS_P // tq
    s0 = ROWS_P // DEC_SEQ
    wide = npair * 2 * HEAD_PAD
    narrow = npair * LANES
    qrow = lambda b, p, t: (q0 + b * (DEC_SEQ // tq) + t, p)
    return pl.pallas_call(
        functools.partial(_attn_kernel, n_pairs=npair, has_ctx=True),
        name="attn_latent",
        grid=(DEC_BATCH, N_HEADS // 2 // npair, DEC_SEQ // tq),
        in_specs=[pl.BlockSpec((tq, wide), qrow),
                  pl.BlockSpec((DEC_SEQ, wide), lambda b, p, t: (s0 + b, p)),
                  pl.BlockSpec((DEC_SEQ, narrow), lambda b, p, t: (s0 + b, p)),
                  pl.BlockSpec((PAST_LEN, wide), lambda b, p, t: (b, p)),
                  pl.BlockSpec((PAST_LEN, narrow), lambda b, p, t: (b, p)),
                  pl.BlockSpec((tq, narrow), qrow)],
        out_specs=pl.BlockSpec((tq, narrow), lambda b, p, t: (b * (DEC_SEQ // tq) + t, p)),
        out_shape=jax.ShapeDtypeStruct((ROWS_S, nv), BF16),
        compiler_params=_cparams(("arbitrary", "arbitrary", "arbitrary"), 48 << 20),
    )(q, k, v, kc, vc, sg)


def _final_kernel(x_ref, g_ref, o_ref):
    o_ref[...] = _rms(x_ref[...], g_ref[...])


def _final_norm(name, x, g, tile0, n_tiles):
    return pl.pallas_call(
        _final_kernel,
        name=name,
        grid=(n_tiles,),
        in_specs=[pl.BlockSpec((TM, D_MODEL), lambda i: (i + tile0, 0)),
                  pl.BlockSpec((1, D_MODEL), lambda i: (0, 0))],
        out_specs=pl.BlockSpec((TM, D_MODEL), lambda i: (i, 0)),
        out_shape=jax.ShapeDtypeStruct((n_tiles * TM, D_MODEL), F32),
        compiler_params=_cparams(("arbitrary",), 24 << 20),
    )(x, g)


def _mla_weights(w_in, q_norm, w_qb, kv_norm, w_kvb, w_o):
    n = w_in.shape[0]
    o_pe = Q_LORA + KV_LORA
    o_gate = o_pe + QK_ROPE
    zeros = lambda k: jnp.zeros((n, D_MODEL, k), F32)
    w_in_r = jnp.concatenate(
        [w_in[..., :o_pe], w_in[..., o_gate:], zeros(QK_NOPE), w_in[..., o_pe:o_gate],
         zeros(HEAD_PAD - QK_NOPE - QK_ROPE)], axis=-1)
    qb = w_qb.reshape(n, Q_LORA, N_HEADS, QK_NOPE + QK_ROPE)
    qb = jnp.pad(qb, ((0, 0), (0, 0), (0, 0), (0, HEAD_PAD - QK_NOPE - QK_ROPE)))
    kvb = w_kvb.reshape(n, KV_LORA, N_HEADS, QK_NOPE + V_HEAD)
    wk = jnp.pad(kvb[..., :QK_NOPE], ((0, 0), (0, 0), (0, 0), (0, HEAD_PAD - QK_NOPE)))
    wv = kvb[..., QK_NOPE:]
    return {
        "w_in": w_in_r.astype(BF16),
        "q_norm": q_norm[:, None, :],
        "kv_norm": kv_norm[:, None, :],
        "w_qb": qb.reshape(n, Q_LORA, N_HEADS * HEAD_PAD).astype(BF16),
        "w_k": wk.reshape(n, KV_LORA, N_HEADS * HEAD_PAD).astype(BF16),
        "w_v": wv.reshape(n, KV_LORA, N_HEADS * V_HEAD).astype(BF16),
        "w_o": w_o.astype(BF16),
    }


def kernel(x_prompt, x_sample, cache_ckv, cache_kpe, c, c_ctx, norm_w, ada_w, ada_b, hy_w_in, hy_conv_w, hy_conv_b, hy_f_w1, hy_f_b1, hy_f_freq, hy_f_w2, hy_f_b2, hy_f_w3, hy_f_bias, hy_w_out, mla_w_in, mla_q_norm, mla_w_qb, mla_kv_norm, mla_w_kvb, mla_w_o, final_norm):
    x = (x_prompt.reshape(ROWS_P, D_MODEL), x_sample.reshape(ROWS_S, D_MODEL))
    cond = jnp.concatenate([c_ctx[None, :], c, jnp.zeros((N_COND - 1 - DEC_BATCH, D_MODEL), F32)], axis=0)
    mod = _ada_all(cond, ada_w, ada_b)

    nw = norm_w[:, None, :]
    hy_w_in_b = hy_w_in.astype(BF16)
    hy_w_out_b = hy_w_out.astype(BF16)
    hy_conv_b3 = hy_conv_b[:, None, :]
    f_w1 = jnp.pad(hy_f_w1, ((0, 0), (0, FILTER_HIDDEN - FILTER_EMB), (0, 0)))
    f_b1, f_freq, f_b2, f_bias = (a[:, None, :] for a in (hy_f_b1, hy_f_freq, hy_f_b2, hy_f_bias))
    w = _mla_weights(mla_w_in, mla_q_norm, mla_w_qb, mla_kv_norm, mla_w_kvb, mla_w_o)
    kpe_ctx = jnp.pad(cache_kpe, ((0, 0), (0, 0), (0, 0), (QK_NOPE, HEAD_PAD - QK_NOPE - QK_ROPE)))

    new_ckv, new_kpe = [], []
    for layer in range(DEPTH):
        j = layer // 2
        if layer % 2 == 0:
            z, m = _hy_in(x, nw, mod, layer, hy_w_in_b, hy_conv_w, hy_conv_b3)
            t = []
            for L, blk0, nblk in ((SEQ, 0, ROWS_P // SEQ_BLOCK), (DEC_SEQ, ROWS_P // SEQ_BLOCK, ROWS_S // SEQ_BLOCK)):
                filt = _hy_filter(L, j, f_w1, f_b1, f_freq, hy_f_w2, f_b2, hy_f_w3)
                cm, sm = _dft_mats(L)
                a, b, nq = _hy_spectrum(L, j, filt, f_bias, cm, sm)
                t.append(_hy_lconv(L, blk0, nblk, z, m, cm, sm, a, b, nq))
            x = _out_proj("hy_out", tuple(t), hy_w_out_b, x, mod, layer)
        else:
            q, k, v, sg, ckv, kpe = _mla_proj(x, nw, mod, layer, w)
            kc, vc = _mla_ctx(j, cache_ckv, kpe_ctx, w)
            o_p = _attn_prompt(q, k, v, sg)
            o_s = _attn_latent(q, k, v, kc, vc, sg)
            x = _out_proj("mla_out", (o_p, o_s), w["w_o"], x, mod, layer)
            new_ckv.append(ckv[:ROWS_P].reshape(BATCH, SEQ, KV_LORA))
            new_kpe.append(kpe[:ROWS_P, QK_NOPE:QK_NOPE + QK_ROPE].reshape(BATCH, SEQ, QK_ROPE))

    y_prompt = _final_norm("final_prompt", x, final_norm[None, :], 0, TILES_P)
    y_sample = _final_norm("final_latent", x, final_norm[None, :], TILES_P, N_TILES - TILES_P)
    return (y_prompt.reshape(BATCH, SEQ, D_MODEL), y_sample.reshape(DEC_BATCH, DEC_SEQ, D_MODEL),
            jnp.stack(new_ckv, axis=1), jnp.stack(new_kpe, axis=1))
```

```python
import functools
import math

import numpy as np
import jax
import jax.numpy as jnp
from jax import lax
from jax.experimental import pallas as pl
from jax.experimental.pallas import tpu as pltpu

F32 = jnp.float32
BF16 = jnp.bfloat16

D_MODEL = 1024
BATCH = 16
SEQ = 256
DEPTH = 4
DEC_BATCH = 2
DEC_SEQ = 2048
PAST_LEN = 512
GRID_W = 64
EPS = 1e-6
HY_WIDTH = D_MODEL
FILTER_BANDS = 16
FILTER_EMB = 1 + 2 * FILTER_BANDS
FILTER_HIDDEN = 64
FAST_DECAY_PCT = 0.3
SLOW_DECAY_PCT = 1.5
DECAY_TARGET = 1e-2
N_HEADS = 16
Q_LORA = 384
KV_LORA = 256
QK_NOPE = 64
QK_ROPE = 32
V_HEAD = 64
ROPE_THETA = 10000.0

LANES = 128
MXU_TILE = 256
HEAD_PAD = LANES
ROWS_P = BATCH * SEQ
ROWS_S = DEC_BATCH * DEC_SEQ
ROWS = ROWS_P + ROWS_S
TM = 512
N_TILES = ROWS // TM
TILES_P = ROWS_P // TM
TILES_PER_DEC_SEQ = DEC_SEQ // TM
N_COND = 8
SEQ_BLOCK = 2048
TK = 512
LCONV_TC = 512
VMEM_CAP = 56 * 1024 * 1024


def _cparams(sem, vmem_bytes):
    return pltpu.CompilerParams(dimension_semantics=sem, vmem_limit_bytes=min(int(vmem_bytes), VMEM_CAP))


def _resident(shape, index_map):
    return pl.BlockSpec(shape, index_map, pipeline_mode=pl.Buffered(1))


def _entry(arr, idx, resident=False):
    zeros = (0,) * (arr.ndim - 1)
    return pl.BlockSpec((None,) + arr.shape[1:], lambda *_: (idx,) + zeros,
                        pipeline_mode=pl.Buffered(1) if resident else None)


def _cond_row(i):
    return jnp.where(i < TILES_P, 0, 1 + (i - TILES_P) // TILES_PER_DEC_SEQ)


def _silu(x):
    return x * jax.nn.sigmoid(x)


def _rms(x, g):
    return x * lax.rsqrt(jnp.mean(x * x, axis=-1, keepdims=True) + EPS) * g


def _row_pair(x):
    if isinstance(x, tuple):
        return x[0], x[1], 0
    return x, x, TILES_P


def _row_pair_specs(x, n):
    _, _, base = _row_pair(x)
    lo = pl.BlockSpec((TM, n), lambda i, *_: (jnp.minimum(i, TILES_P - 1), 0))
    hi = pl.BlockSpec((TM, n), lambda i, *_: (jnp.maximum(i - TILES_P, 0) + base, 0))
    return lo, hi


def _pick_rows(i, lo_ref, hi_ref):
    return jnp.where(i < TILES_P, lo_ref[...], hi_ref[...])


def _ada_kernel(cond_ref, w_ref, b_ref, o_ref):
    s = _silu(cond_ref[...]).astype(BF16)
    o_ref[...] = jnp.dot(s, w_ref[...].astype(BF16), preferred_element_type=F32) + b_ref[...]


def _ada_all(cond, ada_w, ada_b):
    tn = 1024
    return pl.pallas_call(
        _ada_kernel,
        name="ada",
        grid=(DEPTH, 3 * D_MODEL // tn),
        in_specs=[
            pl.BlockSpec((N_COND, D_MODEL), lambda l, j: (0, 0)),
            pl.BlockSpec((None, D_MODEL, tn), lambda l, j: (l, 0, j)),
            pl.BlockSpec((None, 1, tn), lambda l, j: (l, 0, j)),
        ],
        out_specs=pl.BlockSpec((None, N_COND, tn), lambda l, j: (l, 0, j)),
        out_shape=jax.ShapeDtypeStruct((DEPTH, N_COND, 3 * D_MODEL), F32),
        compiler_params=_cparams(("arbitrary", "arbitrary"), 32 << 20),
    )(cond, ada_w, ada_b.reshape(DEPTH, 1, 3 * D_MODEL))


def _mod_spec(layer, part):
    return pl.BlockSpec((None, N_COND, D_MODEL), lambda i, *_: (layer, 0, part))


def _modnorm(i, x, nw_ref, sh_ref, sc_ref):
    c = _cond_row(i)
    return _rms(x, nw_ref[...]) * (1.0 + sc_ref[pl.ds(c, 1), :]) + sh_ref[pl.ds(c, 1), :]


HALO = 16
HALO_PER_TILE = TM // HALO


def _halo_specs(x):
    _, _, base = _row_pair(x)
    r = HALO_PER_TILE
    lo_last = TILES_P * r - 1
    hi_last = (N_TILES - TILES_P) * r - 1
    lo_tile = lambda i: jnp.minimum(i, TILES_P - 1)
    hi_tile = lambda i: jnp.maximum(i - TILES_P, 0)
    blk = lambda f: pl.BlockSpec((HALO, D_MODEL), lambda i: (f(i), 0))
    return [
        blk(lambda i: jnp.maximum(lo_tile(i) * r - 1, 0)),
        blk(lambda i: jnp.maximum(hi_tile(i) * r - 1, 0) + base * r),
        blk(lambda i: jnp.minimum((lo_tile(i) + 1) * r, lo_last)),
        blk(lambda i: jnp.minimum((hi_tile(i) + 1) * r, hi_last) + base * r),
    ]


def _hy_in_kernel(xlo_ref, xhi_ref, plo_ref, phi_ref, nlo_ref, nhi_ref, nw_ref, sh_ref, sc_ref, w_ref,
                  cw_ref, cb_ref, z_ref, m_ref, h_scr, u0_scr, u1_scr, u2_scr):
    i = pl.program_id(0)
    width = z_ref.shape[1]
    norm = lambda lo_ref, hi_ref: _modnorm(i, _pick_rows(i, lo_ref, hi_ref), nw_ref, sh_ref, sc_ref).astype(BF16)
    h_scr[0:HALO, :] = norm(plo_ref, phi_ref)
    h_scr[HALO:HALO + TM, :] = norm(xlo_ref, xhi_ref)
    h_scr[HALO + TM:HALO + TM + HALO, :] = norm(nlo_ref, nhi_ref)

    seq_mask = jnp.where(i < TILES_P, SEQ - 1, DEC_SEQ - 1)
    pos = (lax.broadcasted_iota(jnp.int32, (TM, width), 0) + i * TM) & seq_mask
    first = pos == 0
    last = pos == seq_mask

    def conv(g, u_scr):
        cols = slice(g * width, (g + 1) * width)
        for c0 in range(0, width, MXU_TILE):
            u_scr[:, c0:c0 + MXU_TILE] = jnp.dot(
                h_scr[...], w_ref[:, g * width + c0:g * width + c0 + MXU_TILE], preferred_element_type=F32)
        prev = jnp.where(first, 0.0, u_scr[HALO - 1:HALO - 1 + TM, :])
        nxt = jnp.where(last, 0.0, u_scr[HALO + 1:HALO + 1 + TM, :])
        return (cb_ref[:, cols] + prev * cw_ref[0:1, cols] + u_scr[HALO:HALO + TM, :] * cw_ref[1:2, cols]
                + nxt * cw_ref[2:3, cols])

    z_ref[...] = (conv(2, u2_scr) * conv(1, u1_scr)).astype(BF16)
    gate = jnp.dot(h_scr[HALO:HALO + TM, :], w_ref[:, 3 * width:4 * width], preferred_element_type=F32)
    m_ref[...] = (conv(0, u0_scr) * _silu(gate)).astype(BF16)


def _hy_in(x, norm_w, mod, layer, w_in, conv_w, conv_b):
    j = layer // 2
    xlo, xhi, _ = _row_pair(x)
    rows_h = TM + 2 * HALO
    out = pl.BlockSpec((TM, HY_WIDTH), lambda i: (i, 0))
    return pl.pallas_call(
        _hy_in_kernel,
        name="hy_in",
        grid=(N_TILES,),
        in_specs=[
            *_row_pair_specs(x, D_MODEL),
            *_halo_specs(x),
            _entry(norm_w, layer),
            _mod_spec(layer, 0),
            _mod_spec(layer, 1),
            _entry(w_in, j, resident=True),
            _entry(conv_w, j),
            _entry(conv_b, j),
        ],
        out_specs=[out, out],
        out_shape=[jax.ShapeDtypeStruct((ROWS, HY_WIDTH), BF16)] * 2,
        scratch_shapes=[pltpu.VMEM((rows_h, D_MODEL), BF16)] + [pltpu.VMEM((rows_h, HY_WIDTH), F32)] * 3,
        compiler_params=_cparams(("arbitrary",), 48 << 20),
    )(xlo, xhi, xlo, xhi, xlo, xhi, norm_w, mod, mod, w_in, conv_w, conv_b)


@functools.lru_cache(maxsize=None)
def _filter_consts(L):
    t = np.linspace(0.0, 1.0, L)[:, None]
    w = (2.0 * math.pi / L) * np.arange(L)[:, None]
    bands = np.linspace(1e-4, FILTER_BANDS - 1, FILTER_BANDS)[None, :]
    emb = np.concatenate([t, np.cos(bands * w), -np.sin(bands * w)], axis=-1)
    emb = np.pad(emb, ((0, 0), (0, FILTER_HIDDEN - FILTER_EMB)))
    max_decay = math.log(DECAY_TARGET) / FAST_DECAY_PCT
    min_decay = math.log(DECAY_TARGET) / SLOW_DECAY_PCT
    deltas = np.abs(np.linspace(min_decay, max_decay, HY_WIDTH))
    deltas = np.concatenate([deltas, deltas])[None, :]
    return emb.astype(np.float32), t.astype(np.float32), deltas.astype(np.float32)


def _filter_kernel(emb_ref, t_ref, w1_ref, b1_ref, fr_ref, w2_ref, b2_ref, w3_ref, dl_ref, o_ref, h_scr):
    @pl.when(pl.program_id(0) == 0)
    def _():
        hi = lax.Precision.HIGHEST
        fr = fr_ref[...]
        h = jnp.sin(fr * (jnp.dot(emb_ref[...], w1_ref[...], precision=hi, preferred_element_type=F32) + b1_ref[...]))
        h = jnp.sin(fr * (jnp.dot(h, w2_ref[...], precision=hi, preferred_element_type=F32) + b2_ref[...]))
        h_scr[...] = h.astype(BF16)

    h = jnp.dot(h_scr[...], w3_ref[...].astype(BF16), preferred_element_type=F32)
    h = h * jnp.exp(-t_ref[...] * dl_ref[...])
    o_ref[...] = h / jnp.sum(jnp.abs(h), axis=0, keepdims=True)


def _hy_filter(L, hy, w1, b1, freq, w2, b2, w3):
    emb, t, deltas = _filter_consts(L)
    tcf = 512
    full = lambda shape: pl.BlockSpec(shape, lambda c: (0, 0))
    return pl.pallas_call(
        _filter_kernel,
        name=f"hy_filter{L}",
        grid=(2 * HY_WIDTH // tcf,),
        in_specs=[
            full((L, FILTER_HIDDEN)), full((L, 1)), _entry(w1, hy), _entry(b1, hy), _entry(freq, hy),
            _entry(w2, hy), _entry(b2, hy),
            pl.BlockSpec((None, FILTER_HIDDEN, tcf), lambda c: (hy, 0, c)),
            pl.BlockSpec((1, tcf), lambda c: (0, c)),
        ],
        out_specs=pl.BlockSpec((L, tcf), lambda c: (0, c)),
        out_shape=jax.ShapeDtypeStruct((L, 2 * HY_WIDTH), F32),
        scratch_shapes=[pltpu.VMEM((L, FILTER_HIDDEN), BF16)],
        compiler_params=_cparams(("arbitrary",), 40 << 20),
    )(jnp.asarray(emb), jnp.asarray(t), w1, b1, freq, w2, b2, w3, jnp.asarray(deltas))


@functools.lru_cache(maxsize=None)
def _dft_consts(L):
    k = np.arange(L, dtype=np.int64)
    ang = (np.outer(k, k) % (2 * L)).astype(np.float64) * (math.pi / L)
    return np.cos(ang).astype(np.float32), np.sin(ang).astype(np.float32)


def _dft_mats(L):
    c, s = _dft_consts(L)
    return jnp.asarray(c).astype(BF16), jnp.asarray(s).astype(BF16)


def _alt_sign(shape):
    return (1 - 2 * (lax.broadcasted_iota(jnp.int32, shape, 0) & 1)).astype(F32)


def _spec_kernel(ff_ref, fb_ref, bias_ref, c_ref, s_ref, a_ref, b_ref, nq_ref, f_scr, d_scr, *, L, tk):
    kk = pl.program_id(1)
    inv_n = 1.0 / (2.0 * L)
    bias = bias_ref[...]

    @pl.when(kk == 0)
    def _():
        ff = ff_ref[...]
        fb = fb_ref[...]
        f = ff + fb
        f_scr[...] = f.astype(BF16)
        d_scr[...] = (fb - ff).astype(BF16)
        nyq = (jnp.sum(f * _alt_sign(f.shape), axis=0, keepdims=True) + bias) * inv_n
        nq_ref[...] = jnp.broadcast_to(nyq, nq_ref.shape)

    k0 = pl.multiple_of(kk * tk, tk)
    hre = jnp.dot(c_ref[pl.ds(k0, tk), :], f_scr[...], preferred_element_type=F32) + bias
    him = jnp.dot(s_ref[pl.ds(k0, tk), :], d_scr[...], preferred_element_type=F32)
    dc_row = (lax.broadcasted_iota(jnp.int32, hre.shape, 0) == 0) & (kk == 0)
    a_ref[...] = jnp.where(dc_row, hre * inv_n, hre * (2.0 * inv_n))
    b_ref[...] = him * (2.0 * inv_n)


def _hy_spectrum(L, hy, filt, f_bias, c, s):
    tk = min(TK, L)
    tc = LCONV_TC
    nc = HY_WIDTH // tc
    out = pl.BlockSpec((tk, tc), lambda j, kk: (kk, j))
    mat = _resident((L, L), lambda j, kk: (0, 0))
    return pl.pallas_call(
        functools.partial(_spec_kernel, L=L, tk=tk),
        name=f"hy_spectrum{L}",
        grid=(nc, L // tk),
        in_specs=[
            pl.BlockSpec((L, tc), lambda j, kk: (0, j)),
            pl.BlockSpec((L, tc), lambda j, kk: (0, j + nc)),
            pl.BlockSpec((None, 1, tc), lambda j, kk: (hy, 0, j)),
            mat, mat,
        ],
        out_specs=[out, out, pl.BlockSpec((8, tc), lambda j, kk: (0, j))],
        out_shape=[jax.ShapeDtypeStruct((L, HY_WIDTH), F32)] * 2 + [jax.ShapeDtypeStruct((8, HY_WIDTH), F32)],
        scratch_shapes=[pltpu.VMEM((L, tc), BF16), pltpu.VMEM((L, tc), BF16)],
        compiler_params=_cparams(("arbitrary", "arbitrary"), 48 << 20),
    )(filt, filt, f_bias, c, s)


def _lconv_kernel(z_ref, m_ref, c_ref, s_ref, a_ref, b_ref, nq_ref, t_ref, y_scr, *, L, tk):
    kk = pl.program_id(2)
    rows, tc = y_scr.shape
    seqs = [pl.ds(s0, L) for s0 in range(0, rows, L)]

    @pl.when(kk == 0)
    def _():
        alt = _alt_sign((L, tc))
        for sq in seqs:
            z_nyq = jnp.sum(z_ref[sq, :].astype(F32) * alt, axis=0, keepdims=True)
            y_scr[sq, :] = alt * (z_nyq * nq_ref[0:1, :])

    k0 = pl.multiple_of(kk * tk, tk)
    a = a_ref[...]
    b = b_ref[...]
    for sq in seqs:
        zb = z_ref[sq, :]
        zr = jnp.dot(c_ref[pl.ds(k0, tk), :], zb, preferred_element_type=F32)
        zi = jnp.dot(s_ref[pl.ds(k0, tk), :], zb, preferred_element_type=F32)
        yr = (zr * a + zi * b).astype(BF16)
        yw = (zi * a - zr * b).astype(BF16)
        y_scr[sq, :] += (jnp.dot(c_ref[:, pl.ds(k0, tk)], yr, preferred_element_type=F32)
                         + jnp.dot(s_ref[:, pl.ds(k0, tk)], yw, preferred_element_type=F32))

    @pl.when(kk == pl.num_programs(2) - 1)
    def _():
        t_ref[...] = (y_scr[...] * m_ref[...].astype(F32)).astype(BF16)


def _hy_lconv(L, row_block0, n_row_blocks, z, m, c, s, a, b, nq):
    tk = min(TK, L)
    tc = LCONV_TC
    nc = HY_WIDTH // tc
    blk = pl.BlockSpec((SEQ_BLOCK, tc), lambda r, j, kk: (r + row_block0, j))
    coef = pl.BlockSpec((tk, tc), lambda r, j, kk: (kk, j))
    mat = _resident((L, L), lambda r, j, kk: (0, 0))
    return pl.pallas_call(
        functools.partial(_lconv_kernel, L=L, tk=tk),
        name=f"hy_lconv{L}",
        grid=(n_row_blocks, nc, L // tk),
        in_specs=[blk, blk, mat, mat, coef, coef, pl.BlockSpec((8, tc), lambda r, j, kk: (0, j))],
        out_specs=pl.BlockSpec((SEQ_BLOCK, tc), lambda r, j, kk: (r, j)),
        out_shape=jax.ShapeDtypeStruct((n_row_blocks * SEQ_BLOCK, HY_WIDTH), BF16),
        scratch_shapes=[pltpu.VMEM((SEQ_BLOCK, tc), F32)],
        compiler_params=_cparams(("arbitrary", "arbitrary", "arbitrary"), 56 << 20),
    )(z, m, c, s, a, b, nq)


def _out_kernel(tlo_ref, thi_ref, w_ref, xlo_ref, xhi_ref, g_ref, o_ref):
    i = pl.program_id(0)
    u = jnp.dot(_pick_rows(i, tlo_ref, thi_ref), w_ref[...], preferred_element_type=F32)
    o_ref[...] = _pick_rows(i, xlo_ref, xhi_ref) + g_ref[pl.ds(_cond_row(i), 1), :] * u


def _out_proj(name, t, w_out, x, mod, layer):
    tlo, thi, _ = _row_pair(t)
    xlo, xhi, _ = _row_pair(x)
    return pl.pallas_call(
        _out_kernel,
        name=name,
        grid=(N_TILES,),
        in_specs=[*_row_pair_specs(t, w_out.shape[1]),
                  _entry(w_out, layer // 2, resident=True),
                  *_row_pair_specs(x, D_MODEL),
                  _mod_spec(layer, 2)],
        out_specs=pl.BlockSpec((TM, D_MODEL), lambda i: (i, 0)),
        out_shape=jax.ShapeDtypeStruct((ROWS, D_MODEL), F32),
        compiler_params=_cparams(("arbitrary",), 32 << 20),
    )(tlo, thi, w_out, xlo, xhi, mod)


@functools.lru_cache(maxsize=None)
def _rope_consts():
    axis_dim = QK_ROPE // 2
    nf = axis_dim // 2
    inv = ROPE_THETA ** (-np.arange(0, axis_dim, 2, dtype=np.float64) / axis_dim)
    t = np.arange(DEC_SEQ)
    ang_r = (t // GRID_W)[:, None] * inv
    ang_c = (t % GRID_W)[:, None] * inv
    cos = np.ones((DEC_SEQ, HEAD_PAD))
    sin_up = np.zeros((DEC_SEQ, HEAD_PAD))
    sin_dn = np.zeros((DEC_SEQ, HEAD_PAD))
    for base, ang in ((QK_NOPE, ang_r), (QK_NOPE + axis_dim, ang_c)):
        cos[:, base:base + nf] = np.cos(ang)
        cos[:, base + nf:base + 2 * nf] = np.cos(ang)
        sin_up[:, base:base + nf] = -np.sin(ang)
        sin_dn[:, base + nf:base + 2 * nf] = np.sin(ang)
    return cos.astype(np.float32), sin_up.astype(np.float32), sin_dn.astype(np.float32)


ROPE_HALF = QK_ROPE // 4
Q_SCALE = math.log2(math.e) / math.sqrt(QK_NOPE + QK_ROPE)


def _mla_proj_kernel(x_ref, nw_ref, sh_ref, sc_ref, win_ref, qn_ref, kvn_ref, wqb_ref, wk_ref, wv_ref,
                     cos_ref, sup_ref, sdn_ref,
                     q_ref, k_ref, v_ref, sg_ref, ckv_ref, kpe_ref):
    i = pl.program_id(0)
    h = _modnorm(i, x_ref[...], nw_ref, sh_ref, sc_ref).astype(BF16)
    proj = jnp.dot(h, win_ref[...], preferred_element_type=F32)
    o_kv = Q_LORA
    o_gate = Q_LORA + KV_LORA
    o_pe = o_gate + N_HEADS * V_HEAD
    qn = _rms(proj[:, 0:o_kv], qn_ref[...]).astype(BF16)
    ckv = _rms(proj[:, o_kv:o_gate], kvn_ref[...])
    gate = proj[:, o_gate:o_pe]
    kpe = proj[:, o_pe:o_pe + HEAD_PAD]
    ckv_ref[...] = ckv
    kpe_ref[...] = kpe
    sg_ref[...] = _silu(gate)
    ckv_b = ckv.astype(BF16)
    v_ref[...] = jnp.dot(ckv_b, wv_ref[...], preferred_element_type=F32).astype(BF16)
    q = jnp.dot(qn, wqb_ref[...], preferred_element_type=F32)
    kn = jnp.dot(ckv_b, wk_ref[...], preferred_element_type=F32)

    latent = i >= TILES_P
    cos = jnp.where(latent, cos_ref[...], 1.0)
    sup = jnp.where(latent, sup_ref[...], 0.0)
    sdn = jnp.where(latent, sdn_ref[...], 0.0)

    def rope(u, scale):
        return (u * (cos * scale) + pltpu.roll(u, HEAD_PAD - ROPE_HALF, axis=1) * (sup * scale)
                + pltpu.roll(u, ROPE_HALF, axis=1) * (sdn * scale))

    kpe_r = rope(kpe, 1.0)
    for hd in range(N_HEADS):
        cols = slice(hd * HEAD_PAD, (hd + 1) * HEAD_PAD)
        q_ref[:, cols] = rope(q[:, cols], Q_SCALE).astype(BF16)
        k_ref[:, cols] = (kn[:, cols] + kpe_r).astype(BF16)


def _mla_proj(x, norm_w, mod, layer, w):
    cos, sup, sdn = (jnp.asarray(c) for c in _rope_consts())
    j = layer // 2
    hp = N_HEADS * HEAD_PAD
    rope_blk = pl.BlockSpec(
        (TM, HEAD_PAD), lambda i: (jnp.where(i >= TILES_P, (i - TILES_P) % TILES_PER_DEC_SEQ, 0), 0))
    tile = lambda n: pl.BlockSpec((TM, n), lambda i: (i, 0))
    return pl.pallas_call(
        _mla_proj_kernel,
        name="mla_proj",
        grid=(N_TILES,),
        in_specs=[
            tile(D_MODEL),
            _entry(norm_w, layer),
            _mod_spec(layer, 0), _mod_spec(layer, 1),
            _entry(w["w_in"], j, True), _entry(w["q_norm"], j), _entry(w["kv_norm"], j),
            _entry(w["w_qb"], j, True), _entry(w["w_k"], j, True), _entry(w["w_v"], j, True),
            rope_blk, rope_blk, rope_blk,
        ],
        out_specs=[tile(hp), tile(hp), tile(N_HEADS * V_HEAD),
                   tile(N_HEADS * V_HEAD), tile(KV_LORA), tile(HEAD_PAD)],
        out_shape=[
            jax.ShapeDtypeStruct((ROWS, hp), BF16),
            jax.ShapeDtypeStruct((ROWS, hp), BF16),
            jax.ShapeDtypeStruct((ROWS, N_HEADS * V_HEAD), BF16),
            jax.ShapeDtypeStruct((ROWS, N_HEADS * V_HEAD), F32),
            jax.ShapeDtypeStruct((ROWS, KV_LORA), F32),
            jax.ShapeDtypeStruct((ROWS, HEAD_PAD), F32),
        ],
        compiler_params=_cparams(("arbitrary",), 56 << 20),
    )(x, norm_w, mod, mod, w["w_in"], w["q_norm"], w["kv_norm"], w["w_qb"], w["w_k"], w["w_v"],
      cos, sup, sdn)


def _mla_ctx_kernel(ckv_ref, kpe_ref, wk_ref, wv_ref, k_ref, v_ref):
    ckv_b = ckv_ref[...].astype(BF16)
    kn = jnp.dot(ckv_b, wk_ref[...], preferred_element_type=F32)
    v_ref[...] = jnp.dot(ckv_b, wv_ref[...], preferred_element_type=F32).astype(BF16)
    kpe = kpe_ref[...]
    for hd in range(N_HEADS):
        cols = slice(hd * HEAD_PAD, (hd + 1) * HEAD_PAD)
        k_ref[:, cols] = (kn[:, cols] + kpe).astype(BF16)


def _mla_ctx(j, ckv_ctx, kpe_ctx, w):
    hp = N_HEADS * HEAD_PAD
    rows = DEC_BATCH * PAST_LEN
    tile = lambda n: pl.BlockSpec((PAST_LEN, n), lambda i: (i, 0))
    cache = lambda n: pl.BlockSpec((None, None, PAST_LEN, n), lambda i: (i, j, 0, 0))
    return pl.pallas_call(
        _mla_ctx_kernel,
        name="mla_ctx",
        grid=(DEC_BATCH,),
        in_specs=[cache(KV_LORA), cache(HEAD_PAD), _entry(w["w_k"], j, True), _entry(w["w_v"], j, True)],
        out_specs=[tile(hp), tile(N_HEADS * V_HEAD)],
        out_shape=[jax.ShapeDtypeStruct((rows, hp), BF16),
                   jax.ShapeDtypeStruct((rows, N_HEADS * V_HEAD), BF16)],
        compiler_params=_cparams(("arbitrary",), 32 << 20),
    )(ckv_ctx, kpe_ctx, w["w_k"], w["w_v"])


NT_DIMS = (((1,), (1,)), ((), ()))
ATTN_PAIRS = 4
ATTN_TQ = 256


def _attn_kernel(*refs, n_pairs, n_groups, has_ctx, final):
    refs = list(refs)
    q_ref, k_ref, v_ref = refs[:3]
    del refs[:3]
    if has_ctx:
        kc_ref, vc_ref = refs[:2]
        del refs[:2]
    sg_ref, wo_ref, x_ref, g_ref = refs[:4]
    del refs[:4]
    if final:
        fg_ref = refs.pop(0)
    out_ref, o_scr = refs[:2]
    tq = q_ref.shape[0]
    low_half = lax.broadcasted_iota(jnp.int32, (tq, LANES), 1) < V_HEAD
    for p in range(n_pairs):
        vcols = slice(p * LANES, (p + 1) * LANES)
        vp = v_ref[:, vcols]
        outs = []
        for hh in range(2):
            cols = slice((2 * p + hh) * HEAD_PAD, (2 * p + hh + 1) * HEAD_PAD)
            q = q_ref[:, cols]
            s = lax.dot_general(q, k_ref[:, cols], NT_DIMS, preferred_element_type=F32)
            mx = jnp.max(s, axis=-1, keepdims=True)
            if has_ctx:
                sc = lax.dot_general(q, kc_ref[:, cols], NT_DIMS, preferred_element_type=F32)
                mx = jnp.maximum(mx, jnp.max(sc, axis=-1, keepdims=True))
            e = jnp.exp2(s - mx)
            den = jnp.sum(e, axis=-1, keepdims=True)
            pv = jnp.dot(e.astype(BF16), vp, preferred_element_type=F32)
            if has_ctx:
                ec = jnp.exp2(sc - mx)
                den = den + jnp.sum(ec, axis=-1, keepdims=True)
                pv = pv + jnp.dot(ec.astype(BF16), vc_ref[:, vcols], preferred_element_type=F32)
            outs.append(pv / den)
        o = jnp.where(low_half, outs[0], outs[1])
        o_scr[:, vcols] = (o * sg_ref[:, vcols]).astype(BF16)

    u = jnp.dot(o_scr[...], wo_ref[...], preferred_element_type=F32)
    cond = 1 + pl.program_id(0) if has_ctx else 0

    def finish(u_all):
        xn = x_ref[...] + g_ref[pl.ds(cond, 1), :] * u_all
        out_ref[...] = _rms(xn, fg_ref[...]) if final else xn

    if n_groups == 1:
        finish(u)
    else:
        acc_scr = refs[2]
        grp = pl.program_id(2)

        @pl.when(grp == 0)
        def _():
            acc_scr[...] = u

        @pl.when((grp > 0) & (grp < n_groups - 1))
        def _():
            acc_scr[...] += u

        @pl.when(grp == n_groups - 1)
        def _():
            finish(acc_scr[...] + u)


def _attn_prompt(q, k, v, sg, w_o, x, mod, layer, final_g):
    hp = N_HEADS * HEAD_PAD
    nv = N_HEADS * V_HEAD
    xlo, _, _ = _row_pair(x)
    seq = lambda n: pl.BlockSpec((SEQ, n), lambda b: (b, 0))
    final = final_g is not None
    return pl.pallas_call(
        functools.partial(_attn_kernel, n_pairs=N_HEADS // 2, n_groups=1, has_ctx=False, final=final),
        name="attn_prompt",
        grid=(BATCH,),
        in_specs=[seq(hp), seq(hp), seq(nv), seq(nv), _entry(w_o, layer // 2, True), seq(D_MODEL),
                  _mod_spec(layer, 2)] + ([pl.BlockSpec((1, D_MODEL), lambda b: (0, 0))] if final else []),
        out_specs=seq(D_MODEL),
        out_shape=jax.ShapeDtypeStruct((ROWS_P, D_MODEL), F32),
        scratch_shapes=[pltpu.VMEM((SEQ, nv), BF16)],
        compiler_params=_cparams(("arbitrary",), 40 << 20),
    )(q, k, v, sg, w_o, xlo, mod, *([final_g] if final else []))


def _attn_latent(q, k, v, kc, vc, sg, w_o, x, mod, layer, final_g):
    tq = ATTN_TQ
    npair = ATTN_PAIRS
    ngrp = N_HEADS // 2 // npair
    tiles = DEC_SEQ // tq
    q0 = ROWS_P // tq
    s0 = ROWS_P // DEC_SEQ
    wide = npair * 2 * HEAD_PAD
    narrow = npair * LANES
    _, xhi, base = _row_pair(x)
    x0 = base * TM // tq
    final = final_g is not None
    qrow = lambda b, t, p: (q0 + b * tiles + t, p)
    return pl.pallas_call(
        functools.partial(_attn_kernel, n_pairs=npair, n_groups=ngrp, has_ctx=True, final=final),
        name="attn_latent",
        grid=(DEC_BATCH, tiles, ngrp),
        in_specs=[pl.BlockSpec((tq, wide), qrow),
                  pl.BlockSpec((DEC_SEQ, wide), lambda b, t, p: (s0 + b, p)),
                  pl.BlockSpec((DEC_SEQ, narrow), lambda b, t, p: (s0 + b, p)),
                  pl.BlockSpec((PAST_LEN, wide), lambda b, t, p: (b, p)),
                  pl.BlockSpec((PAST_LEN, narrow), lambda b, t, p: (b, p)),
                  pl.BlockSpec((tq, narrow), qrow),
                  pl.BlockSpec((None, narrow, D_MODEL), lambda b, t, p: (layer // 2, p, 0)),
                  pl.BlockSpec((tq, D_MODEL), lambda b, t, p: (x0 + b * tiles + t, 0)),
                  _mod_spec(layer, 2)] + ([pl.BlockSpec((1, D_MODEL), lambda b, t, p: (0, 0))] if final else []),
        out_specs=pl.BlockSpec((tq, D_MODEL), lambda b, t, p: (b * tiles + t, 0)),
        out_shape=jax.ShapeDtypeStruct((ROWS_S, D_MODEL), F32),
        scratch_shapes=[pltpu.VMEM((tq, narrow), BF16), pltpu.VMEM((tq, D_MODEL), F32)],
        compiler_params=_cparams(("arbitrary", "arbitrary", "arbitrary"), 48 << 20),
    )(q, k, v, kc, vc, sg, w_o, xhi, mod, *([final_g] if final else []))


def _mla_weights(w_in, q_norm, w_qb, kv_norm, w_kvb, w_o):
    n = w_in.shape[0]
    o_pe = Q_LORA + KV_LORA
    o_gate = o_pe + QK_ROPE
    zeros = lambda k: jnp.zeros((n, D_MODEL, k), F32)
    w_in_r = jnp.concatenate(
        [w_in[..., :o_pe], w_in[..., o_gate:], zeros(QK_NOPE), w_in[..., o_pe:o_gate],
         zeros(HEAD_PAD - QK_NOPE - QK_ROPE)], axis=-1)
    qb = w_qb.reshape(n, Q_LORA, N_HEADS, QK_NOPE + QK_ROPE)
    qb = jnp.pad(qb, ((0, 0), (0, 0), (0, 0), (0, HEAD_PAD - QK_NOPE - QK_ROPE)))
    kvb = w_kvb.reshape(n, KV_LORA, N_HEADS, QK_NOPE + V_HEAD)
    wk = jnp.pad(kvb[..., :QK_NOPE], ((0, 0), (0, 0), (0, 0), (0, HEAD_PAD - QK_NOPE)))
    wv = kvb[..., QK_NOPE:]
    return {
        "w_in": w_in_r.astype(BF16),
        "q_norm": q_norm[:, None, :],
        "kv_norm": kv_norm[:, None, :],
        "w_qb": qb.reshape(n, Q_LORA, N_HEADS * HEAD_PAD).astype(BF16),
        "w_k": wk.reshape(n, KV_LORA, N_HEADS * HEAD_PAD).astype(BF16),
        "w_v": wv.reshape(n, KV_LORA, N_HEADS * V_HEAD).astype(BF16),
        "w_o": w_o.astype(BF16),
    }


def kernel(x_prompt, x_sample, cache_ckv, cache_kpe, c, c_ctx, norm_w, ada_w, ada_b, hy_w_in, hy_conv_w, hy_conv_b, hy_f_w1, hy_f_b1, hy_f_freq, hy_f_w2, hy_f_b2, hy_f_w3, hy_f_bias, hy_w_out, mla_w_in, mla_q_norm, mla_w_qb, mla_kv_norm, mla_w_kvb, mla_w_o, final_norm):
    x = (x_prompt.reshape(ROWS_P, D_MODEL), x_sample.reshape(ROWS_S, D_MODEL))
    cond = jnp.concatenate([c_ctx[None, :], c, jnp.zeros((N_COND - 1 - DEC_BATCH, D_MODEL), F32)], axis=0)
    mod = _ada_all(cond, ada_w, ada_b)

    nw = norm_w[:, None, :]
    hy_w_in_b = hy_w_in.astype(BF16)
    hy_w_out_b = hy_w_out.astype(BF16)
    hy_conv_b3 = hy_conv_b[:, None, :]
    f_w1 = jnp.pad(hy_f_w1, ((0, 0), (0, FILTER_HIDDEN - FILTER_EMB), (0, 0)))
    f_b1, f_freq, f_b2, f_bias = (a[:, None, :] for a in (hy_f_b1, hy_f_freq, hy_f_b2, hy_f_bias))
    w = _mla_weights(mla_w_in, mla_q_norm, mla_w_qb, mla_kv_norm, mla_w_kvb, mla_w_o)
    kpe_ctx = jnp.pad(cache_kpe, ((0, 0), (0, 0), (0, 0), (QK_NOPE, HEAD_PAD - QK_NOPE - QK_ROPE)))

    new_ckv, new_kpe = [], []
    for layer in range(DEPTH):
        j = layer // 2
        if layer % 2 == 0:
            z, m = _hy_in(x, nw, mod, layer, hy_w_in_b, hy_conv_w, hy_conv_b3)
            t = []
            for L, blk0, nblk in ((SEQ, 0, ROWS_P // SEQ_BLOCK), (DEC_SEQ, ROWS_P // SEQ_BLOCK, ROWS_S // SEQ_BLOCK)):
                filt = _hy_filter(L, j, f_w1, f_b1, f_freq, hy_f_w2, f_b2, hy_f_w3)
                cm, sm = _dft_mats(L)
                a, b, nq = _hy_spectrum(L, j, filt, f_bias, cm, sm)
                t.append(_hy_lconv(L, blk0, nblk, z, m, cm, sm, a, b, nq))
            x = _out_proj("hy_out", tuple(t), hy_w_out_b, x, mod, layer)
        else:
            q, k, v, sg, ckv, kpe = _mla_proj(x, nw, mod, layer, w)
            kc, vc = _mla_ctx(j, cache_ckv, kpe_ctx, w)
            final_g = final_norm[None, :] if layer == DEPTH - 1 else None
            x = (_attn_prompt(q, k, v, sg, w["w_o"], x, mod, layer, final_g),
                 _attn_latent(q, k, v, kc, vc, sg, w["w_o"], x, mod, layer, final_g))
            new_ckv.append(ckv[:ROWS_P].reshape(BATCH, SEQ, KV_LORA))
            new_kpe.append(kpe[:ROWS_P, QK_NOPE:QK_NOPE + QK_ROPE].reshape(BATCH, SEQ, QK_ROPE))

    assert DEPTH % 2 == 0
    y_prompt, y_sample = x
    return (y_prompt.reshape(BATCH, SEQ, D_MODEL), y_sample.reshape(DEC_BATCH, DEC_SEQ, D_MODEL),
            jnp.stack(new_ckv, axis=1), jnp.stack(new_kpe, axis=1))
```

```python
import functools
import math

import numpy as np
import jax
import jax.numpy as jnp
from jax import lax
from jax.experimental import pallas as pl
from jax.experimental.pallas import tpu as pltpu

F32 = jnp.float32
BF16 = jnp.bfloat16

D_MODEL = 1024
BATCH = 16
SEQ = 256
DEPTH = 4
DEC_BATCH = 2
DEC_SEQ = 2048
PAST_LEN = 512
GRID_W = 64
EPS = 1e-6
HY_WIDTH = D_MODEL
FILTER_BANDS = 16
FILTER_EMB = 1 + 2 * FILTER_BANDS
FILTER_HIDDEN = 64
FAST_DECAY_PCT = 0.3
SLOW_DECAY_PCT = 1.5
DECAY_TARGET = 1e-2
N_HEADS = 16
Q_LORA = 384
KV_LORA = 256
QK_NOPE = 64
QK_ROPE = 32
V_HEAD = 64
ROPE_THETA = 10000.0

LANES = 128
MXU_TILE = 256
HEAD_PAD = LANES
ROWS_P = BATCH * SEQ
ROWS_S = DEC_BATCH * DEC_SEQ
ROWS = ROWS_P + ROWS_S
TM = 512
N_TILES = ROWS // TM
TILES_P = ROWS_P // TM
TILES_PER_DEC_SEQ = DEC_SEQ // TM
N_COND = 8
SEQ_BLOCK = 2048
TK = 512
LCONV_TC = 512
VMEM_CAP = 56 * 1024 * 1024


def _cparams(sem, vmem_bytes):
    return pltpu.CompilerParams(dimension_semantics=sem, vmem_limit_bytes=min(int(vmem_bytes), VMEM_CAP))


def _resident(shape, index_map):
    return pl.BlockSpec(shape, index_map, pipeline_mode=pl.Buffered(1))


def _entry(arr, idx, resident=False):
    zeros = (0,) * (arr.ndim - 1)
    return pl.BlockSpec((None,) + arr.shape[1:], lambda *_: (idx,) + zeros,
                        pipeline_mode=pl.Buffered(1) if resident else None)


def _cond_row(i):
    return jnp.where(i < TILES_P, 0, 1 + (i - TILES_P) // TILES_PER_DEC_SEQ)


def _silu(x):
    return x * jax.nn.sigmoid(x)


def _rms(x, g):
    return x * lax.rsqrt(jnp.mean(x * x, axis=-1, keepdims=True) + EPS) * g


def _row_pair(x):
    if isinstance(x, tuple):
        return x[0], x[1], 0
    return x, x, TILES_P


def _row_pair_specs(x, n):
    _, _, base = _row_pair(x)
    lo = pl.BlockSpec((TM, n), lambda i, *_: (jnp.minimum(i, TILES_P - 1), 0))
    hi = pl.BlockSpec((TM, n), lambda i, *_: (jnp.maximum(i - TILES_P, 0) + base, 0))
    return lo, hi


def _pick_rows(i, lo_ref, hi_ref):
    return jnp.where(i < TILES_P, lo_ref[...], hi_ref[...])


def _ada_kernel(cond_ref, w_ref, b_ref, o_ref):
    s = _silu(cond_ref[...]).astype(BF16)
    o_ref[...] = jnp.dot(s, w_ref[...].astype(BF16), preferred_element_type=F32) + b_ref[...]


def _ada_all(cond, ada_w, ada_b):
    tn = 1024
    return pl.pallas_call(
        _ada_kernel,
        name="ada",
        grid=(DEPTH, 3 * D_MODEL // tn),
        in_specs=[
            pl.BlockSpec((N_COND, D_MODEL), lambda l, j: (0, 0)),
            pl.BlockSpec((None, D_MODEL, tn), lambda l, j: (l, 0, j)),
            pl.BlockSpec((None, 1, tn), lambda l, j: (l, 0, j)),
        ],
        out_specs=pl.BlockSpec((None, N_COND, tn), lambda l, j: (l, 0, j)),
        out_shape=jax.ShapeDtypeStruct((DEPTH, N_COND, 3 * D_MODEL), F32),
        compiler_params=_cparams(("arbitrary", "arbitrary"), 32 << 20),
    )(cond, ada_w, ada_b.reshape(DEPTH, 1, 3 * D_MODEL))


def _mod_spec(layer, part):
    return pl.BlockSpec((None, N_COND, D_MODEL), lambda i, *_: (layer, 0, part))


def _modnorm(i, x, nw_ref, sh_ref, sc_ref):
    c = _cond_row(i)
    return _rms(x, nw_ref[...]) * (1.0 + sc_ref[pl.ds(c, 1), :]) + sh_ref[pl.ds(c, 1), :]


HALO = 16
HALO_PER_TILE = TM // HALO


def _halo_specs(x):
    _, _, base = _row_pair(x)
    r = HALO_PER_TILE
    lo_last = TILES_P * r - 1
    hi_last = (N_TILES - TILES_P) * r - 1
    lo_tile = lambda i: jnp.minimum(i, TILES_P - 1)
    hi_tile = lambda i: jnp.maximum(i - TILES_P, 0)
    blk = lambda f: pl.BlockSpec((HALO, D_MODEL), lambda i: (f(i), 0))
    return [
        blk(lambda i: jnp.maximum(lo_tile(i) * r - 1, 0)),
        blk(lambda i: jnp.maximum(hi_tile(i) * r - 1, 0) + base * r),
        blk(lambda i: jnp.minimum((lo_tile(i) + 1) * r, lo_last)),
        blk(lambda i: jnp.minimum((hi_tile(i) + 1) * r, hi_last) + base * r),
    ]


def _hy_in_kernel(xlo_ref, xhi_ref, plo_ref, phi_ref, nlo_ref, nhi_ref, nw_ref, sh_ref, sc_ref, w_ref,
                  cw_ref, cb_ref, z_ref, m_ref, h_scr, u0_scr, u1_scr, u2_scr):
    i = pl.program_id(0)
    width = z_ref.shape[1]
    norm = lambda lo_ref, hi_ref: _modnorm(i, _pick_rows(i, lo_ref, hi_ref), nw_ref, sh_ref, sc_ref).astype(BF16)
    h_scr[0:HALO, :] = norm(plo_ref, phi_ref)
    h_scr[HALO:HALO + TM, :] = norm(xlo_ref, xhi_ref)
    h_scr[HALO + TM:HALO + TM + HALO, :] = norm(nlo_ref, nhi_ref)

    seq_mask = jnp.where(i < TILES_P, SEQ - 1, DEC_SEQ - 1)
    pos = (lax.broadcasted_iota(jnp.int32, (TM, width), 0) + i * TM) & seq_mask
    first = pos == 0
    last = pos == seq_mask

    def conv(g, u_scr):
        cols = slice(g * width, (g + 1) * width)
        for c0 in range(0, width, MXU_TILE):
            u_scr[:, c0:c0 + MXU_TILE] = jnp.dot(
                h_scr[...], w_ref[:, g * width + c0:g * width + c0 + MXU_TILE], preferred_element_type=F32)
        prev = jnp.where(first, 0.0, u_scr[HALO - 1:HALO - 1 + TM, :])
        nxt = jnp.where(last, 0.0, u_scr[HALO + 1:HALO + 1 + TM, :])
        return (cb_ref[:, cols] + prev * cw_ref[0:1, cols] + u_scr[HALO:HALO + TM, :] * cw_ref[1:2, cols]
                + nxt * cw_ref[2:3, cols])

    z_ref[...] = (conv(2, u2_scr) * conv(1, u1_scr)).astype(BF16)
    gate = jnp.dot(h_scr[HALO:HALO + TM, :], w_ref[:, 3 * width:4 * width], preferred_element_type=F32)
    m_ref[...] = (conv(0, u0_scr) * _silu(gate)).astype(BF16)


def _hy_in(x, norm_w, mod, layer, w_in, conv_w, conv_b):
    j = layer // 2
    xlo, xhi, _ = _row_pair(x)
    rows_h = TM + 2 * HALO
    out = pl.BlockSpec((TM, HY_WIDTH), lambda i: (i, 0))
    return pl.pallas_call(
        _hy_in_kernel,
        name="hy_in",
        grid=(N_TILES,),
        in_specs=[
            *_row_pair_specs(x, D_MODEL),
            *_halo_specs(x),
            _entry(norm_w, layer),
            _mod_spec(layer, 0),
            _mod_spec(layer, 1),
            _entry(w_in, j, resident=True),
            _entry(conv_w, j),
            _entry(conv_b, j),
        ],
        out_specs=[out, out],
        out_shape=[jax.ShapeDtypeStruct((ROWS, HY_WIDTH), BF16)] * 2,
        scratch_shapes=[pltpu.VMEM((rows_h, D_MODEL), BF16)] + [pltpu.VMEM((rows_h, HY_WIDTH), F32)] * 3,
        compiler_params=_cparams(("arbitrary",), 48 << 20),
    )(xlo, xhi, xlo, xhi, xlo, xhi, norm_w, mod, mod, w_in, conv_w, conv_b)


@functools.lru_cache(maxsize=None)
def _filter_consts(L):
    t = np.linspace(0.0, 1.0, L)[:, None]
    w = (2.0 * math.pi / L) * np.arange(L)[:, None]
    bands = np.linspace(1e-4, FILTER_BANDS - 1, FILTER_BANDS)[None, :]
    emb = np.concatenate([t, np.cos(bands * w), -np.sin(bands * w)], axis=-1)
    emb = np.pad(emb, ((0, 0), (0, FILTER_HIDDEN - FILTER_EMB)))
    max_decay = math.log(DECAY_TARGET) / FAST_DECAY_PCT
    min_decay = math.log(DECAY_TARGET) / SLOW_DECAY_PCT
    deltas = np.abs(np.linspace(min_decay, max_decay, HY_WIDTH))
    deltas = np.concatenate([deltas, deltas])[None, :]
    return emb.astype(np.float32), t.astype(np.float32), deltas.astype(np.float32)


def _filter_kernel(emb_ref, t_ref, w1_ref, b1_ref, fr_ref, w2_ref, b2_ref, w3_ref, dl_ref, o_ref, h_scr):
    @pl.when(pl.program_id(0) == 0)
    def _():
        hi = lax.Precision.HIGHEST
        fr = fr_ref[...]
        h = jnp.sin(fr * (jnp.dot(emb_ref[...], w1_ref[...], precision=hi, preferred_element_type=F32) + b1_ref[...]))
        h = jnp.sin(fr * (jnp.dot(h, w2_ref[...], precision=hi, preferred_element_type=F32) + b2_ref[...]))
        h_scr[...] = h.astype(BF16)

    h = jnp.dot(h_scr[...], w3_ref[...].astype(BF16), preferred_element_type=F32)
    h = h * jnp.exp(-t_ref[...] * dl_ref[...])
    o_ref[...] = h / jnp.sum(jnp.abs(h), axis=0, keepdims=True)


def _hy_filter(L, hy, w1, b1, freq, w2, b2, w3):
    emb, t, deltas = _filter_consts(L)
    tcf = 512
    full = lambda shape: pl.BlockSpec(shape, lambda c: (0, 0))
    return pl.pallas_call(
        _filter_kernel,
        name=f"hy_filter{L}",
        grid=(2 * HY_WIDTH // tcf,),
        in_specs=[
            full((L, FILTER_HIDDEN)), full((L, 1)), _entry(w1, hy), _entry(b1, hy), _entry(freq, hy),
            _entry(w2, hy), _entry(b2, hy),
            pl.BlockSpec((None, FILTER_HIDDEN, tcf), lambda c: (hy, 0, c)),
            pl.BlockSpec((1, tcf), lambda c: (0, c)),
        ],
        out_specs=pl.BlockSpec((L, tcf), lambda c: (0, c)),
        out_shape=jax.ShapeDtypeStruct((L, 2 * HY_WIDTH), F32),
        scratch_shapes=[pltpu.VMEM((L, FILTER_HIDDEN), BF16)],
        compiler_params=_cparams(("arbitrary",), 40 << 20),
    )(jnp.asarray(emb), jnp.asarray(t), w1, b1, freq, w2, b2, w3, jnp.asarray(deltas))


@functools.lru_cache(maxsize=None)
def _dft_consts(L):
    k = np.arange(L, dtype=np.int64)
    ang = (np.outer(k, k) % (2 * L)).astype(np.float64) * (math.pi / L)
    return np.cos(ang).astype(np.float32), np.sin(ang).astype(np.float32)


def _dft_mats(L):
    c, s = _dft_consts(L)
    return jnp.asarray(c).astype(BF16), jnp.asarray(s).astype(BF16)


def _alt_sign(shape):
    return (1 - 2 * (lax.broadcasted_iota(jnp.int32, shape, 0) & 1)).astype(F32)


def _spec_kernel(ff_ref, fb_ref, bias_ref, c_ref, s_ref, a_ref, b_ref, nq_ref, f_scr, d_scr, *, L, tk):
    kk = pl.program_id(1)
    inv_n = 1.0 / (2.0 * L)
    bias = bias_ref[...]

    @pl.when(kk == 0)
    def _():
        ff = ff_ref[...]
        fb = fb_ref[...]
        f = ff + fb
        f_scr[...] = f.astype(BF16)
        d_scr[...] = (fb - ff).astype(BF16)
        nyq = (jnp.sum(f * _alt_sign(f.shape), axis=0, keepdims=True) + bias) * inv_n
        nq_ref[...] = jnp.broadcast_to(nyq, nq_ref.shape)

    k0 = pl.multiple_of(kk * tk, tk)
    hre = jnp.dot(c_ref[pl.ds(k0, tk), :], f_scr[...], preferred_element_type=F32) + bias
    him = jnp.dot(s_ref[pl.ds(k0, tk), :], d_scr[...], preferred_element_type=F32)
    dc_row = (lax.broadcasted_iota(jnp.int32, hre.shape, 0) == 0) & (kk == 0)
    a_ref[...] = jnp.where(dc_row, hre * inv_n, hre * (2.0 * inv_n))
    b_ref[...] = him * (2.0 * inv_n)


def _hy_spectrum(L, hy, filt, f_bias, c, s):
    tk = min(TK, L)
    tc = LCONV_TC
    nc = HY_WIDTH // tc
    out = pl.BlockSpec((tk, tc), lambda j, kk: (kk, j))
    mat = _resident((L, L), lambda j, kk: (0, 0))
    return pl.pallas_call(
        functools.partial(_spec_kernel, L=L, tk=tk),
        name=f"hy_spectrum{L}",
        grid=(nc, L // tk),
        in_specs=[
            pl.BlockSpec((L, tc), lambda j, kk: (0, j)),
            pl.BlockSpec((L, tc), lambda j, kk: (0, j + nc)),
            pl.BlockSpec((None, 1, tc), lambda j, kk: (hy, 0, j)),
            mat, mat,
        ],
        out_specs=[out, out, pl.BlockSpec((8, tc), lambda j, kk: (0, j))],
        out_shape=[jax.ShapeDtypeStruct((L, HY_WIDTH), F32)] * 2 + [jax.ShapeDtypeStruct((8, HY_WIDTH), F32)],
        scratch_shapes=[pltpu.VMEM((L, tc), BF16), pltpu.VMEM((L, tc), BF16)],
        compiler_params=_cparams(("arbitrary", "arbitrary"), 48 << 20),
    )(filt, filt, f_bias, c, s)


def _lconv_kernel(z_ref, m_ref, c_ref, s_ref, a_ref, b_ref, nq_ref, t_ref, y_scr, *, L, tk):
    kk = pl.program_id(2)
    rows, tc = y_scr.shape
    seqs = [pl.ds(s0, L) for s0 in range(0, rows, L)]

    @pl.when(kk == 0)
    def _():
        alt = _alt_sign((L, tc))
        for sq in seqs:
            z_nyq = jnp.sum(z_ref[sq, :].astype(F32) * alt, axis=0, keepdims=True)
            y_scr[sq, :] = alt * (z_nyq * nq_ref[0:1, :])

    k0 = pl.multiple_of(kk * tk, tk)
    a = a_ref[...]
    b = b_ref[...]
    for sq in seqs:
        zb = z_ref[sq, :]
        zr = jnp.dot(c_ref[pl.ds(k0, tk), :], zb, preferred_element_type=F32)
        zi = jnp.dot(s_ref[pl.ds(k0, tk), :], zb, preferred_element_type=F32)
        yr = (zr * a + zi * b).astype(BF16)
        yw = (zi * a - zr * b).astype(BF16)
        y_scr[sq, :] += (jnp.dot(c_ref[:, pl.ds(k0, tk)], yr, preferred_element_type=F32)
                         + jnp.dot(s_ref[:, pl.ds(k0, tk)], yw, preferred_element_type=F32))

    @pl.when(kk == pl.num_programs(2) - 1)
    def _():
        t_ref[...] = (y_scr[...] * m_ref[...].astype(F32)).astype(BF16)


def _hy_lconv(L, row_block0, n_row_blocks, z, m, c, s, a, b, nq):
    tk = min(TK, L)
    tc = LCONV_TC
    nc = HY_WIDTH // tc
    blk = pl.BlockSpec((SEQ_BLOCK, tc), lambda r, j, kk: (r + row_block0, j))
    coef = pl.BlockSpec((tk, tc), lambda r, j, kk: (kk, j))
    mat = _resident((L, L), lambda r, j, kk: (0, 0))
    return pl.pallas_call(
        functools.partial(_lconv_kernel, L=L, tk=tk),
        name=f"hy_lconv{L}",
        grid=(n_row_blocks, nc, L // tk),
        in_specs=[blk, blk, mat, mat, coef, coef, pl.BlockSpec((8, tc), lambda r, j, kk: (0, j))],
        out_specs=pl.BlockSpec((SEQ_BLOCK, tc), lambda r, j, kk: (r, j)),
        out_shape=jax.ShapeDtypeStruct((n_row_blocks * SEQ_BLOCK, HY_WIDTH), BF16),
        scratch_shapes=[pltpu.VMEM((SEQ_BLOCK, tc), F32)],
        compiler_params=_cparams(("arbitrary", "arbitrary", "arbitrary"), 56 << 20),
    )(z, m, c, s, a, b, nq)


@functools.lru_cache(maxsize=None)
def _rope_consts():
    axis_dim = QK_ROPE // 2
    nf = axis_dim // 2
    inv = ROPE_THETA ** (-np.arange(0, axis_dim, 2, dtype=np.float64) / axis_dim)
    t = np.arange(DEC_SEQ)
    ang_r = (t // GRID_W)[:, None] * inv
    ang_c = (t % GRID_W)[:, None] * inv
    cos = np.ones((DEC_SEQ, HEAD_PAD))
    sin_up = np.zeros((DEC_SEQ, HEAD_PAD))
    sin_dn = np.zeros((DEC_SEQ, HEAD_PAD))
    for base, ang in ((QK_NOPE, ang_r), (QK_NOPE + axis_dim, ang_c)):
        cos[:, base:base + nf] = np.cos(ang)
        cos[:, base + nf:base + 2 * nf] = np.cos(ang)
        sin_up[:, base:base + nf] = -np.sin(ang)
        sin_dn[:, base + nf:base + 2 * nf] = np.sin(ang)
    return cos.astype(np.float32), sin_up.astype(np.float32), sin_dn.astype(np.float32)


ROPE_HALF = QK_ROPE // 4
Q_SCALE = math.log2(math.e) / math.sqrt(QK_NOPE + QK_ROPE)


def _mla_proj_kernel(tlo_ref, thi_ref, wout_ref, xlo_ref, xhi_ref, gprev_ref,
                     nw_ref, sh_ref, sc_ref, win_ref, qn_ref, kvn_ref, wqb_ref, wk_ref, wv_ref,
                     cos_ref, sup_ref, sdn_ref,
                     xn_ref, q_ref, k_ref, v_ref, sg_ref, ckv_ref, kpe_ref):
    i = pl.program_id(0)
    u = jnp.dot(_pick_rows(i, tlo_ref, thi_ref), wout_ref[...], preferred_element_type=F32)
    x = _pick_rows(i, xlo_ref, xhi_ref) + gprev_ref[pl.ds(_cond_row(i), 1), :] * u
    xn_ref[...] = x
    h = _modnorm(i, x, nw_ref, sh_ref, sc_ref).astype(BF16)
    proj = jnp.dot(h, win_ref[...], preferred_element_type=F32)
    o_kv = Q_LORA
    o_gate = Q_LORA + KV_LORA
    o_pe = o_gate + N_HEADS * V_HEAD
    qn = _rms(proj[:, 0:o_kv], qn_ref[...]).astype(BF16)
    ckv = _rms(proj[:, o_kv:o_gate], kvn_ref[...])
    gate = proj[:, o_gate:o_pe]
    kpe = proj[:, o_pe:o_pe + HEAD_PAD]
    ckv_ref[...] = ckv
    kpe_ref[...] = kpe
    sg_ref[...] = _silu(gate)
    ckv_b = ckv.astype(BF16)
    v_ref[...] = jnp.dot(ckv_b, wv_ref[...], preferred_element_type=F32).astype(BF16)
    q = jnp.dot(qn, wqb_ref[...], preferred_element_type=F32)
    kn = jnp.dot(ckv_b, wk_ref[...], preferred_element_type=F32)

    latent = i >= TILES_P
    cos = jnp.where(latent, cos_ref[...], 1.0)
    sup = jnp.where(latent, sup_ref[...], 0.0)
    sdn = jnp.where(latent, sdn_ref[...], 0.0)

    def rope(u, scale):
        return (u * (cos * scale) + pltpu.roll(u, HEAD_PAD - ROPE_HALF, axis=1) * (sup * scale)
                + pltpu.roll(u, ROPE_HALF, axis=1) * (sdn * scale))

    kpe_r = rope(kpe, 1.0)
    for hd in range(N_HEADS):
        cols = slice(hd * HEAD_PAD, (hd + 1) * HEAD_PAD)
        q_ref[:, cols] = rope(q[:, cols], Q_SCALE).astype(BF16)
        k_ref[:, cols] = (kn[:, cols] + kpe_r).astype(BF16)


def _mla_proj(t, w_out, x, norm_w, mod, layer, w):
    cos, sup, sdn = (jnp.asarray(c) for c in _rope_consts())
    j = layer // 2
    hp = N_HEADS * HEAD_PAD
    tlo, thi, _ = _row_pair(t)
    xlo, xhi, _ = _row_pair(x)
    rope_blk = pl.BlockSpec(
        (TM, HEAD_PAD), lambda i: (jnp.where(i >= TILES_P, (i - TILES_P) % TILES_PER_DEC_SEQ, 0), 0))
    tile = lambda n: pl.BlockSpec((TM, n), lambda i: (i, 0))
    return pl.pallas_call(
        _mla_proj_kernel,
        name="mla_proj",
        grid=(N_TILES,),
        in_specs=[
            *_row_pair_specs(t, HY_WIDTH),
            _entry(w_out, (layer - 1) // 2, True),
            *_row_pair_specs(x, D_MODEL),
            _mod_spec(layer - 1, 2),
            _entry(norm_w, layer),
            _mod_spec(layer, 0), _mod_spec(layer, 1),
            _entry(w["w_in"], j, True), _entry(w["q_norm"], j), _entry(w["kv_norm"], j),
            _entry(w["w_qb"], j, True), _entry(w["w_k"], j, True), _entry(w["w_v"], j, True),
            rope_blk, rope_blk, rope_blk,
        ],
        out_specs=[tile(D_MODEL), tile(hp), tile(hp), tile(N_HEADS * V_HEAD),
                   tile(N_HEADS * V_HEAD), tile(KV_LORA), tile(HEAD_PAD)],
        out_shape=[
            jax.ShapeDtypeStruct((ROWS, D_MODEL), F32),
            jax.ShapeDtypeStruct((ROWS, hp), BF16),
            jax.ShapeDtypeStruct((ROWS, hp), BF16),
            jax.ShapeDtypeStruct((ROWS, N_HEADS * V_HEAD), BF16),
            jax.ShapeDtypeStruct((ROWS, N_HEADS * V_HEAD), F32),
            jax.ShapeDtypeStruct((ROWS, KV_LORA), F32),
            jax.ShapeDtypeStruct((ROWS, HEAD_PAD), F32),
        ],
        compiler_params=_cparams(("arbitrary",), 56 << 20),
    )(tlo, thi, w_out, xlo, xhi, mod, norm_w, mod, mod, w["w_in"], w["q_norm"], w["kv_norm"], w["w_qb"],
      w["w_k"], w["w_v"], cos, sup, sdn)


def _mla_ctx_kernel(ckv_ref, kpe_ref, wk_ref, wv_ref, k_ref, v_ref):
    ckv_b = ckv_ref[...].astype(BF16)
    kn = jnp.dot(ckv_b, wk_ref[...], preferred_element_type=F32)
    v_ref[...] = jnp.dot(ckv_b, wv_ref[...], preferred_element_type=F32).astype(BF16)
    kpe = kpe_ref[...]
    for hd in range(N_HEADS):
        cols = slice(hd * HEAD_PAD, (hd + 1) * HEAD_PAD)
        k_ref[:, cols] = (kn[:, cols] + kpe).astype(BF16)


def _mla_ctx(j, ckv_ctx, kpe_ctx, w):
    hp = N_HEADS * HEAD_PAD
    rows = DEC_BATCH * PAST_LEN
    tile = lambda n: pl.BlockSpec((PAST_LEN, n), lambda i: (i, 0))
    cache = lambda n: pl.BlockSpec((None, None, PAST_LEN, n), lambda i: (i, j, 0, 0))
    return pl.pallas_call(
        _mla_ctx_kernel,
        name="mla_ctx",
        grid=(DEC_BATCH,),
        in_specs=[cache(KV_LORA), cache(HEAD_PAD), _entry(w["w_k"], j, True), _entry(w["w_v"], j, True)],
        out_specs=[tile(hp), tile(N_HEADS * V_HEAD)],
        out_shape=[jax.ShapeDtypeStruct((rows, hp), BF16),
                   jax.ShapeDtypeStruct((rows, N_HEADS * V_HEAD), BF16)],
        compiler_params=_cparams(("arbitrary",), 32 << 20),
    )(ckv_ctx, kpe_ctx, w["w_k"], w["w_v"])


NT_DIMS = (((1,), (1,)), ((), ()))
ATTN_PAIRS = 4
ATTN_TQ = 256


def _attn_kernel(*refs, n_pairs, n_groups, has_ctx, final):
    refs = list(refs)
    q_ref, k_ref, v_ref = refs[:3]
    del refs[:3]
    if has_ctx:
        kc_ref, vc_ref = refs[:2]
        del refs[:2]
    sg_ref, wo_ref, x_ref, g_ref = refs[:4]
    del refs[:4]
    if final:
        fg_ref = refs.pop(0)
    out_ref, o_scr = refs[:2]
    tq = q_ref.shape[0]
    low_half = lax.broadcasted_iota(jnp.int32, (tq, LANES), 1) < V_HEAD
    for p in range(n_pairs):
        vcols = slice(p * LANES, (p + 1) * LANES)
        vp = v_ref[:, vcols]
        outs = []
        for hh in range(2):
            cols = slice((2 * p + hh) * HEAD_PAD, (2 * p + hh + 1) * HEAD_PAD)
            q = q_ref[:, cols]
            s = lax.dot_general(q, k_ref[:, cols], NT_DIMS, preferred_element_type=F32)
            mx = jnp.max(s, axis=-1, keepdims=True)
            if has_ctx:
                sc = lax.dot_general(q, kc_ref[:, cols], NT_DIMS, preferred_element_type=F32)
                mx = jnp.maximum(mx, jnp.max(sc, axis=-1, keepdims=True))
            e = jnp.exp2(s - mx)
            den = jnp.sum(e, axis=-1, keepdims=True)
            pv = jnp.dot(e.astype(BF16), vp, preferred_element_type=F32)
            if has_ctx:
                ec = jnp.exp2(sc - mx)
                den = den + jnp.sum(ec, axis=-1, keepdims=True)
                pv = pv + jnp.dot(ec.astype(BF16), vc_ref[:, vcols], preferred_element_type=F32)
            outs.append(pv / den)
        o = jnp.where(low_half, outs[0], outs[1])
        o_scr[:, vcols] = (o * sg_ref[:, vcols]).astype(BF16)

    u = jnp.dot(o_scr[...], wo_ref[...], preferred_element_type=F32)
    cond = 1 + pl.program_id(0) if has_ctx else 0

    def finish(u_all):
        xn = x_ref[...] + g_ref[pl.ds(cond, 1), :] * u_all
        out_ref[...] = _rms(xn, fg_ref[...]) if final else xn

    if n_groups == 1:
        finish(u)
    else:
        acc_scr = refs[2]
        grp = pl.program_id(2)

        @pl.when(grp == 0)
        def _():
            acc_scr[...] = u

        @pl.when((grp > 0) & (grp < n_groups - 1))
        def _():
            acc_scr[...] += u

        @pl.when(grp == n_groups - 1)
        def _():
            finish(acc_scr[...] + u)


def _attn_prompt(q, k, v, sg, w_o, x, mod, layer, final_g):
    hp = N_HEADS * HEAD_PAD
    nv = N_HEADS * V_HEAD
    xlo, _, _ = _row_pair(x)
    seq = lambda n: pl.BlockSpec((SEQ, n), lambda b: (b, 0))
    final = final_g is not None
    return pl.pallas_call(
        functools.partial(_attn_kernel, n_pairs=N_HEADS // 2, n_groups=1, has_ctx=False, final=final),
        name="attn_prompt",
        grid=(BATCH,),
        in_specs=[seq(hp), seq(hp), seq(nv), seq(nv), _entry(w_o, layer // 2, True), seq(D_MODEL),
                  _mod_spec(layer, 2)] + ([pl.BlockSpec((1, D_MODEL), lambda b: (0, 0))] if final else []),
        out_specs=seq(D_MODEL),
        out_shape=jax.ShapeDtypeStruct((ROWS_P, D_MODEL), F32),
        scratch_shapes=[pltpu.VMEM((SEQ, nv), BF16)],
        compiler_params=_cparams(("arbitrary",), 40 << 20),
    )(q, k, v, sg, w_o, xlo, mod, *([final_g] if final else []))


def _attn_latent(q, k, v, kc, vc, sg, w_o, x, mod, layer, final_g):
    tq = ATTN_TQ
    npair = ATTN_PAIRS
    ngrp = N_HEADS // 2 // npair
    tiles = DEC_SEQ // tq
    q0 = ROWS_P // tq
    s0 = ROWS_P // DEC_SEQ
    wide = npair * 2 * HEAD_PAD
    narrow = npair * LANES
    _, xhi, base = _row_pair(x)
    x0 = base * TM // tq
    final = final_g is not None
    qrow = lambda b, t, p: (q0 + b * tiles + t, p)
    return pl.pallas_call(
        functools.partial(_attn_kernel, n_pairs=npair, n_groups=ngrp, has_ctx=True, final=final),
        name="attn_latent",
        grid=(DEC_BATCH, tiles, ngrp),
        in_specs=[pl.BlockSpec((tq, wide), qrow),
                  pl.BlockSpec((DEC_SEQ, wide), lambda b, t, p: (s0 + b, p)),
                  pl.BlockSpec((DEC_SEQ, narrow), lambda b, t, p: (s0 + b, p)),
                  pl.BlockSpec((PAST_LEN, wide), lambda b, t, p: (b, p)),
                  pl.BlockSpec((PAST_LEN, narrow), lambda b, t, p: (b, p)),
                  pl.BlockSpec((tq, narrow), qrow),
                  pl.BlockSpec((None, narrow, D_MODEL), lambda b, t, p: (layer // 2, p, 0)),
                  pl.BlockSpec((tq, D_MODEL), lambda b, t, p: (x0 + b * tiles + t, 0)),
                  _mod_spec(layer, 2)] + ([pl.BlockSpec((1, D_MODEL), lambda b, t, p: (0, 0))] if final else []),
        out_specs=pl.BlockSpec((tq, D_MODEL), lambda b, t, p: (b * tiles + t, 0)),
        out_shape=jax.ShapeDtypeStruct((ROWS_S, D_MODEL), F32),
        scratch_shapes=[pltpu.VMEM((tq, narrow), BF16), pltpu.VMEM((tq, D_MODEL), F32)],
        compiler_params=_cparams(("arbitrary", "arbitrary", "arbitrary"), 48 << 20),
    )(q, k, v, kc, vc, sg, w_o, xhi, mod, *([final_g] if final else []))


def _mla_weights(w_in, q_norm, w_qb, kv_norm, w_kvb, w_o):
    n = w_in.shape[0]
    o_pe = Q_LORA + KV_LORA
    o_gate = o_pe + QK_ROPE
    zeros = lambda k: jnp.zeros((n, D_MODEL, k), F32)
    w_in_r = jnp.concatenate(
        [w_in[..., :o_pe], w_in[..., o_gate:], zeros(QK_NOPE), w_in[..., o_pe:o_gate],
         zeros(HEAD_PAD - QK_NOPE - QK_ROPE)], axis=-1)
    qb = w_qb.reshape(n, Q_LORA, N_HEADS, QK_NOPE + QK_ROPE)
    qb = jnp.pad(qb, ((0, 0), (0, 0), (0, 0), (0, HEAD_PAD - QK_NOPE - QK_ROPE)))
    kvb = w_kvb.reshape(n, KV_LORA, N_HEADS, QK_NOPE + V_HEAD)
    wk = jnp.pad(kvb[..., :QK_NOPE], ((0, 0), (0, 0), (0, 0), (0, HEAD_PAD - QK_NOPE)))
    wv = kvb[..., QK_NOPE:]
    return {
        "w_in": w_in_r.astype(BF16),
        "q_norm": q_norm[:, None, :],
        "kv_norm": kv_norm[:, None, :],
        "w_qb": qb.reshape(n, Q_LORA, N_HEADS * HEAD_PAD).astype(BF16),
        "w_k": wk.reshape(n, KV_LORA, N_HEADS * HEAD_PAD).astype(BF16),
        "w_v": wv.reshape(n, KV_LORA, N_HEADS * V_HEAD).astype(BF16),
        "w_o": w_o.astype(BF16),
    }


def kernel(x_prompt, x_sample, cache_ckv, cache_kpe, c, c_ctx, norm_w, ada_w, ada_b, hy_w_in, hy_conv_w, hy_conv_b, hy_f_w1, hy_f_b1, hy_f_freq, hy_f_w2, hy_f_b2, hy_f_w3, hy_f_bias, hy_w_out, mla_w_in, mla_q_norm, mla_w_qb, mla_kv_norm, mla_w_kvb, mla_w_o, final_norm):
    x = (x_prompt.reshape(ROWS_P, D_MODEL), x_sample.reshape(ROWS_S, D_MODEL))
    cond = jnp.concatenate([c_ctx[None, :], c, jnp.zeros((N_COND - 1 - DEC_BATCH, D_MODEL), F32)], axis=0)
    mod = _ada_all(cond, ada_w, ada_b)

    nw = norm_w[:, None, :]
    hy_w_in_b = hy_w_in.astype(BF16)
    hy_w_out_b = hy_w_out.astype(BF16)
    hy_conv_b3 = hy_conv_b[:, None, :]
    f_w1 = jnp.pad(hy_f_w1, ((0, 0), (0, FILTER_HIDDEN - FILTER_EMB), (0, 0)))
    f_b1, f_freq, f_b2, f_bias = (a[:, None, :] for a in (hy_f_b1, hy_f_freq, hy_f_b2, hy_f_bias))
    w = _mla_weights(mla_w_in, mla_q_norm, mla_w_qb, mla_kv_norm, mla_w_kvb, mla_w_o)
    kpe_ctx = jnp.pad(cache_kpe, ((0, 0), (0, 0), (0, 0), (QK_NOPE, HEAD_PAD - QK_NOPE - QK_ROPE)))

    new_ckv, new_kpe = [], []
    for layer in range(DEPTH):
        j = layer // 2
        if layer % 2 == 0:
            z, m = _hy_in(x, nw, mod, layer, hy_w_in_b, hy_conv_w, hy_conv_b3)
            t = []
            for L, blk0, nblk in ((SEQ, 0, ROWS_P // SEQ_BLOCK), (DEC_SEQ, ROWS_P // SEQ_BLOCK, ROWS_S // SEQ_BLOCK)):
                filt = _hy_filter(L, j, f_w1, f_b1, f_freq, hy_f_w2, f_b2, hy_f_w3)
                cm, sm = _dft_mats(L)
                a, b, nq = _hy_spectrum(L, j, filt, f_bias, cm, sm)
                t.append(_hy_lconv(L, blk0, nblk, z, m, cm, sm, a, b, nq))
            t = tuple(t)
        else:
            x, q, k, v, sg, ckv, kpe = _mla_proj(t, hy_w_out_b, x, nw, mod, layer, w)
            kc, vc = _mla_ctx(j, cache_ckv, kpe_ctx, w)
            final_g = final_norm[None, :] if layer == DEPTH - 1 else None
            x = (_attn_prompt(q, k, v, sg, w["w_o"], x, mod, layer, final_g),
                 _attn_latent(q, k, v, kc, vc, sg, w["w_o"], x, mod, layer, final_g))
            new_ckv.append(ckv[:ROWS_P].reshape(BATCH, SEQ, KV_LORA))
            new_kpe.append(kpe[:ROWS_P, QK_NOPE:QK_NOPE + QK_ROPE].reshape(BATCH, SEQ, QK_ROPE))

    assert DEPTH % 2 == 0
    y_prompt, y_sample = x
    return (y_prompt.reshape(BATCH, SEQ, D_MODEL), y_sample.reshape(DEC_BATCH, DEC_SEQ, D_MODEL),
            jnp.stack(new_ckv, axis=1), jnp.stack(new_kpe, axis=1))
```

```python
import functools
import math

import numpy as np
import jax
import jax.numpy as jnp
from jax import lax
from jax.experimental import pallas as pl
from jax.experimental.pallas import tpu as pltpu

F32 = jnp.float32
BF16 = jnp.bfloat16

D_MODEL = 1024
BATCH = 16
SEQ = 256
DEPTH = 4
DEC_BATCH = 2
DEC_SEQ = 2048
PAST_LEN = 512
GRID_W = 64
EPS = 1e-6
HY_WIDTH = D_MODEL
FILTER_BANDS = 16
FILTER_EMB = 1 + 2 * FILTER_BANDS
FILTER_HIDDEN = 64
FAST_DECAY_PCT = 0.3
SLOW_DECAY_PCT = 1.5
DECAY_TARGET = 1e-2
N_HEADS = 16
Q_LORA = 384
KV_LORA = 256
QK_NOPE = 64
QK_ROPE = 32
V_HEAD = 64
ROPE_THETA = 10000.0

LANES = 128
MXU_TILE = 256
HEAD_PAD = LANES
ROWS_P = BATCH * SEQ
ROWS_S = DEC_BATCH * DEC_SEQ
ROWS = ROWS_P + ROWS_S
TM = 512
N_TILES = ROWS // TM
TILES_P = ROWS_P // TM
TILES_PER_DEC_SEQ = DEC_SEQ // TM
N_COND = 8
SEQ_BLOCK = 2048
TK = 512
LCONV_TC = 512
VMEM_CAP = 56 * 1024 * 1024


def _cparams(sem, vmem_bytes):
    return pltpu.CompilerParams(dimension_semantics=sem, vmem_limit_bytes=min(int(vmem_bytes), VMEM_CAP))


def _resident(shape, index_map):
    return pl.BlockSpec(shape, index_map, pipeline_mode=pl.Buffered(1))


def _entry(arr, idx, resident=False):
    zeros = (0,) * (arr.ndim - 1)
    return pl.BlockSpec((None,) + arr.shape[1:], lambda *_: (idx,) + zeros,
                        pipeline_mode=pl.Buffered(1) if resident else None)


def _cond_row(i):
    return jnp.where(i < TILES_P, 0, 1 + (i - TILES_P) // TILES_PER_DEC_SEQ)


def _silu(x):
    return x * jax.nn.sigmoid(x)


def _rms(x, g):
    return x * lax.rsqrt(jnp.mean(x * x, axis=-1, keepdims=True) + EPS) * g


def _row_pair(x):
    if isinstance(x, tuple):
        return x[0], x[1], 0
    return x, x, TILES_P


def _row_pair_specs(x, n):
    _, _, base = _row_pair(x)
    lo = pl.BlockSpec((TM, n), lambda i, *_: (jnp.minimum(i, TILES_P - 1), 0))
    hi = pl.BlockSpec((TM, n), lambda i, *_: (jnp.maximum(i - TILES_P, 0) + base, 0))
    return lo, hi


def _pick_rows(i, lo_ref, hi_ref):
    return jnp.where(i < TILES_P, lo_ref[...], hi_ref[...])


def _ada_kernel(cond_ref, w_ref, b_ref, o_ref):
    s = _silu(cond_ref[...]).astype(BF16)
    o_ref[...] = jnp.dot(s, w_ref[...].astype(BF16), preferred_element_type=F32) + b_ref[...]


def _ada_all(cond, ada_w, ada_b):
    tn = 1024
    return pl.pallas_call(
        _ada_kernel,
        name="ada",
        grid=(DEPTH, 3 * D_MODEL // tn),
        in_specs=[
            pl.BlockSpec((N_COND, D_MODEL), lambda l, j: (0, 0)),
            pl.BlockSpec((None, D_MODEL, tn), lambda l, j: (l, 0, j)),
            pl.BlockSpec((None, 1, tn), lambda l, j: (l, 0, j)),
        ],
        out_specs=pl.BlockSpec((None, N_COND, tn), lambda l, j: (l, 0, j)),
        out_shape=jax.ShapeDtypeStruct((DEPTH, N_COND, 3 * D_MODEL), F32),
        compiler_params=_cparams(("arbitrary", "arbitrary"), 32 << 20),
    )(cond, ada_w, ada_b.reshape(DEPTH, 1, 3 * D_MODEL))


def _mod_spec(layer, part):
    return pl.BlockSpec((None, N_COND, D_MODEL), lambda i, *_: (layer, 0, part))


def _modnorm(i, x, nw_ref, sh_ref, sc_ref):
    c = _cond_row(i)
    return _rms(x, nw_ref[...]) * (1.0 + sc_ref[pl.ds(c, 1), :]) + sh_ref[pl.ds(c, 1), :]


HALO = 16
HALO_PER_TILE = TM // HALO


def _halo_specs(x):
    _, _, base = _row_pair(x)
    r = HALO_PER_TILE
    lo_last = TILES_P * r - 1
    hi_last = (N_TILES - TILES_P) * r - 1
    lo_tile = lambda i: jnp.minimum(i, TILES_P - 1)
    hi_tile = lambda i: jnp.maximum(i - TILES_P, 0)
    blk = lambda f: pl.BlockSpec((HALO, D_MODEL), lambda i: (f(i), 0))
    return [
        blk(lambda i: jnp.maximum(lo_tile(i) * r - 1, 0)),
        blk(lambda i: jnp.maximum(hi_tile(i) * r - 1, 0) + base * r),
        blk(lambda i: jnp.minimum((lo_tile(i) + 1) * r, lo_last)),
        blk(lambda i: jnp.minimum((hi_tile(i) + 1) * r, hi_last) + base * r),
    ]


def _hy_in_kernel(xlo_ref, xhi_ref, plo_ref, phi_ref, nlo_ref, nhi_ref, nw_ref, sh_ref, sc_ref, w_ref,
                  cw_ref, cb_ref, z_ref, m_ref, h_scr, u0_scr, u1_scr, u2_scr):
    i = pl.program_id(0)
    width = z_ref.shape[1]
    norm = lambda lo_ref, hi_ref: _modnorm(i, _pick_rows(i, lo_ref, hi_ref), nw_ref, sh_ref, sc_ref).astype(BF16)
    h_scr[0:HALO, :] = norm(plo_ref, phi_ref)
    h_scr[HALO:HALO + TM, :] = norm(xlo_ref, xhi_ref)
    h_scr[HALO + TM:HALO + TM + HALO, :] = norm(nlo_ref, nhi_ref)

    seq_mask = jnp.where(i < TILES_P, SEQ - 1, DEC_SEQ - 1)
    pos = (lax.broadcasted_iota(jnp.int32, (TM, width), 0) + i * TM) & seq_mask
    first = pos == 0
    last = pos == seq_mask

    def conv(g, u_scr):
        cols = slice(g * width, (g + 1) * width)
        for c0 in range(0, width, MXU_TILE):
            u_scr[:, c0:c0 + MXU_TILE] = jnp.dot(
                h_scr[...], w_ref[:, g * width + c0:g * width + c0 + MXU_TILE], preferred_element_type=F32)
        prev = jnp.where(first, 0.0, u_scr[HALO - 1:HALO - 1 + TM, :])
        nxt = jnp.where(last, 0.0, u_scr[HALO + 1:HALO + 1 + TM, :])
        return (cb_ref[:, cols] + prev * cw_ref[0:1, cols] + u_scr[HALO:HALO + TM, :] * cw_ref[1:2, cols]
                + nxt * cw_ref[2:3, cols])

    z_ref[...] = (conv(2, u2_scr) * conv(1, u1_scr)).astype(BF16)
    gate = jnp.dot(h_scr[HALO:HALO + TM, :], w_ref[:, 3 * width:4 * width], preferred_element_type=F32)
    m_ref[...] = (conv(0, u0_scr) * _silu(gate)).astype(BF16)


def _hy_in(x, norm_w, mod, layer, w_in, conv_w, conv_b):
    j = layer // 2
    xlo, xhi, _ = _row_pair(x)
    rows_h = TM + 2 * HALO
    out = pl.BlockSpec((TM, HY_WIDTH), lambda i: (i, 0))
    return pl.pallas_call(
        _hy_in_kernel,
        name="hy_in",
        grid=(N_TILES,),
        in_specs=[
            *_row_pair_specs(x, D_MODEL),
            *_halo_specs(x),
            _entry(norm_w, layer),
            _mod_spec(layer, 0),
            _mod_spec(layer, 1),
            _entry(w_in, j, resident=True),
            _entry(conv_w, j),
            _entry(conv_b, j),
        ],
        out_specs=[out, out],
        out_shape=[jax.ShapeDtypeStruct((ROWS, HY_WIDTH), BF16)] * 2,
        scratch_shapes=[pltpu.VMEM((rows_h, D_MODEL), BF16)] + [pltpu.VMEM((rows_h, HY_WIDTH), F32)] * 3,
        compiler_params=_cparams(("arbitrary",), 48 << 20),
    )(xlo, xhi, xlo, xhi, xlo, xhi, norm_w, mod, mod, w_in, conv_w, conv_b)


@functools.lru_cache(maxsize=None)
def _filter_consts(L):
    t = np.linspace(0.0, 1.0, L)[:, None]
    w = (2.0 * math.pi / L) * np.arange(L)[:, None]
    bands = np.linspace(1e-4, FILTER_BANDS - 1, FILTER_BANDS)[None, :]
    emb = np.concatenate([t, np.cos(bands * w), -np.sin(bands * w)], axis=-1)
    emb = np.pad(emb, ((0, 0), (0, FILTER_HIDDEN - FILTER_EMB)))
    max_decay = math.log(DECAY_TARGET) / FAST_DECAY_PCT
    min_decay = math.log(DECAY_TARGET) / SLOW_DECAY_PCT
    deltas = np.abs(np.linspace(min_decay, max_decay, HY_WIDTH))
    deltas = np.concatenate([deltas, deltas])[None, :]
    return emb.astype(np.float32), t.astype(np.float32), deltas.astype(np.float32)


def _filter_kernel(emb_ref, t_ref, w1_ref, b1_ref, fr_ref, w2_ref, b2_ref, w3_ref, dl_ref, o_ref, h_scr):
    @pl.when(pl.program_id(0) == 0)
    def _():
        hi = lax.Precision.HIGHEST
        fr = fr_ref[...]
        h = jnp.sin(fr * (jnp.dot(emb_ref[...], w1_ref[...], precision=hi, preferred_element_type=F32) + b1_ref[...]))
        h = jnp.sin(fr * (jnp.dot(h, w2_ref[...], precision=hi, preferred_element_type=F32) + b2_ref[...]))
        h_scr[...] = h.astype(BF16)

    h = jnp.dot(h_scr[...], w3_ref[...].astype(BF16), preferred_element_type=F32)
    h = h * jnp.exp(-t_ref[...] * dl_ref[...])
    o_ref[...] = h / jnp.sum(jnp.abs(h), axis=0, keepdims=True)


def _hy_filter(L, hy, w1, b1, freq, w2, b2, w3):
    emb, t, deltas = _filter_consts(L)
    tcf = 512
    full = lambda shape: pl.BlockSpec(shape, lambda c: (0, 0))
    return pl.pallas_call(
        _filter_kernel,
        name=f"hy_filter{L}",
        grid=(2 * HY_WIDTH // tcf,),
        in_specs=[
            full((L, FILTER_HIDDEN)), full((L, 1)), _entry(w1, hy), _entry(b1, hy), _entry(freq, hy),
            _entry(w2, hy), _entry(b2, hy),
            pl.BlockSpec((None, FILTER_HIDDEN, tcf), lambda c: (hy, 0, c)),
            pl.BlockSpec((1, tcf), lambda c: (0, c)),
        ],
        out_specs=pl.BlockSpec((L, tcf), lambda c: (0, c)),
        out_shape=jax.ShapeDtypeStruct((L, 2 * HY_WIDTH), F32),
        scratch_shapes=[pltpu.VMEM((L, FILTER_HIDDEN), BF16)],
        compiler_params=_cparams(("arbitrary",), 40 << 20),
    )(jnp.asarray(emb), jnp.asarray(t), w1, b1, freq, w2, b2, w3, jnp.asarray(deltas))


@functools.lru_cache(maxsize=None)
def _dft_consts(L):
    k = np.arange(L, dtype=np.int64)
    ang = (np.outer(k, k) % (2 * L)).astype(np.float64) * (math.pi / L)
    return np.cos(ang).astype(np.float32), np.sin(ang).astype(np.float32)


def _dft_mats(L):
    c, s = _dft_consts(L)
    return jnp.asarray(c).astype(BF16), jnp.asarray(s).astype(BF16)


def _alt_sign(shape):
    return (1 - 2 * (lax.broadcasted_iota(jnp.int32, shape, 0) & 1)).astype(F32)


def _spec_kernel(ff_ref, fb_ref, bias_ref, c_ref, s_ref, a_ref, b_ref, nq_ref, *, L, tk):
    inv_n = 1.0 / (2.0 * L)
    bias = bias_ref[...]
    ff = ff_ref[...]
    fb = fb_ref[...]
    f = ff + fb
    nyq = (jnp.sum(f * _alt_sign(f.shape), axis=0, keepdims=True) + bias) * inv_n
    nq_ref[...] = jnp.broadcast_to(nyq, nq_ref.shape)
    f_b = f.astype(BF16)
    d_b = (fb - ff).astype(BF16)
    for k0 in range(0, L, tk):
        kt = slice(k0, k0 + tk)
        hre = jnp.dot(c_ref[kt, :], f_b, preferred_element_type=F32) + bias
        him = jnp.dot(s_ref[kt, :], d_b, preferred_element_type=F32)
        a = hre * (2.0 * inv_n)
        if k0 == 0:
            a = jnp.where(lax.broadcasted_iota(jnp.int32, hre.shape, 0) == 0, hre * inv_n, a)
        a_ref[kt, :] = a
        b_ref[kt, :] = him * (2.0 * inv_n)


def _hy_spectrum(L, hy, filt, f_bias, c, s):
    tk = min(TK, L)
    tc = MXU_TILE
    nc = HY_WIDTH // tc
    out = pl.BlockSpec((L, tc), lambda j: (0, j))
    mat = _resident((L, L), lambda j: (0, 0))
    return pl.pallas_call(
        functools.partial(_spec_kernel, L=L, tk=tk),
        name=f"hy_spectrum{L}",
        grid=(nc,),
        in_specs=[
            pl.BlockSpec((L, tc), lambda j: (0, j)),
            pl.BlockSpec((L, tc), lambda j: (0, j + nc)),
            pl.BlockSpec((None, 1, tc), lambda j: (hy, 0, j)),
            mat, mat,
        ],
        out_specs=[out, out, pl.BlockSpec((8, tc), lambda j: (0, j))],
        out_shape=[jax.ShapeDtypeStruct((L, HY_WIDTH), F32)] * 2 + [jax.ShapeDtypeStruct((8, HY_WIDTH), F32)],
        compiler_params=_cparams(("arbitrary",), 48 << 20),
    )(filt, filt, f_bias, c, s)


def _lconv_kernel(z_ref, m_ref, c_ref, s_ref, a_ref, b_ref, nq_ref, t_ref, y_scr, *, L, tk):
    rows, tc = y_scr.shape
    alt = _alt_sign((L, tc))
    for s0 in range(0, rows, L):
        sq = pl.ds(s0, L)
        zb = z_ref[sq, :]
        z_nyq = jnp.sum(zb.astype(F32) * alt, axis=0, keepdims=True)
        y_scr[sq, :] = alt * (z_nyq * nq_ref[0:1, :])
        for k0 in range(0, L, tk):
            kt = slice(k0, k0 + tk)
            zr = jnp.dot(c_ref[kt, :], zb, preferred_element_type=F32)
            zi = jnp.dot(s_ref[kt, :], zb, preferred_element_type=F32)
            a = a_ref[kt, :]
            b = b_ref[kt, :]
            yr = (zr * a + zi * b).astype(BF16)
            yw = (zi * a - zr * b).astype(BF16)
            y_scr[sq, :] += (jnp.dot(c_ref[:, kt], yr, preferred_element_type=F32)
                             + jnp.dot(s_ref[:, kt], yw, preferred_element_type=F32))
        t_ref[sq, :] = (y_scr[sq, :] * m_ref[sq, :].astype(F32)).astype(BF16)


def _hy_lconv(L, row_block0, n_row_blocks, z, m, c, s, a, b, nq):
    tk = min(TK, L)
    tc = LCONV_TC
    nc = HY_WIDTH // tc
    blk = pl.BlockSpec((SEQ_BLOCK, tc), lambda j, r: (r + row_block0, j))
    coef = pl.BlockSpec((L, tc), lambda j, r: (0, j), pipeline_mode=pl.Buffered(1))
    mat = _resident((L, L), lambda j, r: (0, 0))
    return pl.pallas_call(
        functools.partial(_lconv_kernel, L=L, tk=tk),
        name=f"hy_lconv{L}",
        grid=(nc, n_row_blocks),
        in_specs=[blk, blk, mat, mat, coef, coef, pl.BlockSpec((8, tc), lambda j, r: (0, j))],
        out_specs=pl.BlockSpec((SEQ_BLOCK, tc), lambda j, r: (r, j)),
        out_shape=jax.ShapeDtypeStruct((n_row_blocks * SEQ_BLOCK, HY_WIDTH), BF16),
        scratch_shapes=[pltpu.VMEM((SEQ_BLOCK, tc), F32)],
        compiler_params=_cparams(("arbitrary", "arbitrary"), 56 << 20),
    )(z, m, c, s, a, b, nq)


@functools.lru_cache(maxsize=None)
def _rope_consts():
    axis_dim = QK_ROPE // 2
    nf = axis_dim // 2
    inv = ROPE_THETA ** (-np.arange(0, axis_dim, 2, dtype=np.float64) / axis_dim)
    t = np.arange(DEC_SEQ)
    ang_r = (t // GRID_W)[:, None] * inv
    ang_c = (t % GRID_W)[:, None] * inv
    cos = np.ones((DEC_SEQ, HEAD_PAD))
    sin_up = np.zeros((DEC_SEQ, HEAD_PAD))
    sin_dn = np.zeros((DEC_SEQ, HEAD_PAD))
    for base, ang in ((QK_NOPE, ang_r), (QK_NOPE + axis_dim, ang_c)):
        cos[:, base:base + nf] = np.cos(ang)
        cos[:, base + nf:base + 2 * nf] = np.cos(ang)
        sin_up[:, base:base + nf] = -np.sin(ang)
        sin_dn[:, base + nf:base + 2 * nf] = np.sin(ang)
    return cos.astype(np.float32), sin_up.astype(np.float32), sin_dn.astype(np.float32)


ROPE_HALF = QK_ROPE // 4
Q_SCALE = math.log2(math.e) / math.sqrt(QK_NOPE + QK_ROPE)


def _mla_proj_kernel(tlo_ref, thi_ref, wout_ref, xlo_ref, xhi_ref, gprev_ref,
                     nw_ref, sh_ref, sc_ref, wa_ref, wg_ref, wpe_ref, qn_ref, kvn_ref, wqb_ref, wk_ref, wv_ref,
                     cos_ref, sup_ref, sdn_ref,
                     xn_ref, q_ref, k_ref, v_ref, sg_ref, ckv_ref, kpe_ref):
    i = pl.program_id(0)
    u = jnp.dot(_pick_rows(i, tlo_ref, thi_ref), wout_ref[...], preferred_element_type=F32)
    x = _pick_rows(i, xlo_ref, xhi_ref) + gprev_ref[pl.ds(_cond_row(i), 1), :] * u
    xn_ref[...] = x
    h = _modnorm(i, x, nw_ref, sh_ref, sc_ref).astype(BF16)
    lora = jnp.dot(h, wa_ref[...], preferred_element_type=F32)
    gate = jnp.dot(h, wg_ref[...], preferred_element_type=F32)
    kpe = jnp.dot(h, wpe_ref[...], preferred_element_type=F32)
    qn = _rms(lora[:, 0:Q_LORA], qn_ref[...]).astype(BF16)
    ckv = _rms(lora[:, Q_LORA:Q_LORA + KV_LORA], kvn_ref[...])
    ckv_ref[...] = ckv
    kpe_ref[...] = kpe
    sg_ref[...] = _silu(gate)
    ckv_b = ckv.astype(BF16)
    v_ref[...] = jnp.dot(ckv_b, wv_ref[...], preferred_element_type=F32).astype(BF16)
    q = jnp.dot(qn, wqb_ref[...], preferred_element_type=F32)
    kn = jnp.dot(ckv_b, wk_ref[...], preferred_element_type=F32)

    latent = i >= TILES_P
    cos = jnp.where(latent, cos_ref[...], 1.0)
    sup = jnp.where(latent, sup_ref[...], 0.0)
    sdn = jnp.where(latent, sdn_ref[...], 0.0)

    def rope(u, scale):
        return (u * (cos * scale) + pltpu.roll(u, HEAD_PAD - ROPE_HALF, axis=1) * (sup * scale)
                + pltpu.roll(u, ROPE_HALF, axis=1) * (sdn * scale))

    kpe_r = rope(kpe, 1.0)
    for hd in range(N_HEADS):
        cols = slice(hd * HEAD_PAD, (hd + 1) * HEAD_PAD)
        q_ref[:, cols] = rope(q[:, cols], Q_SCALE).astype(BF16)
        k_ref[:, cols] = (kn[:, cols] + kpe_r).astype(BF16)


def _mla_proj(t, w_out, x, norm_w, mod, layer, w):
    cos, sup, sdn = (jnp.asarray(c) for c in _rope_consts())
    j = layer // 2
    hp = N_HEADS * HEAD_PAD
    tlo, thi, _ = _row_pair(t)
    xlo, xhi, _ = _row_pair(x)
    rope_blk = pl.BlockSpec(
        (TM, HEAD_PAD), lambda i: (jnp.where(i >= TILES_P, (i - TILES_P) % TILES_PER_DEC_SEQ, 0), 0))
    tile = lambda n: pl.BlockSpec((TM, n), lambda i: (i, 0))
    return pl.pallas_call(
        _mla_proj_kernel,
        name="mla_proj",
        grid=(N_TILES,),
        in_specs=[
            *_row_pair_specs(t, HY_WIDTH),
            _entry(w_out, (layer - 1) // 2, True),
            *_row_pair_specs(x, D_MODEL),
            _mod_spec(layer - 1, 2),
            _entry(norm_w, layer),
            _mod_spec(layer, 0), _mod_spec(layer, 1),
            _entry(w["w_a"], j, True), _entry(w["w_g"], j, True), _entry(w["w_pe"], j, True),
            _entry(w["q_norm"], j), _entry(w["kv_norm"], j),
            _entry(w["w_qb"], j, True), _entry(w["w_k"], j, True), _entry(w["w_v"], j, True),
            rope_blk, rope_blk, rope_blk,
        ],
        out_specs=[tile(D_MODEL), tile(hp), tile(hp), tile(N_HEADS * V_HEAD),
                   tile(N_HEADS * V_HEAD), tile(KV_LORA), tile(HEAD_PAD)],
        out_shape=[
            jax.ShapeDtypeStruct((ROWS, D_MODEL), F32),
            jax.ShapeDtypeStruct((ROWS, hp), BF16),
            jax.ShapeDtypeStruct((ROWS, hp), BF16),
            jax.ShapeDtypeStruct((ROWS, N_HEADS * V_HEAD), BF16),
            jax.ShapeDtypeStruct((ROWS, N_HEADS * V_HEAD), F32),
            jax.ShapeDtypeStruct((ROWS, KV_LORA), F32),
            jax.ShapeDtypeStruct((ROWS, HEAD_PAD), F32),
        ],
        compiler_params=_cparams(("arbitrary",), 56 << 20),
    )(tlo, thi, w_out, xlo, xhi, mod, norm_w, mod, mod, w["w_a"], w["w_g"], w["w_pe"], w["q_norm"],
      w["kv_norm"], w["w_qb"], w["w_k"], w["w_v"], cos, sup, sdn)


def _mla_ctx_kernel(ckv_ref, kpe_ref, wk_ref, wv_ref, k_ref, v_ref):
    ckv_b = ckv_ref[...].astype(BF16)
    kn = jnp.dot(ckv_b, wk_ref[...], preferred_element_type=F32)
    v_ref[...] = jnp.dot(ckv_b, wv_ref[...], preferred_element_type=F32).astype(BF16)
    kpe = kpe_ref[...]
    for hd in range(N_HEADS):
        cols = slice(hd * HEAD_PAD, (hd + 1) * HEAD_PAD)
        k_ref[:, cols] = (kn[:, cols] + kpe).astype(BF16)


def _mla_ctx(j, ckv_ctx, kpe_ctx, w):
    hp = N_HEADS * HEAD_PAD
    rows = DEC_BATCH * PAST_LEN
    tile = lambda n: pl.BlockSpec((PAST_LEN, n), lambda i: (i, 0))
    cache = lambda n: pl.BlockSpec((None, None, PAST_LEN, n), lambda i: (i, j, 0, 0))
    return pl.pallas_call(
        _mla_ctx_kernel,
        name="mla_ctx",
        grid=(DEC_BATCH,),
        in_specs=[cache(KV_LORA), cache(HEAD_PAD), _entry(w["w_k"], j, True), _entry(w["w_v"], j, True)],
        out_specs=[tile(hp), tile(N_HEADS * V_HEAD)],
        out_shape=[jax.ShapeDtypeStruct((rows, hp), BF16),
                   jax.ShapeDtypeStruct((rows, N_HEADS * V_HEAD), BF16)],
        compiler_params=_cparams(("arbitrary",), 32 << 20),
    )(ckv_ctx, kpe_ctx, w["w_k"], w["w_v"])


NT_DIMS = (((1,), (1,)), ((), ()))
ATTN_PAIRS = 4
ATTN_TQ = 256


def _attn_kernel(*refs, n_pairs, n_groups, has_ctx, final):
    refs = list(refs)
    q_ref, k_ref, v_ref = refs[:3]
    del refs[:3]
    if has_ctx:
        kc_ref, vc_ref = refs[:2]
        del refs[:2]
    sg_ref, wo_ref, x_ref, g_ref = refs[:4]
    del refs[:4]
    if final:
        fg_ref = refs.pop(0)
    out_ref, o_scr = refs[:2]
    tq = q_ref.shape[0]
    low_half = lax.broadcasted_iota(jnp.int32, (tq, LANES), 1) < V_HEAD
    for p in range(n_pairs):
        vcols = slice(p * LANES, (p + 1) * LANES)
        vp = v_ref[:, vcols]
        outs = []
        for hh in range(2):
            cols = slice((2 * p + hh) * HEAD_PAD, (2 * p + hh + 1) * HEAD_PAD)
            q = q_ref[:, cols]
            s = lax.dot_general(q, k_ref[:, cols], NT_DIMS, preferred_element_type=F32)
            mx = jnp.max(s, axis=-1, keepdims=True)
            if has_ctx:
                sc = lax.dot_general(q, kc_ref[:, cols], NT_DIMS, preferred_element_type=F32)
                mx = jnp.maximum(mx, jnp.max(sc, axis=-1, keepdims=True))
            e = jnp.exp2(s - mx)
            den = jnp.sum(e, axis=-1, keepdims=True)
            pv = jnp.dot(e.astype(BF16), vp, preferred_element_type=F32)
            if has_ctx:
                ec = jnp.exp2(sc - mx)
                den = den + jnp.sum(ec, axis=-1, keepdims=True)
                pv = pv + jnp.dot(ec.astype(BF16), vc_ref[:, vcols], preferred_element_type=F32)
            outs.append(pv / den)
        o = jnp.where(low_half, outs[0], outs[1])
        o_scr[:, vcols] = (o * sg_ref[:, vcols]).astype(BF16)

    u = jnp.dot(o_scr[...], wo_ref[...], preferred_element_type=F32)
    cond = 1 + pl.program_id(0) if has_ctx else 0

    def finish(u_all):
        xn = x_ref[...] + g_ref[pl.ds(cond, 1), :] * u_all
        out_ref[...] = _rms(xn, fg_ref[...]) if final else xn

    if n_groups == 1:
        finish(u)
    else:
        acc_scr = refs[2]
        grp = pl.program_id(2)

        @pl.when(grp == 0)
        def _():
            acc_scr[...] = u

        @pl.when((grp > 0) & (grp < n_groups - 1))
        def _():
            acc_scr[...] += u

        @pl.when(grp == n_groups - 1)
        def _():
            finish(acc_scr[...] + u)


def _attn_prompt(q, k, v, sg, w_o, x, mod, layer, final_g):
    hp = N_HEADS * HEAD_PAD
    nv = N_HEADS * V_HEAD
    xlo, _, _ = _row_pair(x)
    seq = lambda n: pl.BlockSpec((SEQ, n), lambda b: (b, 0))
    final = final_g is not None
    return pl.pallas_call(
        functools.partial(_attn_kernel, n_pairs=N_HEADS // 2, n_groups=1, has_ctx=False, final=final),
        name="attn_prompt",
        grid=(BATCH,),
        in_specs=[seq(hp), seq(hp), seq(nv), seq(nv), _entry(w_o, layer // 2, True), seq(D_MODEL),
                  _mod_spec(layer, 2)] + ([pl.BlockSpec((1, D_MODEL), lambda b: (0, 0))] if final else []),
        out_specs=seq(D_MODEL),
        out_shape=jax.ShapeDtypeStruct((ROWS_P, D_MODEL), F32),
        scratch_shapes=[pltpu.VMEM((SEQ, nv), BF16)],
        compiler_params=_cparams(("arbitrary",), 40 << 20),
    )(q, k, v, sg, w_o, xlo, mod, *([final_g] if final else []))


def _attn_latent(q, k, v, kc, vc, sg, w_o, x, mod, layer, final_g):
    tq = ATTN_TQ
    npair = ATTN_PAIRS
    ngrp = N_HEADS // 2 // npair
    tiles = DEC_SEQ // tq
    q0 = ROWS_P // tq
    s0 = ROWS_P // DEC_SEQ
    wide = npair * 2 * HEAD_PAD
    narrow = npair * LANES
    _, xhi, base = _row_pair(x)
    x0 = base * TM // tq
    final = final_g is not None
    qrow = lambda b, t, p: (q0 + b * tiles + t, p)
    return pl.pallas_call(
        functools.partial(_attn_kernel, n_pairs=npair, n_groups=ngrp, has_ctx=True, final=final),
        name="attn_latent",
        grid=(DEC_BATCH, tiles, ngrp),
        in_specs=[pl.BlockSpec((tq, wide), qrow),
                  pl.BlockSpec((DEC_SEQ, wide), lambda b, t, p: (s0 + b, p)),
                  pl.BlockSpec((DEC_SEQ, narrow), lambda b, t, p: (s0 + b, p)),
                  pl.BlockSpec((PAST_LEN, wide), lambda b, t, p: (b, p)),
                  pl.BlockSpec((PAST_LEN, narrow), lambda b, t, p: (b, p)),
                  pl.BlockSpec((tq, narrow), qrow),
                  pl.BlockSpec((None, narrow, D_MODEL), lambda b, t, p: (layer // 2, p, 0)),
                  pl.BlockSpec((tq, D_MODEL), lambda b, t, p: (x0 + b * tiles + t, 0)),
                  _mod_spec(layer, 2)] + ([pl.BlockSpec((1, D_MODEL), lambda b, t, p: (0, 0))] if final else []),
        out_specs=pl.BlockSpec((tq, D_MODEL), lambda b, t, p: (b * tiles + t, 0)),
        out_shape=jax.ShapeDtypeStruct((ROWS_S, D_MODEL), F32),
        scratch_shapes=[pltpu.VMEM((tq, narrow), BF16), pltpu.VMEM((tq, D_MODEL), F32)],
        compiler_params=_cparams(("arbitrary", "arbitrary", "arbitrary"), 48 << 20),
    )(q, k, v, kc, vc, sg, w_o, xhi, mod, *([final_g] if final else []))


def _mla_weights(w_in, q_norm, w_qb, kv_norm, w_kvb, w_o):
    n = w_in.shape[0]
    o_pe = Q_LORA + KV_LORA
    o_gate = o_pe + QK_ROPE
    w_pe = jnp.pad(w_in[..., o_pe:o_gate], ((0, 0), (0, 0), (QK_NOPE, HEAD_PAD - QK_NOPE - QK_ROPE)))
    qb = w_qb.reshape(n, Q_LORA, N_HEADS, QK_NOPE + QK_ROPE)
    qb = jnp.pad(qb, ((0, 0), (0, 0), (0, 0), (0, HEAD_PAD - QK_NOPE - QK_ROPE)))
    kvb = w_kvb.reshape(n, KV_LORA, N_HEADS, QK_NOPE + V_HEAD)
    wk = jnp.pad(kvb[..., :QK_NOPE], ((0, 0), (0, 0), (0, 0), (0, HEAD_PAD - QK_NOPE)))
    wv = kvb[..., QK_NOPE:]
    return {
        "w_a": w_in[..., :o_pe].astype(BF16),
        "w_g": w_in[..., o_gate:].astype(BF16),
        "w_pe": w_pe.astype(BF16),
        "q_norm": q_norm[:, None, :],
        "kv_norm": kv_norm[:, None, :],
        "w_qb": qb.reshape(n, Q_LORA, N_HEADS * HEAD_PAD).astype(BF16),
        "w_k": wk.reshape(n, KV_LORA, N_HEADS * HEAD_PAD).astype(BF16),
        "w_v": wv.reshape(n, KV_LORA, N_HEADS * V_HEAD).astype(BF16),
        "w_o": w_o.astype(BF16),
    }


def kernel(x_prompt, x_sample, cache_ckv, cache_kpe, c, c_ctx, norm_w, ada_w, ada_b, hy_w_in, hy_conv_w, hy_conv_b, hy_f_w1, hy_f_b1, hy_f_freq, hy_f_w2, hy_f_b2, hy_f_w3, hy_f_bias, hy_w_out, mla_w_in, mla_q_norm, mla_w_qb, mla_kv_norm, mla_w_kvb, mla_w_o, final_norm):
    x = (x_prompt.reshape(ROWS_P, D_MODEL), x_sample.reshape(ROWS_S, D_MODEL))
    cond = jnp.concatenate([c_ctx[None, :], c, jnp.zeros((N_COND - 1 - DEC_BATCH, D_MODEL), F32)], axis=0)
    mod = _ada_all(cond, ada_w, ada_b)

    nw = norm_w[:, None, :]
    hy_w_in_b = hy_w_in.astype(BF16)
    hy_w_out_b = hy_w_out.astype(BF16)
    hy_conv_b3 = hy_conv_b[:, None, :]
    f_w1 = jnp.pad(hy_f_w1, ((0, 0), (0, FILTER_HIDDEN - FILTER_EMB), (0, 0)))
    f_b1, f_freq, f_b2, f_bias = (a[:, None, :] for a in (hy_f_b1, hy_f_freq, hy_f_b2, hy_f_bias))
    w = _mla_weights(mla_w_in, mla_q_norm, mla_w_qb, mla_kv_norm, mla_w_kvb, mla_w_o)
    kpe_ctx = jnp.pad(cache_kpe, ((0, 0), (0, 0), (0, 0), (QK_NOPE, HEAD_PAD - QK_NOPE - QK_ROPE)))

    new_ckv, new_kpe = [], []
    for layer in range(DEPTH):
        j = layer // 2
        if layer % 2 == 0:
            z, m = _hy_in(x, nw, mod, layer, hy_w_in_b, hy_conv_w, hy_conv_b3)
            t = []
            for L, blk0, nblk in ((SEQ, 0, ROWS_P // SEQ_BLOCK), (DEC_SEQ, ROWS_P // SEQ_BLOCK, ROWS_S // SEQ_BLOCK)):
                filt = _hy_filter(L, j, f_w1, f_b1, f_freq, hy_f_w2, f_b2, hy_f_w3)
                cm, sm = _dft_mats(L)
                a, b, nq = _hy_spectrum(L, j, filt, f_bias, cm, sm)
                t.append(_hy_lconv(L, blk0, nblk, z, m, cm, sm, a, b, nq))
            t = tuple(t)
        else:
            x, q, k, v, sg, ckv, kpe = _mla_proj(t, hy_w_out_b, x, nw, mod, layer, w)
            kc, vc = _mla_ctx(j, cache_ckv, kpe_ctx, w)
            final_g = final_norm[None, :] if layer == DEPTH - 1 else None
            x = (_attn_prompt(q, k, v, sg, w["w_o"], x, mod, layer, final_g),
                 _attn_latent(q, k, v, kc, vc, sg, w["w_o"], x, mod, layer, final_g))
            new_ckv.append(ckv[:ROWS_P].reshape(BATCH, SEQ, KV_LORA))
            new_kpe.append(kpe[:ROWS_P, QK_NOPE:QK_NOPE + QK_ROPE].reshape(BATCH, SEQ, QK_ROPE))

    assert DEPTH % 2 == 0
    y_prompt, y_sample = x
    return (y_prompt.reshape(BATCH, SEQ, D_MODEL), y_sample.reshape(DEC_BATCH, DEC_SEQ, D_MODEL),
            jnp.stack(new_ckv, axis=1), jnp.stack(new_kpe, axis=1))
```

```python
import functools
import math

import numpy as np
import jax
import jax.numpy as jnp
from jax import lax
from jax.experimental import pallas as pl
from jax.experimental.pallas import tpu as pltpu

F32 = jnp.float32
BF16 = jnp.bfloat16

D_MODEL = 1024
BATCH = 16
SEQ = 256
DEPTH = 4
DEC_BATCH = 2
DEC_SEQ = 2048
PAST_LEN = 512
GRID_W = 64
EPS = 1e-6
HY_WIDTH = D_MODEL
FILTER_BANDS = 16
FILTER_EMB = 1 + 2 * FILTER_BANDS
FILTER_HIDDEN = 64
FAST_DECAY_PCT = 0.3
SLOW_DECAY_PCT = 1.5
DECAY_TARGET = 1e-2
N_HEADS = 16
Q_LORA = 384
KV_LORA = 256
QK_NOPE = 64
QK_ROPE = 32
V_HEAD = 64
ROPE_THETA = 10000.0

LANES = 128
MXU_TILE = 256
HEAD_PAD = LANES
ROWS_P = BATCH * SEQ
ROWS_S = DEC_BATCH * DEC_SEQ
ROWS = ROWS_P + ROWS_S
TM = 512
N_TILES = ROWS // TM
TILES_P = ROWS_P // TM
TILES_PER_DEC_SEQ = DEC_SEQ // TM
N_COND = 8
SEQ_BLOCK = 2048
TK = 512
LCONV_TC = 512
VMEM_CAP = 56 * 1024 * 1024


def _cparams(sem, vmem_bytes):
    return pltpu.CompilerParams(dimension_semantics=sem, vmem_limit_bytes=min(int(vmem_bytes), VMEM_CAP))


def _resident(shape, index_map):
    return pl.BlockSpec(shape, index_map, pipeline_mode=pl.Buffered(1))


def _entry(arr, idx, resident=False):
    zeros = (0,) * (arr.ndim - 1)
    return pl.BlockSpec((None,) + arr.shape[1:], lambda *_: (idx,) + zeros,
                        pipeline_mode=pl.Buffered(1) if resident else None)


def _cond_row(i, rows=TM):
    return jnp.where(i < ROWS_P // rows, 0, 1 + (i - ROWS_P // rows) // (DEC_SEQ // rows))


def _silu(x):
    return x * jax.nn.sigmoid(x)


def _rms(x, g):
    return x * lax.rsqrt(jnp.mean(x * x, axis=-1, keepdims=True) + EPS) * g


def _row_pair(x):
    if isinstance(x, tuple):
        return x[0], x[1], 0
    return x, x, ROWS_P


def _row_pair_specs(x, n, rows=TM):
    _, _, base = _row_pair(x)
    n_lo = ROWS_P // rows
    lo = pl.BlockSpec((rows, n), lambda i, *_: (jnp.minimum(i, n_lo - 1), 0))
    hi = pl.BlockSpec((rows, n), lambda i, *_: (jnp.maximum(i - n_lo, 0) + base // rows, 0))
    return lo, hi


def _pick_rows(i, lo_ref, hi_ref, rows=TM):
    return jnp.where(i < ROWS_P // rows, lo_ref[...], hi_ref[...])


def _ada_kernel(cond_ref, w_ref, b_ref, o_ref):
    s = _silu(cond_ref[...]).astype(BF16)
    o_ref[...] = jnp.dot(s, w_ref[...].astype(BF16), preferred_element_type=F32) + b_ref[...]


def _ada_all(cond, ada_w, ada_b):
    tn = 1024
    return pl.pallas_call(
        _ada_kernel,
        name="ada",
        grid=(DEPTH, 3 * D_MODEL // tn),
        in_specs=[
            pl.BlockSpec((N_COND, D_MODEL), lambda l, j: (0, 0)),
            pl.BlockSpec((None, D_MODEL, tn), lambda l, j: (l, 0, j)),
            pl.BlockSpec((None, 1, tn), lambda l, j: (l, 0, j)),
        ],
        out_specs=pl.BlockSpec((None, N_COND, tn), lambda l, j: (l, 0, j)),
        out_shape=jax.ShapeDtypeStruct((DEPTH, N_COND, 3 * D_MODEL), F32),
        compiler_params=_cparams(("arbitrary", "arbitrary"), 32 << 20),
    )(cond, ada_w, ada_b.reshape(DEPTH, 1, 3 * D_MODEL))


def _mod_spec(layer, part):
    return pl.BlockSpec((None, N_COND, D_MODEL), lambda i, *_: (layer, 0, part))


def _modnorm(i, x, nw_ref, sh_ref, sc_ref):
    c = _cond_row(i)
    return _rms(x, nw_ref[...]) * (1.0 + sc_ref[pl.ds(c, 1), :]) + sh_ref[pl.ds(c, 1), :]


HALO = 16
HY_ROWS = TM


def _halo_specs(x):
    _, _, base = _row_pair(x)
    r = HY_ROWS // HALO
    n_lo = ROWS_P // HY_ROWS
    lo_last = ROWS_P // HALO - 1
    hi_last = ROWS_S // HALO - 1
    lo_blk = lambda i: jnp.minimum(i, n_lo - 1)
    hi_blk = lambda i: jnp.maximum(i - n_lo, 0)
    blk = lambda f: pl.BlockSpec((HALO, D_MODEL), lambda i: (f(i), 0))
    return [
        blk(lambda i: jnp.maximum(lo_blk(i) * r - 1, 0)),
        blk(lambda i: jnp.maximum(hi_blk(i) * r - 1, 0) + base // HALO),
        blk(lambda i: jnp.minimum((lo_blk(i) + 1) * r, lo_last)),
        blk(lambda i: jnp.minimum((hi_blk(i) + 1) * r, hi_last) + base // HALO),
    ]


def _hy_in_kernel(xlo_ref, xhi_ref, plo_ref, phi_ref, nlo_ref, nhi_ref, nw_ref, sh_ref, sc_ref, wf_ref,
                  cw_ref, cb_ref, z_ref, m_ref, w_ref, h_scr, *u_scrs):
    i = pl.program_id(0)
    width = z_ref.shape[1]

    @pl.when(i == 0)
    def _():
        for c0 in range(0, w_ref.shape[1], width):
            w_ref[:, c0:c0 + width] = wf_ref[:, c0:c0 + width].astype(BF16)

    c = _cond_row(i, HY_ROWS)
    norm = lambda lo_ref, hi_ref: (_rms(_pick_rows(i, lo_ref, hi_ref, HY_ROWS), nw_ref[...])
                                   * (1.0 + sc_ref[pl.ds(c, 1), :]) + sh_ref[pl.ds(c, 1), :]).astype(BF16)
    h_scr[0:HALO, :] = norm(plo_ref, phi_ref)
    h_scr[HALO:HALO + HY_ROWS, :] = norm(xlo_ref, xhi_ref)
    h_scr[HALO + HY_ROWS:HALO + HY_ROWS + HALO, :] = norm(nlo_ref, nhi_ref)

    seq_mask = jnp.where(i < ROWS_P // HY_ROWS, SEQ - 1, DEC_SEQ - 1)
    row = lax.broadcasted_iota(jnp.int32, (TM, width), 0)
    for t in range(HY_ROWS // TM):
        r0 = t * TM
        pos = (row + (i * HY_ROWS + r0)) & seq_mask
        first = pos == 0
        last = pos == seq_mask
        rows_h = pl.ds(r0, TM + 2 * HALO)

        def conv(g, u_scr):
            cols = slice(g * width, (g + 1) * width)
            u_scr[...] = jnp.dot(h_scr[rows_h, :], w_ref[:, cols], preferred_element_type=F32)
            prev = jnp.where(first, 0.0, u_scr[HALO - 1:HALO - 1 + TM, :])
            nxt = jnp.where(last, 0.0, u_scr[HALO + 1:HALO + 1 + TM, :])
            return (cb_ref[:, cols] + prev * cw_ref[0:1, cols] + u_scr[HALO:HALO + TM, :] * cw_ref[1:2, cols]
                    + nxt * cw_ref[2:3, cols])

        u0, u1, u2 = u_scrs[3 * t:3 * t + 3]
        out_rows = pl.ds(r0, TM)
        z_ref[out_rows, :] = (conv(2, u2) * conv(1, u1)).astype(BF16)
        gate = jnp.dot(h_scr[pl.ds(r0 + HALO, TM), :], w_ref[:, 3 * width:4 * width], preferred_element_type=F32)
        m_ref[out_rows, :] = (conv(0, u0) * _silu(gate)).astype(BF16)


def _hy_in(x, norm_w, mod, layer, w_in, conv_w, conv_b):
    j = layer // 2
    xlo, xhi, _ = _row_pair(x)
    out = pl.BlockSpec((HY_ROWS, HY_WIDTH), lambda i: (i, 0))
    n_tiles = HY_ROWS // TM
    return pl.pallas_call(
        _hy_in_kernel,
        name="hy_in",
        grid=(ROWS // HY_ROWS,),
        in_specs=[
            *_row_pair_specs(x, D_MODEL, HY_ROWS),
            *_halo_specs(x),
            _entry(norm_w, layer),
            _mod_spec(layer, 0),
            _mod_spec(layer, 1),
            _entry(w_in, j, resident=True),
            _entry(conv_w, j),
            _entry(conv_b, j),
        ],
        out_specs=[out, out],
        out_shape=[jax.ShapeDtypeStruct((ROWS, HY_WIDTH), BF16)] * 2,
        scratch_shapes=([pltpu.VMEM(w_in.shape[1:], BF16), pltpu.VMEM((HY_ROWS + 2 * HALO, D_MODEL), BF16)]
                        + [pltpu.VMEM((TM + 2 * HALO, HY_WIDTH), F32)] * (3 * n_tiles)),
        compiler_params=_cparams(("arbitrary",), 56 << 20),
    )(xlo, xhi, xlo, xhi, xlo, xhi, norm_w, mod, mod, w_in, conv_w, conv_b)


@functools.lru_cache(maxsize=None)
def _filter_consts(L):
    t = np.linspace(0.0, 1.0, L)[:, None]
    w = (2.0 * math.pi / L) * np.arange(L)[:, None]
    bands = np.linspace(1e-4, FILTER_BANDS - 1, FILTER_BANDS)[None, :]
    emb = np.concatenate([t, np.cos(bands * w), -np.sin(bands * w)], axis=-1)
    emb = np.pad(emb, ((0, 0), (0, FILTER_HIDDEN - FILTER_EMB)))
    max_decay = math.log(DECAY_TARGET) / FAST_DECAY_PCT
    min_decay = math.log(DECAY_TARGET) / SLOW_DECAY_PCT
    deltas = np.abs(np.linspace(min_decay, max_decay, HY_WIDTH))
    deltas = np.concatenate([deltas, deltas])[None, :]
    return emb.astype(np.float32), t.astype(np.float32), deltas.astype(np.float32)


def _filter_kernel(emb_ref, t_ref, w1_ref, b1_ref, fr_ref, w2_ref, b2_ref, w3_ref, dl_ref, o_ref, h_scr):
    @pl.when(pl.program_id(0) == 0)
    def _():
        hi = lax.Precision.HIGHEST
        fr = fr_ref[...]
        h = jnp.sin(fr * (jnp.dot(emb_ref[...], w1_ref[...], precision=hi, preferred_element_type=F32) + b1_ref[...]))
        h = jnp.sin(fr * (jnp.dot(h, w2_ref[...], precision=hi, preferred_element_type=F32) + b2_ref[...]))
        h_scr[...] = h.astype(BF16)

    h = jnp.dot(h_scr[...], w3_ref[...].astype(BF16), preferred_element_type=F32)
    h = h * jnp.exp(-t_ref[...] * dl_ref[...])
    o_ref[...] = h / jnp.sum(jnp.abs(h), axis=0, keepdims=True)


def _hy_filter(L, hy, w1, b1, freq, w2, b2, w3):
    emb, t, deltas = _filter_consts(L)
    tcf = 512
    full = lambda shape: pl.BlockSpec(shape, lambda c: (0, 0))
    return pl.pallas_call(
        _filter_kernel,
        name=f"hy_filter{L}",
        grid=(2 * HY_WIDTH // tcf,),
        in_specs=[
            full((L, FILTER_HIDDEN)), full((L, 1)), _entry(w1, hy), _entry(b1, hy), _entry(freq, hy),
            _entry(w2, hy), _entry(b2, hy),
            pl.BlockSpec((None, FILTER_HIDDEN, tcf), lambda c: (hy, 0, c)),
            pl.BlockSpec((1, tcf), lambda c: (0, c)),
        ],
        out_specs=pl.BlockSpec((L, tcf), lambda c: (0, c)),
        out_shape=jax.ShapeDtypeStruct((L, 2 * HY_WIDTH), F32),
        scratch_shapes=[pltpu.VMEM((L, FILTER_HIDDEN), BF16)],
        compiler_params=_cparams(("arbitrary",), 40 << 20),
    )(jnp.asarray(emb), jnp.asarray(t), w1, b1, freq, w2, b2, w3, jnp.asarray(deltas))


@functools.lru_cache(maxsize=None)
def _dft_consts(L):
    k = np.arange(L, dtype=np.int64)
    ang = (np.outer(k, k) % (2 * L)).astype(np.float64) * (math.pi / L)
    return np.cos(ang).astype(np.float32), np.sin(ang).astype(np.float32)


def _dft_mats(L):
    c, s = _dft_consts(L)
    return jnp.asarray(c).astype(BF16), jnp.asarray(s).astype(BF16)


def _alt_sign(shape):
    return (1 - 2 * (lax.broadcasted_iota(jnp.int32, shape, 0) & 1)).astype(F32)


def _spec_kernel(ff_ref, fb_ref, bias_ref, c_ref, s_ref, a_ref, b_ref, nq_ref, *, L, tk):
    inv_n = 1.0 / (2.0 * L)
    bias = bias_ref[...]
    ff = ff_ref[...]
    fb = fb_ref[...]
    f = ff + fb
    nyq = (jnp.sum(f * _alt_sign(f.shape), axis=0, keepdims=True) + bias) * inv_n
    nq_ref[...] = jnp.broadcast_to(nyq, nq_ref.shape)
    f_b = f.astype(BF16)
    d_b = (fb - ff).astype(BF16)
    for k0 in range(0, L, tk):
        kt = slice(k0, k0 + tk)
        hre = jnp.dot(c_ref[kt, :], f_b, preferred_element_type=F32) + bias
        him = jnp.dot(s_ref[kt, :], d_b, preferred_element_type=F32)
        a = hre * (2.0 * inv_n)
        if k0 == 0:
            a = jnp.where(lax.broadcasted_iota(jnp.int32, hre.shape, 0) == 0, hre * inv_n, a)
        a_ref[kt, :] = a
        b_ref[kt, :] = him * (2.0 * inv_n)


def _hy_spectrum(L, hy, filt, f_bias, c, s):
    tk = min(TK, L)
    tc = MXU_TILE
    nc = HY_WIDTH // tc
    out = pl.BlockSpec((L, tc), lambda j: (0, j))
    mat = _resident((L, L), lambda j: (0, 0))
    return pl.pallas_call(
        functools.partial(_spec_kernel, L=L, tk=tk),
        name=f"hy_spectrum{L}",
        grid=(nc,),
        in_specs=[
            pl.BlockSpec((L, tc), lambda j: (0, j)),
            pl.BlockSpec((L, tc), lambda j: (0, j + nc)),
            pl.BlockSpec((None, 1, tc), lambda j: (hy, 0, j)),
            mat, mat,
        ],
        out_specs=[out, out, pl.BlockSpec((8, tc), lambda j: (0, j))],
        out_shape=[jax.ShapeDtypeStruct((L, HY_WIDTH), F32)] * 2 + [jax.ShapeDtypeStruct((8, HY_WIDTH), F32)],
        compiler_params=_cparams(("arbitrary",), 48 << 20),
    )(filt, filt, f_bias, c, s)


def _lconv_kernel(z_ref, m_ref, c_ref, s_ref, a_ref, b_ref, nq_ref, t_ref, y_scr, *, L, tk):
    rows, tc = y_scr.shape
    alt = _alt_sign((L, tc))
    for s0 in range(0, rows, L):
        sq = pl.ds(s0, L)
        zb = z_ref[sq, :]
        z_nyq = jnp.sum(zb.astype(F32) * alt, axis=0, keepdims=True)
        y_scr[sq, :] = alt * (z_nyq * nq_ref[0:1, :])
        for k0 in range(0, L, tk):
            kt = slice(k0, k0 + tk)
            zr = jnp.dot(c_ref[kt, :], zb, preferred_element_type=F32)
            zi = jnp.dot(s_ref[kt, :], zb, preferred_element_type=F32)
            a = a_ref[kt, :]
            b = b_ref[kt, :]
            yr = (zr * a + zi * b).astype(BF16)
            yw = (zi * a - zr * b).astype(BF16)
            y_scr[sq, :] += (jnp.dot(c_ref[:, kt], yr, preferred_element_type=F32)
                             + jnp.dot(s_ref[:, kt], yw, preferred_element_type=F32))
        t_ref[sq, :] = (y_scr[sq, :] * m_ref[sq, :].astype(F32)).astype(BF16)


def _hy_lconv(L, row_block0, n_row_blocks, z, m, c, s, a, b, nq):
    tk = min(TK, L)
    tc = LCONV_TC
    nc = HY_WIDTH // tc
    blk = pl.BlockSpec((SEQ_BLOCK, tc), lambda j, r: (r + row_block0, j))
    coef = pl.BlockSpec((L, tc), lambda j, r: (0, j), pipeline_mode=pl.Buffered(1))
    mat = _resident((L, L), lambda j, r: (0, 0))
    return pl.pallas_call(
        functools.partial(_lconv_kernel, L=L, tk=tk),
        name=f"hy_lconv{L}",
        grid=(nc, n_row_blocks),
        in_specs=[blk, blk, mat, mat, coef, coef, pl.BlockSpec((8, tc), lambda j, r: (0, j))],
        out_specs=pl.BlockSpec((SEQ_BLOCK, tc), lambda j, r: (r, j)),
        out_shape=jax.ShapeDtypeStruct((n_row_blocks * SEQ_BLOCK, HY_WIDTH), BF16),
        scratch_shapes=[pltpu.VMEM((SEQ_BLOCK, tc), F32)],
        compiler_params=_cparams(("arbitrary", "arbitrary"), 56 << 20),
    )(z, m, c, s, a, b, nq)


@functools.lru_cache(maxsize=None)
def _rope_consts():
    axis_dim = QK_ROPE // 2
    nf = axis_dim // 2
    inv = ROPE_THETA ** (-np.arange(0, axis_dim, 2, dtype=np.float64) / axis_dim)
    t = np.arange(DEC_SEQ)
    ang_r = (t // GRID_W)[:, None] * inv
    ang_c = (t % GRID_W)[:, None] * inv
    cos = np.ones((DEC_SEQ, HEAD_PAD))
    sin_up = np.zeros((DEC_SEQ, HEAD_PAD))
    sin_dn = np.zeros((DEC_SEQ, HEAD_PAD))
    for base, ang in ((QK_NOPE, ang_r), (QK_NOPE + axis_dim, ang_c)):
        cos[:, base:base + nf] = np.cos(ang)
        cos[:, base + nf:base + 2 * nf] = np.cos(ang)
        sin_up[:, base:base + nf] = -np.sin(ang)
        sin_dn[:, base + nf:base + 2 * nf] = np.sin(ang)
    return cos.astype(np.float32), sin_up.astype(np.float32), sin_dn.astype(np.float32)


ROPE_HALF = QK_ROPE // 4
Q_SCALE = math.log2(math.e) / math.sqrt(QK_NOPE + QK_ROPE)


def _mla_proj_kernel(tlo_ref, thi_ref, woutf_ref, xlo_ref, xhi_ref, gprev_ref,
                     nw_ref, sh_ref, sc_ref, wa_ref, wg_ref, wpe_ref, qn_ref, kvn_ref, wqb_ref, wk_ref, wv_ref,
                     cos_ref, sup_ref, sdn_ref,
                     xn_ref, q_ref, k_ref, v_ref, sg_ref, ckv_ref, kpe_ref, wout_ref):
    i = pl.program_id(0)

    @pl.when(i == 0)
    def _():
        wout_ref[...] = woutf_ref[...].astype(BF16)

    u = jnp.dot(_pick_rows(i, tlo_ref, thi_ref), wout_ref[...], preferred_element_type=F32)
    x = _pick_rows(i, xlo_ref, xhi_ref) + gprev_ref[pl.ds(_cond_row(i), 1), :] * u
    xn_ref[...] = x
    h = _modnorm(i, x, nw_ref, sh_ref, sc_ref).astype(BF16)
    lora = jnp.dot(h, wa_ref[...], preferred_element_type=F32)
    gate = jnp.dot(h, wg_ref[...], preferred_element_type=F32)
    kpe = jnp.dot(h, wpe_ref[...], preferred_element_type=F32)
    qn = _rms(lora[:, 0:Q_LORA], qn_ref[...]).astype(BF16)
    ckv = _rms(lora[:, Q_LORA:Q_LORA + KV_LORA], kvn_ref[...])
    ckv_ref[...] = ckv
    kpe_ref[...] = kpe
    sg_ref[...] = _silu(gate)
    ckv_b = ckv.astype(BF16)
    v_ref[...] = jnp.dot(ckv_b, wv_ref[...], preferred_element_type=F32).astype(BF16)
    q = jnp.dot(qn, wqb_ref[...], preferred_element_type=F32)
    kn = jnp.dot(ckv_b, wk_ref[...], preferred_element_type=F32)

    latent = i >= TILES_P
    cos = jnp.where(latent, cos_ref[...], 1.0)
    sup = jnp.where(latent, sup_ref[...], 0.0)
    sdn = jnp.where(latent, sdn_ref[...], 0.0)

    def rope(u, scale):
        return (u * (cos * scale) + pltpu.roll(u, HEAD_PAD - ROPE_HALF, axis=1) * (sup * scale)
                + pltpu.roll(u, ROPE_HALF, axis=1) * (sdn * scale))

    kpe_r = rope(kpe, 1.0)
    for hd in range(N_HEADS):
        cols = slice(hd * HEAD_PAD, (hd + 1) * HEAD_PAD)
        q_ref[:, cols] = rope(q[:, cols], Q_SCALE).astype(BF16)
        k_ref[:, cols] = (kn[:, cols] + kpe_r).astype(BF16)


def _mla_proj(t, w_out, x, norm_w, mod, layer, w):
    cos, sup, sdn = (jnp.asarray(c) for c in _rope_consts())
    j = layer // 2
    hp = N_HEADS * HEAD_PAD
    tlo, thi, _ = _row_pair(t)
    xlo, xhi, _ = _row_pair(x)
    rope_blk = pl.BlockSpec(
        (TM, HEAD_PAD), lambda i: (jnp.where(i >= TILES_P, (i - TILES_P) % TILES_PER_DEC_SEQ, 0), 0))
    tile = lambda n: pl.BlockSpec((TM, n), lambda i: (i, 0))
    return pl.pallas_call(
        _mla_proj_kernel,
        name="mla_proj",
        grid=(N_TILES,),
        in_specs=[
            *_row_pair_specs(t, HY_WIDTH),
            _entry(w_out, (layer - 1) // 2, True),
            *_row_pair_specs(x, D_MODEL),
            _mod_spec(layer - 1, 2),
            _entry(norm_w, layer),
            _mod_spec(layer, 0), _mod_spec(layer, 1),
            _entry(w["w_a"], j, True), _entry(w["w_g"], j, True), _entry(w["w_pe"], j, True),
            _entry(w["q_norm"], j), _entry(w["kv_norm"], j),
            _entry(w["w_qb"], j, True), _entry(w["w_k"], j, True), _entry(w["w_v"], j, True),
            rope_blk, rope_blk, rope_blk,
        ],
        out_specs=[tile(D_MODEL), tile(hp), tile(hp), tile(N_HEADS * V_HEAD),
                   tile(N_HEADS * V_HEAD), tile(KV_LORA), tile(HEAD_PAD)],
        out_shape=[
            jax.ShapeDtypeStruct((ROWS, D_MODEL), F32),
            jax.ShapeDtypeStruct((ROWS, hp), BF16),
            jax.ShapeDtypeStruct((ROWS, hp), BF16),
            jax.ShapeDtypeStruct((ROWS, N_HEADS * V_HEAD), BF16),
            jax.ShapeDtypeStruct((ROWS, N_HEADS * V_HEAD), F32),
            jax.ShapeDtypeStruct((ROWS, KV_LORA), F32),
            jax.ShapeDtypeStruct((ROWS, HEAD_PAD), F32),
        ],
        scratch_shapes=[pltpu.VMEM(w_out.shape[1:], BF16)],
        compiler_params=_cparams(("arbitrary",), 56 << 20),
    )(tlo, thi, w_out, xlo, xhi, mod, norm_w, mod, mod, w["w_a"], w["w_g"], w["w_pe"], w["q_norm"],
      w["kv_norm"], w["w_qb"], w["w_k"], w["w_v"], cos, sup, sdn)


def _mla_ctx_kernel(ckv_ref, kpe_ref, wk_ref, wv_ref, k_ref, v_ref):
    ckv_b = ckv_ref[...].astype(BF16)
    kn = jnp.dot(ckv_b, wk_ref[...], preferred_element_type=F32)
    v_ref[...] = jnp.dot(ckv_b, wv_ref[...], preferred_element_type=F32).astype(BF16)
    kpe = kpe_ref[...]
    for hd in range(N_HEADS):
        cols = slice(hd * HEAD_PAD, (hd + 1) * HEAD_PAD)
        k_ref[:, cols] = (kn[:, cols] + kpe).astype(BF16)


def _mla_ctx(j, ckv_ctx, kpe_ctx, w):
    hp = N_HEADS * HEAD_PAD
    rows = DEC_BATCH * PAST_LEN
    tile = lambda n: pl.BlockSpec((PAST_LEN, n), lambda i: (i, 0))
    cache = lambda n: pl.BlockSpec((None, None, PAST_LEN, n), lambda i: (i, j, 0, 0))
    return pl.pallas_call(
        _mla_ctx_kernel,
        name="mla_ctx",
        grid=(DEC_BATCH,),
        in_specs=[cache(KV_LORA), cache(HEAD_PAD), _entry(w["w_k"], j, True), _entry(w["w_v"], j, True)],
        out_specs=[tile(hp), tile(N_HEADS * V_HEAD)],
        out_shape=[jax.ShapeDtypeStruct((rows, hp), BF16),
                   jax.ShapeDtypeStruct((rows, N_HEADS * V_HEAD), BF16)],
        compiler_params=_cparams(("arbitrary",), 32 << 20),
    )(ckv_ctx, kpe_ctx, w["w_k"], w["w_v"])


NT_DIMS = (((1,), (1,)), ((), ()))
ATTN_PAIRS = 4
ATTN_TQ = 256


def _attn_kernel(*refs, n_pairs, n_groups, has_ctx, final):
    refs = list(refs)
    q_ref, k_ref, v_ref = refs[:3]
    del refs[:3]
    if has_ctx:
        kc_ref, vc_ref = refs[:2]
        del refs[:2]
    sg_ref, wo_ref, x_ref, g_ref = refs[:4]
    del refs[:4]
    if final:
        fg_ref = refs.pop(0)
    out_ref, o_scr = refs[:2]
    tq = q_ref.shape[0]
    low_half = lax.broadcasted_iota(jnp.int32, (tq, LANES), 1) < V_HEAD
    for p in range(n_pairs):
        vcols = slice(p * LANES, (p + 1) * LANES)
        vp = v_ref[:, vcols]
        outs = []
        for hh in range(2):
            cols = slice((2 * p + hh) * HEAD_PAD, (2 * p + hh + 1) * HEAD_PAD)
            q = q_ref[:, cols]
            s = lax.dot_general(q, k_ref[:, cols], NT_DIMS, preferred_element_type=F32)
            mx = jnp.max(s, axis=-1, keepdims=True)
            if has_ctx:
                sc = lax.dot_general(q, kc_ref[:, cols], NT_DIMS, preferred_element_type=F32)
                mx = jnp.maximum(mx, jnp.max(sc, axis=-1, keepdims=True))
            e = jnp.exp2(s - mx)
            den = jnp.sum(e, axis=-1, keepdims=True)
            pv = jnp.dot(e.astype(BF16), vp, preferred_element_type=F32)
            if has_ctx:
                ec = jnp.exp2(sc - mx)
                den = den + jnp.sum(ec, axis=-1, keepdims=True)
                pv = pv + jnp.dot(ec.astype(BF16), vc_ref[:, vcols], preferred_element_type=F32)
            outs.append(pv / den)
        o = jnp.where(low_half, outs[0], outs[1])
        o_scr[:, vcols] = (o * sg_ref[:, vcols]).astype(BF16)

    u = jnp.dot(o_scr[...], wo_ref[...], preferred_element_type=F32)
    cond = 1 + pl.program_id(0) if has_ctx else 0

    def finish(u_all):
        xn = x_ref[...] + g_ref[pl.ds(cond, 1), :] * u_all
        out_ref[...] = _rms(xn, fg_ref[...]) if final else xn

    if n_groups == 1:
        finish(u)
    else:
        acc_scr = refs[2]
        grp = pl.program_id(2)

        @pl.when(grp == 0)
        def _():
            acc_scr[...] = u

        @pl.when((grp > 0) & (grp < n_groups - 1))
        def _():
            acc_scr[...] += u

        @pl.when(grp == n_groups - 1)
        def _():
            finish(acc_scr[...] + u)


def _attn_prompt(q, k, v, sg, w_o, x, mod, layer, final_g):
    hp = N_HEADS * HEAD_PAD
    nv = N_HEADS * V_HEAD
    xlo, _, _ = _row_pair(x)
    seq = lambda n: pl.BlockSpec((SEQ, n), lambda b: (b, 0))
    final = final_g is not None
    return pl.pallas_call(
        functools.partial(_attn_kernel, n_pairs=N_HEADS // 2, n_groups=1, has_ctx=False, final=final),
        name="attn_prompt",
        grid=(BATCH,),
        in_specs=[seq(hp), seq(hp), seq(nv), seq(nv), _entry(w_o, layer // 2, True), seq(D_MODEL),
                  _mod_spec(layer, 2)] + ([pl.BlockSpec((1, D_MODEL), lambda b: (0, 0))] if final else []),
        out_specs=seq(D_MODEL),
        out_shape=jax.ShapeDtypeStruct((ROWS_P, D_MODEL), F32),
        scratch_shapes=[pltpu.VMEM((SEQ, nv), BF16)],
        compiler_params=_cparams(("arbitrary",), 40 << 20),
    )(q, k, v, sg, w_o, xlo, mod, *([final_g] if final else []))


def _attn_latent(q, k, v, kc, vc, sg, w_o, x, mod, layer, final_g):
    tq = ATTN_TQ
    npair = ATTN_PAIRS
    ngrp = N_HEADS // 2 // npair
    tiles = DEC_SEQ // tq
    q0 = ROWS_P // tq
    s0 = ROWS_P // DEC_SEQ
    wide = npair * 2 * HEAD_PAD
    narrow = npair * LANES
    _, xhi, base = _row_pair(x)
    x0 = base // tq
    final = final_g is not None
    qrow = lambda b, t, p: (q0 + b * tiles + t, p)
    return pl.pallas_call(
        functools.partial(_attn_kernel, n_pairs=npair, n_groups=ngrp, has_ctx=True, final=final),
        name="attn_latent",
        grid=(DEC_BATCH, tiles, ngrp),
        in_specs=[pl.BlockSpec((tq, wide), qrow),
                  pl.BlockSpec((DEC_SEQ, wide), lambda b, t, p: (s0 + b, p)),
                  pl.BlockSpec((DEC_SEQ, narrow), lambda b, t, p: (s0 + b, p)),
                  pl.BlockSpec((PAST_LEN, wide), lambda b, t, p: (b, p)),
                  pl.BlockSpec((PAST_LEN, narrow), lambda b, t, p: (b, p)),
                  pl.BlockSpec((tq, narrow), qrow),
                  pl.BlockSpec((None, narrow, D_MODEL), lambda b, t, p: (layer // 2, p, 0)),
                  pl.BlockSpec((tq, D_MODEL), lambda b, t, p: (x0 + b * tiles + t, 0)),
                  _mod_spec(layer, 2)] + ([pl.BlockSpec((1, D_MODEL), lambda b, t, p: (0, 0))] if final else []),
        out_specs=pl.BlockSpec((tq, D_MODEL), lambda b, t, p: (b * tiles + t, 0)),
        out_shape=jax.ShapeDtypeStruct((ROWS_S, D_MODEL), F32),
        scratch_shapes=[pltpu.VMEM((tq, narrow), BF16), pltpu.VMEM((tq, D_MODEL), F32)],
        compiler_params=_cparams(("arbitrary", "arbitrary", "arbitrary"), 48 << 20),
    )(q, k, v, kc, vc, sg, w_o, xhi, mod, *([final_g] if final else []))


def _mla_weights(w_in, q_norm, w_qb, kv_norm, w_kvb, w_o):
    n = w_in.shape[0]
    o_pe = Q_LORA + KV_LORA
    o_gate = o_pe + QK_ROPE
    w_pe = jnp.pad(w_in[..., o_pe:o_gate], ((0, 0), (0, 0), (QK_NOPE, HEAD_PAD - QK_NOPE - QK_ROPE)))
    qb = w_qb.reshape(n, Q_LORA, N_HEADS, QK_NOPE + QK_ROPE)
    qb = jnp.pad(qb, ((0, 0), (0, 0), (0, 0), (0, HEAD_PAD - QK_NOPE - QK_ROPE)))
    kvb = w_kvb.reshape(n, KV_LORA, N_HEADS, QK_NOPE + V_HEAD)
    wk = jnp.pad(kvb[..., :QK_NOPE], ((0, 0), (0, 0), (0, 0), (0, HEAD_PAD - QK_NOPE)))
    wv = kvb[..., QK_NOPE:]
    return {
        "w_a": w_in[..., :o_pe].astype(BF16),
        "w_g": w_in[..., o_gate:].astype(BF16),
        "w_pe": w_pe.astype(BF16),
        "q_norm": q_norm[:, None, :],
        "kv_norm": kv_norm[:, None, :],
        "w_qb": qb.reshape(n, Q_LORA, N_HEADS * HEAD_PAD).astype(BF16),
        "w_k": wk.reshape(n, KV_LORA, N_HEADS * HEAD_PAD).astype(BF16),
        "w_v": wv.reshape(n, KV_LORA, N_HEADS * V_HEAD).astype(BF16),
        "w_o": w_o.astype(BF16),
    }


def kernel(x_prompt, x_sample, cache_ckv, cache_kpe, c, c_ctx, norm_w, ada_w, ada_b, hy_w_in, hy_conv_w, hy_conv_b, hy_f_w1, hy_f_b1, hy_f_freq, hy_f_w2, hy_f_b2, hy_f_w3, hy_f_bias, hy_w_out, mla_w_in, mla_q_norm, mla_w_qb, mla_kv_norm, mla_w_kvb, mla_w_o, final_norm):
    x = (x_prompt.reshape(ROWS_P, D_MODEL), x_sample.reshape(ROWS_S, D_MODEL))
    cond = jnp.concatenate([c_ctx[None, :], c, jnp.zeros((N_COND - 1 - DEC_BATCH, D_MODEL), F32)], axis=0)
    mod = _ada_all(cond, ada_w, ada_b)

    nw = norm_w[:, None, :]
    hy_conv_b3 = hy_conv_b[:, None, :]
    f_w1 = jnp.pad(hy_f_w1, ((0, 0), (0, FILTER_HIDDEN - FILTER_EMB), (0, 0)))
    f_b1, f_freq, f_b2, f_bias = (a[:, None, :] for a in (hy_f_b1, hy_f_freq, hy_f_b2, hy_f_bias))
    w = _mla_weights(mla_w_in, mla_q_norm, mla_w_qb, mla_kv_norm, mla_w_kvb, mla_w_o)
    kpe_ctx = jnp.pad(cache_kpe, ((0, 0), (0, 0), (0, 0), (QK_NOPE, HEAD_PAD - QK_NOPE - QK_ROPE)))

    new_ckv, new_kpe = [], []
    for layer in range(DEPTH):
        j = layer // 2
        if layer % 2 == 0:
            z, m = _hy_in(x, nw, mod, layer, hy_w_in, hy_conv_w, hy_conv_b3)
            t = []
            for L, blk0, nblk in ((SEQ, 0, ROWS_P // SEQ_BLOCK), (DEC_SEQ, ROWS_P // SEQ_BLOCK, ROWS_S // SEQ_BLOCK)):
                filt = _hy_filter(L, j, f_w1, f_b1, f_freq, hy_f_w2, f_b2, hy_f_w3)
                cm, sm = _dft_mats(L)
                a, b, nq = _hy_spectrum(L, j, filt, f_bias, cm, sm)
                t.append(_hy_lconv(L, blk0, nblk, z, m, cm, sm, a, b, nq))
            t = tuple(t)
        else:
            x, q, k, v, sg, ckv, kpe = _mla_proj(t, hy_w_out, x, nw, mod, layer, w)
            kc, vc = _mla_ctx(j, cache_ckv, kpe_ctx, w)
            final_g = final_norm[None, :] if layer == DEPTH - 1 else None
            x = (_attn_prompt(q, k, v, sg, w["w_o"], x, mod, layer, final_g),
                 _attn_latent(q, k, v, kc, vc, sg, w["w_o"], x, mod, layer, final_g))
            new_ckv.append(ckv[:ROWS_P].reshape(BATCH, SEQ, KV_LORA))
            new_kpe.append(kpe[:ROWS_P, QK_NOPE:QK_NOPE + QK_ROPE].reshape(BATCH, SEQ, QK_ROPE))

    assert DEPTH % 2 == 0
    y_prompt, y_sample = x
    return (y_prompt.reshape(BATCH, SEQ, D_MODEL), y_sample.reshape(DEC_BATCH, DEC_SEQ, D_MODEL),
            jnp.stack(new_ckv, axis=1), jnp.stack(new_kpe, axis=1))
```

```python
import functools
import math

import numpy as np
import jax
import jax.numpy as jnp
from jax import lax
from jax.experimental import pallas as pl
from jax.experimental.pallas import tpu as pltpu

F32 = jnp.float32
BF16 = jnp.bfloat16

D_MODEL = 1024
BATCH = 16
SEQ = 256
DEPTH = 4
DEC_BATCH = 2
DEC_SEQ = 2048
PAST_LEN = 512
GRID_W = 64
EPS = 1e-6
HY_WIDTH = D_MODEL
FILTER_BANDS = 16
FILTER_EMB = 1 + 2 * FILTER_BANDS
FILTER_HIDDEN = 64
FAST_DECAY_PCT = 0.3
SLOW_DECAY_PCT = 1.5
DECAY_TARGET = 1e-2
N_HEADS = 16
Q_LORA = 384
KV_LORA = 256
QK_NOPE = 64
QK_ROPE = 32
V_HEAD = 64
ROPE_THETA = 10000.0

LANES = 128
MXU_TILE = 256
HEAD_PAD = LANES
ROWS_P = BATCH * SEQ
ROWS_S = DEC_BATCH * DEC_SEQ
ROWS = ROWS_P + ROWS_S
TM = 512
N_TILES = ROWS // TM
TILES_P = ROWS_P // TM
TILES_PER_DEC_SEQ = DEC_SEQ // TM
N_COND = 8
SEQ_BLOCK = 2048
TK = 512
LCONV_TC = 256
VMEM_CAP = 56 * 1024 * 1024


def _cparams(sem, vmem_bytes):
    return pltpu.CompilerParams(dimension_semantics=sem, vmem_limit_bytes=min(int(vmem_bytes), VMEM_CAP))


def _resident(shape, index_map):
    return pl.BlockSpec(shape, index_map, pipeline_mode=pl.Buffered(1))


def _entry(arr, idx, resident=False):
    zeros = (0,) * (arr.ndim - 1)
    return pl.BlockSpec((None,) + arr.shape[1:], lambda *_: (idx,) + zeros,
                        pipeline_mode=pl.Buffered(1) if resident else None)


def _cond_row(i, rows=TM):
    return jnp.where(i < ROWS_P // rows, 0, 1 + (i - ROWS_P // rows) // (DEC_SEQ // rows))


def _silu(x):
    return x * jax.nn.sigmoid(x)


def _rms(x, g):
    return x * lax.rsqrt(jnp.mean(x * x, axis=-1, keepdims=True) + EPS) * g


def _row_pair(x):
    if isinstance(x, tuple):
        return x[0], x[1], 0
    return x, x, ROWS_P


def _row_pair_specs(x, n, rows=TM):
    _, _, base = _row_pair(x)
    n_lo = ROWS_P // rows
    lo = pl.BlockSpec((rows, n), lambda i, *_: (jnp.minimum(i, n_lo - 1), 0))
    hi = pl.BlockSpec((rows, n), lambda i, *_: (jnp.maximum(i - n_lo, 0) + base // rows, 0))
    return lo, hi


def _pick_rows(i, lo_ref, hi_ref, rows=TM):
    return jnp.where(i < ROWS_P // rows, lo_ref[...], hi_ref[...])


def _ada_kernel(cond_ref, w_ref, b_ref, o_ref):
    s = _silu(cond_ref[...]).astype(BF16)
    o_ref[...] = jnp.dot(s, w_ref[...].astype(BF16), preferred_element_type=F32) + b_ref[...]


def _ada_all(cond, ada_w, ada_b):
    tn = 1024
    return pl.pallas_call(
        _ada_kernel,
        name="ada",
        grid=(DEPTH, 3 * D_MODEL // tn),
        in_specs=[
            pl.BlockSpec((N_COND, D_MODEL), lambda l, j: (0, 0)),
            pl.BlockSpec((None, D_MODEL, tn), lambda l, j: (l, 0, j)),
            pl.BlockSpec((None, 1, tn), lambda l, j: (l, 0, j)),
        ],
        out_specs=pl.BlockSpec((None, N_COND, tn), lambda l, j: (l, 0, j)),
        out_shape=jax.ShapeDtypeStruct((DEPTH, N_COND, 3 * D_MODEL), F32),
        compiler_params=_cparams(("arbitrary", "arbitrary"), 32 << 20),
    )(cond, ada_w, ada_b.reshape(DEPTH, 1, 3 * D_MODEL))


def _mod_spec(layer, part):
    return pl.BlockSpec((None, N_COND, D_MODEL), lambda i, *_: (layer, 0, part))


def _modnorm(i, x, nw_ref, sh_ref, sc_ref):
    c = _cond_row(i)
    return _rms(x, nw_ref[...]) * (1.0 + sc_ref[pl.ds(c, 1), :]) + sh_ref[pl.ds(c, 1), :]


HALO = 16
HY_ROWS = TM


def _halo_specs(x):
    _, _, base = _row_pair(x)
    r = HY_ROWS // HALO
    n_lo = ROWS_P // HY_ROWS
    lo_last = ROWS_P // HALO - 1
    hi_last = ROWS_S // HALO - 1
    lo_blk = lambda i: jnp.minimum(i, n_lo - 1)
    hi_blk = lambda i: jnp.maximum(i - n_lo, 0)
    blk = lambda f: pl.BlockSpec((HALO, D_MODEL), lambda i: (f(i), 0))
    return [
        blk(lambda i: jnp.maximum(lo_blk(i) * r - 1, 0)),
        blk(lambda i: jnp.maximum(hi_blk(i) * r - 1, 0) + base // HALO),
        blk(lambda i: jnp.minimum((lo_blk(i) + 1) * r, lo_last)),
        blk(lambda i: jnp.minimum((hi_blk(i) + 1) * r, hi_last) + base // HALO),
    ]


def _hy_in_kernel(xlo_ref, xhi_ref, plo_ref, phi_ref, nlo_ref, nhi_ref, nw_ref, sh_ref, sc_ref, w_ref,
                  cw_ref, cb_ref, z_ref, m_ref, h_scr, *u_scrs):
    i = pl.program_id(0)
    width = z_ref.shape[1]
    c = _cond_row(i, HY_ROWS)
    norm = lambda lo_ref, hi_ref: (_rms(_pick_rows(i, lo_ref, hi_ref, HY_ROWS), nw_ref[...])
                                   * (1.0 + sc_ref[pl.ds(c, 1), :]) + sh_ref[pl.ds(c, 1), :]).astype(BF16)
    h_scr[0:HALO, :] = norm(plo_ref, phi_ref)
    h_scr[HALO:HALO + HY_ROWS, :] = norm(xlo_ref, xhi_ref)
    h_scr[HALO + HY_ROWS:HALO + HY_ROWS + HALO, :] = norm(nlo_ref, nhi_ref)

    seq_mask = jnp.where(i < ROWS_P // HY_ROWS, SEQ - 1, DEC_SEQ - 1)
    row = lax.broadcasted_iota(jnp.int32, (TM, width), 0)
    for t in range(HY_ROWS // TM):
        r0 = t * TM
        pos = (row + (i * HY_ROWS + r0)) & seq_mask
        first = pos == 0
        last = pos == seq_mask
        rows_h = pl.ds(r0, TM + 2 * HALO)

        def conv(g, u_scr):
            cols = slice(g * width, (g + 1) * width)
            u_scr[...] = jnp.dot(h_scr[rows_h, :], w_ref[:, cols], preferred_element_type=F32)
            prev = jnp.where(first, 0.0, u_scr[HALO - 1:HALO - 1 + TM, :])
            nxt = jnp.where(last, 0.0, u_scr[HALO + 1:HALO + 1 + TM, :])
            return (cb_ref[:, cols] + prev * cw_ref[0:1, cols] + u_scr[HALO:HALO + TM, :] * cw_ref[1:2, cols]
                    + nxt * cw_ref[2:3, cols])

        u0, u1, u2 = u_scrs[3 * t:3 * t + 3]
        out_rows = pl.ds(r0, TM)
        z_ref[out_rows, :] = (conv(2, u2) * conv(1, u1)).astype(BF16)
        gate = jnp.dot(h_scr[pl.ds(r0 + HALO, TM), :], w_ref[:, 3 * width:4 * width], preferred_element_type=F32)
        m_ref[out_rows, :] = (conv(0, u0) * _silu(gate)).astype(BF16)


def _hy_in(x, norm_w, mod, layer, w_in, conv_w, conv_b):
    j = layer // 2
    xlo, xhi, _ = _row_pair(x)
    out = pl.BlockSpec((HY_ROWS, HY_WIDTH), lambda i: (i, 0))
    n_tiles = HY_ROWS // TM
    return pl.pallas_call(
        _hy_in_kernel,
        name="hy_in",
        grid=(ROWS // HY_ROWS,),
        in_specs=[
            *_row_pair_specs(x, D_MODEL, HY_ROWS),
            *_halo_specs(x),
            _entry(norm_w, layer),
            _mod_spec(layer, 0),
            _mod_spec(layer, 1),
            _entry(w_in, j, resident=True),
            _entry(conv_w, j),
            _entry(conv_b, j),
        ],
        out_specs=[out, out],
        out_shape=[jax.ShapeDtypeStruct((ROWS, HY_WIDTH), BF16)] * 2,
        scratch_shapes=([pltpu.VMEM((HY_ROWS + 2 * HALO, D_MODEL), BF16)]
                        + [pltpu.VMEM((TM + 2 * HALO, HY_WIDTH), F32)] * (3 * n_tiles)),
        compiler_params=_cparams(("arbitrary",), 56 << 20),
    )(xlo, xhi, xlo, xhi, xlo, xhi, norm_w, mod, mod, w_in, conv_w, conv_b)


@functools.lru_cache(maxsize=None)
def _filter_consts(L):
    t = np.linspace(0.0, 1.0, L)[:, None]
    w = (2.0 * math.pi / L) * np.arange(L)[:, None]
    bands = np.linspace(1e-4, FILTER_BANDS - 1, FILTER_BANDS)[None, :]
    emb = np.concatenate([t, np.cos(bands * w), -np.sin(bands * w)], axis=-1)
    emb = np.pad(emb, ((0, 0), (0, FILTER_HIDDEN - FILTER_EMB)))
    max_decay = math.log(DECAY_TARGET) / FAST_DECAY_PCT
    min_decay = math.log(DECAY_TARGET) / SLOW_DECAY_PCT
    deltas = np.abs(np.linspace(min_decay, max_decay, HY_WIDTH))
    deltas = np.concatenate([deltas, deltas])[None, :]
    return emb.astype(np.float32), t.astype(np.float32), deltas.astype(np.float32)


def _filter_kernel(emb_ref, t_ref, w1_ref, b1_ref, fr_ref, w2_ref, b2_ref, w3_ref, dl_ref, o_ref, h_scr):
    @pl.when(pl.program_id(0) == 0)
    def _():
        hi = lax.Precision.HIGHEST
        fr = fr_ref[...]
        h = jnp.sin(fr * (jnp.dot(emb_ref[...], w1_ref[...], precision=hi, preferred_element_type=F32) + b1_ref[...]))
        h = jnp.sin(fr * (jnp.dot(h, w2_ref[...], precision=hi, preferred_element_type=F32) + b2_ref[...]))
        h_scr[...] = h.astype(BF16)

    h = jnp.dot(h_scr[...], w3_ref[...].astype(BF16), preferred_element_type=F32)
    h = h * jnp.exp(-t_ref[...] * dl_ref[...])
    o_ref[...] = h / jnp.sum(jnp.abs(h), axis=0, keepdims=True)


def _hy_filter(L, hy, w1, b1, freq, w2, b2, w3):
    emb, t, deltas = _filter_consts(L)
    tcf = 512
    full = lambda shape: pl.BlockSpec(shape, lambda c: (0, 0))
    return pl.pallas_call(
        _filter_kernel,
        name=f"hy_filter{L}",
        grid=(2 * HY_WIDTH // tcf,),
        in_specs=[
            full((L, FILTER_HIDDEN)), full((L, 1)), _entry(w1, hy), _entry(b1, hy), _entry(freq, hy),
            _entry(w2, hy), _entry(b2, hy),
            pl.BlockSpec((None, FILTER_HIDDEN, tcf), lambda c: (hy, 0, c)),
            pl.BlockSpec((1, tcf), lambda c: (0, c)),
        ],
        out_specs=pl.BlockSpec((L, tcf), lambda c: (0, c)),
        out_shape=jax.ShapeDtypeStruct((L, 2 * HY_WIDTH), F32),
        scratch_shapes=[pltpu.VMEM((L, FILTER_HIDDEN), BF16)],
        compiler_params=_cparams(("arbitrary",), 40 << 20),
    )(jnp.asarray(emb), jnp.asarray(t), w1, b1, freq, w2, b2, w3, jnp.asarray(deltas))


@functools.lru_cache(maxsize=None)
def _dft_consts(L):
    k = np.arange(L, dtype=np.int64)
    ang = (np.outer(k, k) % (2 * L)).astype(np.float64) * (math.pi / L)
    return np.cos(ang).astype(np.float32), np.sin(ang).astype(np.float32)


def _dft_mats(L):
    c, s = _dft_consts(L)
    return jnp.asarray(c).astype(BF16), jnp.asarray(s).astype(BF16)


def _alt_sign(shape):
    return (1 - 2 * (lax.broadcasted_iota(jnp.int32, shape, 0) & 1)).astype(F32)


def _lconv_kernel(z_ref, m_ref, ff_ref, fb_ref, bias_ref, c_ref, s_ref, t_ref, a_scr, b_scr, nq_scr, y_scr,
                  *, L, tk):
    rows, tc = y_scr.shape
    alt = _alt_sign((L, tc))
    bins = [slice(k0, k0 + tk) for k0 in range(0, L, tk)]

    @pl.when(pl.program_id(1) == 0)
    def _():
        inv_n = 1.0 / (2.0 * L)
        bias = bias_ref[...]
        ff = ff_ref[...]
        fb = fb_ref[...]
        f = ff + fb
        nq_scr[...] = jnp.broadcast_to((jnp.sum(f * alt, axis=0, keepdims=True) + bias) * inv_n, nq_scr.shape)
        f_b = f.astype(BF16)
        d_b = (fb - ff).astype(BF16)
        for kt in bins:
            hre = jnp.dot(c_ref[kt, :], f_b, preferred_element_type=F32) + bias
            him = jnp.dot(s_ref[kt, :], d_b, preferred_element_type=F32)
            a = hre * (2.0 * inv_n)
            if kt.start == 0:
                a = jnp.where(lax.broadcasted_iota(jnp.int32, hre.shape, 0) == 0, hre * inv_n, a)
            a_scr[kt, :] = a
            b_scr[kt, :] = him * (2.0 * inv_n)

    for s0 in range(0, rows, L):
        sq = pl.ds(s0, L)
        zb = z_ref[sq, :]
        z_nyq = jnp.sum(zb.astype(F32) * alt, axis=0, keepdims=True)
        y_scr[sq, :] = alt * (z_nyq * nq_scr[0:1, :])
        for kt in bins:
            zr = jnp.dot(c_ref[kt, :], zb, preferred_element_type=F32)
            zi = jnp.dot(s_ref[kt, :], zb, preferred_element_type=F32)
            a = a_scr[kt, :]
            b = b_scr[kt, :]
            yr = (zr * a + zi * b).astype(BF16)
            yw = (zi * a - zr * b).astype(BF16)
            y_scr[sq, :] += (jnp.dot(c_ref[:, kt], yr, preferred_element_type=F32)
                             + jnp.dot(s_ref[:, kt], yw, preferred_element_type=F32))
        t_ref[sq, :] = (y_scr[sq, :] * m_ref[sq, :].astype(F32)).astype(BF16)


def _hy_lconv(L, hy, row_block0, n_row_blocks, z, m, filt, f_bias, c, s):
    tk = min(TK, L)
    tc = LCONV_TC
    nc = HY_WIDTH // tc
    blk = pl.BlockSpec((SEQ_BLOCK, tc), lambda j, r: (r + row_block0, j))
    mat = _resident((L, L), lambda j, r: (0, 0))
    return pl.pallas_call(
        functools.partial(_lconv_kernel, L=L, tk=tk),
        name=f"hy_lconv{L}",
        grid=(nc, n_row_blocks),
        in_specs=[blk, blk,
                  pl.BlockSpec((L, tc), lambda j, r: (0, j)),
                  pl.BlockSpec((L, tc), lambda j, r: (0, j + nc)),
                  pl.BlockSpec((None, 1, tc), lambda j, r: (hy, 0, j)),
                  mat, mat],
        out_specs=pl.BlockSpec((SEQ_BLOCK, tc), lambda j, r: (r, j)),
        out_shape=jax.ShapeDtypeStruct((n_row_blocks * SEQ_BLOCK, HY_WIDTH), BF16),
        scratch_shapes=[pltpu.VMEM((L, tc), F32), pltpu.VMEM((L, tc), F32), pltpu.VMEM((8, tc), F32),
                        pltpu.VMEM((SEQ_BLOCK, tc), F32)],
        compiler_params=_cparams(("arbitrary", "arbitrary"), 56 << 20),
    )(z, m, filt, filt, f_bias, c, s)


@functools.lru_cache(maxsize=None)
def _rope_consts():
    axis_dim = QK_ROPE // 2
    nf = axis_dim // 2
    inv = ROPE_THETA ** (-np.arange(0, axis_dim, 2, dtype=np.float64) / axis_dim)
    t = np.arange(DEC_SEQ)
    ang_r = (t // GRID_W)[:, None] * inv
    ang_c = (t % GRID_W)[:, None] * inv
    cos = np.ones((DEC_SEQ, HEAD_PAD))
    sin_up = np.zeros((DEC_SEQ, HEAD_PAD))
    sin_dn = np.zeros((DEC_SEQ, HEAD_PAD))
    for base, ang in ((QK_NOPE, ang_r), (QK_NOPE + axis_dim, ang_c)):
        cos[:, base:base + nf] = np.cos(ang)
        cos[:, base + nf:base + 2 * nf] = np.cos(ang)
        sin_up[:, base:base + nf] = -np.sin(ang)
        sin_dn[:, base + nf:base + 2 * nf] = np.sin(ang)
    return cos.astype(np.float32), sin_up.astype(np.float32), sin_dn.astype(np.float32)


ROPE_HALF = QK_ROPE // 4
Q_SCALE = math.log2(math.e) / math.sqrt(QK_NOPE + QK_ROPE)


def _mla_proj_kernel(tlo_ref, thi_ref, wout_ref, xlo_ref, xhi_ref, gprev_ref,
                     nw_ref, sh_ref, sc_ref, wa_ref, wg_ref, wpe_ref, qn_ref, kvn_ref, wqb_ref, wk_ref, wv_ref,
                     cos_ref, sup_ref, sdn_ref,
                     xn_ref, q_ref, k_ref, v_ref, sg_ref, ckv_ref, kpe_ref):
    i = pl.program_id(0)
    u = jnp.dot(_pick_rows(i, tlo_ref, thi_ref), wout_ref[...], preferred_element_type=F32)
    x = _pick_rows(i, xlo_ref, xhi_ref) + gprev_ref[pl.ds(_cond_row(i), 1), :] * u
    xn_ref[...] = x
    h = _modnorm(i, x, nw_ref, sh_ref, sc_ref).astype(BF16)
    lora = jnp.dot(h, wa_ref[...], preferred_element_type=F32)
    gate = jnp.dot(h, wg_ref[...], preferred_element_type=F32)
    kpe = jnp.dot(h, wpe_ref[...], preferred_element_type=F32)
    qn = _rms(lora[:, 0:Q_LORA], qn_ref[...]).astype(BF16)
    ckv = _rms(lora[:, Q_LORA:Q_LORA + KV_LORA], kvn_ref[...])
    ckv_ref[...] = ckv
    kpe_ref[...] = kpe
    sg_ref[...] = _silu(gate)
    ckv_b = ckv.astype(BF16)
    v_ref[...] = jnp.dot(ckv_b, wv_ref[...], preferred_element_type=F32).astype(BF16)
    q = jnp.dot(qn, wqb_ref[...], preferred_element_type=F32)
    kn = jnp.dot(ckv_b, wk_ref[...], preferred_element_type=F32)

    latent = i >= TILES_P
    cos = jnp.where(latent, cos_ref[...], 1.0)
    sup = jnp.where(latent, sup_ref[...], 0.0)
    sdn = jnp.where(latent, sdn_ref[...], 0.0)

    def rope(u, scale):
        return (u * (cos * scale) + pltpu.roll(u, HEAD_PAD - ROPE_HALF, axis=1) * (sup * scale)
                + pltpu.roll(u, ROPE_HALF, axis=1) * (sdn * scale))

    kpe_r = rope(kpe, 1.0)
    for hd in range(N_HEADS):
        cols = slice(hd * HEAD_PAD, (hd + 1) * HEAD_PAD)
        q_ref[:, cols] = rope(q[:, cols], Q_SCALE).astype(BF16)
        k_ref[:, cols] = (kn[:, cols] + kpe_r).astype(BF16)


def _mla_proj(t, w_out, x, norm_w, mod, layer, w):
    cos, sup, sdn = (jnp.asarray(c) for c in _rope_consts())
    j = layer // 2
    hp = N_HEADS * HEAD_PAD
    tlo, thi, _ = _row_pair(t)
    xlo, xhi, _ = _row_pair(x)
    rope_blk = pl.BlockSpec(
        (TM, HEAD_PAD), lambda i: (jnp.where(i >= TILES_P, (i - TILES_P) % TILES_PER_DEC_SEQ, 0), 0))
    tile = lambda n: pl.BlockSpec((TM, n), lambda i: (i, 0))
    return pl.pallas_call(
        _mla_proj_kernel,
        name="mla_proj",
        grid=(N_TILES,),
        in_specs=[
            *_row_pair_specs(t, HY_WIDTH),
            _entry(w_out, (layer - 1) // 2, True),
            *_row_pair_specs(x, D_MODEL),
            _mod_spec(layer - 1, 2),
            _entry(norm_w, layer),
            _mod_spec(layer, 0), _mod_spec(layer, 1),
            _entry(w["w_a"], j, True), _entry(w["w_g"], j, True), _entry(w["w_pe"], j, True),
            _entry(w["q_norm"], j), _entry(w["kv_norm"], j),
            _entry(w["w_qb"], j, True), _entry(w["w_k"], j, True), _entry(w["w_v"], j, True),
            rope_blk, rope_blk, rope_blk,
        ],
        out_specs=[tile(D_MODEL), tile(hp), tile(hp), tile(N_HEADS * V_HEAD),
                   tile(N_HEADS * V_HEAD), tile(KV_LORA), tile(HEAD_PAD)],
        out_shape=[
            jax.ShapeDtypeStruct((ROWS, D_MODEL), F32),
            jax.ShapeDtypeStruct((ROWS, hp), BF16),
            jax.ShapeDtypeStruct((ROWS, hp), BF16),
            jax.ShapeDtypeStruct((ROWS, N_HEADS * V_HEAD), BF16),
            jax.ShapeDtypeStruct((ROWS, N_HEADS * V_HEAD), F32),
            jax.ShapeDtypeStruct((ROWS, KV_LORA), F32),
            jax.ShapeDtypeStruct((ROWS, HEAD_PAD), F32),
        ],
        compiler_params=_cparams(("arbitrary",), 56 << 20),
    )(tlo, thi, w_out, xlo, xhi, mod, norm_w, mod, mod, w["w_a"], w["w_g"], w["w_pe"], w["q_norm"],
      w["kv_norm"], w["w_qb"], w["w_k"], w["w_v"], cos, sup, sdn)


def _mla_ctx_kernel(ckv_ref, kpe_ref, wk_ref, wv_ref, k_ref, v_ref):
    ckv_b = ckv_ref[...].astype(BF16)
    kn = jnp.dot(ckv_b, wk_ref[...], preferred_element_type=F32)
    v_ref[...] = jnp.dot(ckv_b, wv_ref[...], preferred_element_type=F32).astype(BF16)
    kpe = kpe_ref[...]
    for hd in range(N_HEADS):
        cols = slice(hd * HEAD_PAD, (hd + 1) * HEAD_PAD)
        k_ref[:, cols] = (kn[:, cols] + kpe).astype(BF16)


def _mla_ctx(j, ckv_ctx, kpe_ctx, w):
    hp = N_HEADS * HEAD_PAD
    rows = DEC_BATCH * PAST_LEN
    tile = lambda n: pl.BlockSpec((PAST_LEN, n), lambda i: (i, 0))
    cache = lambda n: pl.BlockSpec((None, None, PAST_LEN, n), lambda i: (i, j, 0, 0))
    return pl.pallas_call(
        _mla_ctx_kernel,
        name="mla_ctx",
        grid=(DEC_BATCH,),
        in_specs=[cache(KV_LORA), cache(HEAD_PAD), _entry(w["w_k"], j, True), _entry(w["w_v"], j, True)],
        out_specs=[tile(hp), tile(N_HEADS * V_HEAD)],
        out_shape=[jax.ShapeDtypeStruct((rows, hp), BF16),
                   jax.ShapeDtypeStruct((rows, N_HEADS * V_HEAD), BF16)],
        compiler_params=_cparams(("arbitrary",), 32 << 20),
    )(ckv_ctx, kpe_ctx, w["w_k"], w["w_v"])


NT_DIMS = (((1,), (1,)), ((), ()))
ATTN_PAIRS = 4
ATTN_TQ = 256


def _attn_kernel(*refs, n_pairs, n_groups, has_ctx, final):
    refs = list(refs)
    q_ref, k_ref, v_ref = refs[:3]
    del refs[:3]
    if has_ctx:
        kc_ref, vc_ref = refs[:2]
        del refs[:2]
    sg_ref, wo_ref, x_ref, g_ref = refs[:4]
    del refs[:4]
    if final:
        fg_ref = refs.pop(0)
    out_ref, o_scr = refs[:2]
    tq = q_ref.shape[0]
    low_half = lax.broadcasted_iota(jnp.int32, (tq, LANES), 1) < V_HEAD
    for p in range(n_pairs):
        vcols = slice(p * LANES, (p + 1) * LANES)
        vp = v_ref[:, vcols]
        outs = []
        for hh in range(2):
            cols = slice((2 * p + hh) * HEAD_PAD, (2 * p + hh + 1) * HEAD_PAD)
            q = q_ref[:, cols]
            s = lax.dot_general(q, k_ref[:, cols], NT_DIMS, preferred_element_type=F32)
            mx = jnp.max(s, axis=-1, keepdims=True)
            if has_ctx:
                sc = lax.dot_general(q, kc_ref[:, cols], NT_DIMS, preferred_element_type=F32)
                mx = jnp.maximum(mx, jnp.max(sc, axis=-1, keepdims=True))
            e = jnp.exp2(s - mx)
            den = jnp.sum(e, axis=-1, keepdims=True)
            pv = jnp.dot(e.astype(BF16), vp, preferred_element_type=F32)
            if has_ctx:
                ec = jnp.exp2(sc - mx)
                den = den + jnp.sum(ec, axis=-1, keepdims=True)
                pv = pv + jnp.dot(ec.astype(BF16), vc_ref[:, vcols], preferred_element_type=F32)
            outs.append(pv / den)
        o = jnp.where(low_half, outs[0], outs[1])
        o_scr[:, vcols] = (o * sg_ref[:, vcols]).astype(BF16)

    u = jnp.dot(o_scr[...], wo_ref[...], preferred_element_type=F32)
    cond = 1 + pl.program_id(0) if has_ctx else 0

    def finish(u_all):
        xn = x_ref[...] + g_ref[pl.ds(cond, 1), :] * u_all
        out_ref[...] = _rms(xn, fg_ref[...]) if final else xn

    if n_groups == 1:
        finish(u)
    else:
        acc_scr = refs[2]
        grp = pl.program_id(2)

        @pl.when(grp == 0)
        def _():
            acc_scr[...] = u

        @pl.when((grp > 0) & (grp < n_groups - 1))
        def _():
            acc_scr[...] += u

        @pl.when(grp == n_groups - 1)
        def _():
            finish(acc_scr[...] + u)


def _attn_prompt(q, k, v, sg, w_o, x, mod, layer, final_g):
    hp = N_HEADS * HEAD_PAD
    nv = N_HEADS * V_HEAD
    xlo, _, _ = _row_pair(x)
    seq = lambda n: pl.BlockSpec((SEQ, n), lambda b: (b, 0))
    final = final_g is not None
    return pl.pallas_call(
        functools.partial(_attn_kernel, n_pairs=N_HEADS // 2, n_groups=1, has_ctx=False, final=final),
        name="attn_prompt",
        grid=(BATCH,),
        in_specs=[seq(hp), seq(hp), seq(nv), seq(nv), _entry(w_o, layer // 2, True), seq(D_MODEL),
                  _mod_spec(layer, 2)] + ([pl.BlockSpec((1, D_MODEL), lambda b: (0, 0))] if final else []),
        out_specs=seq(D_MODEL),
        out_shape=jax.ShapeDtypeStruct((ROWS_P, D_MODEL), F32),
        scratch_shapes=[pltpu.VMEM((SEQ, nv), BF16)],
        compiler_params=_cparams(("arbitrary",), 40 << 20),
    )(q, k, v, sg, w_o, xlo, mod, *([final_g] if final else []))


def _attn_latent(q, k, v, kc, vc, sg, w_o, x, mod, layer, final_g):
    tq = ATTN_TQ
    npair = ATTN_PAIRS
    ngrp = N_HEADS // 2 // npair
    tiles = DEC_SEQ // tq
    q0 = ROWS_P // tq
    s0 = ROWS_P // DEC_SEQ
    wide = npair * 2 * HEAD_PAD
    narrow = npair * LANES
    _, xhi, base = _row_pair(x)
    x0 = base // tq
    final = final_g is not None
    qrow = lambda b, t, p: (q0 + b * tiles + t, p)
    return pl.pallas_call(
        functools.partial(_attn_kernel, n_pairs=npair, n_groups=ngrp, has_ctx=True, final=final),
        name="attn_latent",
        grid=(DEC_BATCH, tiles, ngrp),
        in_specs=[pl.BlockSpec((tq, wide), qrow),
                  pl.BlockSpec((DEC_SEQ, wide), lambda b, t, p: (s0 + b, p)),
                  pl.BlockSpec((DEC_SEQ, narrow), lambda b, t, p: (s0 + b, p)),
                  pl.BlockSpec((PAST_LEN, wide), lambda b, t, p: (b, p)),
                  pl.BlockSpec((PAST_LEN, narrow), lambda b, t, p: (b, p)),
                  pl.BlockSpec((tq, narrow), qrow),
                  pl.BlockSpec((None, narrow, D_MODEL), lambda b, t, p: (layer // 2, p, 0)),
                  pl.BlockSpec((tq, D_MODEL), lambda b, t, p: (x0 + b * tiles + t, 0)),
                  _mod_spec(layer, 2)] + ([pl.BlockSpec((1, D_MODEL), lambda b, t, p: (0, 0))] if final else []),
        out_specs=pl.BlockSpec((tq, D_MODEL), lambda b, t, p: (b * tiles + t, 0)),
        out_shape=jax.ShapeDtypeStruct((ROWS_S, D_MODEL), F32),
        scratch_shapes=[pltpu.VMEM((tq, narrow), BF16), pltpu.VMEM((tq, D_MODEL), F32)],
        compiler_params=_cparams(("arbitrary", "arbitrary", "arbitrary"), 48 << 20),
    )(q, k, v, kc, vc, sg, w_o, xhi, mod, *([final_g] if final else []))


def _mla_weights(w_in, q_norm, w_qb, kv_norm, w_kvb, w_o):
    n = w_in.shape[0]
    o_pe = Q_LORA + KV_LORA
    o_gate = o_pe + QK_ROPE
    w_pe = jnp.pad(w_in[..., o_pe:o_gate], ((0, 0), (0, 0), (QK_NOPE, HEAD_PAD - QK_NOPE - QK_ROPE)))
    qb = w_qb.reshape(n, Q_LORA, N_HEADS, QK_NOPE + QK_ROPE)
    qb = jnp.pad(qb, ((0, 0), (0, 0), (0, 0), (0, HEAD_PAD - QK_NOPE - QK_ROPE)))
    kvb = w_kvb.reshape(n, KV_LORA, N_HEADS, QK_NOPE + V_HEAD)
    wk = jnp.pad(kvb[..., :QK_NOPE], ((0, 0), (0, 0), (0, 0), (0, HEAD_PAD - QK_NOPE)))
    wv = kvb[..., QK_NOPE:]
    return {
        "w_a": w_in[..., :o_pe].astype(BF16),
        "w_g": w_in[..., o_gate:].astype(BF16),
        "w_pe": w_pe.astype(BF16),
        "q_norm": q_norm[:, None, :],
        "kv_norm": kv_norm[:, None, :],
        "w_qb": qb.reshape(n, Q_LORA, N_HEADS * HEAD_PAD).astype(BF16),
        "w_k": wk.reshape(n, KV_LORA, N_HEADS * HEAD_PAD).astype(BF16),
        "w_v": wv.reshape(n, KV_LORA, N_HEADS * V_HEAD).astype(BF16),
        "w_o": w_o.astype(BF16),
    }


def kernel(x_prompt, x_sample, cache_ckv, cache_kpe, c, c_ctx, norm_w, ada_w, ada_b, hy_w_in, hy_conv_w, hy_conv_b, hy_f_w1, hy_f_b1, hy_f_freq, hy_f_w2, hy_f_b2, hy_f_w3, hy_f_bias, hy_w_out, mla_w_in, mla_q_norm, mla_w_qb, mla_kv_norm, mla_w_kvb, mla_w_o, final_norm):
    x = (x_prompt.reshape(ROWS_P, D_MODEL), x_sample.reshape(ROWS_S, D_MODEL))
    cond = jnp.concatenate([c_ctx[None, :], c, jnp.zeros((N_COND - 1 - DEC_BATCH, D_MODEL), F32)], axis=0)
    mod = _ada_all(cond, ada_w, ada_b)

    nw = norm_w[:, None, :]
    hy_w_in_b = hy_w_in.astype(BF16)
    hy_w_out_b = hy_w_out.astype(BF16)
    hy_conv_b3 = hy_conv_b[:, None, :]
    f_w1 = jnp.pad(hy_f_w1, ((0, 0), (0, FILTER_HIDDEN - FILTER_EMB), (0, 0)))
    f_b1, f_freq, f_b2, f_bias = (a[:, None, :] for a in (hy_f_b1, hy_f_freq, hy_f_b2, hy_f_bias))
    w = _mla_weights(mla_w_in, mla_q_norm, mla_w_qb, mla_kv_norm, mla_w_kvb, mla_w_o)
    kpe_ctx = jnp.pad(cache_kpe, ((0, 0), (0, 0), (0, 0), (QK_NOPE, HEAD_PAD - QK_NOPE - QK_ROPE)))

    new_ckv, new_kpe = [], []
    for layer in range(DEPTH):
        j = layer // 2
        if layer % 2 == 0:
            z, m = _hy_in(x, nw, mod, layer, hy_w_in_b, hy_conv_w, hy_conv_b3)
            t = []
            for L, blk0, nblk in ((SEQ, 0, ROWS_P // SEQ_BLOCK), (DEC_SEQ, ROWS_P // SEQ_BLOCK, ROWS_S // SEQ_BLOCK)):
                filt = _hy_filter(L, j, f_w1, f_b1, f_freq, hy_f_w2, f_b2, hy_f_w3)
                cm, sm = _dft_mats(L)
                t.append(_hy_lconv(L, j, blk0, nblk, z, m, filt, f_bias, cm, sm))
            t = tuple(t)
        else:
            x, q, k, v, sg, ckv, kpe = _mla_proj(t, hy_w_out_b, x, nw, mod, layer, w)
            kc, vc = _mla_ctx(j, cache_ckv, kpe_ctx, w)
            final_g = final_norm[None, :] if layer == DEPTH - 1 else None
            x = (_attn_prompt(q, k, v, sg, w["w_o"], x, mod, layer, final_g),
                 _attn_latent(q, k, v, kc, vc, sg, w["w_o"], x, mod, layer, final_g))
            new_ckv.append(ckv[:ROWS_P].reshape(BATCH, SEQ, KV_LORA))
            new_kpe.append(kpe[:ROWS_P, QK_NOPE:QK_NOPE + QK_ROPE].reshape(BATCH, SEQ, QK_ROPE))

    assert DEPTH % 2 == 0
    y_prompt, y_sample = x
    return (y_prompt.reshape(BATCH, SEQ, D_MODEL), y_sample.reshape(DEC_BATCH, DEC_SEQ, D_MODEL),
            jnp.stack(new_ckv, axis=1), jnp.stack(new_kpe, axis=1))
```

```python
import functools
import math

import numpy as np
import jax
import jax.numpy as jnp
from jax import lax
from jax.experimental import pallas as pl
from jax.experimental.pallas import tpu as pltpu

F32 = jnp.float32
BF16 = jnp.bfloat16

D_MODEL = 1024
BATCH = 16
SEQ = 256
DEPTH = 4
DEC_BATCH = 2
DEC_SEQ = 2048
PAST_LEN = 512
GRID_W = 64
EPS = 1e-6
HY_WIDTH = D_MODEL
FILTER_BANDS = 16
FILTER_EMB = 1 + 2 * FILTER_BANDS
FILTER_HIDDEN = 64
FAST_DECAY_PCT = 0.3
SLOW_DECAY_PCT = 1.5
DECAY_TARGET = 1e-2
N_HEADS = 16
Q_LORA = 384
KV_LORA = 256
QK_NOPE = 64
QK_ROPE = 32
V_HEAD = 64
ROPE_THETA = 10000.0

LANES = 128
MXU_TILE = 256
HEAD_PAD = LANES
ROWS_P = BATCH * SEQ
ROWS_S = DEC_BATCH * DEC_SEQ
ROWS = ROWS_P + ROWS_S
TM = 512
N_TILES = ROWS // TM
TILES_P = ROWS_P // TM
TILES_PER_DEC_SEQ = DEC_SEQ // TM
N_COND = 8
SEQ_BLOCK = 2048
TK = 512
LCONV_TC = 512
VMEM_CAP = 56 * 1024 * 1024


def _cparams(sem, vmem_bytes):
    return pltpu.CompilerParams(dimension_semantics=sem, vmem_limit_bytes=min(int(vmem_bytes), VMEM_CAP))


def _resident(shape, index_map):
    return pl.BlockSpec(shape, index_map, pipeline_mode=pl.Buffered(1))


def _entry(arr, idx, resident=False):
    zeros = (0,) * (arr.ndim - 1)
    return pl.BlockSpec((None,) + arr.shape[1:], lambda *_: (idx,) + zeros,
                        pipeline_mode=pl.Buffered(1) if resident else None)


def _cond_row(i, rows=TM):
    return jnp.where(i < ROWS_P // rows, 0, 1 + (i - ROWS_P // rows) // (DEC_SEQ // rows))


def _silu(x):
    return x * jax.nn.sigmoid(x)


def _rms(x, g):
    return x * lax.rsqrt(jnp.mean(x * x, axis=-1, keepdims=True) + EPS) * g


def _row_pair(x):
    if isinstance(x, tuple):
        return x[0], x[1], 0
    return x, x, ROWS_P


def _row_pair_specs(x, n, rows=TM):
    _, _, base = _row_pair(x)
    n_lo = ROWS_P // rows
    lo = pl.BlockSpec((rows, n), lambda i, *_: (jnp.minimum(i, n_lo - 1), 0))
    hi = pl.BlockSpec((rows, n), lambda i, *_: (jnp.maximum(i - n_lo, 0) + base // rows, 0))
    return lo, hi


def _pick_rows(i, lo_ref, hi_ref, rows=TM):
    return jnp.where(i < ROWS_P // rows, lo_ref[...], hi_ref[...])


def _ada_kernel(cond_ref, w_ref, b_ref, o_ref):
    s = _silu(cond_ref[...]).astype(BF16)
    o_ref[...] = jnp.dot(s, w_ref[...].astype(BF16), preferred_element_type=F32) + b_ref[...]


def _ada_all(cond, ada_w, ada_b):
    tn = 1024
    return pl.pallas_call(
        _ada_kernel,
        name="ada",
        grid=(DEPTH, 3 * D_MODEL // tn),
        in_specs=[
            pl.BlockSpec((N_COND, D_MODEL), lambda l, j: (0, 0)),
            pl.BlockSpec((None, D_MODEL, tn), lambda l, j: (l, 0, j)),
            pl.BlockSpec((None, 1, tn), lambda l, j: (l, 0, j)),
        ],
        out_specs=pl.BlockSpec((None, N_COND, tn), lambda l, j: (l, 0, j)),
        out_shape=jax.ShapeDtypeStruct((DEPTH, N_COND, 3 * D_MODEL), F32),
        compiler_params=_cparams(("arbitrary", "arbitrary"), 32 << 20),
    )(cond, ada_w, ada_b.reshape(DEPTH, 1, 3 * D_MODEL))


def _mod_spec(layer, part):
    return pl.BlockSpec((None, N_COND, D_MODEL), lambda i, *_: (layer, 0, part))


def _modnorm(i, x, nw_ref, sh_ref, sc_ref):
    c = _cond_row(i)
    return _rms(x, nw_ref[...]) * (1.0 + sc_ref[pl.ds(c, 1), :]) + sh_ref[pl.ds(c, 1), :]


HALO = 16
HY_ROWS = TM


def _halo_specs(x):
    _, _, base = _row_pair(x)
    r = HY_ROWS // HALO
    n_lo = ROWS_P // HY_ROWS
    lo_last = ROWS_P // HALO - 1
    hi_last = ROWS_S // HALO - 1
    lo_blk = lambda i: jnp.minimum(i, n_lo - 1)
    hi_blk = lambda i: jnp.maximum(i - n_lo, 0)
    blk = lambda f: pl.BlockSpec((HALO, D_MODEL), lambda i: (f(i), 0))
    return [
        blk(lambda i: jnp.maximum(lo_blk(i) * r - 1, 0)),
        blk(lambda i: jnp.maximum(hi_blk(i) * r - 1, 0) + base // HALO),
        blk(lambda i: jnp.minimum((lo_blk(i) + 1) * r, lo_last)),
        blk(lambda i: jnp.minimum((hi_blk(i) + 1) * r, hi_last) + base // HALO),
    ]


def _hy_in_kernel(xlo_ref, xhi_ref, plo_ref, phi_ref, nlo_ref, nhi_ref, nw_ref, sh_ref, sc_ref, w_ref,
                  cw_ref, cb_ref, z_ref, m_ref, h_scr, *u_scrs):
    i = pl.program_id(0)
    width = z_ref.shape[1]
    c = _cond_row(i, HY_ROWS)
    norm = lambda lo_ref, hi_ref: (_rms(_pick_rows(i, lo_ref, hi_ref, HY_ROWS), nw_ref[...])
                                   * (1.0 + sc_ref[pl.ds(c, 1), :]) + sh_ref[pl.ds(c, 1), :]).astype(BF16)
    h_scr[0:HALO, :] = norm(plo_ref, phi_ref)
    h_scr[HALO:HALO + HY_ROWS, :] = norm(xlo_ref, xhi_ref)
    h_scr[HALO + HY_ROWS:HALO + HY_ROWS + HALO, :] = norm(nlo_ref, nhi_ref)

    seq_mask = jnp.where(i < ROWS_P // HY_ROWS, SEQ - 1, DEC_SEQ - 1)
    row = lax.broadcasted_iota(jnp.int32, (TM, width), 0)
    for t in range(HY_ROWS // TM):
        r0 = t * TM
        pos = (row + (i * HY_ROWS + r0)) & seq_mask
        first = pos == 0
        last = pos == seq_mask
        rows_h = pl.ds(r0, TM + 2 * HALO)

        def conv(g, u_scr):
            cols = slice(g * width, (g + 1) * width)
            u_scr[...] = jnp.dot(h_scr[rows_h, :], w_ref[:, cols], preferred_element_type=F32)
            prev = jnp.where(first, 0.0, u_scr[HALO - 1:HALO - 1 + TM, :])
            nxt = jnp.where(last, 0.0, u_scr[HALO + 1:HALO + 1 + TM, :])
            return (cb_ref[:, cols] + prev * cw_ref[0:1, cols] + u_scr[HALO:HALO + TM, :] * cw_ref[1:2, cols]
                    + nxt * cw_ref[2:3, cols])

        u0, u1, u2 = u_scrs[3 * t:3 * t + 3]
        out_rows = pl.ds(r0, TM)
        z_ref[out_rows, :] = (conv(2, u2) * conv(1, u1)).astype(BF16)
        gate = jnp.dot(h_scr[pl.ds(r0 + HALO, TM), :], w_ref[:, 3 * width:4 * width], preferred_element_type=F32)
        m_ref[out_rows, :] = (conv(0, u0) * _silu(gate)).astype(BF16)


def _hy_in(x, norm_w, mod, layer, w_in, conv_w, conv_b):
    j = layer // 2
    xlo, xhi, _ = _row_pair(x)
    out = pl.BlockSpec((HY_ROWS, HY_WIDTH), lambda i: (i, 0))
    n_tiles = HY_ROWS // TM
    return pl.pallas_call(
        _hy_in_kernel,
        name="hy_in",
        grid=(ROWS // HY_ROWS,),
        in_specs=[
            *_row_pair_specs(x, D_MODEL, HY_ROWS),
            *_halo_specs(x),
            _entry(norm_w, layer),
            _mod_spec(layer, 0),
            _mod_spec(layer, 1),
            _entry(w_in, j, resident=True),
            _entry(conv_w, j),
            _entry(conv_b, j),
        ],
        out_specs=[out, out],
        out_shape=[jax.ShapeDtypeStruct((ROWS, HY_WIDTH), BF16)] * 2,
        scratch_shapes=([pltpu.VMEM((HY_ROWS + 2 * HALO, D_MODEL), BF16)]
                        + [pltpu.VMEM((TM + 2 * HALO, HY_WIDTH), F32)] * (3 * n_tiles)),
        compiler_params=_cparams(("arbitrary",), 56 << 20),
    )(xlo, xhi, xlo, xhi, xlo, xhi, norm_w, mod, mod, w_in, conv_w, conv_b)


@functools.lru_cache(maxsize=None)
def _filter_consts(L):
    t = np.linspace(0.0, 1.0, L)[:, None]
    w = (2.0 * math.pi / L) * np.arange(L)[:, None]
    bands = np.linspace(1e-4, FILTER_BANDS - 1, FILTER_BANDS)[None, :]
    emb = np.concatenate([t, np.cos(bands * w), -np.sin(bands * w)], axis=-1)
    emb = np.pad(emb, ((0, 0), (0, FILTER_HIDDEN - FILTER_EMB)))
    max_decay = math.log(DECAY_TARGET) / FAST_DECAY_PCT
    min_decay = math.log(DECAY_TARGET) / SLOW_DECAY_PCT
    deltas = np.abs(np.linspace(min_decay, max_decay, HY_WIDTH))
    deltas = np.concatenate([deltas, deltas])[None, :]
    return emb.astype(np.float32), t.astype(np.float32), deltas.astype(np.float32)


def _filter_kernel(emb_ref, t_ref, w1_ref, b1_ref, fr_ref, w2_ref, b2_ref, w3_ref, dl_ref, o_ref, h_scr):
    @pl.when(pl.program_id(0) == 0)
    def _():
        hi = lax.Precision.HIGHEST
        fr = fr_ref[...]
        h = jnp.sin(fr * (jnp.dot(emb_ref[...], w1_ref[...], precision=hi, preferred_element_type=F32) + b1_ref[...]))
        h = jnp.sin(fr * (jnp.dot(h, w2_ref[...], precision=hi, preferred_element_type=F32) + b2_ref[...]))
        h_scr[...] = h.astype(BF16)

    h = jnp.dot(h_scr[...], w3_ref[...].astype(BF16), preferred_element_type=F32)
    h = h * jnp.exp(-t_ref[...] * dl_ref[...])
    o_ref[...] = h / jnp.sum(jnp.abs(h), axis=0, keepdims=True)


def _hy_filter(L, hy, w1, b1, freq, w2, b2, w3):
    emb, t, deltas = _filter_consts(L)
    tcf = 512
    full = lambda shape: pl.BlockSpec(shape, lambda c: (0, 0))
    return pl.pallas_call(
        _filter_kernel,
        name=f"hy_filter{L}",
        grid=(2 * HY_WIDTH // tcf,),
        in_specs=[
            full((L, FILTER_HIDDEN)), full((L, 1)), _entry(w1, hy), _entry(b1, hy), _entry(freq, hy),
            _entry(w2, hy), _entry(b2, hy),
            pl.BlockSpec((None, FILTER_HIDDEN, tcf), lambda c: (hy, 0, c)),
            pl.BlockSpec((1, tcf), lambda c: (0, c)),
        ],
        out_specs=pl.BlockSpec((L, tcf), lambda c: (0, c)),
        out_shape=jax.ShapeDtypeStruct((L, 2 * HY_WIDTH), F32),
        scratch_shapes=[pltpu.VMEM((L, FILTER_HIDDEN), BF16)],
        compiler_params=_cparams(("arbitrary",), 40 << 20),
    )(jnp.asarray(emb), jnp.asarray(t), w1, b1, freq, w2, b2, w3, jnp.asarray(deltas))


@functools.lru_cache(maxsize=None)
def _dft_consts(L):
    k = np.arange(L, dtype=np.int64)
    ang = (np.outer(k, k) % (2 * L)).astype(np.float64) * (math.pi / L)
    return np.cos(ang).astype(np.float32), np.sin(ang).astype(np.float32)


def _dft_mats(L):
    c, s = _dft_consts(L)
    return jnp.asarray(c).astype(BF16), jnp.asarray(s).astype(BF16)


def _alt_sign(shape):
    return (1 - 2 * (lax.broadcasted_iota(jnp.int32, shape, 0) & 1)).astype(F32)


def _lconv_kernel(z_ref, m_ref, ff_ref, fb_ref, bias_ref, c_ref, s_ref, t_ref, a_scr, b_scr, nq_scr, y_scr,
                  *, L, tk):
    rows, tc = y_scr.shape
    alt = _alt_sign((L, tc))
    bins = [slice(k0, k0 + tk) for k0 in range(0, L, tk)]

    @pl.when(pl.program_id(1) == 0)
    def _():
        inv_n = 1.0 / (2.0 * L)
        bias = bias_ref[...]
        ff = ff_ref[...]
        fb = fb_ref[...]
        f = ff + fb
        nq_scr[...] = jnp.broadcast_to((jnp.sum(f * alt, axis=0, keepdims=True) + bias) * inv_n, nq_scr.shape)
        f_b = f.astype(BF16)
        d_b = (fb - ff).astype(BF16)
        for kt in bins:
            hre = jnp.dot(c_ref[kt, :], f_b, preferred_element_type=F32) + bias
            him = jnp.dot(s_ref[kt, :], d_b, preferred_element_type=F32)
            a = hre * (2.0 * inv_n)
            if kt.start == 0:
                a = jnp.where(lax.broadcasted_iota(jnp.int32, hre.shape, 0) == 0, hre * inv_n, a)
            a_scr[kt, :] = a
            b_scr[kt, :] = him * (2.0 * inv_n)

    for s0 in range(0, rows, L):
        sq = pl.ds(s0, L)
        zb = z_ref[sq, :]
        z_nyq = jnp.sum(zb.astype(F32) * alt, axis=0, keepdims=True)
        y_scr[sq, :] = alt * (z_nyq * nq_scr[0:1, :])
        for kt in bins:
            zr = jnp.dot(c_ref[kt, :], zb, preferred_element_type=F32)
            zi = jnp.dot(s_ref[kt, :], zb, preferred_element_type=F32)
            a = a_scr[kt, :]
            b = b_scr[kt, :]
            yr = (zr * a + zi * b).astype(BF16)
            yw = (zi * a - zr * b).astype(BF16)
            y_scr[sq, :] += (jnp.dot(c_ref[:, kt], yr, preferred_element_type=F32)
                             + jnp.dot(s_ref[:, kt], yw, preferred_element_type=F32))
        t_ref[sq, :] = (y_scr[sq, :] * m_ref[sq, :].astype(F32)).astype(BF16)


def _hy_lconv(L, hy, row_block0, n_row_blocks, z, m, filt, f_bias, c, s):
    tk = min(TK, L)
    tc = LCONV_TC
    nc = HY_WIDTH // tc
    blk = pl.BlockSpec((SEQ_BLOCK, tc), lambda j, r: (r + row_block0, j))
    mat = _resident((L, L), lambda j, r: (0, 0))
    return pl.pallas_call(
        functools.partial(_lconv_kernel, L=L, tk=tk),
        name=f"hy_lconv{L}",
        grid=(nc, n_row_blocks),
        in_specs=[blk, blk,
                  pl.BlockSpec((L, tc), lambda j, r: (0, j), pipeline_mode=pl.Buffered(1)),
                  pl.BlockSpec((L, tc), lambda j, r: (0, j + nc), pipeline_mode=pl.Buffered(1)),
                  pl.BlockSpec((None, 1, tc), lambda j, r: (hy, 0, j)),
                  mat, mat],
        out_specs=pl.BlockSpec((SEQ_BLOCK, tc), lambda j, r: (r, j)),
        out_shape=jax.ShapeDtypeStruct((n_row_blocks * SEQ_BLOCK, HY_WIDTH), BF16),
        scratch_shapes=[pltpu.VMEM((L, tc), F32), pltpu.VMEM((L, tc), F32), pltpu.VMEM((8, tc), F32),
                        pltpu.VMEM((SEQ_BLOCK, tc), F32)],
        compiler_params=_cparams(("arbitrary", "arbitrary"), 56 << 20),
    )(z, m, filt, filt, f_bias, c, s)


@functools.lru_cache(maxsize=None)
def _rope_consts():
    axis_dim = QK_ROPE // 2
    nf = axis_dim // 2
    inv = ROPE_THETA ** (-np.arange(0, axis_dim, 2, dtype=np.float64) / axis_dim)
    t = np.arange(DEC_SEQ)
    ang_r = (t // GRID_W)[:, None] * inv
    ang_c = (t % GRID_W)[:, None] * inv
    cos = np.ones((DEC_SEQ, HEAD_PAD))
    sin_up = np.zeros((DEC_SEQ, HEAD_PAD))
    sin_dn = np.zeros((DEC_SEQ, HEAD_PAD))
    for base, ang in ((QK_NOPE, ang_r), (QK_NOPE + axis_dim, ang_c)):
        cos[:, base:base + nf] = np.cos(ang)
        cos[:, base + nf:base + 2 * nf] = np.cos(ang)
        sin_up[:, base:base + nf] = -np.sin(ang)
        sin_dn[:, base + nf:base + 2 * nf] = np.sin(ang)
    return cos.astype(np.float32), sin_up.astype(np.float32), sin_dn.astype(np.float32)


ROPE_HALF = QK_ROPE // 4
Q_SCALE = math.log2(math.e) / math.sqrt(QK_NOPE + QK_ROPE)


def _mla_proj_kernel(tlo_ref, thi_ref, wout_ref, xlo_ref, xhi_ref, gprev_ref,
                     nw_ref, sh_ref, sc_ref, wa_ref, wg_ref, wpe_ref, qn_ref, kvn_ref, wqb_ref, wk_ref, wv_ref,
                     cos_ref, sup_ref, sdn_ref,
                     xn_ref, q_ref, k_ref, v_ref, sg_ref, ckv_ref, kpe_ref):
    i = pl.program_id(0)
    u = jnp.dot(_pick_rows(i, tlo_ref, thi_ref), wout_ref[...], preferred_element_type=F32)
    x = _pick_rows(i, xlo_ref, xhi_ref) + gprev_ref[pl.ds(_cond_row(i), 1), :] * u
    xn_ref[...] = x
    h = _modnorm(i, x, nw_ref, sh_ref, sc_ref).astype(BF16)
    lora = jnp.dot(h, wa_ref[...], preferred_element_type=F32)
    gate = jnp.dot(h, wg_ref[...], preferred_element_type=F32)
    kpe = jnp.dot(h, wpe_ref[...], preferred_element_type=F32)
    qn = _rms(lora[:, 0:Q_LORA], qn_ref[...]).astype(BF16)
    ckv = _rms(lora[:, Q_LORA:Q_LORA + KV_LORA], kvn_ref[...])
    ckv_ref[...] = ckv
    kpe_ref[...] = kpe
    sg_ref[...] = _silu(gate)
    ckv_b = ckv.astype(BF16)
    v_ref[...] = jnp.dot(ckv_b, wv_ref[...], preferred_element_type=F32).astype(BF16)
    q = jnp.dot(qn, wqb_ref[...], preferred_element_type=F32)
    kn = jnp.dot(ckv_b, wk_ref[...], preferred_element_type=F32)

    latent = i >= TILES_P
    cos = jnp.where(latent, cos_ref[...], 1.0)
    sup = jnp.where(latent, sup_ref[...], 0.0)
    sdn = jnp.where(latent, sdn_ref[...], 0.0)

    def rope(u, scale):
        return (u * (cos * scale) + pltpu.roll(u, HEAD_PAD - ROPE_HALF, axis=1) * (sup * scale)
                + pltpu.roll(u, ROPE_HALF, axis=1) * (sdn * scale))

    kpe_r = rope(kpe, 1.0)
    for hd in range(N_HEADS):
        cols = slice(hd * HEAD_PAD, (hd + 1) * HEAD_PAD)
        q_ref[:, cols] = rope(q[:, cols], Q_SCALE).astype(BF16)
        k_ref[:, cols] = (kn[:, cols] + kpe_r).astype(BF16)


def _mla_proj(t, w_out, x, norm_w, mod, layer, w):
    cos, sup, sdn = (jnp.asarray(c) for c in _rope_consts())
    j = layer // 2
    hp = N_HEADS * HEAD_PAD
    tlo, thi, _ = _row_pair(t)
    xlo, xhi, _ = _row_pair(x)
    rope_blk = pl.BlockSpec(
        (TM, HEAD_PAD), lambda i: (jnp.where(i >= TILES_P, (i - TILES_P) % TILES_PER_DEC_SEQ, 0), 0))
    tile = lambda n: pl.BlockSpec((TM, n), lambda i: (i, 0))
    return pl.pallas_call(
        _mla_proj_kernel,
        name="mla_proj",
        grid=(N_TILES,),
        in_specs=[
            *_row_pair_specs(t, HY_WIDTH),
            _entry(w_out, (layer - 1) // 2, True),
            *_row_pair_specs(x, D_MODEL),
            _mod_spec(layer - 1, 2),
            _entry(norm_w, layer),
            _mod_spec(layer, 0), _mod_spec(layer, 1),
            _entry(w["w_a"], j, True), _entry(w["w_g"], j, True), _entry(w["w_pe"], j, True),
            _entry(w["q_norm"], j), _entry(w["kv_norm"], j),
            _entry(w["w_qb"], j, True), _entry(w["w_k"], j, True), _entry(w["w_v"], j, True),
            rope_blk, rope_blk, rope_blk,
        ],
        out_specs=[tile(D_MODEL), tile(hp), tile(hp), tile(N_HEADS * V_HEAD),
                   tile(N_HEADS * V_HEAD), tile(KV_LORA), tile(HEAD_PAD)],
        out_shape=[
            jax.ShapeDtypeStruct((ROWS, D_MODEL), F32),
            jax.ShapeDtypeStruct((ROWS, hp), BF16),
            jax.ShapeDtypeStruct((ROWS, hp), BF16),
            jax.ShapeDtypeStruct((ROWS, N_HEADS * V_HEAD), BF16),
            jax.ShapeDtypeStruct((ROWS, N_HEADS * V_HEAD), F32),
            jax.ShapeDtypeStruct((ROWS, KV_LORA), F32),
            jax.ShapeDtypeStruct((ROWS, HEAD_PAD), F32),
        ],
        compiler_params=_cparams(("arbitrary",), 56 << 20),
    )(tlo, thi, w_out, xlo, xhi, mod, norm_w, mod, mod, w["w_a"], w["w_g"], w["w_pe"], w["q_norm"],
      w["kv_norm"], w["w_qb"], w["w_k"], w["w_v"], cos, sup, sdn)


def _mla_ctx_kernel(ckv_ref, kpe_ref, wk_ref, wv_ref, k_ref, v_ref):
    ckv_b = ckv_ref[...].astype(BF16)
    kn = jnp.dot(ckv_b, wk_ref[...], preferred_element_type=F32)
    v_ref[...] = jnp.dot(ckv_b, wv_ref[...], preferred_element_type=F32).astype(BF16)
    kpe = kpe_ref[...]
    for hd in range(N_HEADS):
        cols = slice(hd * HEAD_PAD, (hd + 1) * HEAD_PAD)
        k_ref[:, cols] = (kn[:, cols] + kpe).astype(BF16)


def _mla_ctx(j, ckv_ctx, kpe_ctx, w):
    hp = N_HEADS * HEAD_PAD
    rows = DEC_BATCH * PAST_LEN
    tile = lambda n: pl.BlockSpec((PAST_LEN, n), lambda i: (i, 0))
    cache = lambda n: pl.BlockSpec((None, None, PAST_LEN, n), lambda i: (i, j, 0, 0))
    return pl.pallas_call(
        _mla_ctx_kernel,
        name="mla_ctx",
        grid=(DEC_BATCH,),
        in_specs=[cache(KV_LORA), cache(HEAD_PAD), _entry(w["w_k"], j, True), _entry(w["w_v"], j, True)],
        out_specs=[tile(hp), tile(N_HEADS * V_HEAD)],
        out_shape=[jax.ShapeDtypeStruct((rows, hp), BF16),
                   jax.ShapeDtypeStruct((rows, N_HEADS * V_HEAD), BF16)],
        compiler_params=_cparams(("arbitrary",), 32 << 20),
    )(ckv_ctx, kpe_ctx, w["w_k"], w["w_v"])


NT_DIMS = (((1,), (1,)), ((), ()))
ATTN_PAIRS = 4
ATTN_TQ = 256


def _attn_kernel(*refs, n_pairs, n_groups, has_ctx, final):
    refs = list(refs)
    q_ref, k_ref, v_ref = refs[:3]
    del refs[:3]
    if has_ctx:
        kc_ref, vc_ref = refs[:2]
        del refs[:2]
    sg_ref, wo_ref, x_ref, g_ref = refs[:4]
    del refs[:4]
    if final:
        fg_ref = refs.pop(0)
    out_ref, o_scr = refs[:2]
    tq = q_ref.shape[0]
    low_half = lax.broadcasted_iota(jnp.int32, (tq, LANES), 1) < V_HEAD
    for p in range(n_pairs):
        vcols = slice(p * LANES, (p + 1) * LANES)
        vp = v_ref[:, vcols]
        outs = []
        for hh in range(2):
            cols = slice((2 * p + hh) * HEAD_PAD, (2 * p + hh + 1) * HEAD_PAD)
            q = q_ref[:, cols]
            s = lax.dot_general(q, k_ref[:, cols], NT_DIMS, preferred_element_type=F32)
            mx = jnp.max(s, axis=-1, keepdims=True)
            if has_ctx:
                sc = lax.dot_general(q, kc_ref[:, cols], NT_DIMS, preferred_element_type=F32)
                mx = jnp.maximum(mx, jnp.max(sc, axis=-1, keepdims=True))
            e = jnp.exp2(s - mx)
            den = jnp.sum(e, axis=-1, keepdims=True)
            pv = jnp.dot(e.astype(BF16), vp, preferred_element_type=F32)
            if has_ctx:
                ec = jnp.exp2(sc - mx)
                den = den + jnp.sum(ec, axis=-1, keepdims=True)
                pv = pv + jnp.dot(ec.astype(BF16), vc_ref[:, vcols], preferred_element_type=F32)
            outs.append(pv / den)
        o = jnp.where(low_half, outs[0], outs[1])
        o_scr[:, vcols] = (o * sg_ref[:, vcols]).astype(BF16)

    u = jnp.dot(o_scr[...], wo_ref[...], preferred_element_type=F32)
    cond = 1 + pl.program_id(0) if has_ctx else 0

    def finish(u_all):
        xn = x_ref[...] + g_ref[pl.ds(cond, 1), :] * u_all
        out_ref[...] = _rms(xn, fg_ref[...]) if final else xn

    if n_groups == 1:
        finish(u)
    else:
        acc_scr = refs[2]
        grp = pl.program_id(2)

        @pl.when(grp == 0)
        def _():
            acc_scr[...] = u

        @pl.when((grp > 0) & (grp < n_groups - 1))
        def _():
            acc_scr[...] += u

        @pl.when(grp == n_groups - 1)
        def _():
            finish(acc_scr[...] + u)


def _attn_prompt(q, k, v, sg, w_o, x, mod, layer, final_g):
    hp = N_HEADS * HEAD_PAD
    nv = N_HEADS * V_HEAD
    xlo, _, _ = _row_pair(x)
    seq = lambda n: pl.BlockSpec((SEQ, n), lambda b: (b, 0))
    final = final_g is not None
    return pl.pallas_call(
        functools.partial(_attn_kernel, n_pairs=N_HEADS // 2, n_groups=1, has_ctx=False, final=final),
        name="attn_prompt",
        grid=(BATCH,),
        in_specs=[seq(hp), seq(hp), seq(nv), seq(nv), _entry(w_o, layer // 2, True), seq(D_MODEL),
                  _mod_spec(layer, 2)] + ([pl.BlockSpec((1, D_MODEL), lambda b: (0, 0))] if final else []),
        out_specs=seq(D_MODEL),
        out_shape=jax.ShapeDtypeStruct((ROWS_P, D_MODEL), F32),
        scratch_shapes=[pltpu.VMEM((SEQ, nv), BF16)],
        compiler_params=_cparams(("arbitrary",), 40 << 20),
    )(q, k, v, sg, w_o, xlo, mod, *([final_g] if final else []))


def _attn_latent(q, k, v, kc, vc, sg, w_o, x, mod, layer, final_g):
    tq = ATTN_TQ
    npair = ATTN_PAIRS
    ngrp = N_HEADS // 2 // npair
    tiles = DEC_SEQ // tq
    q0 = ROWS_P // tq
    s0 = ROWS_P // DEC_SEQ
    wide = npair * 2 * HEAD_PAD
    narrow = npair * LANES
    _, xhi, base = _row_pair(x)
    x0 = base // tq
    final = final_g is not None
    qrow = lambda b, t, p: (q0 + b * tiles + t, p)
    return pl.pallas_call(
        functools.partial(_attn_kernel, n_pairs=npair, n_groups=ngrp, has_ctx=True, final=final),
        name="attn_latent",
        grid=(DEC_BATCH, tiles, ngrp),
        in_specs=[pl.BlockSpec((tq, wide), qrow),
                  pl.BlockSpec((DEC_SEQ, wide), lambda b, t, p: (s0 + b, p)),
                  pl.BlockSpec((DEC_SEQ, narrow), lambda b, t, p: (s0 + b, p)),
                  pl.BlockSpec((PAST_LEN, wide), lambda b, t, p: (b, p)),
                  pl.BlockSpec((PAST_LEN, narrow), lambda b, t, p: (b, p)),
                  pl.BlockSpec((tq, narrow), qrow),
                  pl.BlockSpec((None, narrow, D_MODEL), lambda b, t, p: (layer // 2, p, 0)),
                  pl.BlockSpec((tq, D_MODEL), lambda b, t, p: (x0 + b * tiles + t, 0)),
                  _mod_spec(layer, 2)] + ([pl.BlockSpec((1, D_MODEL), lambda b, t, p: (0, 0))] if final else []),
        out_specs=pl.BlockSpec((tq, D_MODEL), lambda b, t, p: (b * tiles + t, 0)),
        out_shape=jax.ShapeDtypeStruct((ROWS_S, D_MODEL), F32),
        scratch_shapes=[pltpu.VMEM((tq, narrow), BF16), pltpu.VMEM((tq, D_MODEL), F32)],
        compiler_params=_cparams(("arbitrary", "arbitrary", "arbitrary"), 48 << 20),
    )(q, k, v, kc, vc, sg, w_o, xhi, mod, *([final_g] if final else []))


def _mla_weights(w_in, q_norm, w_qb, kv_norm, w_kvb, w_o):
    n = w_in.shape[0]
    o_pe = Q_LORA + KV_LORA
    o_gate = o_pe + QK_ROPE
    w_pe = jnp.pad(w_in[..., o_pe:o_gate], ((0, 0), (0, 0), (QK_NOPE, HEAD_PAD - QK_NOPE - QK_ROPE)))
    qb = w_qb.reshape(n, Q_LORA, N_HEADS, QK_NOPE + QK_ROPE)
    qb = jnp.pad(qb, ((0, 0), (0, 0), (0, 0), (0, HEAD_PAD - QK_NOPE - QK_ROPE)))
    kvb = w_kvb.reshape(n, KV_LORA, N_HEADS, QK_NOPE + V_HEAD)
    wk = jnp.pad(kvb[..., :QK_NOPE], ((0, 0), (0, 0), (0, 0), (0, HEAD_PAD - QK_NOPE)))
    wv = kvb[..., QK_NOPE:]
    return {
        "w_a": w_in[..., :o_pe].astype(BF16),
        "w_g": w_in[..., o_gate:].astype(BF16),
        "w_pe": w_pe.astype(BF16),
        "q_norm": q_norm[:, None, :],
        "kv_norm": kv_norm[:, None, :],
        "w_qb": qb.reshape(n, Q_LORA, N_HEADS * HEAD_PAD).astype(BF16),
        "w_k": wk.reshape(n, KV_LORA, N_HEADS * HEAD_PAD).astype(BF16),
        "w_v": wv.reshape(n, KV_LORA, N_HEADS * V_HEAD).astype(BF16),
        "w_o": w_o.astype(BF16),
    }


def kernel(x_prompt, x_sample, cache_ckv, cache_kpe, c, c_ctx, norm_w, ada_w, ada_b, hy_w_in, hy_conv_w, hy_conv_b, hy_f_w1, hy_f_b1, hy_f_freq, hy_f_w2, hy_f_b2, hy_f_w3, hy_f_bias, hy_w_out, mla_w_in, mla_q_norm, mla_w_qb, mla_kv_norm, mla_w_kvb, mla_w_o, final_norm):
    x = (x_prompt.reshape(ROWS_P, D_MODEL), x_sample.reshape(ROWS_S, D_MODEL))
    cond = jnp.concatenate([c_ctx[None, :], c, jnp.zeros((N_COND - 1 - DEC_BATCH, D_MODEL), F32)], axis=0)
    mod = _ada_all(cond, ada_w, ada_b)

    nw = norm_w[:, None, :]
    hy_w_in_b = hy_w_in.astype(BF16)
    hy_w_out_b = hy_w_out.astype(BF16)
    hy_conv_b3 = hy_conv_b[:, None, :]
    f_w1 = jnp.pad(hy_f_w1, ((0, 0), (0, FILTER_HIDDEN - FILTER_EMB), (0, 0)))
    f_b1, f_freq, f_b2, f_bias = (a[:, None, :] for a in (hy_f_b1, hy_f_freq, hy_f_b2, hy_f_bias))
    w = _mla_weights(mla_w_in, mla_q_norm, mla_w_qb, mla_kv_norm, mla_w_kvb, mla_w_o)
    kpe_ctx = jnp.pad(cache_kpe, ((0, 0), (0, 0), (0, 0), (QK_NOPE, HEAD_PAD - QK_NOPE - QK_ROPE)))

    new_ckv, new_kpe = [], []
    for layer in range(DEPTH):
        j = layer // 2
        if layer % 2 == 0:
            z, m = _hy_in(x, nw, mod, layer, hy_w_in_b, hy_conv_w, hy_conv_b3)
            t = []
            for L, blk0, nblk in ((SEQ, 0, ROWS_P // SEQ_BLOCK), (DEC_SEQ, ROWS_P // SEQ_BLOCK, ROWS_S // SEQ_BLOCK)):
                filt = _hy_filter(L, j, f_w1, f_b1, f_freq, hy_f_w2, f_b2, hy_f_w3)
                cm, sm = _dft_mats(L)
                t.append(_hy_lconv(L, j, blk0, nblk, z, m, filt, f_bias, cm, sm))
            t = tuple(t)
        else:
            x, q, k, v, sg, ckv, kpe = _mla_proj(t, hy_w_out_b, x, nw, mod, layer, w)
            kc, vc = _mla_ctx(j, cache_ckv, kpe_ctx, w)
            final_g = final_norm[None, :] if layer == DEPTH - 1 else None
            x = (_attn_prompt(q, k, v, sg, w["w_o"], x, mod, layer, final_g),
                 _attn_latent(q, k, v, kc, vc, sg, w["w_o"], x, mod, layer, final_g))
            new_ckv.append(ckv[:ROWS_P].reshape(BATCH, SEQ, KV_LORA))
            new_kpe.append(kpe[:ROWS_P, QK_NOPE:QK_NOPE + QK_ROPE].reshape(BATCH, SEQ, QK_ROPE))

    assert DEPTH % 2 == 0
    y_prompt, y_sample = x
    return (y_prompt.reshape(BATCH, SEQ, D_MODEL), y_sample.reshape(DEC_BATCH, DEC_SEQ, D_MODEL),
            jnp.stack(new_ckv, axis=1), jnp.stack(new_kpe, axis=1))
```

```python
import functools
import math

import numpy as np
import jax
import jax.numpy as jnp
from jax import lax
from jax.experimental import pallas as pl
from jax.experimental.pallas import tpu as pltpu

F32 = jnp.float32
BF16 = jnp.bfloat16

D_MODEL = 1024
BATCH = 16
SEQ = 256
DEPTH = 4
DEC_BATCH = 2
DEC_SEQ = 2048
PAST_LEN = 512
GRID_W = 64
EPS = 1e-6
HY_WIDTH = D_MODEL
FILTER_BANDS = 16
FILTER_EMB = 1 + 2 * FILTER_BANDS
FILTER_HIDDEN = 64
FAST_DECAY_PCT = 0.3
SLOW_DECAY_PCT = 1.5
DECAY_TARGET = 1e-2
N_HEADS = 16
Q_LORA = 384
KV_LORA = 256
QK_NOPE = 64
QK_ROPE = 32
V_HEAD = 64
ROPE_THETA = 10000.0

LANES = 128
MXU_TILE = 256
HEAD_PAD = LANES
ROWS_P = BATCH * SEQ
ROWS_S = DEC_BATCH * DEC_SEQ
ROWS = ROWS_P + ROWS_S
TM = 512
N_TILES = ROWS // TM
TILES_P = ROWS_P // TM
TILES_PER_DEC_SEQ = DEC_SEQ // TM
N_COND = 8
SEQ_BLOCK = 2048
TK = 512
LCONV_TC = 512
VMEM_CAP = 56 * 1024 * 1024


def _cparams(sem, vmem_bytes):
    return pltpu.CompilerParams(dimension_semantics=sem, vmem_limit_bytes=min(int(vmem_bytes), VMEM_CAP))


def _resident(shape, index_map):
    return pl.BlockSpec(shape, index_map, pipeline_mode=pl.Buffered(1))


def _entry(arr, idx, resident=False):
    zeros = (0,) * (arr.ndim - 1)
    return pl.BlockSpec((None,) + arr.shape[1:], lambda *_: (idx,) + zeros,
                        pipeline_mode=pl.Buffered(1) if resident else None)


def _cond_row(i, rows=TM):
    return jnp.where(i < ROWS_P // rows, 0, 1 + (i - ROWS_P // rows) // (DEC_SEQ // rows))


def _silu(x):
    return x * jax.nn.sigmoid(x)


def _rms(x, g):
    return x * lax.rsqrt(jnp.mean(x * x, axis=-1, keepdims=True) + EPS) * g


def _row_pair(x):
    if isinstance(x, tuple):
        return x[0], x[1], 0
    return x, x, ROWS_P


def _row_pair_specs(x, n, rows=TM):
    _, _, base = _row_pair(x)
    n_lo = ROWS_P // rows
    lo = pl.BlockSpec((rows, n), lambda i, *_: (jnp.minimum(i, n_lo - 1), 0))
    hi = pl.BlockSpec((rows, n), lambda i, *_: (jnp.maximum(i - n_lo, 0) + base // rows, 0))
    return lo, hi


def _pick_rows(i, lo_ref, hi_ref, rows=TM):
    return jnp.where(i < ROWS_P // rows, lo_ref[...], hi_ref[...])


def _ada_kernel(cond_ref, w_ref, b_ref, o_ref):
    s = _silu(cond_ref[...]).astype(BF16)
    o_ref[...] = jnp.dot(s, w_ref[...].astype(BF16), preferred_element_type=F32) + b_ref[...]


def _ada_all(cond, ada_w, ada_b):
    tn = 1024
    return pl.pallas_call(
        _ada_kernel,
        name="ada",
        grid=(DEPTH, 3 * D_MODEL // tn),
        in_specs=[
            pl.BlockSpec((N_COND, D_MODEL), lambda l, j: (0, 0)),
            pl.BlockSpec((None, D_MODEL, tn), lambda l, j: (l, 0, j)),
            pl.BlockSpec((None, 1, tn), lambda l, j: (l, 0, j)),
        ],
        out_specs=pl.BlockSpec((None, N_COND, tn), lambda l, j: (l, 0, j)),
        out_shape=jax.ShapeDtypeStruct((DEPTH, N_COND, 3 * D_MODEL), F32),
        compiler_params=_cparams(("arbitrary", "arbitrary"), 32 << 20),
    )(cond, ada_w, ada_b.reshape(DEPTH, 1, 3 * D_MODEL))


def _mod_spec(layer, part):
    return pl.BlockSpec((None, N_COND, D_MODEL), lambda i, *_: (layer, 0, part))


def _modnorm(i, x, nw_ref, sh_ref, sc_ref):
    c = _cond_row(i)
    return _rms(x, nw_ref[...]) * (1.0 + sc_ref[pl.ds(c, 1), :]) + sh_ref[pl.ds(c, 1), :]


HALO = 16
HY_ROWS = TM


def _halo_specs(x):
    _, _, base = _row_pair(x)
    r = HY_ROWS // HALO
    n_lo = ROWS_P // HY_ROWS
    lo_last = ROWS_P // HALO - 1
    hi_last = ROWS_S // HALO - 1
    lo_blk = lambda i: jnp.minimum(i, n_lo - 1)
    hi_blk = lambda i: jnp.maximum(i - n_lo, 0)
    blk = lambda f: pl.BlockSpec((HALO, D_MODEL), lambda i: (f(i), 0))
    return [
        blk(lambda i: jnp.maximum(lo_blk(i) * r - 1, 0)),
        blk(lambda i: jnp.maximum(hi_blk(i) * r - 1, 0) + base // HALO),
        blk(lambda i: jnp.minimum((lo_blk(i) + 1) * r, lo_last)),
        blk(lambda i: jnp.minimum((hi_blk(i) + 1) * r, hi_last) + base // HALO),
    ]


def _hy_in_kernel(xlo_ref, xhi_ref, plo_ref, phi_ref, nlo_ref, nhi_ref, nw_ref, sh_ref, sc_ref, w_ref,
                  cw_ref, cb_ref, z_ref, m_ref, h_scr, *u_scrs):
    i = pl.program_id(0)
    width = z_ref.shape[1]
    c = _cond_row(i, HY_ROWS)
    norm = lambda lo_ref, hi_ref: (_rms(_pick_rows(i, lo_ref, hi_ref, HY_ROWS), nw_ref[...])
                                   * (1.0 + sc_ref[pl.ds(c, 1), :]) + sh_ref[pl.ds(c, 1), :]).astype(BF16)
    h_scr[0:HALO, :] = norm(plo_ref, phi_ref)
    h_scr[HALO:HALO + HY_ROWS, :] = norm(xlo_ref, xhi_ref)
    h_scr[HALO + HY_ROWS:HALO + HY_ROWS + HALO, :] = norm(nlo_ref, nhi_ref)

    seq_mask = jnp.where(i < ROWS_P // HY_ROWS, SEQ - 1, DEC_SEQ - 1)
    row = lax.broadcasted_iota(jnp.int32, (TM, width), 0)
    for t in range(HY_ROWS // TM):
        r0 = t * TM
        pos = (row + (i * HY_ROWS + r0)) & seq_mask
        first = pos == 0
        last = pos == seq_mask
        rows_h = pl.ds(r0, TM + 2 * HALO)

        def conv(g, u_scr):
            cols = slice(g * width, (g + 1) * width)
            u_scr[...] = jnp.dot(h_scr[rows_h, :], w_ref[:, cols], preferred_element_type=F32)
            prev = jnp.where(first, 0.0, u_scr[HALO - 1:HALO - 1 + TM, :])
            nxt = jnp.where(last, 0.0, u_scr[HALO + 1:HALO + 1 + TM, :])
            return (cb_ref[:, cols] + prev * cw_ref[0:1, cols] + u_scr[HALO:HALO + TM, :] * cw_ref[1:2, cols]
                    + nxt * cw_ref[2:3, cols])

        u0, u1, u2 = u_scrs[3 * t:3 * t + 3]
        out_rows = pl.ds(r0, TM)
        z_ref[out_rows, :] = (conv(2, u2) * conv(1, u1)).astype(BF16)
        gate = jnp.dot(h_scr[pl.ds(r0 + HALO, TM), :], w_ref[:, 3 * width:4 * width], preferred_element_type=F32)
        m_ref[out_rows, :] = (conv(0, u0) * _silu(gate)).astype(BF16)


def _hy_in(x, norm_w, mod, layer, w_in, conv_w, conv_b):
    j = layer // 2
    xlo, xhi, _ = _row_pair(x)
    out = pl.BlockSpec((HY_ROWS, HY_WIDTH), lambda i: (i, 0))
    n_tiles = HY_ROWS // TM
    return pl.pallas_call(
        _hy_in_kernel,
        name="hy_in",
        grid=(ROWS // HY_ROWS,),
        in_specs=[
            *_row_pair_specs(x, D_MODEL, HY_ROWS),
            *_halo_specs(x),
            _entry(norm_w, layer),
            _mod_spec(layer, 0),
            _mod_spec(layer, 1),
            _entry(w_in, j, resident=True),
            _entry(conv_w, j),
            _entry(conv_b, j),
        ],
        out_specs=[out, out],
        out_shape=[jax.ShapeDtypeStruct((ROWS, HY_WIDTH), BF16)] * 2,
        scratch_shapes=([pltpu.VMEM((HY_ROWS + 2 * HALO, D_MODEL), BF16)]
                        + [pltpu.VMEM((TM + 2 * HALO, HY_WIDTH), F32)] * (3 * n_tiles)),
        compiler_params=_cparams(("arbitrary",), 56 << 20),
    )(xlo, xhi, xlo, xhi, xlo, xhi, norm_w, mod, mod, w_in, conv_w, conv_b)


@functools.lru_cache(maxsize=None)
def _filter_consts(L):
    t = np.linspace(0.0, 1.0, L)[:, None]
    w = (2.0 * math.pi / L) * np.arange(L)[:, None]
    bands = np.linspace(1e-4, FILTER_BANDS - 1, FILTER_BANDS)[None, :]
    emb = np.concatenate([t, np.cos(bands * w), -np.sin(bands * w)], axis=-1)
    emb = np.pad(emb, ((0, 0), (0, FILTER_HIDDEN - FILTER_EMB)))
    max_decay = math.log(DECAY_TARGET) / FAST_DECAY_PCT
    min_decay = math.log(DECAY_TARGET) / SLOW_DECAY_PCT
    deltas = np.abs(np.linspace(min_decay, max_decay, HY_WIDTH))
    deltas = np.concatenate([deltas, deltas])[None, :]
    return emb.astype(np.float32), t.astype(np.float32), deltas.astype(np.float32)


def _filter_kernel(emb_ref, t_ref, w1_ref, b1_ref, fr_ref, w2_ref, b2_ref, w3_ref, dl_ref, o_ref, h_scr):
    @pl.when(pl.program_id(0) == 0)
    def _():
        hi = lax.Precision.HIGHEST
        fr = fr_ref[...]
        h = jnp.sin(fr * (jnp.dot(emb_ref[...], w1_ref[...], precision=hi, preferred_element_type=F32) + b1_ref[...]))
        h = jnp.sin(fr * (jnp.dot(h, w2_ref[...], precision=hi, preferred_element_type=F32) + b2_ref[...]))
        h_scr[...] = h.astype(BF16)

    h = jnp.dot(h_scr[...], w3_ref[...].astype(BF16), preferred_element_type=F32)
    h = h * jnp.exp(-t_ref[...] * dl_ref[...])
    o_ref[...] = h / jnp.sum(jnp.abs(h), axis=0, keepdims=True)


def _hy_filter(L, hy, w1, b1, freq, w2, b2, w3):
    emb, t, deltas = _filter_consts(L)
    tcf = 512
    full = lambda shape: pl.BlockSpec(shape, lambda c: (0, 0))
    return pl.pallas_call(
        _filter_kernel,
        name=f"hy_filter{L}",
        grid=(2 * HY_WIDTH // tcf,),
        in_specs=[
            full((L, FILTER_HIDDEN)), full((L, 1)), _entry(w1, hy), _entry(b1, hy), _entry(freq, hy),
            _entry(w2, hy), _entry(b2, hy),
            pl.BlockSpec((None, FILTER_HIDDEN, tcf), lambda c: (hy, 0, c)),
            pl.BlockSpec((1, tcf), lambda c: (0, c)),
        ],
        out_specs=pl.BlockSpec((L, tcf), lambda c: (0, c)),
        out_shape=jax.ShapeDtypeStruct((L, 2 * HY_WIDTH), F32),
        scratch_shapes=[pltpu.VMEM((L, FILTER_HIDDEN), BF16)],
        compiler_params=_cparams(("arbitrary",), 40 << 20),
    )(jnp.asarray(emb), jnp.asarray(t), w1, b1, freq, w2, b2, w3, jnp.asarray(deltas))


@functools.lru_cache(maxsize=None)
def _dft_consts(L):
    k = np.arange(L, dtype=np.int64)
    ang = (np.outer(k, k) % (2 * L)).astype(np.float64) * (math.pi / L)
    return np.cos(ang).astype(np.float32), np.sin(ang).astype(np.float32)


def _dft_mats(L):
    c, s = _dft_consts(L)
    return jnp.asarray(c).astype(BF16), jnp.asarray(s).astype(BF16)


def _alt_sign(shape):
    return (1 - 2 * (lax.broadcasted_iota(jnp.int32, shape, 0) & 1)).astype(F32)


def _lconv_kernel(z_ref, m_ref, ff_ref, fb_ref, bias_ref, c_ref, s_ref, t_ref, a_scr, b_scr, nq_scr, y_scr,
                  *, L, tk):
    rows, tc = y_scr.shape
    alt = _alt_sign((L, tc))
    bins = [slice(k0, k0 + tk) for k0 in range(0, L, tk)]

    @pl.when(pl.program_id(1) == 0)
    def _():
        inv_n = 1.0 / (2.0 * L)
        bias = bias_ref[...]
        ff = ff_ref[...]
        fb = fb_ref[...]
        f = ff + fb
        nq_scr[...] = jnp.broadcast_to((jnp.sum(f * alt, axis=0, keepdims=True) + bias) * inv_n, nq_scr.shape)
        f_b = f.astype(BF16)
        d_b = (fb - ff).astype(BF16)
        for kt in bins:
            hre = jnp.dot(c_ref[kt, :], f_b, preferred_element_type=F32) + bias
            him = jnp.dot(s_ref[kt, :], d_b, preferred_element_type=F32)
            a = hre * (2.0 * inv_n)
            if kt.start == 0:
                a = jnp.where(lax.broadcasted_iota(jnp.int32, hre.shape, 0) == 0, hre * inv_n, a)
            a_scr[kt, :] = a
            b_scr[kt, :] = him * (2.0 * inv_n)

    for s0 in range(0, rows, L):
        sq = pl.ds(s0, L)
        zb = z_ref[sq, :]
        z_nyq = jnp.sum(zb.astype(F32) * alt, axis=0, keepdims=True)
        y_scr[sq, :] = alt * (z_nyq * nq_scr[0:1, :])
        for kt in bins:
            zr = jnp.dot(c_ref[kt, :], zb, preferred_element_type=F32)
            zi = jnp.dot(s_ref[kt, :], zb, preferred_element_type=F32)
            a = a_scr[kt, :]
            b = b_scr[kt, :]
            yr = (zr * a + zi * b).astype(BF16)
            yw = (zi * a - zr * b).astype(BF16)
            y_scr[sq, :] += (jnp.dot(c_ref[:, kt], yr, preferred_element_type=F32)
                             + jnp.dot(s_ref[:, kt], yw, preferred_element_type=F32))
        t_ref[sq, :] = (y_scr[sq, :] * m_ref[sq, :].astype(F32)).astype(BF16)


def _hy_lconv(L, hy, row_block0, n_row_blocks, z, m, filt, f_bias, c, s):
    tk = min(TK, L)
    long_seq = L == SEQ_BLOCK
    tc = LCONV_TC if long_seq else MXU_TILE
    nc = HY_WIDTH // tc
    blk = pl.BlockSpec((SEQ_BLOCK, tc), lambda j, r: (r + row_block0, j))
    mat = _resident((L, L), lambda j, r: (0, 0))
    filt_mode = pl.Buffered(1) if long_seq else None
    return pl.pallas_call(
        functools.partial(_lconv_kernel, L=L, tk=tk),
        name=f"hy_lconv{L}",
        grid=(nc, n_row_blocks),
        in_specs=[blk, blk,
                  pl.BlockSpec((L, tc), lambda j, r: (0, j), pipeline_mode=filt_mode),
                  pl.BlockSpec((L, tc), lambda j, r: (0, j + nc), pipeline_mode=filt_mode),
                  pl.BlockSpec((None, 1, tc), lambda j, r: (hy, 0, j)),
                  mat, mat],
        out_specs=pl.BlockSpec((SEQ_BLOCK, tc), lambda j, r: (r, j)),
        out_shape=jax.ShapeDtypeStruct((n_row_blocks * SEQ_BLOCK, HY_WIDTH), BF16),
        scratch_shapes=[pltpu.VMEM((L, tc), F32), pltpu.VMEM((L, tc), F32), pltpu.VMEM((8, tc), F32),
                        pltpu.VMEM((SEQ_BLOCK, tc), F32)],
        compiler_params=_cparams(("arbitrary", "arbitrary"), 56 << 20),
    )(z, m, filt, filt, f_bias, c, s)


@functools.lru_cache(maxsize=None)
def _rope_consts():
    axis_dim = QK_ROPE // 2
    nf = axis_dim // 2
    inv = ROPE_THETA ** (-np.arange(0, axis_dim, 2, dtype=np.float64) / axis_dim)
    t = np.arange(DEC_SEQ)
    ang_r = (t // GRID_W)[:, None] * inv
    ang_c = (t % GRID_W)[:, None] * inv
    cos = np.ones((DEC_SEQ, HEAD_PAD))
    sin_up = np.zeros((DEC_SEQ, HEAD_PAD))
    sin_dn = np.zeros((DEC_SEQ, HEAD_PAD))
    for base, ang in ((QK_NOPE, ang_r), (QK_NOPE + axis_dim, ang_c)):
        cos[:, base:base + nf] = np.cos(ang)
        cos[:, base + nf:base + 2 * nf] = np.cos(ang)
        sin_up[:, base:base + nf] = -np.sin(ang)
        sin_dn[:, base + nf:base + 2 * nf] = np.sin(ang)
    return cos.astype(np.float32), sin_up.astype(np.float32), sin_dn.astype(np.float32)


ROPE_HALF = QK_ROPE // 4
Q_SCALE = math.log2(math.e) / math.sqrt(QK_NOPE + QK_ROPE)


def _mla_proj_kernel(tlo_ref, thi_ref, wout_ref, xlo_ref, xhi_ref, gprev_ref,
                     nw_ref, sh_ref, sc_ref, wa_ref, wg_ref, wpe_ref, qn_ref, kvn_ref, wqb_ref, wk_ref, wv_ref,
                     cos_ref, sup_ref, sdn_ref,
                     xn_ref, q_ref, k_ref, v_ref, sg_ref, ckv_ref, kpe_ref):
    i = pl.program_id(0)
    u = jnp.dot(_pick_rows(i, tlo_ref, thi_ref), wout_ref[...], preferred_element_type=F32)
    x = _pick_rows(i, xlo_ref, xhi_ref) + gprev_ref[pl.ds(_cond_row(i), 1), :] * u
    xn_ref[...] = x
    h = _modnorm(i, x, nw_ref, sh_ref, sc_ref).astype(BF16)
    lora = jnp.dot(h, wa_ref[...], preferred_element_type=F32)
    gate = jnp.dot(h, wg_ref[...], preferred_element_type=F32)
    kpe = jnp.dot(h, wpe_ref[...], preferred_element_type=F32)
    qn = _rms(lora[:, 0:Q_LORA], qn_ref[...]).astype(BF16)
    ckv = _rms(lora[:, Q_LORA:Q_LORA + KV_LORA], kvn_ref[...])
    ckv_ref[...] = ckv
    kpe_ref[...] = kpe
    sg_ref[...] = _silu(gate)
    ckv_b = ckv.astype(BF16)
    v_ref[...] = jnp.dot(ckv_b, wv_ref[...], preferred_element_type=F32).astype(BF16)
    q = jnp.dot(qn, wqb_ref[...], preferred_element_type=F32)
    kn = jnp.dot(ckv_b, wk_ref[...], preferred_element_type=F32)

    latent = i >= TILES_P
    cos = jnp.where(latent, cos_ref[...], 1.0)
    sup = jnp.where(latent, sup_ref[...], 0.0)
    sdn = jnp.where(latent, sdn_ref[...], 0.0)

    def rope(u, scale):
        return (u * (cos * scale) + pltpu.roll(u, HEAD_PAD - ROPE_HALF, axis=1) * (sup * scale)
                + pltpu.roll(u, ROPE_HALF, axis=1) * (sdn * scale))

    kpe_r = rope(kpe, 1.0)
    for hd in range(N_HEADS):
        cols = slice(hd * HEAD_PAD, (hd + 1) * HEAD_PAD)
        q_ref[:, cols] = rope(q[:, cols], Q_SCALE).astype(BF16)
        k_ref[:, cols] = (kn[:, cols] + kpe_r).astype(BF16)


def _mla_proj(t, w_out, x, norm_w, mod, layer, w):
    cos, sup, sdn = (jnp.asarray(c) for c in _rope_consts())
    j = layer // 2
    hp = N_HEADS * HEAD_PAD
    tlo, thi, _ = _row_pair(t)
    xlo, xhi, _ = _row_pair(x)
    rope_blk = pl.BlockSpec(
        (TM, HEAD_PAD), lambda i: (jnp.where(i >= TILES_P, (i - TILES_P) % TILES_PER_DEC_SEQ, 0), 0))
    tile = lambda n: pl.BlockSpec((TM, n), lambda i: (i, 0))
    return pl.pallas_call(
        _mla_proj_kernel,
        name="mla_proj",
        grid=(N_TILES,),
        in_specs=[
            *_row_pair_specs(t, HY_WIDTH),
            _entry(w_out, (layer - 1) // 2, True),
            *_row_pair_specs(x, D_MODEL),
            _mod_spec(layer - 1, 2),
            _entry(norm_w, layer),
            _mod_spec(layer, 0), _mod_spec(layer, 1),
            _entry(w["w_a"], j, True), _entry(w["w_g"], j, True), _entry(w["w_pe"], j, True),
            _entry(w["q_norm"], j), _entry(w["kv_norm"], j),
            _entry(w["w_qb"], j, True), _entry(w["w_k"], j, True), _entry(w["w_v"], j, True),
            rope_blk, rope_blk, rope_blk,
        ],
        out_specs=[tile(D_MODEL), tile(hp), tile(hp), tile(N_HEADS * V_HEAD),
                   tile(N_HEADS * V_HEAD), tile(KV_LORA), tile(HEAD_PAD)],
        out_shape=[
            jax.ShapeDtypeStruct((ROWS, D_MODEL), F32),
            jax.ShapeDtypeStruct((ROWS, hp), BF16),
            jax.ShapeDtypeStruct((ROWS, hp), BF16),
            jax.ShapeDtypeStruct((ROWS, N_HEADS * V_HEAD), BF16),
            jax.ShapeDtypeStruct((ROWS, N_HEADS * V_HEAD), F32),
            jax.ShapeDtypeStruct((ROWS, KV_LORA), F32),
            jax.ShapeDtypeStruct((ROWS, HEAD_PAD), F32),
        ],
        compiler_params=_cparams(("arbitrary",), 56 << 20),
    )(tlo, thi, w_out, xlo, xhi, mod, norm_w, mod, mod, w["w_a"], w["w_g"], w["w_pe"], w["q_norm"],
      w["kv_norm"], w["w_qb"], w["w_k"], w["w_v"], cos, sup, sdn)


def _mla_ctx_kernel(ckv_ref, kpe_ref, wk_ref, wv_ref, k_ref, v_ref):
    ckv_b = ckv_ref[...].astype(BF16)
    kn = jnp.dot(ckv_b, wk_ref[...], preferred_element_type=F32)
    v_ref[...] = jnp.dot(ckv_b, wv_ref[...], preferred_element_type=F32).astype(BF16)
    kpe = kpe_ref[...]
    for hd in range(N_HEADS):
        cols = slice(hd * HEAD_PAD, (hd + 1) * HEAD_PAD)
        k_ref[:, cols] = (kn[:, cols] + kpe).astype(BF16)


def _mla_ctx(j, ckv_ctx, kpe_ctx, w):
    hp = N_HEADS * HEAD_PAD
    rows = DEC_BATCH * PAST_LEN
    tile = lambda n: pl.BlockSpec((PAST_LEN, n), lambda i: (i, 0))
    cache = lambda n: pl.BlockSpec((None, None, PAST_LEN, n), lambda i: (i, j, 0, 0))
    return pl.pallas_call(
        _mla_ctx_kernel,
        name="mla_ctx",
        grid=(DEC_BATCH,),
        in_specs=[cache(KV_LORA), cache(HEAD_PAD), _entry(w["w_k"], j, True), _entry(w["w_v"], j, True)],
        out_specs=[tile(hp), tile(N_HEADS * V_HEAD)],
        out_shape=[jax.ShapeDtypeStruct((rows, hp), BF16),
                   jax.ShapeDtypeStruct((rows, N_HEADS * V_HEAD), BF16)],
        compiler_params=_cparams(("arbitrary",), 32 << 20),
    )(ckv_ctx, kpe_ctx, w["w_k"], w["w_v"])


NT_DIMS = (((1,), (1,)), ((), ()))
ATTN_PAIRS = 8
ATTN_TQ = 256


def _attn_kernel(*refs, n_pairs, n_groups, has_ctx, final):
    refs = list(refs)
    q_ref, k_ref, v_ref = refs[:3]
    del refs[:3]
    if has_ctx:
        kc_ref, vc_ref = refs[:2]
        del refs[:2]
    sg_ref, wo_ref, x_ref, g_ref = refs[:4]
    del refs[:4]
    if final:
        fg_ref = refs.pop(0)
    out_ref, o_scr = refs[:2]
    tq = q_ref.shape[0]
    low_half = lax.broadcasted_iota(jnp.int32, (tq, LANES), 1) < V_HEAD
    for p in range(n_pairs):
        vcols = slice(p * LANES, (p + 1) * LANES)
        vp = v_ref[:, vcols]
        outs = []
        for hh in range(2):
            cols = slice((2 * p + hh) * HEAD_PAD, (2 * p + hh + 1) * HEAD_PAD)
            q = q_ref[:, cols]
            s = lax.dot_general(q, k_ref[:, cols], NT_DIMS, preferred_element_type=F32)
            mx = jnp.max(s, axis=-1, keepdims=True)
            if has_ctx:
                sc = lax.dot_general(q, kc_ref[:, cols], NT_DIMS, preferred_element_type=F32)
                mx = jnp.maximum(mx, jnp.max(sc, axis=-1, keepdims=True))
            e = jnp.exp2(s - mx)
            den = jnp.sum(e, axis=-1, keepdims=True)
            pv = jnp.dot(e.astype(BF16), vp, preferred_element_type=F32)
            if has_ctx:
                ec = jnp.exp2(sc - mx)
                den = den + jnp.sum(ec, axis=-1, keepdims=True)
                pv = pv + jnp.dot(ec.astype(BF16), vc_ref[:, vcols], preferred_element_type=F32)
            outs.append(pv / den)
        o = jnp.where(low_half, outs[0], outs[1])
        o_scr[:, vcols] = (o * sg_ref[:, vcols]).astype(BF16)

    u = jnp.dot(o_scr[...], wo_ref[...], preferred_element_type=F32)
    cond = 1 + pl.program_id(0) if has_ctx else 0

    def finish(u_all):
        xn = x_ref[...] + g_ref[pl.ds(cond, 1), :] * u_all
        out_ref[...] = _rms(xn, fg_ref[...]) if final else xn

    if n_groups == 1:
        finish(u)
    else:
        acc_scr = refs[2]
        grp = pl.program_id(2)

        @pl.when(grp == 0)
        def _():
            acc_scr[...] = u

        @pl.when((grp > 0) & (grp < n_groups - 1))
        def _():
            acc_scr[...] += u

        @pl.when(grp == n_groups - 1)
        def _():
            finish(acc_scr[...] + u)


def _attn_prompt(q, k, v, sg, w_o, x, mod, layer, final_g):
    hp = N_HEADS * HEAD_PAD
    nv = N_HEADS * V_HEAD
    xlo, _, _ = _row_pair(x)
    seq = lambda n: pl.BlockSpec((SEQ, n), lambda b: (b, 0))
    final = final_g is not None
    return pl.pallas_call(
        functools.partial(_attn_kernel, n_pairs=N_HEADS // 2, n_groups=1, has_ctx=False, final=final),
        name="attn_prompt",
        grid=(BATCH,),
        in_specs=[seq(hp), seq(hp), seq(nv), seq(nv), _entry(w_o, layer // 2, True), seq(D_MODEL),
                  _mod_spec(layer, 2)] + ([pl.BlockSpec((1, D_MODEL), lambda b: (0, 0))] if final else []),
        out_specs=seq(D_MODEL),
        out_shape=jax.ShapeDtypeStruct((ROWS_P, D_MODEL), F32),
        scratch_shapes=[pltpu.VMEM((SEQ, nv), BF16)],
        compiler_params=_cparams(("arbitrary",), 40 << 20),
    )(q, k, v, sg, w_o, xlo, mod, *([final_g] if final else []))


def _attn_latent(q, k, v, kc, vc, sg, w_o, x, mod, layer, final_g):
    tq = ATTN_TQ
    npair = ATTN_PAIRS
    ngrp = N_HEADS // 2 // npair
    tiles = DEC_SEQ // tq
    q0 = ROWS_P // tq
    s0 = ROWS_P // DEC_SEQ
    wide = npair * 2 * HEAD_PAD
    narrow = npair * LANES
    _, xhi, base = _row_pair(x)
    x0 = base // tq
    final = final_g is not None
    qrow = lambda b, t, p: (q0 + b * tiles + t, p)
    kv_mode = pl.Buffered(1) if ngrp == 1 else None
    return pl.pallas_call(
        functools.partial(_attn_kernel, n_pairs=npair, n_groups=ngrp, has_ctx=True, final=final),
        name="attn_latent",
        grid=(DEC_BATCH, tiles, ngrp),
        in_specs=[pl.BlockSpec((tq, wide), qrow),
                  pl.BlockSpec((DEC_SEQ, wide), lambda b, t, p: (s0 + b, p), pipeline_mode=kv_mode),
                  pl.BlockSpec((DEC_SEQ, narrow), lambda b, t, p: (s0 + b, p), pipeline_mode=kv_mode),
                  pl.BlockSpec((PAST_LEN, wide), lambda b, t, p: (b, p), pipeline_mode=kv_mode),
                  pl.BlockSpec((PAST_LEN, narrow), lambda b, t, p: (b, p), pipeline_mode=kv_mode),
                  pl.BlockSpec((tq, narrow), qrow),
                  pl.BlockSpec((None, narrow, D_MODEL), lambda b, t, p: (layer // 2, p, 0)),
                  pl.BlockSpec((tq, D_MODEL), lambda b, t, p: (x0 + b * tiles + t, 0)),
                  _mod_spec(layer, 2)] + ([pl.BlockSpec((1, D_MODEL), lambda b, t, p: (0, 0))] if final else []),
        out_specs=pl.BlockSpec((tq, D_MODEL), lambda b, t, p: (b * tiles + t, 0)),
        out_shape=jax.ShapeDtypeStruct((ROWS_S, D_MODEL), F32),
        scratch_shapes=[pltpu.VMEM((tq, narrow), BF16), pltpu.VMEM((tq, D_MODEL), F32)],
        compiler_params=_cparams(("arbitrary", "arbitrary", "arbitrary"), 56 << 20),
    )(q, k, v, kc, vc, sg, w_o, xhi, mod, *([final_g] if final else []))


def _mla_weights(w_in, q_norm, w_qb, kv_norm, w_kvb, w_o):
    n = w_in.shape[0]
    o_pe = Q_LORA + KV_LORA
    o_gate = o_pe + QK_ROPE
    w_pe = jnp.pad(w_in[..., o_pe:o_gate], ((0, 0), (0, 0), (QK_NOPE, HEAD_PAD - QK_NOPE - QK_ROPE)))
    qb = w_qb.reshape(n, Q_LORA, N_HEADS, QK_NOPE + QK_ROPE)
    qb = jnp.pad(qb, ((0, 0), (0, 0), (0, 0), (0, HEAD_PAD - QK_NOPE - QK_ROPE)))
    kvb = w_kvb.reshape(n, KV_LORA, N_HEADS, QK_NOPE + V_HEAD)
    wk = jnp.pad(kvb[..., :QK_NOPE], ((0, 0), (0, 0), (0, 0), (0, HEAD_PAD - QK_NOPE)))
    wv = kvb[..., QK_NOPE:]
    return {
        "w_a": w_in[..., :o_pe].astype(BF16),
        "w_g": w_in[..., o_gate:].astype(BF16),
        "w_pe": w_pe.astype(BF16),
        "q_norm": q_norm[:, None, :],
        "kv_norm": kv_norm[:, None, :],
        "w_qb": qb.reshape(n, Q_LORA, N_HEADS * HEAD_PAD).astype(BF16),
        "w_k": wk.reshape(n, KV_LORA, N_HEADS * HEAD_PAD).astype(BF16),
        "w_v": wv.reshape(n, KV_LORA, N_HEADS * V_HEAD).astype(BF16),
        "w_o": w_o.astype(BF16),
    }


def kernel(x_prompt, x_sample, cache_ckv, cache_kpe, c, c_ctx, norm_w, ada_w, ada_b, hy_w_in, hy_conv_w, hy_conv_b, hy_f_w1, hy_f_b1, hy_f_freq, hy_f_w2, hy_f_b2, hy_f_w3, hy_f_bias, hy_w_out, mla_w_in, mla_q_norm, mla_w_qb, mla_kv_norm, mla_w_kvb, mla_w_o, final_norm):
    x = (x_prompt.reshape(ROWS_P, D_MODEL), x_sample.reshape(ROWS_S, D_MODEL))
    cond = jnp.concatenate([c_ctx[None, :], c, jnp.zeros((N_COND - 1 - DEC_BATCH, D_MODEL), F32)], axis=0)
    mod = _ada_all(cond, ada_w, ada_b)

    nw = norm_w[:, None, :]
    hy_w_in_b = hy_w_in.astype(BF16)
    hy_w_out_b = hy_w_out.astype(BF16)
    hy_conv_b3 = hy_conv_b[:, None, :]
    f_w1 = jnp.pad(hy_f_w1, ((0, 0), (0, FILTER_HIDDEN - FILTER_EMB), (0, 0)))
    f_b1, f_freq, f_b2, f_bias = (a[:, None, :] for a in (hy_f_b1, hy_f_freq, hy_f_b2, hy_f_bias))
    w = _mla_weights(mla_w_in, mla_q_norm, mla_w_qb, mla_kv_norm, mla_w_kvb, mla_w_o)
    kpe_ctx = jnp.pad(cache_kpe, ((0, 0), (0, 0), (0, 0), (QK_NOPE, HEAD_PAD - QK_NOPE - QK_ROPE)))

    new_ckv, new_kpe = [], []
    for layer in range(DEPTH):
        j = layer // 2
        if layer % 2 == 0:
            z, m = _hy_in(x, nw, mod, layer, hy_w_in_b, hy_conv_w, hy_conv_b3)
            t = []
            for L, blk0, nblk in ((SEQ, 0, ROWS_P // SEQ_BLOCK), (DEC_SEQ, ROWS_P // SEQ_BLOCK, ROWS_S // SEQ_BLOCK)):
                filt = _hy_filter(L, j, f_w1, f_b1, f_freq, hy_f_w2, f_b2, hy_f_w3)
                cm, sm = _dft_mats(L)
                t.append(_hy_lconv(L, j, blk0, nblk, z, m, filt, f_bias, cm, sm))
            t = tuple(t)
        else:
            x, q, k, v, sg, ckv, kpe = _mla_proj(t, hy_w_out_b, x, nw, mod, layer, w)
            kc, vc = _mla_ctx(j, cache_ckv, kpe_ctx, w)
            final_g = final_norm[None, :] if layer == DEPTH - 1 else None
            x = (_attn_prompt(q, k, v, sg, w["w_o"], x, mod, layer, final_g),
                 _attn_latent(q, k, v, kc, vc, sg, w["w_o"], x, mod, layer, final_g))
            new_ckv.append(ckv[:ROWS_P].reshape(BATCH, SEQ, KV_LORA))
            new_kpe.append(kpe[:ROWS_P, QK_NOPE:QK_NOPE + QK_ROPE].reshape(BATCH, SEQ, QK_ROPE))

    assert DEPTH % 2 == 0
    y_prompt, y_sample = x
    return (y_prompt.reshape(BATCH, SEQ, D_MODEL), y_sample.reshape(DEC_BATCH, DEC_SEQ, D_MODEL),
            jnp.stack(new_ckv, axis=1), jnp.stack(new_kpe, axis=1))
```

```python
import functools
import math

import numpy as np
import jax
import jax.numpy as jnp
from jax import lax
from jax.experimental import pallas as pl
from jax.experimental.pallas import tpu as pltpu

F32 = jnp.float32
BF16 = jnp.bfloat16

D_MODEL = 1024
BATCH = 16
SEQ = 256
DEPTH = 4
DEC_BATCH = 2
DEC_SEQ = 2048
PAST_LEN = 512
GRID_W = 64
EPS = 1e-6
HY_WIDTH = D_MODEL
FILTER_BANDS = 16
FILTER_EMB = 1 + 2 * FILTER_BANDS
FILTER_HIDDEN = 64
FAST_DECAY_PCT = 0.3
SLOW_DECAY_PCT = 1.5
DECAY_TARGET = 1e-2
N_HEADS = 16
Q_LORA = 384
KV_LORA = 256
QK_NOPE = 64
QK_ROPE = 32
V_HEAD = 64
ROPE_THETA = 10000.0

LANES = 128
MXU_TILE = 256
HEAD_PAD = LANES
ROWS_P = BATCH * SEQ
ROWS_S = DEC_BATCH * DEC_SEQ
ROWS = ROWS_P + ROWS_S
TM = 512
N_TILES = ROWS // TM
TILES_P = ROWS_P // TM
TILES_PER_DEC_SEQ = DEC_SEQ // TM
N_COND = 8
SEQ_BLOCK = 2048
TK = 512
LCONV_TC = 512
VMEM_CAP = 56 * 1024 * 1024


def _cparams(sem, vmem_bytes):
    return pltpu.CompilerParams(dimension_semantics=sem, vmem_limit_bytes=min(int(vmem_bytes), VMEM_CAP))


def _resident(shape, index_map):
    return pl.BlockSpec(shape, index_map, pipeline_mode=pl.Buffered(1))


def _entry(arr, idx, resident=False):
    zeros = (0,) * (arr.ndim - 1)
    return pl.BlockSpec((None,) + arr.shape[1:], lambda *_: (idx,) + zeros,
                        pipeline_mode=pl.Buffered(1) if resident else None)


def _cond_row(i, rows=TM):
    return jnp.where(i < ROWS_P // rows, 0, 1 + (i - ROWS_P // rows) // (DEC_SEQ // rows))


def _silu(x):
    return x * jax.nn.sigmoid(x)


def _rms(x, g):
    return x * lax.rsqrt(jnp.mean(x * x, axis=-1, keepdims=True) + EPS) * g


def _row_pair(x):
    if isinstance(x, tuple):
        return x[0], x[1], 0
    return x, x, ROWS_P


def _row_pair_specs(x, n, rows=TM):
    _, _, base = _row_pair(x)
    n_lo = ROWS_P // rows
    lo = pl.BlockSpec((rows, n), lambda i, *_: (jnp.minimum(i, n_lo - 1), 0))
    hi = pl.BlockSpec((rows, n), lambda i, *_: (jnp.maximum(i - n_lo, 0) + base // rows, 0))
    return lo, hi


def _pick_rows(i, lo_ref, hi_ref, rows=TM):
    return jnp.where(i < ROWS_P // rows, lo_ref[...], hi_ref[...])


def _ada_kernel(cond_ref, w_ref, b_ref, o_ref):
    s = _silu(cond_ref[...]).astype(BF16)
    o_ref[...] = jnp.dot(s, w_ref[...].astype(BF16), preferred_element_type=F32) + b_ref[...]


def _ada_all(cond, ada_w, ada_b):
    tn = 1024
    return pl.pallas_call(
        _ada_kernel,
        name="ada",
        grid=(DEPTH, 3 * D_MODEL // tn),
        in_specs=[
            pl.BlockSpec((N_COND, D_MODEL), lambda l, j: (0, 0)),
            pl.BlockSpec((None, D_MODEL, tn), lambda l, j: (l, 0, j)),
            pl.BlockSpec((None, 1, tn), lambda l, j: (l, 0, j)),
        ],
        out_specs=pl.BlockSpec((None, N_COND, tn), lambda l, j: (l, 0, j)),
        out_shape=jax.ShapeDtypeStruct((DEPTH, N_COND, 3 * D_MODEL), F32),
        compiler_params=_cparams(("arbitrary", "arbitrary"), 32 << 20),
    )(cond, ada_w, ada_b.reshape(DEPTH, 1, 3 * D_MODEL))


def _mod_spec(layer, part):
    return pl.BlockSpec((None, N_COND, D_MODEL), lambda i, *_: (layer, 0, part))


def _modnorm(i, x, nw_ref, sh_ref, sc_ref):
    c = _cond_row(i)
    return _rms(x, nw_ref[...]) * (1.0 + sc_ref[pl.ds(c, 1), :]) + sh_ref[pl.ds(c, 1), :]


HALO = 16
HY_ROWS = TM


def _halo_specs(x):
    _, _, base = _row_pair(x)
    r = HY_ROWS // HALO
    n_lo = ROWS_P // HY_ROWS
    lo_last = ROWS_P // HALO - 1
    hi_last = ROWS_S // HALO - 1
    lo_blk = lambda i: jnp.minimum(i, n_lo - 1)
    hi_blk = lambda i: jnp.maximum(i - n_lo, 0)
    blk = lambda f: pl.BlockSpec((HALO, D_MODEL), lambda i: (f(i), 0))
    return [
        blk(lambda i: jnp.maximum(lo_blk(i) * r - 1, 0)),
        blk(lambda i: jnp.maximum(hi_blk(i) * r - 1, 0) + base // HALO),
        blk(lambda i: jnp.minimum((lo_blk(i) + 1) * r, lo_last)),
        blk(lambda i: jnp.minimum((hi_blk(i) + 1) * r, hi_last) + base // HALO),
    ]


def _hy_in_kernel(xlo_ref, xhi_ref, plo_ref, phi_ref, nlo_ref, nhi_ref, nw_ref, sh_ref, sc_ref, w_ref,
                  cw_ref, cb_ref, z_ref, m_ref, h_scr, *u_scrs):
    i = pl.program_id(0)
    width = z_ref.shape[1]
    c = _cond_row(i, HY_ROWS)
    norm = lambda lo_ref, hi_ref: (_rms(_pick_rows(i, lo_ref, hi_ref, HY_ROWS), nw_ref[...])
                                   * (1.0 + sc_ref[pl.ds(c, 1), :]) + sh_ref[pl.ds(c, 1), :]).astype(BF16)
    h_scr[0:HALO, :] = norm(plo_ref, phi_ref)
    h_scr[HALO:HALO + HY_ROWS, :] = norm(xlo_ref, xhi_ref)
    h_scr[HALO + HY_ROWS:HALO + HY_ROWS + HALO, :] = norm(nlo_ref, nhi_ref)

    seq_mask = jnp.where(i < ROWS_P // HY_ROWS, SEQ - 1, DEC_SEQ - 1)
    row = lax.broadcasted_iota(jnp.int32, (TM, width), 0)
    for t in range(HY_ROWS // TM):
        r0 = t * TM
        pos = (row + (i * HY_ROWS + r0)) & seq_mask
        first = pos == 0
        last = pos == seq_mask
        rows_h = pl.ds(r0, TM + 2 * HALO)

        def conv(g, u_scr):
            cols = slice(g * width, (g + 1) * width)
            u_scr[...] = jnp.dot(h_scr[rows_h, :], w_ref[:, cols], preferred_element_type=F32)
            prev = jnp.where(first, 0.0, u_scr[HALO - 1:HALO - 1 + TM, :])
            nxt = jnp.where(last, 0.0, u_scr[HALO + 1:HALO + 1 + TM, :])
            return (cb_ref[:, cols] + prev * cw_ref[0:1, cols] + u_scr[HALO:HALO + TM, :] * cw_ref[1:2, cols]
                    + nxt * cw_ref[2:3, cols])

        u0, u1, u2 = u_scrs[3 * t:3 * t + 3]
        out_rows = pl.ds(r0, TM)
        z_ref[out_rows, :] = (conv(2, u2) * conv(1, u1)).astype(BF16)
        gate = jnp.dot(h_scr[pl.ds(r0 + HALO, TM), :], w_ref[:, 3 * width:4 * width], preferred_element_type=F32)
        m_ref[out_rows, :] = (conv(0, u0) * _silu(gate)).astype(BF16)


def _hy_in(x, norm_w, mod, layer, w_in, conv_w, conv_b):
    j = layer // 2
    xlo, xhi, _ = _row_pair(x)
    out = pl.BlockSpec((HY_ROWS, HY_WIDTH), lambda i: (i, 0))
    n_tiles = HY_ROWS // TM
    return pl.pallas_call(
        _hy_in_kernel,
        name="hy_in",
        grid=(ROWS // HY_ROWS,),
        in_specs=[
            *_row_pair_specs(x, D_MODEL, HY_ROWS),
            *_halo_specs(x),
            _entry(norm_w, layer),
            _mod_spec(layer, 0),
            _mod_spec(layer, 1),
            _entry(w_in, j, resident=True),
            _entry(conv_w, j),
            _entry(conv_b, j),
        ],
        out_specs=[out, out],
        out_shape=[jax.ShapeDtypeStruct((ROWS, HY_WIDTH), BF16)] * 2,
        scratch_shapes=([pltpu.VMEM((HY_ROWS + 2 * HALO, D_MODEL), BF16)]
                        + [pltpu.VMEM((TM + 2 * HALO, HY_WIDTH), F32)] * (3 * n_tiles)),
        compiler_params=_cparams(("arbitrary",), 56 << 20),
    )(xlo, xhi, xlo, xhi, xlo, xhi, norm_w, mod, mod, w_in, conv_w, conv_b)


@functools.lru_cache(maxsize=None)
def _filter_consts(L):
    t = np.linspace(0.0, 1.0, L)[:, None]
    w = (2.0 * math.pi / L) * np.arange(L)[:, None]
    bands = np.linspace(1e-4, FILTER_BANDS - 1, FILTER_BANDS)[None, :]
    emb = np.concatenate([t, np.cos(bands * w), -np.sin(bands * w)], axis=-1)
    emb = np.pad(emb, ((0, 0), (0, FILTER_HIDDEN - FILTER_EMB)))
    max_decay = math.log(DECAY_TARGET) / FAST_DECAY_PCT
    min_decay = math.log(DECAY_TARGET) / SLOW_DECAY_PCT
    deltas = np.abs(np.linspace(min_decay, max_decay, HY_WIDTH))
    deltas = np.concatenate([deltas, deltas])[None, :]
    return emb.astype(np.float32), t.astype(np.float32), deltas.astype(np.float32)


def _filter_kernel(emb_ref, t_ref, w1_ref, b1_ref, fr_ref, w2_ref, b2_ref, w3_ref, dl_ref, o_ref, h_scr):
    @pl.when(pl.program_id(0) == 0)
    def _():
        hi = lax.Precision.HIGHEST
        fr = fr_ref[...]
        h = jnp.sin(fr * (jnp.dot(emb_ref[...], w1_ref[...], precision=hi, preferred_element_type=F32) + b1_ref[...]))
        h = jnp.sin(fr * (jnp.dot(h, w2_ref[...], precision=hi, preferred_element_type=F32) + b2_ref[...]))
        h_scr[...] = h.astype(BF16)

    h = jnp.dot(h_scr[...], w3_ref[...].astype(BF16), preferred_element_type=F32)
    h = h * jnp.exp(-t_ref[...] * dl_ref[...])
    o_ref[...] = h / jnp.sum(jnp.abs(h), axis=0, keepdims=True)


def _hy_filter(L, hy, w1, b1, freq, w2, b2, w3):
    emb, t, deltas = _filter_consts(L)
    tcf = 512
    full = lambda shape: pl.BlockSpec(shape, lambda c: (0, 0))
    return pl.pallas_call(
        _filter_kernel,
        name=f"hy_filter{L}",
        grid=(2 * HY_WIDTH // tcf,),
        in_specs=[
            full((L, FILTER_HIDDEN)), full((L, 1)), _entry(w1, hy), _entry(b1, hy), _entry(freq, hy),
            _entry(w2, hy), _entry(b2, hy),
            pl.BlockSpec((None, FILTER_HIDDEN, tcf), lambda c: (hy, 0, c)),
            pl.BlockSpec((1, tcf), lambda c: (0, c)),
        ],
        out_specs=pl.BlockSpec((L, tcf), lambda c: (0, c)),
        out_shape=jax.ShapeDtypeStruct((L, 2 * HY_WIDTH), F32),
        scratch_shapes=[pltpu.VMEM((L, FILTER_HIDDEN), BF16)],
        compiler_params=_cparams(("arbitrary",), 40 << 20),
    )(jnp.asarray(emb), jnp.asarray(t), w1, b1, freq, w2, b2, w3, jnp.asarray(deltas))


@functools.lru_cache(maxsize=None)
def _dft_consts(L):
    k = np.arange(L, dtype=np.int64)
    ang = (np.outer(k, k) % (2 * L)).astype(np.float64) * (math.pi / L)
    return np.cos(ang).astype(np.float32), np.sin(ang).astype(np.float32)


def _dft_mats(L):
    c, s = _dft_consts(L)
    return jnp.asarray(c).astype(BF16), jnp.asarray(s).astype(BF16)


def _alt_sign(shape):
    return (1 - 2 * (lax.broadcasted_iota(jnp.int32, shape, 0) & 1)).astype(F32)


def _lconv_kernel(z_ref, m_ref, ff_ref, fb_ref, bias_ref, c_ref, s_ref, t_ref, a_scr, b_scr, nq_scr, y_scr,
                  *, L, tk):
    rows, tc = y_scr.shape
    alt = _alt_sign((L, tc))
    bins = [slice(k0, k0 + tk) for k0 in range(0, L, tk)]

    @pl.when(pl.program_id(1) == 0)
    def _():
        inv_n = 1.0 / (2.0 * L)
        bias = bias_ref[...]
        ff = ff_ref[...]
        fb = fb_ref[...]
        f = ff + fb
        nq_scr[...] = jnp.broadcast_to((jnp.sum(f * alt, axis=0, keepdims=True) + bias) * inv_n, nq_scr.shape)
        f_b = f.astype(BF16)
        d_b = (fb - ff).astype(BF16)
        for kt in bins:
            hre = jnp.dot(c_ref[kt, :], f_b, preferred_element_type=F32) + bias
            him = jnp.dot(s_ref[kt, :], d_b, preferred_element_type=F32)
            a = hre * (2.0 * inv_n)
            if kt.start == 0:
                a = jnp.where(lax.broadcasted_iota(jnp.int32, hre.shape, 0) == 0, hre * inv_n, a)
            a_scr[kt, :] = a
            b_scr[kt, :] = him * (2.0 * inv_n)

    for s0 in range(0, rows, L):
        sq = pl.ds(s0, L)
        zb = z_ref[sq, :]
        z_nyq = jnp.sum(zb.astype(F32) * alt, axis=0, keepdims=True)
        y_scr[sq, :] = alt * (z_nyq * nq_scr[0:1, :])
        for kt in bins:
            zr = jnp.dot(c_ref[kt, :], zb, preferred_element_type=F32)
            zi = jnp.dot(s_ref[kt, :], zb, preferred_element_type=F32)
            a = a_scr[kt, :]
            b = b_scr[kt, :]
            yr = (zr * a + zi * b).astype(BF16)
            yw = (zi * a - zr * b).astype(BF16)
            y_scr[sq, :] += (jnp.dot(c_ref[:, kt], yr, preferred_element_type=F32)
                             + jnp.dot(s_ref[:, kt], yw, preferred_element_type=F32))
        t_ref[sq, :] = (y_scr[sq, :] * m_ref[sq, :].astype(F32)).astype(BF16)


def _hy_lconv(L, hy, row_block0, n_row_blocks, z, m, filt, f_bias, c, s):
    tk = min(TK, L)
    long_seq = L == SEQ_BLOCK
    tc = LCONV_TC if long_seq else MXU_TILE
    nc = HY_WIDTH // tc
    blk = pl.BlockSpec((SEQ_BLOCK, tc), lambda j, r: (r + row_block0, j))
    mat = _resident((L, L), lambda j, r: (0, 0))
    filt_mode = pl.Buffered(1) if long_seq else None
    return pl.pallas_call(
        functools.partial(_lconv_kernel, L=L, tk=tk),
        name=f"hy_lconv{L}",
        grid=(nc, n_row_blocks),
        in_specs=[blk, blk,
                  pl.BlockSpec((L, tc), lambda j, r: (0, j), pipeline_mode=filt_mode),
                  pl.BlockSpec((L, tc), lambda j, r: (0, j + nc), pipeline_mode=filt_mode),
                  pl.BlockSpec((None, 1, tc), lambda j, r: (hy, 0, j)),
                  mat, mat],
        out_specs=pl.BlockSpec((SEQ_BLOCK, tc), lambda j, r: (r, j)),
        out_shape=jax.ShapeDtypeStruct((n_row_blocks * SEQ_BLOCK, HY_WIDTH), BF16),
        scratch_shapes=[pltpu.VMEM((L, tc), F32), pltpu.VMEM((L, tc), F32), pltpu.VMEM((8, tc), F32),
                        pltpu.VMEM((SEQ_BLOCK, tc), F32)],
        compiler_params=_cparams(("arbitrary", "arbitrary"), 56 << 20),
    )(z, m, filt, filt, f_bias, c, s)


@functools.lru_cache(maxsize=None)
def _rope_consts():
    axis_dim = QK_ROPE // 2
    nf = axis_dim // 2
    inv = ROPE_THETA ** (-np.arange(0, axis_dim, 2, dtype=np.float64) / axis_dim)
    t = np.arange(DEC_SEQ)
    ang_r = (t // GRID_W)[:, None] * inv
    ang_c = (t % GRID_W)[:, None] * inv
    cos = np.ones((DEC_SEQ, HEAD_PAD))
    sin_up = np.zeros((DEC_SEQ, HEAD_PAD))
    sin_dn = np.zeros((DEC_SEQ, HEAD_PAD))
    for base, ang in ((QK_NOPE, ang_r), (QK_NOPE + axis_dim, ang_c)):
        cos[:, base:base + nf] = np.cos(ang)
        cos[:, base + nf:base + 2 * nf] = np.cos(ang)
        sin_up[:, base:base + nf] = -np.sin(ang)
        sin_dn[:, base + nf:base + 2 * nf] = np.sin(ang)
    return cos.astype(np.float32), sin_up.astype(np.float32), sin_dn.astype(np.float32)


ROPE_HALF = QK_ROPE // 4
Q_SCALE = math.log2(math.e) / math.sqrt(QK_NOPE + QK_ROPE)


def _mla_proj_kernel(tlo_ref, thi_ref, wout_ref, xlo_ref, xhi_ref, gprev_ref,
                     nw_ref, sh_ref, sc_ref, wa_ref, wg_ref, wpe_ref, qn_ref, kvn_ref, wqb_ref, wk_ref, wv_ref,
                     cos_ref, sup_ref, sdn_ref,
                     xn_ref, q_ref, k_ref, v_ref, sg_ref, ckv_ref, kpe_ref):
    i = pl.program_id(0)
    u = jnp.dot(_pick_rows(i, tlo_ref, thi_ref), wout_ref[...], preferred_element_type=F32)
    x = _pick_rows(i, xlo_ref, xhi_ref) + gprev_ref[pl.ds(_cond_row(i), 1), :] * u
    xn_ref[...] = x
    h = _modnorm(i, x, nw_ref, sh_ref, sc_ref).astype(BF16)
    lora = jnp.dot(h, wa_ref[...], preferred_element_type=F32)
    gate = jnp.dot(h, wg_ref[...], preferred_element_type=F32)
    kpe = jnp.dot(h, wpe_ref[...], preferred_element_type=F32)
    qn = _rms(lora[:, 0:Q_LORA], qn_ref[...]).astype(BF16)
    ckv = _rms(lora[:, Q_LORA:Q_LORA + KV_LORA], kvn_ref[...])
    ckv_ref[...] = ckv
    kpe_ref[...] = kpe
    sg_ref[...] = _silu(gate)
    ckv_b = ckv.astype(BF16)
    v_ref[...] = jnp.dot(ckv_b, wv_ref[...], preferred_element_type=F32).astype(BF16)
    q = jnp.dot(qn, wqb_ref[...], preferred_element_type=F32)
    kn = jnp.dot(ckv_b, wk_ref[...], preferred_element_type=F32)

    latent = i >= TILES_P
    cos = jnp.where(latent, cos_ref[...], 1.0)
    sup = jnp.where(latent, sup_ref[...], 0.0)
    sdn = jnp.where(latent, sdn_ref[...], 0.0)

    def rope(u, scale):
        return (u * (cos * scale) + pltpu.roll(u, HEAD_PAD - ROPE_HALF, axis=1) * (sup * scale)
                + pltpu.roll(u, ROPE_HALF, axis=1) * (sdn * scale))

    kpe_r = rope(kpe, 1.0)
    for hd in range(N_HEADS):
        cols = slice(hd * HEAD_PAD, (hd + 1) * HEAD_PAD)
        q_ref[:, cols] = rope(q[:, cols], Q_SCALE).astype(BF16)
        k_ref[:, cols] = (kn[:, cols] + kpe_r).astype(BF16)


def _mla_proj(t, w_out, x, norm_w, mod, layer, w):
    cos, sup, sdn = (jnp.asarray(c) for c in _rope_consts())
    j = layer // 2
    hp = N_HEADS * HEAD_PAD
    tlo, thi, _ = _row_pair(t)
    xlo, xhi, _ = _row_pair(x)
    rope_blk = pl.BlockSpec(
        (TM, HEAD_PAD), lambda i: (jnp.where(i >= TILES_P, (i - TILES_P) % TILES_PER_DEC_SEQ, 0), 0))
    tile = lambda n: pl.BlockSpec((TM, n), lambda i: (i, 0))
    return pl.pallas_call(
        _mla_proj_kernel,
        name="mla_proj",
        grid=(N_TILES,),
        in_specs=[
            *_row_pair_specs(t, HY_WIDTH),
            _entry(w_out, (layer - 1) // 2, True),
            *_row_pair_specs(x, D_MODEL),
            _mod_spec(layer - 1, 2),
            _entry(norm_w, layer),
            _mod_spec(layer, 0), _mod_spec(layer, 1),
            _entry(w["w_a"], j, True), _entry(w["w_g"], j, True), _entry(w["w_pe"], j, True),
            _entry(w["q_norm"], j), _entry(w["kv_norm"], j),
            _entry(w["w_qb"], j, True), _entry(w["w_k"], j, True), _entry(w["w_v"], j, True),
            rope_blk, rope_blk, rope_blk,
        ],
        out_specs=[tile(D_MODEL), tile(hp), tile(hp), tile(N_HEADS * V_HEAD),
                   tile(N_HEADS * V_HEAD), tile(KV_LORA), tile(HEAD_PAD)],
        out_shape=[
            jax.ShapeDtypeStruct((ROWS, D_MODEL), F32),
            jax.ShapeDtypeStruct((ROWS, hp), BF16),
            jax.ShapeDtypeStruct((ROWS, hp), BF16),
            jax.ShapeDtypeStruct((ROWS, N_HEADS * V_HEAD), BF16),
            jax.ShapeDtypeStruct((ROWS, N_HEADS * V_HEAD), F32),
            jax.ShapeDtypeStruct((ROWS, KV_LORA), F32),
            jax.ShapeDtypeStruct((ROWS, HEAD_PAD), F32),
        ],
        compiler_params=_cparams(("arbitrary",), 56 << 20),
    )(tlo, thi, w_out, xlo, xhi, mod, norm_w, mod, mod, w["w_a"], w["w_g"], w["w_pe"], w["q_norm"],
      w["kv_norm"], w["w_qb"], w["w_k"], w["w_v"], cos, sup, sdn)


def _mla_ctx_kernel(ckv_ref, kpe_ref, wk_ref, wv_ref, k_ref, v_ref):
    ckv_b = ckv_ref[...].astype(BF16)
    kn = jnp.dot(ckv_b, wk_ref[...], preferred_element_type=F32)
    v_ref[...] = jnp.dot(ckv_b, wv_ref[...], preferred_element_type=F32).astype(BF16)
    kpe = kpe_ref[...]
    for hd in range(N_HEADS):
        cols = slice(hd * HEAD_PAD, (hd + 1) * HEAD_PAD)
        k_ref[:, cols] = (kn[:, cols] + kpe).astype(BF16)


def _mla_ctx(j, ckv_ctx, kpe_ctx, w):
    hp = N_HEADS * HEAD_PAD
    rows = DEC_BATCH * PAST_LEN
    tile = lambda n: pl.BlockSpec((PAST_LEN, n), lambda i: (i, 0))
    cache = lambda n: pl.BlockSpec((None, None, PAST_LEN, n), lambda i: (i, j, 0, 0))
    return pl.pallas_call(
        _mla_ctx_kernel,
        name="mla_ctx",
        grid=(DEC_BATCH,),
        in_specs=[cache(KV_LORA), cache(HEAD_PAD), _entry(w["w_k"], j, True), _entry(w["w_v"], j, True)],
        out_specs=[tile(hp), tile(N_HEADS * V_HEAD)],
        out_shape=[jax.ShapeDtypeStruct((rows, hp), BF16),
                   jax.ShapeDtypeStruct((rows, N_HEADS * V_HEAD), BF16)],
        compiler_params=_cparams(("arbitrary",), 32 << 20),
    )(ckv_ctx, kpe_ctx, w["w_k"], w["w_v"])


NT_DIMS = (((1,), (1,)), ((), ()))
ATTN_PAIRS = 4
ATTN_TQ = 256


def _attn_kernel(*refs, n_pairs, n_groups, has_ctx, final):
    refs = list(refs)
    q_ref, k_ref, v_ref = refs[:3]
    del refs[:3]
    if has_ctx:
        kc_ref, vc_ref = refs[:2]
        del refs[:2]
    sg_ref, wo_ref, x_ref, g_ref = refs[:4]
    del refs[:4]
    if final:
        fg_ref = refs.pop(0)
    out_ref, o_scr = refs[:2]
    tq = q_ref.shape[0]
    low_half = lax.broadcasted_iota(jnp.int32, (tq, LANES), 1) < V_HEAD
    for p in range(n_pairs):
        vcols = slice(p * LANES, (p + 1) * LANES)
        vp = v_ref[:, vcols]
        outs = []
        for hh in range(2):
            cols = slice((2 * p + hh) * HEAD_PAD, (2 * p + hh + 1) * HEAD_PAD)
            q = q_ref[:, cols]
            s = lax.dot_general(q, k_ref[:, cols], NT_DIMS, preferred_element_type=F32)
            mx = jnp.max(s, axis=-1, keepdims=True)
            if has_ctx:
                sc = lax.dot_general(q, kc_ref[:, cols], NT_DIMS, preferred_element_type=F32)
                mx = jnp.maximum(mx, jnp.max(sc, axis=-1, keepdims=True))
            e = jnp.exp2(s - mx)
            den = jnp.sum(e, axis=-1, keepdims=True)
            pv = jnp.dot(e.astype(BF16), vp, preferred_element_type=F32)
            if has_ctx:
                ec = jnp.exp2(sc - mx)
                den = den + jnp.sum(ec, axis=-1, keepdims=True)
                pv = pv + jnp.dot(ec.astype(BF16), vc_ref[:, vcols], preferred_element_type=F32)
            outs.append(pv / den)
        o = jnp.where(low_half, outs[0], outs[1])
        o_scr[:, vcols] = (o * sg_ref[:, vcols]).astype(BF16)

    u = jnp.dot(o_scr[...], wo_ref[...], preferred_element_type=F32)
    cond = 1 + pl.program_id(0) if has_ctx else 0

    def finish(u_all):
        xn = x_ref[...] + g_ref[pl.ds(cond, 1), :] * u_all
        out_ref[...] = _rms(xn, fg_ref[...]) if final else xn

    if n_groups == 1:
        finish(u)
    else:
        acc_scr = refs[2]
        grp = pl.program_id(2)

        @pl.when(grp == 0)
        def _():
            acc_scr[...] = u

        @pl.when((grp > 0) & (grp < n_groups - 1))
        def _():
            acc_scr[...] += u

        @pl.when(grp == n_groups - 1)
        def _():
            finish(acc_scr[...] + u)


def _attn_prompt(q, k, v, sg, w_o, x, mod, layer, final_g):
    hp = N_HEADS * HEAD_PAD
    nv = N_HEADS * V_HEAD
    xlo, _, _ = _row_pair(x)
    seq = lambda n: pl.BlockSpec((SEQ, n), lambda b: (b, 0))
    final = final_g is not None
    return pl.pallas_call(
        functools.partial(_attn_kernel, n_pairs=N_HEADS // 2, n_groups=1, has_ctx=False, final=final),
        name="attn_prompt",
        grid=(BATCH,),
        in_specs=[seq(hp), seq(hp), seq(nv), seq(nv), _entry(w_o, layer // 2, True), seq(D_MODEL),
                  _mod_spec(layer, 2)] + ([pl.BlockSpec((1, D_MODEL), lambda b: (0, 0))] if final else []),
        out_specs=seq(D_MODEL),
        out_shape=jax.ShapeDtypeStruct((ROWS_P, D_MODEL), F32),
        scratch_shapes=[pltpu.VMEM((SEQ, nv), BF16)],
        compiler_params=_cparams(("arbitrary",), 40 << 20),
    )(q, k, v, sg, w_o, xlo, mod, *([final_g] if final else []))


def _attn_latent(q, k, v, kc, vc, sg, w_o, x, mod, layer, final_g):
    tq = ATTN_TQ
    npair = ATTN_PAIRS
    ngrp = N_HEADS // 2 // npair
    tiles = DEC_SEQ // tq
    q0 = ROWS_P // tq
    s0 = ROWS_P // DEC_SEQ
    wide = npair * 2 * HEAD_PAD
    narrow = npair * LANES
    _, xhi, base = _row_pair(x)
    x0 = base // tq
    final = final_g is not None
    qrow = lambda b, t, p: (q0 + b * tiles + t, p)
    kv_mode = pl.Buffered(1) if ngrp == 1 else None
    return pl.pallas_call(
        functools.partial(_attn_kernel, n_pairs=npair, n_groups=ngrp, has_ctx=True, final=final),
        name="attn_latent",
        grid=(DEC_BATCH, tiles, ngrp),
        in_specs=[pl.BlockSpec((tq, wide), qrow),
                  pl.BlockSpec((DEC_SEQ, wide), lambda b, t, p: (s0 + b, p), pipeline_mode=kv_mode),
                  pl.BlockSpec((DEC_SEQ, narrow), lambda b, t, p: (s0 + b, p), pipeline_mode=kv_mode),
                  pl.BlockSpec((PAST_LEN, wide), lambda b, t, p: (b, p), pipeline_mode=kv_mode),
                  pl.BlockSpec((PAST_LEN, narrow), lambda b, t, p: (b, p), pipeline_mode=kv_mode),
                  pl.BlockSpec((tq, narrow), qrow),
                  pl.BlockSpec((None, narrow, D_MODEL), lambda b, t, p: (layer // 2, p, 0)),
                  pl.BlockSpec((tq, D_MODEL), lambda b, t, p: (x0 + b * tiles + t, 0)),
                  _mod_spec(layer, 2)] + ([pl.BlockSpec((1, D_MODEL), lambda b, t, p: (0, 0))] if final else []),
        out_specs=pl.BlockSpec((tq, D_MODEL), lambda b, t, p: (b * tiles + t, 0)),
        out_shape=jax.ShapeDtypeStruct((ROWS_S, D_MODEL), F32),
        scratch_shapes=[pltpu.VMEM((tq, narrow), BF16), pltpu.VMEM((tq, D_MODEL), F32)],
        compiler_params=_cparams(("arbitrary", "arbitrary", "arbitrary"), 48 << 20),
    )(q, k, v, kc, vc, sg, w_o, xhi, mod, *([final_g] if final else []))


def _mla_weights(w_in, q_norm, w_qb, kv_norm, w_kvb, w_o):
    n = w_in.shape[0]
    o_pe = Q_LORA + KV_LORA
    o_gate = o_pe + QK_ROPE
    w_pe = jnp.pad(w_in[..., o_pe:o_gate], ((0, 0), (0, 0), (QK_NOPE, HEAD_PAD - QK_NOPE - QK_ROPE)))
    qb = w_qb.reshape(n, Q_LORA, N_HEADS, QK_NOPE + QK_ROPE)
    qb = jnp.pad(qb, ((0, 0), (0, 0), (0, 0), (0, HEAD_PAD - QK_NOPE - QK_ROPE)))
    kvb = w_kvb.reshape(n, KV_LORA, N_HEADS, QK_NOPE + V_HEAD)
    wk = jnp.pad(kvb[..., :QK_NOPE], ((0, 0), (0, 0), (0, 0), (0, HEAD_PAD - QK_NOPE)))
    wv = kvb[..., QK_NOPE:]
    return {
        "w_a": w_in[..., :o_pe].astype(BF16),
        "w_g": w_in[..., o_gate:].astype(BF16),
        "w_pe": w_pe.astype(BF16),
        "q_norm": q_norm[:, None, :],
        "kv_norm": kv_norm[:, None, :],
        "w_qb": qb.reshape(n, Q_LORA, N_HEADS * HEAD_PAD).astype(BF16),
        "w_k": wk.reshape(n, KV_LORA, N_HEADS * HEAD_PAD).astype(BF16),
        "w_v": wv.reshape(n, KV_LORA, N_HEADS * V_HEAD).astype(BF16),
        "w_o": w_o.astype(BF16),
    }


def kernel(x_prompt, x_sample, cache_ckv, cache_kpe, c, c_ctx, norm_w, ada_w, ada_b, hy_w_in, hy_conv_w, hy_conv_b, hy_f_w1, hy_f_b1, hy_f_freq, hy_f_w2, hy_f_b2, hy_f_w3, hy_f_bias, hy_w_out, mla_w_in, mla_q_norm, mla_w_qb, mla_kv_norm, mla_w_kvb, mla_w_o, final_norm):
    x = (x_prompt.reshape(ROWS_P, D_MODEL), x_sample.reshape(ROWS_S, D_MODEL))
    cond = jnp.concatenate([c_ctx[None, :], c, jnp.zeros((N_COND - 1 - DEC_BATCH, D_MODEL), F32)], axis=0)
    mod = _ada_all(cond, ada_w, ada_b)

    nw = norm_w[:, None, :]
    hy_w_in_b = hy_w_in.astype(BF16)
    hy_w_out_b = hy_w_out.astype(BF16)
    hy_conv_b3 = hy_conv_b[:, None, :]
    f_w1 = jnp.pad(hy_f_w1, ((0, 0), (0, FILTER_HIDDEN - FILTER_EMB), (0, 0)))
    f_b1, f_freq, f_b2, f_bias = (a[:, None, :] for a in (hy_f_b1, hy_f_freq, hy_f_b2, hy_f_bias))
    w = _mla_weights(mla_w_in, mla_q_norm, mla_w_qb, mla_kv_norm, mla_w_kvb, mla_w_o)
    kpe_ctx = jnp.pad(cache_kpe, ((0, 0), (0, 0), (0, 0), (QK_NOPE, HEAD_PAD - QK_NOPE - QK_ROPE)))

    new_ckv, new_kpe = [], []
    for layer in range(DEPTH):
        j = layer // 2
        if layer % 2 == 0:
            z, m = _hy_in(x, nw, mod, layer, hy_w_in_b, hy_conv_w, hy_conv_b3)
            t = []
            for L, blk0, nblk in ((SEQ, 0, ROWS_P // SEQ_BLOCK), (DEC_SEQ, ROWS_P // SEQ_BLOCK, ROWS_S // SEQ_BLOCK)):
                filt = _hy_filter(L, j, f_w1, f_b1, f_freq, hy_f_w2, f_b2, hy_f_w3)
                cm, sm = _dft_mats(L)
                t.append(_hy_lconv(L, j, blk0, nblk, z, m, filt, f_bias, cm, sm))
            t = tuple(t)
        else:
            x, q, k, v, sg, ckv, kpe = _mla_proj(t, hy_w_out_b, x, nw, mod, layer, w)
            kc, vc = _mla_ctx(j, cache_ckv, kpe_ctx, w)
            final_g = final_norm[None, :] if layer == DEPTH - 1 else None
            x = (_attn_prompt(q, k, v, sg, w["w_o"], x, mod, layer, final_g),
                 _attn_latent(q, k, v, kc, vc, sg, w["w_o"], x, mod, layer, final_g))
            new_ckv.append(ckv[:ROWS_P].reshape(BATCH, SEQ, KV_LORA))
            new_kpe.append(kpe[:ROWS_P, QK_NOPE:QK_NOPE + QK_ROPE].reshape(BATCH, SEQ, QK_ROPE))

    assert DEPTH % 2 == 0
    y_prompt, y_sample = x
    return (y_prompt.reshape(BATCH, SEQ, D_MODEL), y_sample.reshape(DEC_BATCH, DEC_SEQ, D_MODEL),
            jnp.stack(new_ckv, axis=1), jnp.stack(new_kpe, axis=1))
```

```python
import functools
import math

import numpy as np
import jax
import jax.numpy as jnp
from jax import lax
from jax.experimental import pallas as pl
from jax.experimental.pallas import tpu as pltpu

F32 = jnp.float32
BF16 = jnp.bfloat16

D_MODEL = 1024
BATCH = 16
SEQ = 256
DEPTH = 4
DEC_BATCH = 2
DEC_SEQ = 2048
PAST_LEN = 512
GRID_W = 64
EPS = 1e-6
HY_WIDTH = D_MODEL
FILTER_BANDS = 16
FILTER_EMB = 1 + 2 * FILTER_BANDS
FILTER_HIDDEN = 64
FAST_DECAY_PCT = 0.3
SLOW_DECAY_PCT = 1.5
DECAY_TARGET = 1e-2
N_HEADS = 16
Q_LORA = 384
KV_LORA = 256
QK_NOPE = 64
QK_ROPE = 32
V_HEAD = 64
ROPE_THETA = 10000.0

LANES = 128
MXU_TILE = 256
HEAD_PAD = LANES
ROWS_P = BATCH * SEQ
ROWS_S = DEC_BATCH * DEC_SEQ
ROWS = ROWS_P + ROWS_S
TM = 512
N_TILES = ROWS // TM
TILES_P = ROWS_P // TM
TILES_PER_DEC_SEQ = DEC_SEQ // TM
N_COND = 8
SEQ_BLOCK = 2048
TK = 512
LCONV_TC = 256
VMEM_CAP = 56 * 1024 * 1024


def _cparams(sem, vmem_bytes):
    return pltpu.CompilerParams(dimension_semantics=sem, vmem_limit_bytes=min(int(vmem_bytes), VMEM_CAP))


def _resident(shape, index_map):
    return pl.BlockSpec(shape, index_map, pipeline_mode=pl.Buffered(1))


def _entry(arr, idx, resident=False):
    zeros = (0,) * (arr.ndim - 1)
    return pl.BlockSpec((None,) + arr.shape[1:], lambda *_: (idx,) + zeros,
                        pipeline_mode=pl.Buffered(1) if resident else None)


def _cond_row(i, rows=TM):
    return jnp.where(i < ROWS_P // rows, 0, 1 + (i - ROWS_P // rows) // (DEC_SEQ // rows))


def _silu(x):
    return x * jax.nn.sigmoid(x)


def _rms(x, g):
    return x * lax.rsqrt(jnp.mean(x * x, axis=-1, keepdims=True) + EPS) * g


def _row_pair(x):
    if isinstance(x, tuple):
        return x[0], x[1], 0
    return x, x, ROWS_P


def _row_pair_specs(x, n, rows=TM):
    _, _, base = _row_pair(x)
    n_lo = ROWS_P // rows
    lo = pl.BlockSpec((rows, n), lambda i, *_: (jnp.minimum(i, n_lo - 1), 0))
    hi = pl.BlockSpec((rows, n), lambda i, *_: (jnp.maximum(i - n_lo, 0) + base // rows, 0))
    return lo, hi


def _pick_rows(i, lo_ref, hi_ref, rows=TM):
    return jnp.where(i < ROWS_P // rows, lo_ref[...], hi_ref[...])


def _ada_kernel(cond_ref, w_ref, b_ref, o_ref):
    s = _silu(cond_ref[...]).astype(BF16)
    o_ref[...] = jnp.dot(s, w_ref[...].astype(BF16), preferred_element_type=F32) + b_ref[...]


def _ada_all(cond, ada_w, ada_b):
    tn = 1024
    return pl.pallas_call(
        _ada_kernel,
        name="ada",
        grid=(DEPTH, 3 * D_MODEL // tn),
        in_specs=[
            pl.BlockSpec((N_COND, D_MODEL), lambda l, j: (0, 0)),
            pl.BlockSpec((None, D_MODEL, tn), lambda l, j: (l, 0, j)),
            pl.BlockSpec((None, 1, tn), lambda l, j: (l, 0, j)),
        ],
        out_specs=pl.BlockSpec((None, N_COND, tn), lambda l, j: (l, 0, j)),
        out_shape=jax.ShapeDtypeStruct((DEPTH, N_COND, 3 * D_MODEL), F32),
        compiler_params=_cparams(("arbitrary", "arbitrary"), 32 << 20),
    )(cond, ada_w, ada_b.reshape(DEPTH, 1, 3 * D_MODEL))


def _mod_spec(layer, part):
    return pl.BlockSpec((None, N_COND, D_MODEL), lambda i, *_: (layer, 0, part))


def _modnorm(i, x, nw_ref, sh_ref, sc_ref):
    c = _cond_row(i)
    return _rms(x, nw_ref[...]) * (1.0 + sc_ref[pl.ds(c, 1), :]) + sh_ref[pl.ds(c, 1), :]


HALO = 16
HY_ROWS = TM


def _halo_specs(x):
    _, _, base = _row_pair(x)
    r = HY_ROWS // HALO
    n_lo = ROWS_P // HY_ROWS
    lo_last = ROWS_P // HALO - 1
    hi_last = ROWS_S // HALO - 1
    lo_blk = lambda i: jnp.minimum(i, n_lo - 1)
    hi_blk = lambda i: jnp.maximum(i - n_lo, 0)
    blk = lambda f: pl.BlockSpec((HALO, D_MODEL), lambda i: (f(i), 0))
    return [
        blk(lambda i: jnp.maximum(lo_blk(i) * r - 1, 0)),
        blk(lambda i: jnp.maximum(hi_blk(i) * r - 1, 0) + base // HALO),
        blk(lambda i: jnp.minimum((lo_blk(i) + 1) * r, lo_last)),
        blk(lambda i: jnp.minimum((hi_blk(i) + 1) * r, hi_last) + base // HALO),
    ]


def _hy_in_kernel(xlo_ref, xhi_ref, plo_ref, phi_ref, nlo_ref, nhi_ref, nw_ref, sh_ref, sc_ref, w_ref,
                  cw_ref, cb_ref, z_ref, m_ref, h_scr, *u_scrs):
    i = pl.program_id(0)
    width = z_ref.shape[1]
    c = _cond_row(i, HY_ROWS)
    norm = lambda lo_ref, hi_ref: (_rms(_pick_rows(i, lo_ref, hi_ref, HY_ROWS), nw_ref[...])
                                   * (1.0 + sc_ref[pl.ds(c, 1), :]) + sh_ref[pl.ds(c, 1), :]).astype(BF16)
    h_scr[0:HALO, :] = norm(plo_ref, phi_ref)
    h_scr[HALO:HALO + HY_ROWS, :] = norm(xlo_ref, xhi_ref)
    h_scr[HALO + HY_ROWS:HALO + HY_ROWS + HALO, :] = norm(nlo_ref, nhi_ref)

    seq_mask = jnp.where(i < ROWS_P // HY_ROWS, SEQ - 1, DEC_SEQ - 1)
    row = lax.broadcasted_iota(jnp.int32, (TM, width), 0)
    for t in range(HY_ROWS // TM):
        r0 = t * TM
        pos = (row + (i * HY_ROWS + r0)) & seq_mask
        first = pos == 0
        last = pos == seq_mask
        rows_h = pl.ds(r0, TM + 2 * HALO)

        def conv(g, u_scr):
            cols = slice(g * width, (g + 1) * width)
            u_scr[...] = jnp.dot(h_scr[rows_h, :], w_ref[:, cols], preferred_element_type=F32)
            prev = jnp.where(first, 0.0, u_scr[HALO - 1:HALO - 1 + TM, :])
            nxt = jnp.where(last, 0.0, u_scr[HALO + 1:HALO + 1 + TM, :])
            return (cb_ref[:, cols] + prev * cw_ref[0:1, cols] + u_scr[HALO:HALO + TM, :] * cw_ref[1:2, cols]
                    + nxt * cw_ref[2:3, cols])

        u0, u1, u2 = u_scrs[3 * t:3 * t + 3]
        out_rows = pl.ds(r0, TM)
        z_ref[out_rows, :] = (conv(2, u2) * conv(1, u1)).astype(BF16)
        gate = jnp.dot(h_scr[pl.ds(r0 + HALO, TM), :], w_ref[:, 3 * width:4 * width], preferred_element_type=F32)
        m_ref[out_rows, :] = (conv(0, u0) * _silu(gate)).astype(BF16)


def _hy_in(x, norm_w, mod, layer, w_in, conv_w, conv_b):
    j = layer // 2
    xlo, xhi, _ = _row_pair(x)
    out = pl.BlockSpec((HY_ROWS, HY_WIDTH), lambda i: (i, 0))
    n_tiles = HY_ROWS // TM
    return pl.pallas_call(
        _hy_in_kernel,
        name="hy_in",
        grid=(ROWS // HY_ROWS,),
        in_specs=[
            *_row_pair_specs(x, D_MODEL, HY_ROWS),
            *_halo_specs(x),
            _entry(norm_w, layer),
            _mod_spec(layer, 0),
            _mod_spec(layer, 1),
            _entry(w_in, j, resident=True),
            _entry(conv_w, j),
            _entry(conv_b, j),
        ],
        out_specs=[out, out],
        out_shape=[jax.ShapeDtypeStruct((ROWS, HY_WIDTH), BF16)] * 2,
        scratch_shapes=([pltpu.VMEM((HY_ROWS + 2 * HALO, D_MODEL), BF16)]
                        + [pltpu.VMEM((TM + 2 * HALO, HY_WIDTH), F32)] * (3 * n_tiles)),
        compiler_params=_cparams(("arbitrary",), 56 << 20),
    )(xlo, xhi, xlo, xhi, xlo, xhi, norm_w, mod, mod, w_in, conv_w, conv_b)


@functools.lru_cache(maxsize=None)
def _filter_consts(L):
    t = np.linspace(0.0, 1.0, L)[:, None]
    w = (2.0 * math.pi / L) * np.arange(L)[:, None]
    bands = np.linspace(1e-4, FILTER_BANDS - 1, FILTER_BANDS)[None, :]
    emb = np.concatenate([t, np.cos(bands * w), -np.sin(bands * w)], axis=-1)
    emb = np.pad(emb, ((0, 0), (0, FILTER_HIDDEN - FILTER_EMB)))
    max_decay = math.log(DECAY_TARGET) / FAST_DECAY_PCT
    min_decay = math.log(DECAY_TARGET) / SLOW_DECAY_PCT
    deltas = np.abs(np.linspace(min_decay, max_decay, HY_WIDTH))
    deltas = np.concatenate([deltas, deltas])[None, :]
    return emb.astype(np.float32), t.astype(np.float32), deltas.astype(np.float32)


def _filter_kernel(emb_ref, t_ref, w1_ref, b1_ref, fr_ref, w2_ref, b2_ref, w3_ref, dl_ref, o_ref, h_scr):
    @pl.when(pl.program_id(0) == 0)
    def _():
        hi = lax.Precision.HIGHEST
        fr = fr_ref[...]
        h = jnp.sin(fr * (jnp.dot(emb_ref[...], w1_ref[...], precision=hi, preferred_element_type=F32) + b1_ref[...]))
        h = jnp.sin(fr * (jnp.dot(h, w2_ref[...], precision=hi, preferred_element_type=F32) + b2_ref[...]))
        h_scr[...] = h.astype(BF16)

    h = jnp.dot(h_scr[...], w3_ref[...].astype(BF16), preferred_element_type=F32)
    h = h * jnp.exp(-t_ref[...] * dl_ref[...])
    o_ref[...] = h / jnp.sum(jnp.abs(h), axis=0, keepdims=True)


def _hy_filter(L, hy, w1, b1, freq, w2, b2, w3):
    emb, t, deltas = _filter_consts(L)
    tcf = 512
    full = lambda shape: pl.BlockSpec(shape, lambda c: (0, 0))
    return pl.pallas_call(
        _filter_kernel,
        name=f"hy_filter{L}",
        grid=(2 * HY_WIDTH // tcf,),
        in_specs=[
            full((L, FILTER_HIDDEN)), full((L, 1)), _entry(w1, hy), _entry(b1, hy), _entry(freq, hy),
            _entry(w2, hy), _entry(b2, hy),
            pl.BlockSpec((None, FILTER_HIDDEN, tcf), lambda c: (hy, 0, c)),
            pl.BlockSpec((1, tcf), lambda c: (0, c)),
        ],
        out_specs=pl.BlockSpec((L, tcf), lambda c: (0, c)),
        out_shape=jax.ShapeDtypeStruct((L, 2 * HY_WIDTH), F32),
        scratch_shapes=[pltpu.VMEM((L, FILTER_HIDDEN), BF16)],
        compiler_params=_cparams(("arbitrary",), 40 << 20),
    )(jnp.asarray(emb), jnp.asarray(t), w1, b1, freq, w2, b2, w3, jnp.asarray(deltas))


@functools.lru_cache(maxsize=None)
def _dft_consts(L):
    k = np.arange(L, dtype=np.int64)
    ang = (np.outer(k, k) % (2 * L)).astype(np.float64) * (math.pi / L)
    return np.cos(ang).astype(np.float32), np.sin(ang).astype(np.float32)


def _dft_mats(L):
    c, s = _dft_consts(L)
    return jnp.asarray(c).astype(BF16), jnp.asarray(s).astype(BF16)


def _alt_sign(shape):
    return (1 - 2 * (lax.broadcasted_iota(jnp.int32, shape, 0) & 1)).astype(F32)


def _lconv_kernel(z_ref, m_ref, ff_ref, fb_ref, bias_ref, c_ref, s_ref, t_ref, a_scr, b_scr, nq_scr, y_scr,
                  *, L, tk):
    rows, tc = y_scr.shape
    alt = _alt_sign((L, tc))
    bins = [slice(k0, k0 + tk) for k0 in range(0, L, tk)]

    @pl.when(pl.program_id(1) == 0)
    def _():
        inv_n = 1.0 / (2.0 * L)
        bias = bias_ref[...]
        ff = ff_ref[...]
        fb = fb_ref[...]
        f = ff + fb
        nq_scr[...] = jnp.broadcast_to((jnp.sum(f * alt, axis=0, keepdims=True) + bias) * inv_n, nq_scr.shape)
        f_b = f.astype(BF16)
        d_b = (fb - ff).astype(BF16)
        for kt in bins:
            hre = jnp.dot(c_ref[kt, :], f_b, preferred_element_type=F32) + bias
            him = jnp.dot(s_ref[kt, :], d_b, preferred_element_type=F32)
            a = hre * (2.0 * inv_n)
            if kt.start == 0:
                a = jnp.where(lax.broadcasted_iota(jnp.int32, hre.shape, 0) == 0, hre * inv_n, a)
            a_scr[kt, :] = a
            b_scr[kt, :] = him * (2.0 * inv_n)

    for s0 in range(0, rows, L):
        sq = pl.ds(s0, L)
        zb = z_ref[sq, :]
        z_nyq = jnp.sum(zb.astype(F32) * alt, axis=0, keepdims=True)
        y_scr[sq, :] = alt * (z_nyq * nq_scr[0:1, :])
        for kt in bins:
            zr = jnp.dot(c_ref[kt, :], zb, preferred_element_type=F32)
            zi = jnp.dot(s_ref[kt, :], zb, preferred_element_type=F32)
            a = a_scr[kt, :]
            b = b_scr[kt, :]
            yr = (zr * a + zi * b).astype(BF16)
            yw = (zi * a - zr * b).astype(BF16)
            y_scr[sq, :] += (jnp.dot(c_ref[:, kt], yr, preferred_element_type=F32)
                             + jnp.dot(s_ref[:, kt], yw, preferred_element_type=F32))
        t_ref[sq, :] = (y_scr[sq, :] * m_ref[sq, :].astype(F32)).astype(BF16)


def _hy_lconv(L, hy, row_block0, n_row_blocks, z, m, filt, f_bias, c, s):
    tk = min(TK, L)
    tc = LCONV_TC
    nc = HY_WIDTH // tc
    blk = pl.BlockSpec((SEQ_BLOCK, tc), lambda j, r: (r + row_block0, j))
    mat = _resident((L, L), lambda j, r: (0, 0))
    return pl.pallas_call(
        functools.partial(_lconv_kernel, L=L, tk=tk),
        name=f"hy_lconv{L}",
        grid=(nc, n_row_blocks),
        in_specs=[blk, blk,
                  pl.BlockSpec((L, tc), lambda j, r: (0, j)),
                  pl.BlockSpec((L, tc), lambda j, r: (0, j + nc)),
                  pl.BlockSpec((None, 1, tc), lambda j, r: (hy, 0, j)),
                  mat, mat],
        out_specs=pl.BlockSpec((SEQ_BLOCK, tc), lambda j, r: (r, j)),
        out_shape=jax.ShapeDtypeStruct((n_row_blocks * SEQ_BLOCK, HY_WIDTH), BF16),
        scratch_shapes=[pltpu.VMEM((L, tc), F32), pltpu.VMEM((L, tc), F32), pltpu.VMEM((8, tc), F32),
                        pltpu.VMEM((SEQ_BLOCK, tc), F32)],
        compiler_params=_cparams(("arbitrary", "arbitrary"), 56 << 20),
    )(z, m, filt, filt, f_bias, c, s)


@functools.lru_cache(maxsize=None)
def _rope_consts():
    axis_dim = QK_ROPE // 2
    nf = axis_dim // 2
    inv = ROPE_THETA ** (-np.arange(0, axis_dim, 2, dtype=np.float64) / axis_dim)
    t = np.arange(DEC_SEQ)
    ang_r = (t // GRID_W)[:, None] * inv
    ang_c = (t % GRID_W)[:, None] * inv
    cos = np.ones((DEC_SEQ, HEAD_PAD))
    sin_up = np.zeros((DEC_SEQ, HEAD_PAD))
    sin_dn = np.zeros((DEC_SEQ, HEAD_PAD))
    for base, ang in ((QK_NOPE, ang_r), (QK_NOPE + axis_dim, ang_c)):
        cos[:, base:base + nf] = np.cos(ang)
        cos[:, base + nf:base + 2 * nf] = np.cos(ang)
        sin_up[:, base:base + nf] = -np.sin(ang)
        sin_dn[:, base + nf:base + 2 * nf] = np.sin(ang)
    return cos.astype(np.float32), sin_up.astype(np.float32), sin_dn.astype(np.float32)


ROPE_HALF = QK_ROPE // 4
Q_SCALE = math.log2(math.e) / math.sqrt(QK_NOPE + QK_ROPE)


def _mla_proj_kernel(tlo_ref, thi_ref, wout_ref, xlo_ref, xhi_ref, gprev_ref,
                     nw_ref, sh_ref, sc_ref, wa_ref, wg_ref, wpe_ref, qn_ref, kvn_ref, wqb_ref, wk_ref, wv_ref,
                     cos_ref, sup_ref, sdn_ref,
                     xn_ref, q_ref, k_ref, v_ref, sg_ref, ckv_ref, kpe_ref):
    i = pl.program_id(0)
    u = jnp.dot(_pick_rows(i, tlo_ref, thi_ref), wout_ref[...], preferred_element_type=F32)
    x = _pick_rows(i, xlo_ref, xhi_ref) + gprev_ref[pl.ds(_cond_row(i), 1), :] * u
    xn_ref[...] = x
    h = _modnorm(i, x, nw_ref, sh_ref, sc_ref).astype(BF16)
    lora = jnp.dot(h, wa_ref[...], preferred_element_type=F32)
    gate = jnp.dot(h, wg_ref[...], preferred_element_type=F32)
    kpe = jnp.dot(h, wpe_ref[...], preferred_element_type=F32)
    qn = _rms(lora[:, 0:Q_LORA], qn_ref[...]).astype(BF16)
    ckv = _rms(lora[:, Q_LORA:Q_LORA + KV_LORA], kvn_ref[...])
    ckv_ref[...] = ckv
    kpe_ref[...] = kpe
    sg_ref[...] = _silu(gate)
    ckv_b = ckv.astype(BF16)
    v_ref[...] = jnp.dot(ckv_b, wv_ref[...], preferred_element_type=F32).astype(BF16)
    q = jnp.dot(qn, wqb_ref[...], preferred_element_type=F32)
    kn = jnp.dot(ckv_b, wk_ref[...], preferred_element_type=F32)

    latent = i >= TILES_P
    cos = jnp.where(latent, cos_ref[...], 1.0)
    sup = jnp.where(latent, sup_ref[...], 0.0)
    sdn = jnp.where(latent, sdn_ref[...], 0.0)

    def rope(u, scale):
        return (u * (cos * scale) + pltpu.roll(u, HEAD_PAD - ROPE_HALF, axis=1) * (sup * scale)
                + pltpu.roll(u, ROPE_HALF, axis=1) * (sdn * scale))

    kpe_r = rope(kpe, 1.0)
    for hd in range(N_HEADS):
        cols = slice(hd * HEAD_PAD, (hd + 1) * HEAD_PAD)
        q_ref[:, cols] = rope(q[:, cols], Q_SCALE).astype(BF16)
        k_ref[:, cols] = (kn[:, cols] + kpe_r).astype(BF16)


def _mla_proj(t, w_out, x, norm_w, mod, layer, w):
    cos, sup, sdn = (jnp.asarray(c) for c in _rope_consts())
    j = layer // 2
    hp = N_HEADS * HEAD_PAD
    tlo, thi, _ = _row_pair(t)
    xlo, xhi, _ = _row_pair(x)
    rope_blk = pl.BlockSpec(
        (TM, HEAD_PAD), lambda i: (jnp.where(i >= TILES_P, (i - TILES_P) % TILES_PER_DEC_SEQ, 0), 0))
    tile = lambda n: pl.BlockSpec((TM, n), lambda i: (i, 0))
    return pl.pallas_call(
        _mla_proj_kernel,
        name="mla_proj",
        grid=(N_TILES,),
        in_specs=[
            *_row_pair_specs(t, HY_WIDTH),
            _entry(w_out, (layer - 1) // 2, True),
            *_row_pair_specs(x, D_MODEL),
            _mod_spec(layer - 1, 2),
            _entry(norm_w, layer),
            _mod_spec(layer, 0), _mod_spec(layer, 1),
            _entry(w["w_a"], j, True), _entry(w["w_g"], j, True), _entry(w["w_pe"], j, True),
            _entry(w["q_norm"], j), _entry(w["kv_norm"], j),
            _entry(w["w_qb"], j, True), _entry(w["w_k"], j, True), _entry(w["w_v"], j, True),
            rope_blk, rope_blk, rope_blk,
        ],
        out_specs=[tile(D_MODEL), tile(hp), tile(hp), tile(N_HEADS * V_HEAD),
                   tile(N_HEADS * V_HEAD), tile(KV_LORA), tile(HEAD_PAD)],
        out_shape=[
            jax.ShapeDtypeStruct((ROWS, D_MODEL), F32),
            jax.ShapeDtypeStruct((ROWS, hp), BF16),
            jax.ShapeDtypeStruct((ROWS, hp), BF16),
            jax.ShapeDtypeStruct((ROWS, N_HEADS * V_HEAD), BF16),
            jax.ShapeDtypeStruct((ROWS, N_HEADS * V_HEAD), F32),
            jax.ShapeDtypeStruct((ROWS, KV_LORA), F32),
            jax.ShapeDtypeStruct((ROWS, HEAD_PAD), F32),
        ],
        compiler_params=_cparams(("arbitrary",), 56 << 20),
    )(tlo, thi, w_out, xlo, xhi, mod, norm_w, mod, mod, w["w_a"], w["w_g"], w["w_pe"], w["q_norm"],
      w["kv_norm"], w["w_qb"], w["w_k"], w["w_v"], cos, sup, sdn)


def _mla_ctx_kernel(ckv_ref, kpe_ref, wk_ref, wv_ref, k_ref, v_ref):
    ckv_b = ckv_ref[...].astype(BF16)
    kn = jnp.dot(ckv_b, wk_ref[...], preferred_element_type=F32)
    v_ref[...] = jnp.dot(ckv_b, wv_ref[...], preferred_element_type=F32).astype(BF16)
    kpe = kpe_ref[...]
    for hd in range(N_HEADS):
        cols = slice(hd * HEAD_PAD, (hd + 1) * HEAD_PAD)
        k_ref[:, cols] = (kn[:, cols] + kpe).astype(BF16)


def _mla_ctx(j, ckv_ctx, kpe_ctx, w):
    hp = N_HEADS * HEAD_PAD
    rows = DEC_BATCH * PAST_LEN
    tile = lambda n: pl.BlockSpec((PAST_LEN, n), lambda i: (i, 0))
    cache = lambda n: pl.BlockSpec((None, None, PAST_LEN, n), lambda i: (i, j, 0, 0))
    return pl.pallas_call(
        _mla_ctx_kernel,
        name="mla_ctx",
        grid=(DEC_BATCH,),
        in_specs=[cache(KV_LORA), cache(HEAD_PAD), _entry(w["w_k"], j, True), _entry(w["w_v"], j, True)],
        out_specs=[tile(hp), tile(N_HEADS * V_HEAD)],
        out_shape=[jax.ShapeDtypeStruct((rows, hp), BF16),
                   jax.ShapeDtypeStruct((rows, N_HEADS * V_HEAD), BF16)],
        compiler_params=_cparams(("arbitrary",), 32 << 20),
    )(ckv_ctx, kpe_ctx, w["w_k"], w["w_v"])


NT_DIMS = (((1,), (1,)), ((), ()))
ATTN_PAIRS = 4
ATTN_TQ = 256
PROMPT_SEQS = 2


def _attn_kernel(*refs, n_pairs, n_groups, has_ctx, final, n_seq=1):
    refs = list(refs)
    q_ref, k_ref, v_ref = refs[:3]
    del refs[:3]
    if has_ctx:
        kc_ref, vc_ref = refs[:2]
        del refs[:2]
    sg_ref, wo_ref, x_ref, g_ref = refs[:4]
    del refs[:4]
    if final:
        fg_ref = refs.pop(0)
    out_ref, o_scr = refs[:2]
    tq = q_ref.shape[0] // n_seq
    lk = k_ref.shape[0] // n_seq
    low_half = lax.broadcasted_iota(jnp.int32, (tq, LANES), 1) < V_HEAD
    for sq, p in [(sq, p) for sq in range(n_seq) for p in range(n_pairs)]:
        qrows = pl.ds(sq * tq, tq)
        krows = pl.ds(sq * lk, lk)
        vcols = slice(p * LANES, (p + 1) * LANES)
        vp = v_ref[krows, vcols]
        outs = []
        for hh in range(2):
            cols = slice((2 * p + hh) * HEAD_PAD, (2 * p + hh + 1) * HEAD_PAD)
            q = q_ref[qrows, cols]
            s = lax.dot_general(q, k_ref[krows, cols], NT_DIMS, preferred_element_type=F32)
            mx = jnp.max(s, axis=-1, keepdims=True)
            if has_ctx:
                sc = lax.dot_general(q, kc_ref[:, cols], NT_DIMS, preferred_element_type=F32)
                mx = jnp.maximum(mx, jnp.max(sc, axis=-1, keepdims=True))
            e = jnp.exp2(s - mx)
            den = jnp.sum(e, axis=-1, keepdims=True)
            pv = jnp.dot(e.astype(BF16), vp, preferred_element_type=F32)
            if has_ctx:
                ec = jnp.exp2(sc - mx)
                den = den + jnp.sum(ec, axis=-1, keepdims=True)
                pv = pv + jnp.dot(ec.astype(BF16), vc_ref[:, vcols], preferred_element_type=F32)
            outs.append(pv / den)
        o = jnp.where(low_half, outs[0], outs[1])
        o_scr[qrows, vcols] = (o * sg_ref[qrows, vcols]).astype(BF16)

    u = jnp.dot(o_scr[...], wo_ref[...], preferred_element_type=F32)
    cond = 1 + pl.program_id(0) if has_ctx else 0

    def finish(u_all):
        xn = x_ref[...] + g_ref[pl.ds(cond, 1), :] * u_all
        out_ref[...] = _rms(xn, fg_ref[...]) if final else xn

    if n_groups == 1:
        finish(u)
    else:
        acc_scr = refs[2]
        grp = pl.program_id(2)

        @pl.when(grp == 0)
        def _():
            acc_scr[...] = u

        @pl.when((grp > 0) & (grp < n_groups - 1))
        def _():
            acc_scr[...] += u

        @pl.when(grp == n_groups - 1)
        def _():
            finish(acc_scr[...] + u)


def _attn_prompt(q, k, v, sg, w_o, x, mod, layer, final_g):
    hp = N_HEADS * HEAD_PAD
    nv = N_HEADS * V_HEAD
    xlo, _, _ = _row_pair(x)
    rows = PROMPT_SEQS * SEQ
    seq = lambda n: pl.BlockSpec((rows, n), lambda b: (b, 0))
    final = final_g is not None
    return pl.pallas_call(
        functools.partial(_attn_kernel, n_pairs=N_HEADS // 2, n_groups=1, has_ctx=False, final=final,
                          n_seq=PROMPT_SEQS),
        name="attn_prompt",
        grid=(BATCH // PROMPT_SEQS,),
        in_specs=[seq(hp), seq(hp), seq(nv), seq(nv), _entry(w_o, layer // 2, True), seq(D_MODEL),
                  _mod_spec(layer, 2)] + ([pl.BlockSpec((1, D_MODEL), lambda b: (0, 0))] if final else []),
        out_specs=seq(D_MODEL),
        out_shape=jax.ShapeDtypeStruct((ROWS_P, D_MODEL), F32),
        scratch_shapes=[pltpu.VMEM((rows, nv), BF16)],
        compiler_params=_cparams(("arbitrary",), 40 << 20),
    )(q, k, v, sg, w_o, xlo, mod, *([final_g] if final else []))


def _attn_latent(q, k, v, kc, vc, sg, w_o, x, mod, layer, final_g):
    tq = ATTN_TQ
    npair = ATTN_PAIRS
    ngrp = N_HEADS // 2 // npair
    tiles = DEC_SEQ // tq
    q0 = ROWS_P // tq
    s0 = ROWS_P // DEC_SEQ
    wide = npair * 2 * HEAD_PAD
    narrow = npair * LANES
    _, xhi, base = _row_pair(x)
    x0 = base // tq
    final = final_g is not None
    qrow = lambda b, t, p: (q0 + b * tiles + t, p)
    return pl.pallas_call(
        functools.partial(_attn_kernel, n_pairs=npair, n_groups=ngrp, has_ctx=True, final=final),
        name="attn_latent",
        grid=(DEC_BATCH, tiles, ngrp),
        in_specs=[pl.BlockSpec((tq, wide), qrow),
                  pl.BlockSpec((DEC_SEQ, wide), lambda b, t, p: (s0 + b, p)),
                  pl.BlockSpec((DEC_SEQ, narrow), lambda b, t, p: (s0 + b, p)),
                  pl.BlockSpec((PAST_LEN, wide), lambda b, t, p: (b, p)),
                  pl.BlockSpec((PAST_LEN, narrow), lambda b, t, p: (b, p)),
                  pl.BlockSpec((tq, narrow), qrow),
                  pl.BlockSpec((None, narrow, D_MODEL), lambda b, t, p: (layer // 2, p, 0)),
                  pl.BlockSpec((tq, D_MODEL), lambda b, t, p: (x0 + b * tiles + t, 0)),
                  _mod_spec(layer, 2)] + ([pl.BlockSpec((1, D_MODEL), lambda b, t, p: (0, 0))] if final else []),
        out_specs=pl.BlockSpec((tq, D_MODEL), lambda b, t, p: (b * tiles + t, 0)),
        out_shape=jax.ShapeDtypeStruct((ROWS_S, D_MODEL), F32),
        scratch_shapes=[pltpu.VMEM((tq, narrow), BF16), pltpu.VMEM((tq, D_MODEL), F32)],
        compiler_params=_cparams(("arbitrary", "arbitrary", "arbitrary"), 48 << 20),
    )(q, k, v, kc, vc, sg, w_o, xhi, mod, *([final_g] if final else []))


def _mla_weights(w_in, q_norm, w_qb, kv_norm, w_kvb, w_o):
    n = w_in.shape[0]
    o_pe = Q_LORA + KV_LORA
    o_gate = o_pe + QK_ROPE
    w_pe = jnp.pad(w_in[..., o_pe:o_gate], ((0, 0), (0, 0), (QK_NOPE, HEAD_PAD - QK_NOPE - QK_ROPE)))
    qb = w_qb.reshape(n, Q_LORA, N_HEADS, QK_NOPE + QK_ROPE)
    qb = jnp.pad(qb, ((0, 0), (0, 0), (0, 0), (0, HEAD_PAD - QK_NOPE - QK_ROPE)))
    kvb = w_kvb.reshape(n, KV_LORA, N_HEADS, QK_NOPE + V_HEAD)
    wk = jnp.pad(kvb[..., :QK_NOPE], ((0, 0), (0, 0), (0, 0), (0, HEAD_PAD - QK_NOPE)))
    wv = kvb[..., QK_NOPE:]
    return {
        "w_a": w_in[..., :o_pe].astype(BF16),
        "w_g": w_in[..., o_gate:].astype(BF16),
        "w_pe": w_pe.astype(BF16),
        "q_norm": q_norm[:, None, :],
        "kv_norm": kv_norm[:, None, :],
        "w_qb": qb.reshape(n, Q_LORA, N_HEADS * HEAD_PAD).astype(BF16),
        "w_k": wk.reshape(n, KV_LORA, N_HEADS * HEAD_PAD).astype(BF16),
        "w_v": wv.reshape(n, KV_LORA, N_HEADS * V_HEAD).astype(BF16),
        "w_o": w_o.astype(BF16),
    }


def kernel(x_prompt, x_sample, cache_ckv, cache_kpe, c, c_ctx, norm_w, ada_w, ada_b, hy_w_in, hy_conv_w, hy_conv_b, hy_f_w1, hy_f_b1, hy_f_freq, hy_f_w2, hy_f_b2, hy_f_w3, hy_f_bias, hy_w_out, mla_w_in, mla_q_norm, mla_w_qb, mla_kv_norm, mla_w_kvb, mla_w_o, final_norm):
    x = (x_prompt.reshape(ROWS_P, D_MODEL), x_sample.reshape(ROWS_S, D_MODEL))
    cond = jnp.concatenate([c_ctx[None, :], c, jnp.zeros((N_COND - 1 - DEC_BATCH, D_MODEL), F32)], axis=0)
    mod = _ada_all(cond, ada_w, ada_b)

    nw = norm_w[:, None, :]
    hy_w_in_b = hy_w_in.astype(BF16)
    hy_w_out_b = hy_w_out.astype(BF16)
    hy_conv_b3 = hy_conv_b[:, None, :]
    f_w1 = jnp.pad(hy_f_w1, ((0, 0), (0, FILTER_HIDDEN - FILTER_EMB), (0, 0)))
    f_b1, f_freq, f_b2, f_bias = (a[:, None, :] for a in (hy_f_b1, hy_f_freq, hy_f_b2, hy_f_bias))
    w = _mla_weights(mla_w_in, mla_q_norm, mla_w_qb, mla_kv_norm, mla_w_kvb, mla_w_o)
    kpe_ctx = jnp.pad(cache_kpe, ((0, 0), (0, 0), (0, 0), (QK_NOPE, HEAD_PAD - QK_NOPE - QK_ROPE)))

    new_ckv, new_kpe = [], []
    for layer in range(DEPTH):
        j = layer // 2
        if layer % 2 == 0:
            z, m = _hy_in(x, nw, mod, layer, hy_w_in_b, hy_conv_w, hy_conv_b3)
            t = []
            for L, blk0, nblk in ((SEQ, 0, ROWS_P // SEQ_BLOCK), (DEC_SEQ, ROWS_P // SEQ_BLOCK, ROWS_S // SEQ_BLOCK)):
                filt = _hy_filter(L, j, f_w1, f_b1, f_freq, hy_f_w2, f_b2, hy_f_w3)
                cm, sm = _dft_mats(L)
                t.append(_hy_lconv(L, j, blk0, nblk, z, m, filt, f_bias, cm, sm))
            t = tuple(t)
        else:
            x, q, k, v, sg, ckv, kpe = _mla_proj(t, hy_w_out_b, x, nw, mod, layer, w)
            kc, vc = _mla_ctx(j, cache_ckv, kpe_ctx, w)
            final_g = final_norm[None, :] if layer == DEPTH - 1 else None
            x = (_attn_prompt(q, k, v, sg, w["w_o"], x, mod, layer, final_g),
                 _attn_latent(q, k, v, kc, vc, sg, w["w_o"], x, mod, layer, final_g))
            new_ckv.append(ckv[:ROWS_P].reshape(BATCH, SEQ, KV_LORA))
            new_kpe.append(kpe[:ROWS_P, QK_NOPE:QK_NOPE + QK_ROPE].reshape(BATCH, SEQ, QK_ROPE))

    assert DEPTH % 2 == 0
    y_prompt, y_sample = x
    return (y_prompt.reshape(BATCH, SEQ, D_MODEL), y_sample.reshape(DEC_BATCH, DEC_SEQ, D_MODEL),
            jnp.stack(new_ckv, axis=1), jnp.stack(new_kpe, axis=1))
```

```python
import functools
import math

import numpy as np
import jax
import jax.numpy as jnp
from jax import lax
from jax.experimental import pallas as pl
from jax.experimental.pallas import tpu as pltpu

F32 = jnp.float32
BF16 = jnp.bfloat16

D_MODEL = 1024
BATCH = 16
SEQ = 256
DEPTH = 4
DEC_BATCH = 2
DEC_SEQ = 2048
PAST_LEN = 512
GRID_W = 64
EPS = 1e-6
HY_WIDTH = D_MODEL
FILTER_BANDS = 16
FILTER_EMB = 1 + 2 * FILTER_BANDS
FILTER_HIDDEN = 64
FAST_DECAY_PCT = 0.3
SLOW_DECAY_PCT = 1.5
DECAY_TARGET = 1e-2
N_HEADS = 16
Q_LORA = 384
KV_LORA = 256
QK_NOPE = 64
QK_ROPE = 32
V_HEAD = 64
ROPE_THETA = 10000.0

LANES = 128
MXU_TILE = 256
HEAD_PAD = LANES
ROWS_P = BATCH * SEQ
ROWS_S = DEC_BATCH * DEC_SEQ
ROWS = ROWS_P + ROWS_S
TM = 512
N_TILES = ROWS // TM
TILES_P = ROWS_P // TM
TILES_PER_DEC_SEQ = DEC_SEQ // TM
N_COND = 8
SEQ_BLOCK = 2048
TK = 512
LCONV_TC = 256
VMEM_CAP = 56 * 1024 * 1024


def _cparams(sem, vmem_bytes):
    return pltpu.CompilerParams(dimension_semantics=sem, vmem_limit_bytes=min(int(vmem_bytes), VMEM_CAP))


def _resident(shape, index_map):
    return pl.BlockSpec(shape, index_map, pipeline_mode=pl.Buffered(1))


def _entry(arr, idx, resident=False):
    zeros = (0,) * (arr.ndim - 1)
    return pl.BlockSpec((None,) + arr.shape[1:], lambda *_: (idx,) + zeros,
                        pipeline_mode=pl.Buffered(1) if resident else None)


def _cond_row(i, rows=TM):
    return jnp.where(i < ROWS_P // rows, 0, 1 + (i - ROWS_P // rows) // (DEC_SEQ // rows))


def _silu(x):
    return x * jax.nn.sigmoid(x)


def _rms(x, g):
    return x * lax.rsqrt(jnp.mean(x * x, axis=-1, keepdims=True) + EPS) * g


def _row_pair(x):
    if isinstance(x, tuple):
        return x[0], x[1], 0
    return x, x, ROWS_P


def _row_pair_specs(x, n, rows=TM):
    _, _, base = _row_pair(x)
    n_lo = ROWS_P // rows
    lo = pl.BlockSpec((rows, n), lambda i, *_: (jnp.minimum(i, n_lo - 1), 0))
    hi = pl.BlockSpec((rows, n), lambda i, *_: (jnp.maximum(i - n_lo, 0) + base // rows, 0))
    return lo, hi


def _pick_rows(i, lo_ref, hi_ref, rows=TM):
    return jnp.where(i < ROWS_P // rows, lo_ref[...], hi_ref[...])


def _ada_kernel(cond_ref, w_ref, b_ref, o_ref):
    s = _silu(cond_ref[...]).astype(BF16)
    o_ref[...] = jnp.dot(s, w_ref[...].astype(BF16), preferred_element_type=F32) + b_ref[...]


def _ada_all(cond, ada_w, ada_b):
    tn = 1024
    return pl.pallas_call(
        _ada_kernel,
        name="ada",
        grid=(DEPTH, 3 * D_MODEL // tn),
        in_specs=[
            pl.BlockSpec((N_COND, D_MODEL), lambda l, j: (0, 0)),
            pl.BlockSpec((None, D_MODEL, tn), lambda l, j: (l, 0, j)),
            pl.BlockSpec((None, 1, tn), lambda l, j: (l, 0, j)),
        ],
        out_specs=pl.BlockSpec((None, N_COND, tn), lambda l, j: (l, 0, j)),
        out_shape=jax.ShapeDtypeStruct((DEPTH, N_COND, 3 * D_MODEL), F32),
        compiler_params=_cparams(("arbitrary", "arbitrary"), 32 << 20),
    )(cond, ada_w, ada_b.reshape(DEPTH, 1, 3 * D_MODEL))


def _mod_spec(layer, part):
    return pl.BlockSpec((None, N_COND, D_MODEL), lambda i, *_: (layer, 0, part))


def _modnorm(i, x, nw_ref, sh_ref, sc_ref):
    c = _cond_row(i)
    return _rms(x, nw_ref[...]) * (1.0 + sc_ref[pl.ds(c, 1), :]) + sh_ref[pl.ds(c, 1), :]


HALO = 16
HY_ROWS = TM


def _halo_specs(x):
    _, _, base = _row_pair(x)
    r = HY_ROWS // HALO
    n_lo = ROWS_P // HY_ROWS
    lo_last = ROWS_P // HALO - 1
    hi_last = ROWS_S // HALO - 1
    lo_blk = lambda i: jnp.minimum(i, n_lo - 1)
    hi_blk = lambda i: jnp.maximum(i - n_lo, 0)
    blk = lambda f: pl.BlockSpec((HALO, D_MODEL), lambda i: (f(i), 0))
    return [
        blk(lambda i: jnp.maximum(lo_blk(i) * r - 1, 0)),
        blk(lambda i: jnp.maximum(hi_blk(i) * r - 1, 0) + base // HALO),
        blk(lambda i: jnp.minimum((lo_blk(i) + 1) * r, lo_last)),
        blk(lambda i: jnp.minimum((hi_blk(i) + 1) * r, hi_last) + base // HALO),
    ]


def _hy_in_kernel(xlo_ref, xhi_ref, plo_ref, phi_ref, nlo_ref, nhi_ref, nw_ref, sh_ref, sc_ref, w_ref,
                  cw_ref, cb_ref, z_ref, m_ref, h_scr, *u_scrs):
    i = pl.program_id(0)
    width = z_ref.shape[1]
    c = _cond_row(i, HY_ROWS)
    norm = lambda lo_ref, hi_ref: (_rms(_pick_rows(i, lo_ref, hi_ref, HY_ROWS), nw_ref[...])
                                   * (1.0 + sc_ref[pl.ds(c, 1), :]) + sh_ref[pl.ds(c, 1), :]).astype(BF16)
    h_scr[0:HALO, :] = norm(plo_ref, phi_ref)
    h_scr[HALO:HALO + HY_ROWS, :] = norm(xlo_ref, xhi_ref)
    h_scr[HALO + HY_ROWS:HALO + HY_ROWS + HALO, :] = norm(nlo_ref, nhi_ref)

    seq_mask = jnp.where(i < ROWS_P // HY_ROWS, SEQ - 1, DEC_SEQ - 1)
    row = lax.broadcasted_iota(jnp.int32, (TM, width), 0)
    for t in range(HY_ROWS // TM):
        r0 = t * TM
        pos = (row + (i * HY_ROWS + r0)) & seq_mask
        first = pos == 0
        last = pos == seq_mask
        rows_h = pl.ds(r0, TM + 2 * HALO)

        def conv(g, u_scr):
            cols = slice(g * width, (g + 1) * width)
            u_scr[...] = jnp.dot(h_scr[rows_h, :], w_ref[:, cols], preferred_element_type=F32)
            prev = jnp.where(first, 0.0, u_scr[HALO - 1:HALO - 1 + TM, :])
            nxt = jnp.where(last, 0.0, u_scr[HALO + 1:HALO + 1 + TM, :])
            return (cb_ref[:, cols] + prev * cw_ref[0:1, cols] + u_scr[HALO:HALO + TM, :] * cw_ref[1:2, cols]
                    + nxt * cw_ref[2:3, cols])

        u0, u1, u2 = u_scrs[3 * t:3 * t + 3]
        out_rows = pl.ds(r0, TM)
        z_ref[out_rows, :] = (conv(2, u2) * conv(1, u1)).astype(BF16)
        gate = jnp.dot(h_scr[pl.ds(r0 + HALO, TM), :], w_ref[:, 3 * width:4 * width], preferred_element_type=F32)
        m_ref[out_rows, :] = (conv(0, u0) * _silu(gate)).astype(BF16)


def _hy_in(x, norm_w, mod, layer, w_in, conv_w, conv_b):
    j = layer // 2
    xlo, xhi, _ = _row_pair(x)
    out = pl.BlockSpec((HY_ROWS, HY_WIDTH), lambda i: (i, 0))
    n_tiles = HY_ROWS // TM
    return pl.pallas_call(
        _hy_in_kernel,
        name="hy_in",
        grid=(ROWS // HY_ROWS,),
        in_specs=[
            *_row_pair_specs(x, D_MODEL, HY_ROWS),
            *_halo_specs(x),
            _entry(norm_w, layer),
            _mod_spec(layer, 0),
            _mod_spec(layer, 1),
            _entry(w_in, j, resident=True),
            _entry(conv_w, j),
            _entry(conv_b, j),
        ],
        out_specs=[out, out],
        out_shape=[jax.ShapeDtypeStruct((ROWS, HY_WIDTH), BF16)] * 2,
        scratch_shapes=([pltpu.VMEM((HY_ROWS + 2 * HALO, D_MODEL), BF16)]
                        + [pltpu.VMEM((TM + 2 * HALO, HY_WIDTH), F32)] * (3 * n_tiles)),
        compiler_params=_cparams(("arbitrary",), 56 << 20),
    )(xlo, xhi, xlo, xhi, xlo, xhi, norm_w, mod, mod, w_in, conv_w, conv_b)


@functools.lru_cache(maxsize=None)
def _filter_consts(L):
    t = np.linspace(0.0, 1.0, L)[:, None]
    w = (2.0 * math.pi / L) * np.arange(L)[:, None]
    bands = np.linspace(1e-4, FILTER_BANDS - 1, FILTER_BANDS)[None, :]
    emb = np.concatenate([t, np.cos(bands * w), -np.sin(bands * w)], axis=-1)
    emb = np.pad(emb, ((0, 0), (0, FILTER_HIDDEN - FILTER_EMB)))
    max_decay = math.log(DECAY_TARGET) / FAST_DECAY_PCT
    min_decay = math.log(DECAY_TARGET) / SLOW_DECAY_PCT
    deltas = np.abs(np.linspace(min_decay, max_decay, HY_WIDTH))
    deltas = np.concatenate([deltas, deltas])[None, :]
    return emb.astype(np.float32), t.astype(np.float32), deltas.astype(np.float32)


def _filter_kernel(emb_ref, t_ref, w1_ref, b1_ref, fr_ref, w2_ref, b2_ref, w3_ref, dl_ref, o_ref, h_scr):
    @pl.when(pl.program_id(0) == 0)
    def _():
        hi = lax.Precision.HIGHEST
        fr = fr_ref[...]
        h = jnp.sin(fr * (jnp.dot(emb_ref[...], w1_ref[...], precision=hi, preferred_element_type=F32) + b1_ref[...]))
        h = jnp.sin(fr * (jnp.dot(h, w2_ref[...], precision=hi, preferred_element_type=F32) + b2_ref[...]))
        h_scr[...] = h.astype(BF16)

    h = jnp.dot(h_scr[...], w3_ref[...].astype(BF16), preferred_element_type=F32)
    h = h * jnp.exp(-t_ref[...] * dl_ref[...])
    o_ref[...] = h / jnp.sum(jnp.abs(h), axis=0, keepdims=True)


def _hy_filter(L, hy, w1, b1, freq, w2, b2, w3):
    emb, t, deltas = _filter_consts(L)
    tcf = 512
    full = lambda shape: pl.BlockSpec(shape, lambda c: (0, 0))
    return pl.pallas_call(
        _filter_kernel,
        name=f"hy_filter{L}",
        grid=(2 * HY_WIDTH // tcf,),
        in_specs=[
            full((L, FILTER_HIDDEN)), full((L, 1)), _entry(w1, hy), _entry(b1, hy), _entry(freq, hy),
            _entry(w2, hy), _entry(b2, hy),
            pl.BlockSpec((None, FILTER_HIDDEN, tcf), lambda c: (hy, 0, c)),
            pl.BlockSpec((1, tcf), lambda c: (0, c)),
        ],
        out_specs=pl.BlockSpec((L, tcf), lambda c: (0, c)),
        out_shape=jax.ShapeDtypeStruct((L, 2 * HY_WIDTH), F32),
        scratch_shapes=[pltpu.VMEM((L, FILTER_HIDDEN), BF16)],
        compiler_params=_cparams(("arbitrary",), 40 << 20),
    )(jnp.asarray(emb), jnp.asarray(t), w1, b1, freq, w2, b2, w3, jnp.asarray(deltas))


@functools.lru_cache(maxsize=None)
def _dft_consts(L):
    k = np.arange(L, dtype=np.int64)
    ang = (np.outer(k, k) % (2 * L)).astype(np.float64) * (math.pi / L)
    return np.cos(ang).astype(np.float32), np.sin(ang).astype(np.float32)


def _dft_mats(L):
    c, s = _dft_consts(L)
    return jnp.asarray(c).astype(BF16), jnp.asarray(s).astype(BF16)


def _alt_sign(shape):
    return (1 - 2 * (lax.broadcasted_iota(jnp.int32, shape, 0) & 1)).astype(F32)


def _lconv_kernel(z_ref, m_ref, ff_ref, fb_ref, bias_ref, c_ref, s_ref, t_ref, a_scr, b_scr, nq_scr, y_scr,
                  *, L, tk):
    rows, tc = y_scr.shape
    alt = _alt_sign((L, tc))
    bins = [slice(k0, k0 + tk) for k0 in range(0, L, tk)]

    @pl.when(pl.program_id(1) == 0)
    def _():
        inv_n = 1.0 / (2.0 * L)
        bias = bias_ref[...]
        ff = ff_ref[...]
        fb = fb_ref[...]
        f = ff + fb
        nq_scr[...] = jnp.broadcast_to((jnp.sum(f * alt, axis=0, keepdims=True) + bias) * inv_n, nq_scr.shape)
        f_b = f.astype(BF16)
        d_b = (fb - ff).astype(BF16)
        for kt in bins:
            hre = jnp.dot(c_ref[kt, :], f_b, preferred_element_type=F32) + bias
            him = jnp.dot(s_ref[kt, :], d_b, preferred_element_type=F32)
            a = hre * (2.0 * inv_n)
            if kt.start == 0:
                a = jnp.where(lax.broadcasted_iota(jnp.int32, hre.shape, 0) == 0, hre * inv_n, a)
            a_scr[kt, :] = a
            b_scr[kt, :] = him * (2.0 * inv_n)

    for s0 in range(0, rows, L):
        sq = pl.ds(s0, L)
        zb = z_ref[sq, :]
        z_nyq = jnp.sum(zb.astype(F32) * alt, axis=0, keepdims=True)
        y_scr[sq, :] = alt * (z_nyq * nq_scr[0:1, :])
        for kt in bins:
            zr = jnp.dot(c_ref[kt, :], zb, preferred_element_type=F32)
            zi = jnp.dot(s_ref[kt, :], zb, preferred_element_type=F32)
            a = a_scr[kt, :]
            b = b_scr[kt, :]
            yr = (zr * a + zi * b).astype(BF16)
            yw = (zi * a - zr * b).astype(BF16)
            y_scr[sq, :] += (jnp.dot(c_ref[:, kt], yr, preferred_element_type=F32)
                             + jnp.dot(s_ref[:, kt], yw, preferred_element_type=F32))
        t_ref[sq, :] = (y_scr[sq, :] * m_ref[sq, :].astype(F32)).astype(BF16)


def _hy_lconv(L, hy, row_block0, n_row_blocks, z, m, filt, f_bias, c, s):
    tk = min(TK, L)
    long_seq = L == SEQ_BLOCK
    tc = 2 * LCONV_TC if long_seq else LCONV_TC
    nc = HY_WIDTH // tc
    blk = pl.BlockSpec((SEQ_BLOCK, tc), lambda j, r: (r + row_block0, j))
    mat = _resident((L, L), lambda j, r: (0, 0))
    filt_mode = pl.Buffered(1) if long_seq else None
    return pl.pallas_call(
        functools.partial(_lconv_kernel, L=L, tk=tk),
        name=f"hy_lconv{L}",
        grid=(nc, n_row_blocks),
        in_specs=[blk, blk,
                  pl.BlockSpec((L, tc), lambda j, r: (0, j), pipeline_mode=filt_mode),
                  pl.BlockSpec((L, tc), lambda j, r: (0, j + nc), pipeline_mode=filt_mode),
                  pl.BlockSpec((None, 1, tc), lambda j, r: (hy, 0, j)),
                  mat, mat],
        out_specs=pl.BlockSpec((SEQ_BLOCK, tc), lambda j, r: (r, j)),
        out_shape=jax.ShapeDtypeStruct((n_row_blocks * SEQ_BLOCK, HY_WIDTH), BF16),
        scratch_shapes=[pltpu.VMEM((L, tc), F32), pltpu.VMEM((L, tc), F32), pltpu.VMEM((8, tc), F32),
                        pltpu.VMEM((SEQ_BLOCK, tc), F32)],
        compiler_params=_cparams(("arbitrary", "arbitrary"), 56 << 20),
    )(z, m, filt, filt, f_bias, c, s)


@functools.lru_cache(maxsize=None)
def _rope_consts():
    axis_dim = QK_ROPE // 2
    nf = axis_dim // 2
    inv = ROPE_THETA ** (-np.arange(0, axis_dim, 2, dtype=np.float64) / axis_dim)
    t = np.arange(DEC_SEQ)
    ang_r = (t // GRID_W)[:, None] * inv
    ang_c = (t % GRID_W)[:, None] * inv
    cos = np.ones((DEC_SEQ, HEAD_PAD))
    sin_up = np.zeros((DEC_SEQ, HEAD_PAD))
    sin_dn = np.zeros((DEC_SEQ, HEAD_PAD))
    for base, ang in ((QK_NOPE, ang_r), (QK_NOPE + axis_dim, ang_c)):
        cos[:, base:base + nf] = np.cos(ang)
        cos[:, base + nf:base + 2 * nf] = np.cos(ang)
        sin_up[:, base:base + nf] = -np.sin(ang)
        sin_dn[:, base + nf:base + 2 * nf] = np.sin(ang)
    return cos.astype(np.float32), sin_up.astype(np.float32), sin_dn.astype(np.float32)


ROPE_HALF = QK_ROPE // 4
Q_SCALE = math.log2(math.e) / math.sqrt(QK_NOPE + QK_ROPE)


def _mla_proj_kernel(tlo_ref, thi_ref, wout_ref, xlo_ref, xhi_ref, gprev_ref,
                     nw_ref, sh_ref, sc_ref, wa_ref, wg_ref, wpe_ref, qn_ref, kvn_ref, wqb_ref, wk_ref, wv_ref,
                     cos_ref, sup_ref, sdn_ref,
                     xn_ref, q_ref, k_ref, v_ref, sg_ref, ckv_ref, kpe_ref):
    i = pl.program_id(0)
    u = jnp.dot(_pick_rows(i, tlo_ref, thi_ref), wout_ref[...], preferred_element_type=F32)
    x = _pick_rows(i, xlo_ref, xhi_ref) + gprev_ref[pl.ds(_cond_row(i), 1), :] * u
    xn_ref[...] = x
    h = _modnorm(i, x, nw_ref, sh_ref, sc_ref).astype(BF16)
    lora = jnp.dot(h, wa_ref[...], preferred_element_type=F32)
    gate = jnp.dot(h, wg_ref[...], preferred_element_type=F32)
    kpe = jnp.dot(h, wpe_ref[...], preferred_element_type=F32)
    qn = _rms(lora[:, 0:Q_LORA], qn_ref[...]).astype(BF16)
    ckv = _rms(lora[:, Q_LORA:Q_LORA + KV_LORA], kvn_ref[...])
    ckv_ref[...] = ckv
    kpe_ref[...] = kpe
    sg_ref[...] = _silu(gate)
    ckv_b = ckv.astype(BF16)
    v_ref[...] = jnp.dot(ckv_b, wv_ref[...], preferred_element_type=F32).astype(BF16)
    q = jnp.dot(qn, wqb_ref[...], preferred_element_type=F32)
    kn = jnp.dot(ckv_b, wk_ref[...], preferred_element_type=F32)

    latent = i >= TILES_P
    cos = jnp.where(latent, cos_ref[...], 1.0)
    sup = jnp.where(latent, sup_ref[...], 0.0)
    sdn = jnp.where(latent, sdn_ref[...], 0.0)

    def rope(u, scale):
        return (u * (cos * scale) + pltpu.roll(u, HEAD_PAD - ROPE_HALF, axis=1) * (sup * scale)
                + pltpu.roll(u, ROPE_HALF, axis=1) * (sdn * scale))

    kpe_r = rope(kpe, 1.0)
    for hd in range(N_HEADS):
        cols = slice(hd * HEAD_PAD, (hd + 1) * HEAD_PAD)
        q_ref[:, cols] = rope(q[:, cols], Q_SCALE).astype(BF16)
        k_ref[:, cols] = (kn[:, cols] + kpe_r).astype(BF16)


def _mla_proj(t, w_out, x, norm_w, mod, layer, w):
    cos, sup, sdn = (jnp.asarray(c) for c in _rope_consts())
    j = layer // 2
    hp = N_HEADS * HEAD_PAD
    tlo, thi, _ = _row_pair(t)
    xlo, xhi, _ = _row_pair(x)
    rope_blk = pl.BlockSpec(
        (TM, HEAD_PAD), lambda i: (jnp.where(i >= TILES_P, (i - TILES_P) % TILES_PER_DEC_SEQ, 0), 0))
    tile = lambda n: pl.BlockSpec((TM, n), lambda i: (i, 0))
    return pl.pallas_call(
        _mla_proj_kernel,
        name="mla_proj",
        grid=(N_TILES,),
        in_specs=[
            *_row_pair_specs(t, HY_WIDTH),
            _entry(w_out, (layer - 1) // 2, True),
            *_row_pair_specs(x, D_MODEL),
            _mod_spec(layer - 1, 2),
            _entry(norm_w, layer),
            _mod_spec(layer, 0), _mod_spec(layer, 1),
            _entry(w["w_a"], j, True), _entry(w["w_g"], j, True), _entry(w["w_pe"], j, True),
            _entry(w["q_norm"], j), _entry(w["kv_norm"], j),
            _entry(w["w_qb"], j, True), _entry(w["w_k"], j, True), _entry(w["w_v"], j, True),
            rope_blk, rope_blk, rope_blk,
        ],
        out_specs=[tile(D_MODEL), tile(hp), tile(hp), tile(N_HEADS * V_HEAD),
                   tile(N_HEADS * V_HEAD), tile(KV_LORA), tile(HEAD_PAD)],
        out_shape=[
            jax.ShapeDtypeStruct((ROWS, D_MODEL), F32),
            jax.ShapeDtypeStruct((ROWS, hp), BF16),
            jax.ShapeDtypeStruct((ROWS, hp), BF16),
            jax.ShapeDtypeStruct((ROWS, N_HEADS * V_HEAD), BF16),
            jax.ShapeDtypeStruct((ROWS, N_HEADS * V_HEAD), F32),
            jax.ShapeDtypeStruct((ROWS, KV_LORA), F32),
            jax.ShapeDtypeStruct((ROWS, HEAD_PAD), F32),
        ],
        compiler_params=_cparams(("arbitrary",), 56 << 20),
    )(tlo, thi, w_out, xlo, xhi, mod, norm_w, mod, mod, w["w_a"], w["w_g"], w["w_pe"], w["q_norm"],
      w["kv_norm"], w["w_qb"], w["w_k"], w["w_v"], cos, sup, sdn)


def _mla_ctx_kernel(ckv_ref, kpe_ref, wk_ref, wv_ref, k_ref, v_ref):
    ckv_b = ckv_ref[...].astype(BF16)
    kn = jnp.dot(ckv_b, wk_ref[...], preferred_element_type=F32)
    v_ref[...] = jnp.dot(ckv_b, wv_ref[...], preferred_element_type=F32).astype(BF16)
    kpe = kpe_ref[...]
    for hd in range(N_HEADS):
        cols = slice(hd * HEAD_PAD, (hd + 1) * HEAD_PAD)
        k_ref[:, cols] = (kn[:, cols] + kpe).astype(BF16)


def _mla_ctx(j, ckv_ctx, kpe_ctx, w):
    hp = N_HEADS * HEAD_PAD
    rows = DEC_BATCH * PAST_LEN
    tile = lambda n: pl.BlockSpec((PAST_LEN, n), lambda i: (i, 0))
    cache = lambda n: pl.BlockSpec((None, None, PAST_LEN, n), lambda i: (i, j, 0, 0))
    return pl.pallas_call(
        _mla_ctx_kernel,
        name="mla_ctx",
        grid=(DEC_BATCH,),
        in_specs=[cache(KV_LORA), cache(HEAD_PAD), _entry(w["w_k"], j, True), _entry(w["w_v"], j, True)],
        out_specs=[tile(hp), tile(N_HEADS * V_HEAD)],
        out_shape=[jax.ShapeDtypeStruct((rows, hp), BF16),
                   jax.ShapeDtypeStruct((rows, N_HEADS * V_HEAD), BF16)],
        compiler_params=_cparams(("arbitrary",), 32 << 20),
    )(ckv_ctx, kpe_ctx, w["w_k"], w["w_v"])


NT_DIMS = (((1,), (1,)), ((), ()))
ATTN_PAIRS = 4
ATTN_TQ = 256
PROMPT_SEQS = 2


def _attn_kernel(*refs, n_pairs, n_groups, has_ctx, final, n_seq=1):
    refs = list(refs)
    q_ref, k_ref, v_ref = refs[:3]
    del refs[:3]
    if has_ctx:
        kc_ref, vc_ref = refs[:2]
        del refs[:2]
    sg_ref, wo_ref, x_ref, g_ref = refs[:4]
    del refs[:4]
    if final:
        fg_ref = refs.pop(0)
    out_ref, o_scr = refs[:2]
    tq = q_ref.shape[0] // n_seq
    lk = k_ref.shape[0] // n_seq
    low_half = lax.broadcasted_iota(jnp.int32, (tq, LANES), 1) < V_HEAD
    for sq, p in [(sq, p) for sq in range(n_seq) for p in range(n_pairs)]:
        qrows = pl.ds(sq * tq, tq)
        krows = pl.ds(sq * lk, lk)
        vcols = slice(p * LANES, (p + 1) * LANES)
        vp = v_ref[krows, vcols]
        outs = []
        for hh in range(2):
            cols = slice((2 * p + hh) * HEAD_PAD, (2 * p + hh + 1) * HEAD_PAD)
            q = q_ref[qrows, cols]
            s = lax.dot_general(q, k_ref[krows, cols], NT_DIMS, preferred_element_type=F32)
            mx = jnp.max(s, axis=-1, keepdims=True)
            if has_ctx:
                sc = lax.dot_general(q, kc_ref[:, cols], NT_DIMS, preferred_element_type=F32)
                mx = jnp.maximum(mx, jnp.max(sc, axis=-1, keepdims=True))
            e = jnp.exp2(s - mx)
            den = jnp.sum(e, axis=-1, keepdims=True)
            pv = jnp.dot(e.astype(BF16), vp, preferred_element_type=F32)
            if has_ctx:
                ec = jnp.exp2(sc - mx)
                den = den + jnp.sum(ec, axis=-1, keepdims=True)
                pv = pv + jnp.dot(ec.astype(BF16), vc_ref[:, vcols], preferred_element_type=F32)
            outs.append(pv / den)
        o = jnp.where(low_half, outs[0], outs[1])
        o_scr[qrows, vcols] = (o * sg_ref[qrows, vcols]).astype(BF16)

    u = jnp.dot(o_scr[...], wo_ref[...], preferred_element_type=F32)
    cond = 1 + pl.program_id(0) if has_ctx else 0

    def finish(u_all):
        xn = x_ref[...] + g_ref[pl.ds(cond, 1), :] * u_all
        out_ref[...] = _rms(xn, fg_ref[...]) if final else xn

    if n_groups == 1:
        finish(u)
    else:
        acc_scr = refs[2]
        grp = pl.program_id(2)

        @pl.when(grp == 0)
        def _():
            acc_scr[...] = u

        @pl.when((grp > 0) & (grp < n_groups - 1))
        def _():
            acc_scr[...] += u

        @pl.when(grp == n_groups - 1)
        def _():
            finish(acc_scr[...] + u)


def _attn_prompt(q, k, v, sg, w_o, x, mod, layer, final_g):
    hp = N_HEADS * HEAD_PAD
    nv = N_HEADS * V_HEAD
    xlo, _, _ = _row_pair(x)
    rows = PROMPT_SEQS * SEQ
    seq = lambda n: pl.BlockSpec((rows, n), lambda b: (b, 0))
    final = final_g is not None
    return pl.pallas_call(
        functools.partial(_attn_kernel, n_pairs=N_HEADS // 2, n_groups=1, has_ctx=False, final=final,
                          n_seq=PROMPT_SEQS),
        name="attn_prompt",
        grid=(BATCH // PROMPT_SEQS,),
        in_specs=[seq(hp), seq(hp), seq(nv), seq(nv), _entry(w_o, layer // 2, True), seq(D_MODEL),
                  _mod_spec(layer, 2)] + ([pl.BlockSpec((1, D_MODEL), lambda b: (0, 0))] if final else []),
        out_specs=seq(D_MODEL),
        out_shape=jax.ShapeDtypeStruct((ROWS_P, D_MODEL), F32),
        scratch_shapes=[pltpu.VMEM((rows, nv), BF16)],
        compiler_params=_cparams(("arbitrary",), 40 << 20),
    )(q, k, v, sg, w_o, xlo, mod, *([final_g] if final else []))


def _attn_latent(q, k, v, kc, vc, sg, w_o, x, mod, layer, final_g):
    tq = ATTN_TQ
    npair = ATTN_PAIRS
    ngrp = N_HEADS // 2 // npair
    tiles = DEC_SEQ // tq
    q0 = ROWS_P // tq
    s0 = ROWS_P // DEC_SEQ
    wide = npair * 2 * HEAD_PAD
    narrow = npair * LANES
    _, xhi, base = _row_pair(x)
    x0 = base // tq
    final = final_g is not None
    qrow = lambda b, t, p: (q0 + b * tiles + t, p)
    return pl.pallas_call(
        functools.partial(_attn_kernel, n_pairs=npair, n_groups=ngrp, has_ctx=True, final=final),
        name="attn_latent",
        grid=(DEC_BATCH, tiles, ngrp),
        in_specs=[pl.BlockSpec((tq, wide), qrow),
                  pl.BlockSpec((DEC_SEQ, wide), lambda b, t, p: (s0 + b, p)),
                  pl.BlockSpec((DEC_SEQ, narrow), lambda b, t, p: (s0 + b, p)),
                  pl.BlockSpec((PAST_LEN, wide), lambda b, t, p: (b, p)),
                  pl.BlockSpec((PAST_LEN, narrow), lambda b, t, p: (b, p)),
                  pl.BlockSpec((tq, narrow), qrow),
                  pl.BlockSpec((None, narrow, D_MODEL), lambda b, t, p: (layer // 2, p, 0)),
                  pl.BlockSpec((tq, D_MODEL), lambda b, t, p: (x0 + b * tiles + t, 0)),
                  _mod_spec(layer, 2)] + ([pl.BlockSpec((1, D_MODEL), lambda b, t, p: (0, 0))] if final else []),
        out_specs=pl.BlockSpec((tq, D_MODEL), lambda b, t, p: (b * tiles + t, 0)),
        out_shape=jax.ShapeDtypeStruct((ROWS_S, D_MODEL), F32),
        scratch_shapes=[pltpu.VMEM((tq, narrow), BF16), pltpu.VMEM((tq, D_MODEL), F32)],
        compiler_params=_cparams(("arbitrary", "arbitrary", "arbitrary"), 48 << 20),
    )(q, k, v, kc, vc, sg, w_o, xhi, mod, *([final_g] if final else []))


def _mla_weights(w_in, q_norm, w_qb, kv_norm, w_kvb, w_o):
    n = w_in.shape[0]
    o_pe = Q_LORA + KV_LORA
    o_gate = o_pe + QK_ROPE
    w_pe = jnp.pad(w_in[..., o_pe:o_gate], ((0, 0), (0, 0), (QK_NOPE, HEAD_PAD - QK_NOPE - QK_ROPE)))
    qb = w_qb.reshape(n, Q_LORA, N_HEADS, QK_NOPE + QK_ROPE)
    qb = jnp.pad(qb, ((0, 0), (0, 0), (0, 0), (0, HEAD_PAD - QK_NOPE - QK_ROPE)))
    kvb = w_kvb.reshape(n, KV_LORA, N_HEADS, QK_NOPE + V_HEAD)
    wk = jnp.pad(kvb[..., :QK_NOPE], ((0, 0), (0, 0), (0, 0), (0, HEAD_PAD - QK_NOPE)))
    wv = kvb[..., QK_NOPE:]
    return {
        "w_a": w_in[..., :o_pe].astype(BF16),
        "w_g": w_in[..., o_gate:].astype(BF16),
        "w_pe": w_pe.astype(BF16),
        "q_norm": q_norm[:, None, :],
        "kv_norm": kv_norm[:, None, :],
        "w_qb": qb.reshape(n, Q_LORA, N_HEADS * HEAD_PAD).astype(BF16),
        "w_k": wk.reshape(n, KV_LORA, N_HEADS * HEAD_PAD).astype(BF16),
        "w_v": wv.reshape(n, KV_LORA, N_HEADS * V_HEAD).astype(BF16),
        "w_o": w_o.astype(BF16),
    }


def kernel(x_prompt, x_sample, cache_ckv, cache_kpe, c, c_ctx, norm_w, ada_w, ada_b, hy_w_in, hy_conv_w, hy_conv_b, hy_f_w1, hy_f_b1, hy_f_freq, hy_f_w2, hy_f_b2, hy_f_w3, hy_f_bias, hy_w_out, mla_w_in, mla_q_norm, mla_w_qb, mla_kv_norm, mla_w_kvb, mla_w_o, final_norm):
    x = (x_prompt.reshape(ROWS_P, D_MODEL), x_sample.reshape(ROWS_S, D_MODEL))
    cond = jnp.concatenate([c_ctx[None, :], c, jnp.zeros((N_COND - 1 - DEC_BATCH, D_MODEL), F32)], axis=0)
    mod = _ada_all(cond, ada_w, ada_b)

    nw = norm_w[:, None, :]
    hy_w_in_b = hy_w_in.astype(BF16)
    hy_w_out_b = hy_w_out.astype(BF16)
    hy_conv_b3 = hy_conv_b[:, None, :]
    f_w1 = jnp.pad(hy_f_w1, ((0, 0), (0, FILTER_HIDDEN - FILTER_EMB), (0, 0)))
    f_b1, f_freq, f_b2, f_bias = (a[:, None, :] for a in (hy_f_b1, hy_f_freq, hy_f_b2, hy_f_bias))
    w = _mla_weights(mla_w_in, mla_q_norm, mla_w_qb, mla_kv_norm, mla_w_kvb, mla_w_o)
    kpe_ctx = jnp.pad(cache_kpe, ((0, 0), (0, 0), (0, 0), (QK_NOPE, HEAD_PAD - QK_NOPE - QK_ROPE)))

    new_ckv, new_kpe = [], []
    for layer in range(DEPTH):
        j = layer // 2
        if layer % 2 == 0:
            z, m = _hy_in(x, nw, mod, layer, hy_w_in_b, hy_conv_w, hy_conv_b3)
            t = []
            for L, blk0, nblk in ((DEC_SEQ, ROWS_P // SEQ_BLOCK, ROWS_S // SEQ_BLOCK), (SEQ, 0, ROWS_P // SEQ_BLOCK)):
                filt = _hy_filter(L, j, f_w1, f_b1, f_freq, hy_f_w2, f_b2, hy_f_w3)
                cm, sm = _dft_mats(L)
                t.append(_hy_lconv(L, j, blk0, nblk, z, m, filt, f_bias, cm, sm))
            t = (t[1], t[0])
        else:
            x, q, k, v, sg, ckv, kpe = _mla_proj(t, hy_w_out_b, x, nw, mod, layer, w)
            kc, vc = _mla_ctx(j, cache_ckv, kpe_ctx, w)
            final_g = final_norm[None, :] if layer == DEPTH - 1 else None
            x = (_attn_prompt(q, k, v, sg, w["w_o"], x, mod, layer, final_g),
                 _attn_latent(q, k, v, kc, vc, sg, w["w_o"], x, mod, layer, final_g))
            new_ckv.append(ckv[:ROWS_P].reshape(BATCH, SEQ, KV_LORA))
            new_kpe.append(kpe[:ROWS_P, QK_NOPE:QK_NOPE + QK_ROPE].reshape(BATCH, SEQ, QK_ROPE))

    assert DEPTH % 2 == 0
    y_prompt, y_sample = x
    return (y_prompt.reshape(BATCH, SEQ, D_MODEL), y_sample.reshape(DEC_BATCH, DEC_SEQ, D_MODEL),
            jnp.stack(new_ckv, axis=1), jnp.stack(new_kpe, axis=1))
```

```python
import functools
import math

import numpy as np
import jax
import jax.numpy as jnp
from jax import lax
from jax.experimental import pallas as pl
from jax.experimental.pallas import tpu as pltpu

F32 = jnp.float32
BF16 = jnp.bfloat16

D_MODEL = 1024
BATCH = 16
SEQ = 256
DEPTH = 4
DEC_BATCH = 2
DEC_SEQ = 2048
PAST_LEN = 512
GRID_W = 64
EPS = 1e-6
HY_WIDTH = D_MODEL
FILTER_BANDS = 16
FILTER_EMB = 1 + 2 * FILTER_BANDS
FILTER_HIDDEN = 64
FAST_DECAY_PCT = 0.3
SLOW_DECAY_PCT = 1.5
DECAY_TARGET = 1e-2
N_HEADS = 16
Q_LORA = 384
KV_LORA = 256
QK_NOPE = 64
QK_ROPE = 32
V_HEAD = 64
ROPE_THETA = 10000.0

LANES = 128
MXU_TILE = 256
HEAD_PAD = LANES
ROWS_P = BATCH * SEQ
ROWS_S = DEC_BATCH * DEC_SEQ
ROWS = ROWS_P + ROWS_S
TM = 512
N_TILES = ROWS // TM
TILES_P = ROWS_P // TM
TILES_PER_DEC_SEQ = DEC_SEQ // TM
N_COND = 8
SEQ_BLOCK = 2048
TK = 512
LCONV_TC = 256
VMEM_CAP = 56 * 1024 * 1024


def _cparams(sem, vmem_bytes):
    return pltpu.CompilerParams(dimension_semantics=sem, vmem_limit_bytes=min(int(vmem_bytes), VMEM_CAP))


def _resident(shape, index_map):
    return pl.BlockSpec(shape, index_map, pipeline_mode=pl.Buffered(1))


def _entry(arr, idx, resident=False):
    zeros = (0,) * (arr.ndim - 1)
    return pl.BlockSpec((None,) + arr.shape[1:], lambda *_: (idx,) + zeros,
                        pipeline_mode=pl.Buffered(1) if resident else None)


def _cond_row(i, rows=TM):
    return jnp.where(i < ROWS_P // rows, 0, 1 + (i - ROWS_P // rows) // (DEC_SEQ // rows))


def _silu(x):
    return x * jax.nn.sigmoid(x)


def _rms(x, g):
    return x * lax.rsqrt(jnp.mean(x * x, axis=-1, keepdims=True) + EPS) * g


def _row_pair(x):
    if isinstance(x, tuple):
        return x[0], x[1], 0
    return x, x, ROWS_P


def _row_pair_specs(x, n, rows=TM):
    _, _, base = _row_pair(x)
    n_lo = ROWS_P // rows
    lo = pl.BlockSpec((rows, n), lambda i, *_: (jnp.minimum(i, n_lo - 1), 0))
    hi = pl.BlockSpec((rows, n), lambda i, *_: (jnp.maximum(i - n_lo, 0) + base // rows, 0))
    return lo, hi


def _pick_rows(i, lo_ref, hi_ref, rows=TM):
    return jnp.where(i < ROWS_P // rows, lo_ref[...], hi_ref[...])


def _ada_kernel(cond_ref, w_ref, b_ref, o_ref):
    s = _silu(cond_ref[...]).astype(BF16)
    o_ref[...] = jnp.dot(s, w_ref[...].astype(BF16), preferred_element_type=F32) + b_ref[...]


def _ada_all(cond, ada_w, ada_b):
    tn = 1024
    return pl.pallas_call(
        _ada_kernel,
        name="ada",
        grid=(DEPTH, 3 * D_MODEL // tn),
        in_specs=[
            pl.BlockSpec((N_COND, D_MODEL), lambda l, j: (0, 0)),
            pl.BlockSpec((None, D_MODEL, tn), lambda l, j: (l, 0, j)),
            pl.BlockSpec((None, 1, tn), lambda l, j: (l, 0, j)),
        ],
        out_specs=pl.BlockSpec((None, N_COND, tn), lambda l, j: (l, 0, j)),
        out_shape=jax.ShapeDtypeStruct((DEPTH, N_COND, 3 * D_MODEL), F32),
        compiler_params=_cparams(("arbitrary", "arbitrary"), 32 << 20),
    )(cond, ada_w, ada_b.reshape(DEPTH, 1, 3 * D_MODEL))


def _mod_spec(layer, part):
    return pl.BlockSpec((None, N_COND, D_MODEL), lambda i, *_: (layer, 0, part))


def _modnorm(i, x, nw_ref, sh_ref, sc_ref):
    c = _cond_row(i)
    return _rms(x, nw_ref[...]) * (1.0 + sc_ref[pl.ds(c, 1), :]) + sh_ref[pl.ds(c, 1), :]


HALO = 16
HY_ROWS = TM


def _halo_specs(x):
    _, _, base = _row_pair(x)
    r = HY_ROWS // HALO
    n_lo = ROWS_P // HY_ROWS
    lo_last = ROWS_P // HALO - 1
    hi_last = ROWS_S // HALO - 1
    lo_blk = lambda i: jnp.minimum(i, n_lo - 1)
    hi_blk = lambda i: jnp.maximum(i - n_lo, 0)
    blk = lambda f: pl.BlockSpec((HALO, D_MODEL), lambda i: (f(i), 0))
    return [
        blk(lambda i: jnp.maximum(lo_blk(i) * r - 1, 0)),
        blk(lambda i: jnp.maximum(hi_blk(i) * r - 1, 0) + base // HALO),
        blk(lambda i: jnp.minimum((lo_blk(i) + 1) * r, lo_last)),
        blk(lambda i: jnp.minimum((hi_blk(i) + 1) * r, hi_last) + base // HALO),
    ]


def _hy_in_kernel(xlo_ref, xhi_ref, plo_ref, phi_ref, nlo_ref, nhi_ref, nw_ref, sh_ref, sc_ref, w_ref,
                  cw_ref, cb_ref, z_ref, m_ref, h_scr, *u_scrs):
    i = pl.program_id(0)
    width = z_ref.shape[1]
    c = _cond_row(i, HY_ROWS)
    norm = lambda lo_ref, hi_ref: (_rms(_pick_rows(i, lo_ref, hi_ref, HY_ROWS), nw_ref[...])
                                   * (1.0 + sc_ref[pl.ds(c, 1), :]) + sh_ref[pl.ds(c, 1), :]).astype(BF16)
    h_scr[0:HALO, :] = norm(plo_ref, phi_ref)
    h_scr[HALO:HALO + HY_ROWS, :] = norm(xlo_ref, xhi_ref)
    h_scr[HALO + HY_ROWS:HALO + HY_ROWS + HALO, :] = norm(nlo_ref, nhi_ref)

    seq_mask = jnp.where(i < ROWS_P // HY_ROWS, SEQ - 1, DEC_SEQ - 1)
    row = lax.broadcasted_iota(jnp.int32, (TM, width), 0)
    for t in range(HY_ROWS // TM):
        r0 = t * TM
        pos = (row + (i * HY_ROWS + r0)) & seq_mask
        first = pos == 0
        last = pos == seq_mask
        rows_h = pl.ds(r0, TM + 2 * HALO)

        def conv(g, u_scr):
            cols = slice(g * width, (g + 1) * width)
            u_scr[...] = jnp.dot(h_scr[rows_h, :], w_ref[:, cols], preferred_element_type=F32)
            prev = jnp.where(first, 0.0, u_scr[HALO - 1:HALO - 1 + TM, :])
            nxt = jnp.where(last, 0.0, u_scr[HALO + 1:HALO + 1 + TM, :])
            return (cb_ref[:, cols] + prev * cw_ref[0:1, cols] + u_scr[HALO:HALO + TM, :] * cw_ref[1:2, cols]
                    + nxt * cw_ref[2:3, cols])

        u0, u1, u2 = u_scrs[3 * t:3 * t + 3]
        out_rows = pl.ds(r0, TM)
        z_ref[out_rows, :] = (conv(2, u2) * conv(1, u1)).astype(BF16)
        gate = jnp.dot(h_scr[pl.ds(r0 + HALO, TM), :], w_ref[:, 3 * width:4 * width], preferred_element_type=F32)
        m_ref[out_rows, :] = (conv(0, u0) * _silu(gate)).astype(BF16)


def _hy_in(x, norm_w, mod, layer, w_in, conv_w, conv_b):
    j = layer // 2
    xlo, xhi, _ = _row_pair(x)
    out = pl.BlockSpec((HY_ROWS, HY_WIDTH), lambda i: (i, 0))
    n_tiles = HY_ROWS // TM
    return pl.pallas_call(
        _hy_in_kernel,
        name="hy_in",
        grid=(ROWS // HY_ROWS,),
        in_specs=[
            *_row_pair_specs(x, D_MODEL, HY_ROWS),
            *_halo_specs(x),
            _entry(norm_w, layer),
            _mod_spec(layer, 0),
            _mod_spec(layer, 1),
            _entry(w_in, j, resident=True),
            _entry(conv_w, j),
            _entry(conv_b, j),
        ],
        out_specs=[out, out],
        out_shape=[jax.ShapeDtypeStruct((ROWS, HY_WIDTH), BF16)] * 2,
        scratch_shapes=([pltpu.VMEM((HY_ROWS + 2 * HALO, D_MODEL), BF16)]
                        + [pltpu.VMEM((TM + 2 * HALO, HY_WIDTH), F32)] * (3 * n_tiles)),
        compiler_params=_cparams(("arbitrary",), 56 << 20),
    )(xlo, xhi, xlo, xhi, xlo, xhi, norm_w, mod, mod, w_in, conv_w, conv_b)


@functools.lru_cache(maxsize=None)
def _filter_consts(L):
    t = np.linspace(0.0, 1.0, L)[:, None]
    w = (2.0 * math.pi / L) * np.arange(L)[:, None]
    bands = np.linspace(1e-4, FILTER_BANDS - 1, FILTER_BANDS)[None, :]
    emb = np.concatenate([t, np.cos(bands * w), -np.sin(bands * w)], axis=-1)
    emb = np.pad(emb, ((0, 0), (0, FILTER_HIDDEN - FILTER_EMB)))
    emb = np.concatenate([emb[:L // 2], emb[L // 2:]], axis=1)
    max_decay = math.log(DECAY_TARGET) / FAST_DECAY_PCT
    min_decay = math.log(DECAY_TARGET) / SLOW_DECAY_PCT
    deltas = np.abs(np.linspace(min_decay, max_decay, HY_WIDTH))
    deltas = np.concatenate([deltas, deltas])[None, :]
    return emb.astype(np.float32), t.astype(np.float32), deltas.astype(np.float32)


def _filter_kernel(emb_ref, t_ref, w1_ref, b1_ref, fr_ref, w2_ref, b2_ref, w3_ref, dl_ref, o_ref, h_scr):
    @pl.when(pl.program_id(0) == 0)
    def _():
        hi = lax.Precision.HIGHEST
        fr = fr_ref[...]
        h = jnp.sin(fr * (jnp.dot(emb_ref[...], w1_ref[...], precision=hi, preferred_element_type=F32) + b1_ref[...]))
        h = jnp.sin(fr * (jnp.dot(h, w2_ref[...], precision=hi, preferred_element_type=F32) + b2_ref[...]))
        h_scr[...] = h.astype(BF16)

    w3 = w3_ref[...].astype(BF16)
    zero = jnp.zeros_like(w3)
    hid = h_scr[...]
    h = jnp.concatenate([jnp.dot(hid, jnp.concatenate([w3, zero], axis=0), preferred_element_type=F32),
                         jnp.dot(hid, jnp.concatenate([zero, w3], axis=0), preferred_element_type=F32)], axis=0)
    h = h * jnp.exp(-t_ref[...] * dl_ref[...])
    o_ref[...] = h / jnp.sum(jnp.abs(h), axis=0, keepdims=True)


def _hy_filter(L, hy, w1, b1, freq, w2, b2, w3):
    emb, t, deltas = _filter_consts(L)
    tcf = 512
    two = 2 * FILTER_HIDDEN
    full = lambda shape: pl.BlockSpec(shape, lambda c: (0, 0))
    return pl.pallas_call(
        _filter_kernel,
        name=f"hy_filter{L}",
        grid=(2 * HY_WIDTH // tcf,),
        in_specs=[
            full((L // 2, two)), full((L, 1)), _entry(w1, hy), _entry(b1, hy), _entry(freq, hy),
            _entry(w2, hy), _entry(b2, hy),
            pl.BlockSpec((None, FILTER_HIDDEN, tcf), lambda c: (hy, 0, c)),
            pl.BlockSpec((1, tcf), lambda c: (0, c)),
        ],
        out_specs=pl.BlockSpec((L, tcf), lambda c: (0, c)),
        out_shape=jax.ShapeDtypeStruct((L, 2 * HY_WIDTH), F32),
        scratch_shapes=[pltpu.VMEM((L // 2, two), BF16)],
        compiler_params=_cparams(("arbitrary",), 40 << 20),
    )(jnp.asarray(emb), jnp.asarray(t), w1, b1, freq, w2, b2, w3, jnp.asarray(deltas))


@functools.lru_cache(maxsize=None)
def _dft_consts(L):
    k = np.arange(L, dtype=np.int64)
    ang = (np.outer(k, k) % (2 * L)).astype(np.float64) * (math.pi / L)
    return np.cos(ang).astype(np.float32), np.sin(ang).astype(np.float32)


def _dft_mats(L):
    c, s = _dft_consts(L)
    return jnp.asarray(c).astype(BF16), jnp.asarray(s).astype(BF16)


def _alt_sign(shape):
    return (1 - 2 * (lax.broadcasted_iota(jnp.int32, shape, 0) & 1)).astype(F32)


def _lconv_kernel(z_ref, m_ref, ff_ref, fb_ref, bias_ref, c_ref, s_ref, t_ref, a_scr, b_scr, nq_scr, y_scr,
                  *, L, tk):
    rows, tc = y_scr.shape
    alt = _alt_sign((L, tc))
    bins = [slice(k0, k0 + tk) for k0 in range(0, L, tk)]

    @pl.when(pl.program_id(1) == 0)
    def _():
        inv_n = 1.0 / (2.0 * L)
        bias = bias_ref[...]
        ff = ff_ref[...]
        fb = fb_ref[...]
        f = ff + fb
        nq_scr[...] = jnp.broadcast_to((jnp.sum(f * alt, axis=0, keepdims=True) + bias) * inv_n, nq_scr.shape)
        f_b = f.astype(BF16)
        d_b = (fb - ff).astype(BF16)
        for kt in bins:
            hre = jnp.dot(c_ref[kt, :], f_b, preferred_element_type=F32) + bias
            him = jnp.dot(s_ref[kt, :], d_b, preferred_element_type=F32)
            a = hre * (2.0 * inv_n)
            if kt.start == 0:
                a = jnp.where(lax.broadcasted_iota(jnp.int32, hre.shape, 0) == 0, hre * inv_n, a)
            a_scr[kt, :] = a
            b_scr[kt, :] = him * (2.0 * inv_n)

    for s0 in range(0, rows, L):
        sq = pl.ds(s0, L)
        zb = z_ref[sq, :]
        z_nyq = jnp.sum(zb.astype(F32) * alt, axis=0, keepdims=True)
        y_scr[sq, :] = alt * (z_nyq * nq_scr[0:1, :])
        for kt in bins:
            zr = jnp.dot(c_ref[kt, :], zb, preferred_element_type=F32)
            zi = jnp.dot(s_ref[kt, :], zb, preferred_element_type=F32)
            a = a_scr[kt, :]
            b = b_scr[kt, :]
            yr = (zr * a + zi * b).astype(BF16)
            yw = (zi * a - zr * b).astype(BF16)
            y_scr[sq, :] += (jnp.dot(c_ref[:, kt], yr, preferred_element_type=F32)
                             + jnp.dot(s_ref[:, kt], yw, preferred_element_type=F32))
        t_ref[sq, :] = (y_scr[sq, :] * m_ref[sq, :].astype(F32)).astype(BF16)


def _hy_lconv(L, hy, row_block0, n_row_blocks, z, m, filt, f_bias, c, s):
    tk = min(TK, L)
    long_seq = L == SEQ_BLOCK
    tc = 2 * LCONV_TC if long_seq else LCONV_TC
    nc = HY_WIDTH // tc
    blk = pl.BlockSpec((SEQ_BLOCK, tc), lambda j, r: (r + row_block0, j))
    mat = _resident((L, L), lambda j, r: (0, 0))
    filt_mode = pl.Buffered(1) if long_seq else None
    return pl.pallas_call(
        functools.partial(_lconv_kernel, L=L, tk=tk),
        name=f"hy_lconv{L}",
        grid=(nc, n_row_blocks),
        in_specs=[blk, blk,
                  pl.BlockSpec((L, tc), lambda j, r: (0, j), pipeline_mode=filt_mode),
                  pl.BlockSpec((L, tc), lambda j, r: (0, j + nc), pipeline_mode=filt_mode),
                  pl.BlockSpec((None, 1, tc), lambda j, r: (hy, 0, j)),
                  mat, mat],
        out_specs=pl.BlockSpec((SEQ_BLOCK, tc), lambda j, r: (r, j)),
        out_shape=jax.ShapeDtypeStruct((n_row_blocks * SEQ_BLOCK, HY_WIDTH), BF16),
        scratch_shapes=[pltpu.VMEM((L, tc), F32), pltpu.VMEM((L, tc), F32), pltpu.VMEM((8, tc), F32),
                        pltpu.VMEM((SEQ_BLOCK, tc), F32)],
        compiler_params=_cparams(("arbitrary", "arbitrary"), 56 << 20),
    )(z, m, filt, filt, f_bias, c, s)


@functools.lru_cache(maxsize=None)
def _rope_consts():
    axis_dim = QK_ROPE // 2
    nf = axis_dim // 2
    inv = ROPE_THETA ** (-np.arange(0, axis_dim, 2, dtype=np.float64) / axis_dim)
    t = np.arange(DEC_SEQ)
    ang_r = (t // GRID_W)[:, None] * inv
    ang_c = (t % GRID_W)[:, None] * inv
    cos = np.ones((DEC_SEQ, HEAD_PAD))
    sin_up = np.zeros((DEC_SEQ, HEAD_PAD))
    sin_dn = np.zeros((DEC_SEQ, HEAD_PAD))
    for base, ang in ((QK_NOPE, ang_r), (QK_NOPE + axis_dim, ang_c)):
        cos[:, base:base + nf] = np.cos(ang)
        cos[:, base + nf:base + 2 * nf] = np.cos(ang)
        sin_up[:, base:base + nf] = -np.sin(ang)
        sin_dn[:, base + nf:base + 2 * nf] = np.sin(ang)
    return cos.astype(np.float32), sin_up.astype(np.float32), sin_dn.astype(np.float32)


ROPE_HALF = QK_ROPE // 4
Q_SCALE = math.log2(math.e) / math.sqrt(QK_NOPE + QK_ROPE)


def _mla_proj_kernel(tlo_ref, thi_ref, wout_ref, xlo_ref, xhi_ref, gprev_ref,
                     nw_ref, sh_ref, sc_ref, wa_ref, wg_ref, wpe_ref, qn_ref, kvn_ref, wqb_ref, wk_ref, wv_ref,
                     cos_ref, sup_ref, sdn_ref,
                     xn_ref, q_ref, k_ref, v_ref, sg_ref, ckv_ref, kpe_ref):
    i = pl.program_id(0)
    u = jnp.dot(_pick_rows(i, tlo_ref, thi_ref), wout_ref[...], preferred_element_type=F32)
    x = _pick_rows(i, xlo_ref, xhi_ref) + gprev_ref[pl.ds(_cond_row(i), 1), :] * u
    xn_ref[...] = x
    h = _modnorm(i, x, nw_ref, sh_ref, sc_ref).astype(BF16)
    lora = jnp.dot(h, wa_ref[...], preferred_element_type=F32)
    kpe = jnp.dot(h, wpe_ref[...], preferred_element_type=F32)
    qn = _rms(lora[:, 0:Q_LORA], qn_ref[...]).astype(BF16)
    ckv = _rms(lora[:, Q_LORA:Q_LORA + KV_LORA], kvn_ref[...])
    ckv_ref[...] = ckv
    kpe_ref[...] = kpe
    ckv_b = ckv.astype(BF16)
    q = jnp.dot(qn, wqb_ref[...], preferred_element_type=F32)
    kn = jnp.dot(ckv_b, wk_ref[...], preferred_element_type=F32)

    latent = i >= TILES_P
    cos = jnp.where(latent, cos_ref[...], 1.0)
    sup = jnp.where(latent, sup_ref[...], 0.0)
    sdn = jnp.where(latent, sdn_ref[...], 0.0)

    def rope(u, scale):
        return (u * (cos * scale) + pltpu.roll(u, HEAD_PAD - ROPE_HALF, axis=1) * (sup * scale)
                + pltpu.roll(u, ROPE_HALF, axis=1) * (sdn * scale))

    kpe_r = rope(kpe, 1.0)
    for hd in range(N_HEADS):
        cols = slice(hd * HEAD_PAD, (hd + 1) * HEAD_PAD)
        q_ref[:, cols] = rope(q[:, cols], Q_SCALE).astype(BF16)
        k_ref[:, cols] = (kn[:, cols] + kpe_r).astype(BF16)
    v_ref[...] = jnp.dot(ckv_b, wv_ref[...], preferred_element_type=F32).astype(BF16)
    sg_ref[...] = _silu(jnp.dot(h, wg_ref[...], preferred_element_type=F32))


def _mla_proj(t, w_out, x, norm_w, mod, layer, w):
    cos, sup, sdn = (jnp.asarray(c) for c in _rope_consts())
    j = layer // 2
    hp = N_HEADS * HEAD_PAD
    tlo, thi, _ = _row_pair(t)
    xlo, xhi, _ = _row_pair(x)
    rope_blk = pl.BlockSpec(
        (TM, HEAD_PAD), lambda i: (jnp.where(i >= TILES_P, (i - TILES_P) % TILES_PER_DEC_SEQ, 0), 0))
    tile = lambda n: pl.BlockSpec((TM, n), lambda i: (i, 0))
    return pl.pallas_call(
        _mla_proj_kernel,
        name="mla_proj",
        grid=(N_TILES,),
        in_specs=[
            *_row_pair_specs(t, HY_WIDTH),
            _entry(w_out, (layer - 1) // 2, True),
            *_row_pair_specs(x, D_MODEL),
            _mod_spec(layer - 1, 2),
            _entry(norm_w, layer),
            _mod_spec(layer, 0), _mod_spec(layer, 1),
            _entry(w["w_a"], j, True), _entry(w["w_g"], j, True), _entry(w["w_pe"], j, True),
            _entry(w["q_norm"], j), _entry(w["kv_norm"], j),
            _entry(w["w_qb"], j, True), _entry(w["w_k"], j, True), _entry(w["w_v"], j, True),
            rope_blk, rope_blk, rope_blk,
        ],
        out_specs=[tile(D_MODEL), tile(hp), tile(hp), tile(N_HEADS * V_HEAD),
                   tile(N_HEADS * V_HEAD), tile(KV_LORA), tile(HEAD_PAD)],
        out_shape=[
            jax.ShapeDtypeStruct((ROWS, D_MODEL), F32),
            jax.ShapeDtypeStruct((ROWS, hp), BF16),
            jax.ShapeDtypeStruct((ROWS, hp), BF16),
            jax.ShapeDtypeStruct((ROWS, N_HEADS * V_HEAD), BF16),
            jax.ShapeDtypeStruct((ROWS, N_HEADS * V_HEAD), F32),
            jax.ShapeDtypeStruct((ROWS, KV_LORA), F32),
            jax.ShapeDtypeStruct((ROWS, HEAD_PAD), F32),
        ],
        compiler_params=_cparams(("arbitrary",), 56 << 20),
    )(tlo, thi, w_out, xlo, xhi, mod, norm_w, mod, mod, w["w_a"], w["w_g"], w["w_pe"], w["q_norm"],
      w["kv_norm"], w["w_qb"], w["w_k"], w["w_v"], cos, sup, sdn)


def _mla_ctx_kernel(ckv_ref, kpe_ref, wk_ref, wv_ref, k_ref, v_ref):
    ckv_b = ckv_ref[...].astype(BF16)
    kn = jnp.dot(ckv_b, wk_ref[...], preferred_element_type=F32)
    v_ref[...] = jnp.dot(ckv_b, wv_ref[...], preferred_element_type=F32).astype(BF16)
    kpe = kpe_ref[...]
    for hd in range(N_HEADS):
        cols = slice(hd * HEAD_PAD, (hd + 1) * HEAD_PAD)
        k_ref[:, cols] = (kn[:, cols] + kpe).astype(BF16)


def _mla_ctx(j, ckv_ctx, kpe_ctx, w):
    hp = N_HEADS * HEAD_PAD
    rows = DEC_BATCH * PAST_LEN
    tile = lambda n: pl.BlockSpec((PAST_LEN, n), lambda i: (i, 0))
    cache = lambda n: pl.BlockSpec((None, None, PAST_LEN, n), lambda i: (i, j, 0, 0))
    return pl.pallas_call(
        _mla_ctx_kernel,
        name="mla_ctx",
        grid=(DEC_BATCH,),
        in_specs=[cache(KV_LORA), cache(HEAD_PAD), _entry(w["w_k"], j, True), _entry(w["w_v"], j, True)],
        out_specs=[tile(hp), tile(N_HEADS * V_HEAD)],
        out_shape=[jax.ShapeDtypeStruct((rows, hp), BF16),
                   jax.ShapeDtypeStruct((rows, N_HEADS * V_HEAD), BF16)],
        compiler_params=_cparams(("arbitrary",), 32 << 20),
    )(ckv_ctx, kpe_ctx, w["w_k"], w["w_v"])


NT_DIMS = (((1,), (1,)), ((), ()))
ATTN_PAIRS = 4
ATTN_TQ = 256
PROMPT_SEQS = 2


def _attn_kernel(*refs, n_pairs, n_groups, has_ctx, final, n_seq=1):
    refs = list(refs)
    q_ref, k_ref, v_ref = refs[:3]
    del refs[:3]
    if has_ctx:
        kc_ref, vc_ref = refs[:2]
        del refs[:2]
    sg_ref, wo_ref, x_ref, g_ref = refs[:4]
    del refs[:4]
    if final:
        fg_ref = refs.pop(0)
    out_ref, o_scr = refs[:2]
    tq = q_ref.shape[0] // n_seq
    lk = k_ref.shape[0] // n_seq
    low_half = lax.broadcasted_iota(jnp.int32, (tq, LANES), 1) < V_HEAD
    for sq, p in [(sq, p) for sq in range(n_seq) for p in range(n_pairs)]:
        qrows = pl.ds(sq * tq, tq)
        krows = pl.ds(sq * lk, lk)
        vcols = slice(p * LANES, (p + 1) * LANES)
        vp = v_ref[krows, vcols]
        outs = []
        for hh in range(2):
            cols = slice((2 * p + hh) * HEAD_PAD, (2 * p + hh + 1) * HEAD_PAD)
            q = q_ref[qrows, cols]
            s = lax.dot_general(q, k_ref[krows, cols], NT_DIMS, preferred_element_type=F32)
            mx = jnp.max(s, axis=-1, keepdims=True)
            if has_ctx:
                sc = lax.dot_general(q, kc_ref[:, cols], NT_DIMS, preferred_element_type=F32)
                mx = jnp.maximum(mx, jnp.max(sc, axis=-1, keepdims=True))
            e = jnp.exp2(s - mx)
            den = jnp.sum(e, axis=-1, keepdims=True)
            pv = jnp.dot(e.astype(BF16), vp, preferred_element_type=F32)
            if has_ctx:
                ec = jnp.exp2(sc - mx)
                den = den + jnp.sum(ec, axis=-1, keepdims=True)
                pv = pv + jnp.dot(ec.astype(BF16), vc_ref[:, vcols], preferred_element_type=F32)
            outs.append(pv / den)
        o = jnp.where(low_half, outs[0], outs[1])
        o_scr[qrows, vcols] = (o * sg_ref[qrows, vcols]).astype(BF16)

    u = jnp.dot(o_scr[...], wo_ref[...], preferred_element_type=F32)
    cond = 1 + pl.program_id(0) if has_ctx else 0

    def finish(u_all):
        xn = x_ref[...] + g_ref[pl.ds(cond, 1), :] * u_all
        out_ref[...] = _rms(xn, fg_ref[...]) if final else xn

    if n_groups == 1:
        finish(u)
    else:
        acc_scr = refs[2]
        grp = pl.program_id(2)

        @pl.when(grp == 0)
        def _():
            acc_scr[...] = u

        @pl.when((grp > 0) & (grp < n_groups - 1))
        def _():
            acc_scr[...] += u

        @pl.when(grp == n_groups - 1)
        def _():
            finish(acc_scr[...] + u)


def _attn_prompt(q, k, v, sg, w_o, x, mod, layer, final_g):
    hp = N_HEADS * HEAD_PAD
    nv = N_HEADS * V_HEAD
    xlo, _, _ = _row_pair(x)
    rows = PROMPT_SEQS * SEQ
    seq = lambda n: pl.BlockSpec((rows, n), lambda b: (b, 0))
    final = final_g is not None
    return pl.pallas_call(
        functools.partial(_attn_kernel, n_pairs=N_HEADS // 2, n_groups=1, has_ctx=False, final=final,
                          n_seq=PROMPT_SEQS),
        name="attn_prompt",
        grid=(BATCH // PROMPT_SEQS,),
        in_specs=[seq(hp), seq(hp), seq(nv), seq(nv), _entry(w_o, layer // 2, True), seq(D_MODEL),
                  _mod_spec(layer, 2)] + ([pl.BlockSpec((1, D_MODEL), lambda b: (0, 0))] if final else []),
        out_specs=seq(D_MODEL),
        out_shape=jax.ShapeDtypeStruct((ROWS_P, D_MODEL), F32),
        scratch_shapes=[pltpu.VMEM((rows, nv), BF16)],
        compiler_params=_cparams(("arbitrary",), 40 << 20),
    )(q, k, v, sg, w_o, xlo, mod, *([final_g] if final else []))


def _attn_latent(q, k, v, kc, vc, sg, w_o, x, mod, layer, final_g):
    tq = ATTN_TQ
    npair = ATTN_PAIRS
    ngrp = N_HEADS // 2 // npair
    tiles = DEC_SEQ // tq
    q0 = ROWS_P // tq
    s0 = ROWS_P // DEC_SEQ
    wide = npair * 2 * HEAD_PAD
    narrow = npair * LANES
    _, xhi, base = _row_pair(x)
    x0 = base // tq
    final = final_g is not None
    qrow = lambda b, t, p: (q0 + b * tiles + t, p)
    return pl.pallas_call(
        functools.partial(_attn_kernel, n_pairs=npair, n_groups=ngrp, has_ctx=True, final=final),
        name="attn_latent",
        grid=(DEC_BATCH, tiles, ngrp),
        in_specs=[pl.BlockSpec((tq, wide), qrow),
                  pl.BlockSpec((DEC_SEQ, wide), lambda b, t, p: (s0 + b, p)),
                  pl.BlockSpec((DEC_SEQ, narrow), lambda b, t, p: (s0 + b, p)),
                  pl.BlockSpec((PAST_LEN, wide), lambda b, t, p: (b, p)),
                  pl.BlockSpec((PAST_LEN, narrow), lambda b, t, p: (b, p)),
                  pl.BlockSpec((tq, narrow), qrow),
                  pl.BlockSpec((None, narrow, D_MODEL), lambda b, t, p: (layer // 2, p, 0)),
                  pl.BlockSpec((tq, D_MODEL), lambda b, t, p: (x0 + b * tiles + t, 0)),
                  _mod_spec(layer, 2)] + ([pl.BlockSpec((1, D_MODEL), lambda b, t, p: (0, 0))] if final else []),
        out_specs=pl.BlockSpec((tq, D_MODEL), lambda b, t, p: (b * tiles + t, 0)),
        out_shape=jax.ShapeDtypeStruct((ROWS_S, D_MODEL), F32),
        scratch_shapes=[pltpu.VMEM((tq, narrow), BF16), pltpu.VMEM((tq, D_MODEL), F32)],
        compiler_params=_cparams(("arbitrary", "arbitrary", "arbitrary"), 48 << 20),
    )(q, k, v, kc, vc, sg, w_o, xhi, mod, *([final_g] if final else []))


def _mla_weights(w_in, q_norm, w_qb, kv_norm, w_kvb, w_o):
    n = w_in.shape[0]
    o_pe = Q_LORA + KV_LORA
    o_gate = o_pe + QK_ROPE
    w_pe = jnp.pad(w_in[..., o_pe:o_gate], ((0, 0), (0, 0), (QK_NOPE, HEAD_PAD - QK_NOPE - QK_ROPE)))
    qb = w_qb.reshape(n, Q_LORA, N_HEADS, QK_NOPE + QK_ROPE)
    qb = jnp.pad(qb, ((0, 0), (0, 0), (0, 0), (0, HEAD_PAD - QK_NOPE - QK_ROPE)))
    kvb = w_kvb.reshape(n, KV_LORA, N_HEADS, QK_NOPE + V_HEAD)
    wk = jnp.pad(kvb[..., :QK_NOPE], ((0, 0), (0, 0), (0, 0), (0, HEAD_PAD - QK_NOPE)))
    wv = kvb[..., QK_NOPE:]
    return {
        "w_a": w_in[..., :o_pe].astype(BF16),
        "w_g": w_in[..., o_gate:].astype(BF16),
        "w_pe": w_pe.astype(BF16),
        "q_norm": q_norm[:, None, :],
        "kv_norm": kv_norm[:, None, :],
        "w_qb": qb.reshape(n, Q_LORA, N_HEADS * HEAD_PAD).astype(BF16),
        "w_k": wk.reshape(n, KV_LORA, N_HEADS * HEAD_PAD).astype(BF16),
        "w_v": wv.reshape(n, KV_LORA, N_HEADS * V_HEAD).astype(BF16),
        "w_o": w_o.astype(BF16),
    }


def kernel(x_prompt, x_sample, cache_ckv, cache_kpe, c, c_ctx, norm_w, ada_w, ada_b, hy_w_in, hy_conv_w, hy_conv_b, hy_f_w1, hy_f_b1, hy_f_freq, hy_f_w2, hy_f_b2, hy_f_w3, hy_f_bias, hy_w_out, mla_w_in, mla_q_norm, mla_w_qb, mla_kv_norm, mla_w_kvb, mla_w_o, final_norm):
    x = (x_prompt.reshape(ROWS_P, D_MODEL), x_sample.reshape(ROWS_S, D_MODEL))
    cond = jnp.concatenate([c_ctx[None, :], c, jnp.zeros((N_COND - 1 - DEC_BATCH, D_MODEL), F32)], axis=0)
    mod = _ada_all(cond, ada_w, ada_b)

    nw = norm_w[:, None, :]
    hy_w_in_b = hy_w_in.astype(BF16)
    hy_w_out_b = hy_w_out.astype(BF16)
    hy_conv_b3 = hy_conv_b[:, None, :]
    fh = FILTER_HIDDEN
    twice = lambda w: (jnp.pad(w, ((0, 0), (0, 2 * fh - w.shape[1]), (0, fh)))
                       + jnp.pad(w, ((0, 0), (fh, fh - w.shape[1]), (fh, 0))))
    f_w1, f_w2 = twice(hy_f_w1), twice(hy_f_w2)
    f_b1, f_freq, f_b2 = (jnp.tile(a, (1, 2))[:, None, :] for a in (hy_f_b1, hy_f_freq, hy_f_b2))
    f_bias = hy_f_bias[:, None, :]
    w = _mla_weights(mla_w_in, mla_q_norm, mla_w_qb, mla_kv_norm, mla_w_kvb, mla_w_o)
    kpe_ctx = jnp.pad(cache_kpe, ((0, 0), (0, 0), (0, 0), (QK_NOPE, HEAD_PAD - QK_NOPE - QK_ROPE)))

    new_ckv, new_kpe = [], []
    for layer in range(DEPTH):
        j = layer // 2
        if layer % 2 == 0:
            z, m = _hy_in(x, nw, mod, layer, hy_w_in_b, hy_conv_w, hy_conv_b3)
            t = []
            for L, blk0, nblk in ((DEC_SEQ, ROWS_P // SEQ_BLOCK, ROWS_S // SEQ_BLOCK), (SEQ, 0, ROWS_P // SEQ_BLOCK)):
                filt = _hy_filter(L, j, f_w1, f_b1, f_freq, f_w2, f_b2, hy_f_w3)
                cm, sm = _dft_mats(L)
                t.append(_hy_lconv(L, j, blk0, nblk, z, m, filt, f_bias, cm, sm))
            t = (t[1], t[0])
        else:
            x, q, k, v, sg, ckv, kpe = _mla_proj(t, hy_w_out_b, x, nw, mod, layer, w)
            kc, vc = _mla_ctx(j, cache_ckv, kpe_ctx, w)
            final_g = final_norm[None, :] if layer == DEPTH - 1 else None
            x = (_attn_prompt(q, k, v, sg, w["w_o"], x, mod, layer, final_g),
                 _attn_latent(q, k, v, kc, vc, sg, w["w_o"], x, mod, layer, final_g))
            new_ckv.append(ckv[:ROWS_P].reshape(BATCH, SEQ, KV_LORA))
            new_kpe.append(kpe[:ROWS_P, QK_NOPE:QK_NOPE + QK_ROPE].reshape(BATCH, SEQ, QK_ROPE))

    assert DEPTH % 2 == 0
    y_prompt, y_sample = x
    return (y_prompt.reshape(BATCH, SEQ, D_MODEL), y_sample.reshape(DEC_BATCH, DEC_SEQ, D_MODEL),
            jnp.stack(new_ckv, axis=1), jnp.stack(new_kpe, axis=1))
```

```python
import functools
import math

import numpy as np
import jax
import jax.numpy as jnp
from jax import lax
from jax.experimental import pallas as pl
from jax.experimental.pallas import tpu as pltpu

F32 = jnp.float32
BF16 = jnp.bfloat16

D_MODEL = 1024
BATCH = 16
SEQ = 256
DEPTH = 4
DEC_BATCH = 2
DEC_SEQ = 2048
PAST_LEN = 512
GRID_W = 64
EPS = 1e-6
HY_WIDTH = D_MODEL
FILTER_BANDS = 16
FILTER_EMB = 1 + 2 * FILTER_BANDS
FILTER_HIDDEN = 64
FAST_DECAY_PCT = 0.3
SLOW_DECAY_PCT = 1.5
DECAY_TARGET = 1e-2
N_HEADS = 16
Q_LORA = 384
KV_LORA = 256
QK_NOPE = 64
QK_ROPE = 32
V_HEAD = 64
ROPE_THETA = 10000.0

LANES = 128
MXU_TILE = 256
HEAD_PAD = LANES
ROWS_P = BATCH * SEQ
ROWS_S = DEC_BATCH * DEC_SEQ
ROWS = ROWS_P + ROWS_S
TM = 512
N_TILES = ROWS // TM
TILES_P = ROWS_P // TM
TILES_PER_DEC_SEQ = DEC_SEQ // TM
N_COND = 8
SEQ_BLOCK = 2048
TK = 512
LCONV_TC = 256
VMEM_CAP = 56 * 1024 * 1024


def _cparams(sem, vmem_bytes):
    return pltpu.CompilerParams(dimension_semantics=sem, vmem_limit_bytes=min(int(vmem_bytes), VMEM_CAP))


def _resident(shape, index_map):
    return pl.BlockSpec(shape, index_map, pipeline_mode=pl.Buffered(1))


def _entry(arr, idx, resident=False):
    zeros = (0,) * (arr.ndim - 1)
    return pl.BlockSpec((None,) + arr.shape[1:], lambda *_: (idx,) + zeros,
                        pipeline_mode=pl.Buffered(1) if resident else None)


def _cond_row(i, rows=TM):
    return jnp.where(i < ROWS_P // rows, 0, 1 + (i - ROWS_P // rows) // (DEC_SEQ // rows))


def _silu(x):
    return x * jax.nn.sigmoid(x)


def _rms(x, g):
    return x * lax.rsqrt(jnp.mean(x * x, axis=-1, keepdims=True) + EPS) * g


def _row_pair(x):
    if isinstance(x, tuple):
        return x[0], x[1], 0
    return x, x, ROWS_P


def _row_pair_specs(x, n, rows=TM):
    _, _, base = _row_pair(x)
    n_lo = ROWS_P // rows
    lo = pl.BlockSpec((rows, n), lambda i, *_: (jnp.minimum(i, n_lo - 1), 0))
    hi = pl.BlockSpec((rows, n), lambda i, *_: (jnp.maximum(i - n_lo, 0) + base // rows, 0))
    return lo, hi


def _pick_rows(i, lo_ref, hi_ref, rows=TM):
    return jnp.where(i < ROWS_P // rows, lo_ref[...], hi_ref[...])


def _ada_kernel(cond_ref, w_ref, b_ref, o_ref):
    s = _silu(cond_ref[...]).astype(BF16)
    o_ref[...] = jnp.dot(s, w_ref[...].astype(BF16), preferred_element_type=F32) + b_ref[...]


def _ada_all(cond, ada_w, ada_b):
    tn = 1024
    return pl.pallas_call(
        _ada_kernel,
        name="ada",
        grid=(DEPTH, 3 * D_MODEL // tn),
        in_specs=[
            pl.BlockSpec((N_COND, D_MODEL), lambda l, j: (0, 0)),
            pl.BlockSpec((None, D_MODEL, tn), lambda l, j: (l, 0, j)),
            pl.BlockSpec((None, 1, tn), lambda l, j: (l, 0, j)),
        ],
        out_specs=pl.BlockSpec((None, N_COND, tn), lambda l, j: (l, 0, j)),
        out_shape=jax.ShapeDtypeStruct((DEPTH, N_COND, 3 * D_MODEL), F32),
        compiler_params=_cparams(("arbitrary", "arbitrary"), 32 << 20),
    )(cond, ada_w, ada_b.reshape(DEPTH, 1, 3 * D_MODEL))


def _mod_spec(layer, part):
    return pl.BlockSpec((None, N_COND, D_MODEL), lambda i, *_: (layer, 0, part))


def _modnorm(i, x, nw_ref, sh_ref, sc_ref):
    c = _cond_row(i)
    return _rms(x, nw_ref[...]) * (1.0 + sc_ref[pl.ds(c, 1), :]) + sh_ref[pl.ds(c, 1), :]


HALO = 16
HY_ROWS = TM


def _halo_specs(x):
    _, _, base = _row_pair(x)
    r = HY_ROWS // HALO
    n_lo = ROWS_P // HY_ROWS
    lo_last = ROWS_P // HALO - 1
    hi_last = ROWS_S // HALO - 1
    lo_blk = lambda i: jnp.minimum(i, n_lo - 1)
    hi_blk = lambda i: jnp.maximum(i - n_lo, 0)
    blk = lambda f: pl.BlockSpec((HALO, D_MODEL), lambda i: (f(i), 0))
    return [
        blk(lambda i: jnp.maximum(lo_blk(i) * r - 1, 0)),
        blk(lambda i: jnp.maximum(hi_blk(i) * r - 1, 0) + base // HALO),
        blk(lambda i: jnp.minimum((lo_blk(i) + 1) * r, lo_last)),
        blk(lambda i: jnp.minimum((hi_blk(i) + 1) * r, hi_last) + base // HALO),
    ]


def _hy_in_kernel(xlo_ref, xhi_ref, plo_ref, phi_ref, nlo_ref, nhi_ref, nw_ref, sh_ref, sc_ref, w_ref,
                  cw_ref, cb_ref, z_ref, m_ref, h_scr, *u_scrs):
    i = pl.program_id(0)
    width = z_ref.shape[1]
    c = _cond_row(i, HY_ROWS)
    norm = lambda lo_ref, hi_ref: (_rms(_pick_rows(i, lo_ref, hi_ref, HY_ROWS), nw_ref[...])
                                   * (1.0 + sc_ref[pl.ds(c, 1), :]) + sh_ref[pl.ds(c, 1), :]).astype(BF16)
    h_scr[0:HALO, :] = norm(plo_ref, phi_ref)
    h_scr[HALO:HALO + HY_ROWS, :] = norm(xlo_ref, xhi_ref)
    h_scr[HALO + HY_ROWS:HALO + HY_ROWS + HALO, :] = norm(nlo_ref, nhi_ref)

    seq_mask = jnp.where(i < ROWS_P // HY_ROWS, SEQ - 1, DEC_SEQ - 1)
    row = lax.broadcasted_iota(jnp.int32, (TM, width), 0)
    for t in range(HY_ROWS // TM):
        r0 = t * TM
        pos = (row + (i * HY_ROWS + r0)) & seq_mask
        first = pos == 0
        last = pos == seq_mask
        rows_h = pl.ds(r0, TM + 2 * HALO)

        def conv(g, u_scr):
            cols = slice(g * width, (g + 1) * width)
            u_scr[...] = jnp.dot(h_scr[rows_h, :], w_ref[:, cols], preferred_element_type=F32)
            prev = jnp.where(first, 0.0, u_scr[HALO - 1:HALO - 1 + TM, :])
            nxt = jnp.where(last, 0.0, u_scr[HALO + 1:HALO + 1 + TM, :])
            return (cb_ref[:, cols] + prev * cw_ref[0:1, cols] + u_scr[HALO:HALO + TM, :] * cw_ref[1:2, cols]
                    + nxt * cw_ref[2:3, cols])

        u0, u1, u2 = u_scrs[3 * t:3 * t + 3]
        out_rows = pl.ds(r0, TM)
        x0 = conv(0, u0)
        z_ref[out_rows, :] = (conv(2, u2) * conv(1, u1)).astype(BF16)
        gate = jnp.dot(h_scr[pl.ds(r0 + HALO, TM), :], w_ref[:, 3 * width:4 * width], preferred_element_type=F32)
        m_ref[out_rows, :] = (x0 * _silu(gate)).astype(BF16)


def _hy_in(x, norm_w, mod, layer, w_in, conv_w, conv_b):
    j = layer // 2
    xlo, xhi, _ = _row_pair(x)
    out = pl.BlockSpec((HY_ROWS, HY_WIDTH), lambda i: (i, 0))
    n_tiles = HY_ROWS // TM
    return pl.pallas_call(
        _hy_in_kernel,
        name="hy_in",
        grid=(ROWS // HY_ROWS,),
        in_specs=[
            *_row_pair_specs(x, D_MODEL, HY_ROWS),
            *_halo_specs(x),
            _entry(norm_w, layer),
            _mod_spec(layer, 0),
            _mod_spec(layer, 1),
            _entry(w_in, j, resident=True),
            _entry(conv_w, j),
            _entry(conv_b, j),
        ],
        out_specs=[out, out],
        out_shape=[jax.ShapeDtypeStruct((ROWS, HY_WIDTH), BF16)] * 2,
        scratch_shapes=([pltpu.VMEM((HY_ROWS + 2 * HALO, D_MODEL), BF16)]
                        + [pltpu.VMEM((TM + 2 * HALO, HY_WIDTH), F32)] * (3 * n_tiles)),
        compiler_params=_cparams(("arbitrary",), 56 << 20),
    )(xlo, xhi, xlo, xhi, xlo, xhi, norm_w, mod, mod, w_in, conv_w, conv_b)


@functools.lru_cache(maxsize=None)
def _filter_consts(L):
    t = np.linspace(0.0, 1.0, L)[:, None]
    w = (2.0 * math.pi / L) * np.arange(L)[:, None]
    bands = np.linspace(1e-4, FILTER_BANDS - 1, FILTER_BANDS)[None, :]
    emb = np.concatenate([t, np.cos(bands * w), -np.sin(bands * w)], axis=-1)
    emb = np.pad(emb, ((0, 0), (0, FILTER_HIDDEN - FILTER_EMB)))
    emb = np.concatenate([emb[:L // 2], emb[L // 2:]], axis=1)
    max_decay = math.log(DECAY_TARGET) / FAST_DECAY_PCT
    min_decay = math.log(DECAY_TARGET) / SLOW_DECAY_PCT
    deltas = np.abs(np.linspace(min_decay, max_decay, HY_WIDTH))
    deltas = np.concatenate([deltas, deltas])[None, :]
    return emb.astype(np.float32), t.astype(np.float32), deltas.astype(np.float32)


def _filter_kernel(emb_ref, t_ref, w1_ref, b1_ref, fr_ref, w2_ref, b2_ref, w3_ref, dl_ref, o_ref, h_scr):
    @pl.when(pl.program_id(0) == 0)
    def _():
        hi = lax.Precision.HIGHEST
        fr = fr_ref[...]
        h = jnp.sin(fr * (jnp.dot(emb_ref[...], w1_ref[...], precision=hi, preferred_element_type=F32) + b1_ref[...]))
        h = jnp.sin(fr * (jnp.dot(h, w2_ref[...], precision=hi, preferred_element_type=F32) + b2_ref[...]))
        h_scr[...] = h.astype(BF16)

    w3 = w3_ref[...].astype(BF16)
    zero = jnp.zeros_like(w3)
    hid = h_scr[...]
    h = jnp.concatenate([jnp.dot(hid, jnp.concatenate([w3, zero], axis=0), preferred_element_type=F32),
                         jnp.dot(hid, jnp.concatenate([zero, w3], axis=0), preferred_element_type=F32)], axis=0)
    h = h * jnp.exp(-t_ref[...] * dl_ref[...])
    o_ref[...] = h / jnp.sum(jnp.abs(h), axis=0, keepdims=True)


def _hy_filter(L, hy, w1, b1, freq, w2, b2, w3):
    emb, t, deltas = _filter_consts(L)
    tcf = 512
    two = 2 * FILTER_HIDDEN
    full = lambda shape: pl.BlockSpec(shape, lambda c: (0, 0))
    return pl.pallas_call(
        _filter_kernel,
        name=f"hy_filter{L}",
        grid=(2 * HY_WIDTH // tcf,),
        in_specs=[
            full((L // 2, two)), full((L, 1)), _entry(w1, hy), _entry(b1, hy), _entry(freq, hy),
            _entry(w2, hy), _entry(b2, hy),
            pl.BlockSpec((None, FILTER_HIDDEN, tcf), lambda c: (hy, 0, c)),
            pl.BlockSpec((1, tcf), lambda c: (0, c)),
        ],
        out_specs=pl.BlockSpec((L, tcf), lambda c: (0, c)),
        out_shape=jax.ShapeDtypeStruct((L, 2 * HY_WIDTH), F32),
        scratch_shapes=[pltpu.VMEM((L // 2, two), BF16)],
        compiler_params=_cparams(("arbitrary",), 40 << 20),
    )(jnp.asarray(emb), jnp.asarray(t), w1, b1, freq, w2, b2, w3, jnp.asarray(deltas))


@functools.lru_cache(maxsize=None)
def _dft_consts(L):
    k = np.arange(L, dtype=np.int64)
    ang = (np.outer(k, k) % (2 * L)).astype(np.float64) * (math.pi / L)
    return np.cos(ang).astype(np.float32), np.sin(ang).astype(np.float32)


def _dft_mats(L):
    c, s = _dft_consts(L)
    return jnp.asarray(c).astype(BF16), jnp.asarray(s).astype(BF16)


def _alt_sign(shape):
    return (1 - 2 * (lax.broadcasted_iota(jnp.int32, shape, 0) & 1)).astype(F32)


def _lconv_kernel(z_ref, m_ref, ff_ref, fb_ref, bias_ref, c_ref, s_ref, t_ref, a_scr, b_scr, nq_scr, y_scr,
                  *, L, tk):
    rows, tc = y_scr.shape
    alt = _alt_sign((L, tc))
    bins = [slice(k0, k0 + tk) for k0 in range(0, L, tk)]

    @pl.when(pl.program_id(1) == 0)
    def _():
        inv_n = 1.0 / (2.0 * L)
        bias = bias_ref[...]
        ff = ff_ref[...]
        fb = fb_ref[...]
        f = ff + fb
        nq_scr[...] = jnp.broadcast_to((jnp.sum(f * alt, axis=0, keepdims=True) + bias) * inv_n, nq_scr.shape)
        f_b = f.astype(BF16)
        d_b = (fb - ff).astype(BF16)
        for kt in bins:
            hre = jnp.dot(c_ref[kt, :], f_b, preferred_element_type=F32) + bias
            him = jnp.dot(s_ref[kt, :], d_b, preferred_element_type=F32)
            a = hre * (2.0 * inv_n)
            if kt.start == 0:
                a = jnp.where(lax.broadcasted_iota(jnp.int32, hre.shape, 0) == 0, hre * inv_n, a)
            a_scr[kt, :] = a
            b_scr[kt, :] = him * (2.0 * inv_n)

    for s0 in range(0, rows, L):
        sq = pl.ds(s0, L)
        zb = z_ref[sq, :]
        z_nyq = jnp.sum(zb.astype(F32) * alt, axis=0, keepdims=True)
        y_scr[sq, :] = alt * (z_nyq * nq_scr[0:1, :])
        for kt in bins:
            zr = jnp.dot(c_ref[kt, :], zb, preferred_element_type=F32)
            zi = jnp.dot(s_ref[kt, :], zb, preferred_element_type=F32)
            a = a_scr[kt, :]
            b = b_scr[kt, :]
            yr = (zr * a + zi * b).astype(BF16)
            yw = (zi * a - zr * b).astype(BF16)
            y_scr[sq, :] += (jnp.dot(c_ref[:, kt], yr, preferred_element_type=F32)
                             + jnp.dot(s_ref[:, kt], yw, preferred_element_type=F32))
        t_ref[sq, :] = (y_scr[sq, :] * m_ref[sq, :].astype(F32)).astype(BF16)


def _hy_lconv(L, hy, row_block0, n_row_blocks, z, m, filt, f_bias, c, s):
    tk = min(TK, L)
    long_seq = L == SEQ_BLOCK
    tc = 2 * LCONV_TC if long_seq else LCONV_TC
    nc = HY_WIDTH // tc
    blk = pl.BlockSpec((SEQ_BLOCK, tc), lambda j, r: (r + row_block0, j))
    mat = _resident((L, L), lambda j, r: (0, 0))
    filt_mode = pl.Buffered(1) if long_seq else None
    return pl.pallas_call(
        functools.partial(_lconv_kernel, L=L, tk=tk),
        name=f"hy_lconv{L}",
        grid=(nc, n_row_blocks),
        in_specs=[blk, blk,
                  pl.BlockSpec((L, tc), lambda j, r: (0, j), pipeline_mode=filt_mode),
                  pl.BlockSpec((L, tc), lambda j, r: (0, j + nc), pipeline_mode=filt_mode),
                  pl.BlockSpec((None, 1, tc), lambda j, r: (hy, 0, j)),
                  mat, mat],
        out_specs=pl.BlockSpec((SEQ_BLOCK, tc), lambda j, r: (r, j)),
        out_shape=jax.ShapeDtypeStruct((n_row_blocks * SEQ_BLOCK, HY_WIDTH), BF16),
        scratch_shapes=[pltpu.VMEM((L, tc), F32), pltpu.VMEM((L, tc), F32), pltpu.VMEM((8, tc), F32),
                        pltpu.VMEM((SEQ_BLOCK, tc), F32)],
        compiler_params=_cparams(("arbitrary", "arbitrary"), 56 << 20),
    )(z, m, filt, filt, f_bias, c, s)


@functools.lru_cache(maxsize=None)
def _rope_consts():
    axis_dim = QK_ROPE // 2
    nf = axis_dim // 2
    inv = ROPE_THETA ** (-np.arange(0, axis_dim, 2, dtype=np.float64) / axis_dim)
    t = np.arange(DEC_SEQ)
    ang_r = (t // GRID_W)[:, None] * inv
    ang_c = (t % GRID_W)[:, None] * inv
    cos = np.ones((DEC_SEQ, HEAD_PAD))
    sin_up = np.zeros((DEC_SEQ, HEAD_PAD))
    sin_dn = np.zeros((DEC_SEQ, HEAD_PAD))
    for base, ang in ((QK_NOPE, ang_r), (QK_NOPE + axis_dim, ang_c)):
        cos[:, base:base + nf] = np.cos(ang)
        cos[:, base + nf:base + 2 * nf] = np.cos(ang)
        sin_up[:, base:base + nf] = -np.sin(ang)
        sin_dn[:, base + nf:base + 2 * nf] = np.sin(ang)
    return cos.astype(np.float32), sin_up.astype(np.float32), sin_dn.astype(np.float32)


ROPE_HALF = QK_ROPE // 4
Q_SCALE = math.log2(math.e) / math.sqrt(QK_NOPE + QK_ROPE)


def _mla_proj_kernel(tlo_ref, thi_ref, wout_ref, xlo_ref, xhi_ref, gprev_ref,
                     nw_ref, sh_ref, sc_ref, wa_ref, wg_ref, wpe_ref, qn_ref, kvn_ref, wqb_ref, wk_ref, wv_ref,
                     cos_ref, sup_ref, sdn_ref,
                     xn_ref, q_ref, k_ref, v_ref, sg_ref, ckv_ref, kpe_ref):
    i = pl.program_id(0)
    u = jnp.dot(_pick_rows(i, tlo_ref, thi_ref), wout_ref[...], preferred_element_type=F32)
    x = _pick_rows(i, xlo_ref, xhi_ref) + gprev_ref[pl.ds(_cond_row(i), 1), :] * u
    xn_ref[...] = x
    h = _modnorm(i, x, nw_ref, sh_ref, sc_ref).astype(BF16)
    lora = jnp.dot(h, wa_ref[...], preferred_element_type=F32)
    kpe = jnp.dot(h, wpe_ref[...], preferred_element_type=F32)
    qn = _rms(lora[:, 0:Q_LORA], qn_ref[...]).astype(BF16)
    ckv = _rms(lora[:, Q_LORA:Q_LORA + KV_LORA], kvn_ref[...])
    ckv_ref[...] = ckv
    kpe_ref[...] = kpe
    ckv_b = ckv.astype(BF16)
    q = jnp.dot(qn, wqb_ref[...], preferred_element_type=F32)
    kn = jnp.dot(ckv_b, wk_ref[...], preferred_element_type=F32)

    latent = i >= TILES_P
    cos = jnp.where(latent, cos_ref[...], 1.0)
    sup = jnp.where(latent, sup_ref[...], 0.0)
    sdn = jnp.where(latent, sdn_ref[...], 0.0)

    def rope(u, scale):
        return (u * (cos * scale) + pltpu.roll(u, HEAD_PAD - ROPE_HALF, axis=1) * (sup * scale)
                + pltpu.roll(u, ROPE_HALF, axis=1) * (sdn * scale))

    kpe_r = rope(kpe, 1.0)
    for hd in range(N_HEADS):
        cols = slice(hd * HEAD_PAD, (hd + 1) * HEAD_PAD)
        q_ref[:, cols] = rope(q[:, cols], Q_SCALE).astype(BF16)
        k_ref[:, cols] = (kn[:, cols] + kpe_r).astype(BF16)
    v_ref[...] = jnp.dot(ckv_b, wv_ref[...], preferred_element_type=F32).astype(BF16)
    sg_ref[...] = _silu(jnp.dot(h, wg_ref[...], preferred_element_type=F32))


def _mla_proj(t, w_out, x, norm_w, mod, layer, w):
    cos, sup, sdn = (jnp.asarray(c) for c in _rope_consts())
    j = layer // 2
    hp = N_HEADS * HEAD_PAD
    tlo, thi, _ = _row_pair(t)
    xlo, xhi, _ = _row_pair(x)
    rope_blk = pl.BlockSpec(
        (TM, HEAD_PAD), lambda i: (jnp.where(i >= TILES_P, (i - TILES_P) % TILES_PER_DEC_SEQ, 0), 0))
    tile = lambda n: pl.BlockSpec((TM, n), lambda i: (i, 0))
    return pl.pallas_call(
        _mla_proj_kernel,
        name="mla_proj",
        grid=(N_TILES,),
        in_specs=[
            *_row_pair_specs(t, HY_WIDTH),
            _entry(w_out, (layer - 1) // 2, True),
            *_row_pair_specs(x, D_MODEL),
            _mod_spec(layer - 1, 2),
            _entry(norm_w, layer),
            _mod_spec(layer, 0), _mod_spec(layer, 1),
            _entry(w["w_a"], j, True), _entry(w["w_g"], j, True), _entry(w["w_pe"], j, True),
            _entry(w["q_norm"], j), _entry(w["kv_norm"], j),
            _entry(w["w_qb"], j, True), _entry(w["w_k"], j, True), _entry(w["w_v"], j, True),
            rope_blk, rope_blk, rope_blk,
        ],
        out_specs=[tile(D_MODEL), tile(hp), tile(hp), tile(N_HEADS * V_HEAD),
                   tile(N_HEADS * V_HEAD), tile(KV_LORA), tile(HEAD_PAD)],
        out_shape=[
            jax.ShapeDtypeStruct((ROWS, D_MODEL), F32),
            jax.ShapeDtypeStruct((ROWS, hp), BF16),
            jax.ShapeDtypeStruct((ROWS, hp), BF16),
            jax.ShapeDtypeStruct((ROWS, N_HEADS * V_HEAD), BF16),
            jax.ShapeDtypeStruct((ROWS, N_HEADS * V_HEAD), F32),
            jax.ShapeDtypeStruct((ROWS, KV_LORA), F32),
            jax.ShapeDtypeStruct((ROWS, HEAD_PAD), F32),
        ],
        compiler_params=_cparams(("arbitrary",), 56 << 20),
    )(tlo, thi, w_out, xlo, xhi, mod, norm_w, mod, mod, w["w_a"], w["w_g"], w["w_pe"], w["q_norm"],
      w["kv_norm"], w["w_qb"], w["w_k"], w["w_v"], cos, sup, sdn)


def _mla_ctx_kernel(ckv_ref, kpe_ref, wk_ref, wv_ref, k_ref, v_ref):
    ckv_b = ckv_ref[...].astype(BF16)
    kn = jnp.dot(ckv_b, wk_ref[...], preferred_element_type=F32)
    v_ref[...] = jnp.dot(ckv_b, wv_ref[...], preferred_element_type=F32).astype(BF16)
    kpe = kpe_ref[...]
    for hd in range(N_HEADS):
        cols = slice(hd * HEAD_PAD, (hd + 1) * HEAD_PAD)
        k_ref[:, cols] = (kn[:, cols] + kpe).astype(BF16)


def _mla_ctx(j, ckv_ctx, kpe_ctx, w):
    hp = N_HEADS * HEAD_PAD
    rows = DEC_BATCH * PAST_LEN
    tile = lambda n: pl.BlockSpec((PAST_LEN, n), lambda i: (i, 0))
    cache = lambda n: pl.BlockSpec((None, None, PAST_LEN, n), lambda i: (i, j, 0, 0))
    return pl.pallas_call(
        _mla_ctx_kernel,
        name="mla_ctx",
        grid=(DEC_BATCH,),
        in_specs=[cache(KV_LORA), cache(HEAD_PAD), _entry(w["w_k"], j, True), _entry(w["w_v"], j, True)],
        out_specs=[tile(hp), tile(N_HEADS * V_HEAD)],
        out_shape=[jax.ShapeDtypeStruct((rows, hp), BF16),
                   jax.ShapeDtypeStruct((rows, N_HEADS * V_HEAD), BF16)],
        compiler_params=_cparams(("arbitrary",), 32 << 20),
    )(ckv_ctx, kpe_ctx, w["w_k"], w["w_v"])


NT_DIMS = (((1,), (1,)), ((), ()))
ATTN_PAIRS = 4
ATTN_TQ = 256
PROMPT_SEQS = 2


def _attn_kernel(*refs, n_pairs, n_groups, has_ctx, final, n_seq=1):
    refs = list(refs)
    q_ref, k_ref, v_ref = refs[:3]
    del refs[:3]
    if has_ctx:
        kc_ref, vc_ref = refs[:2]
        del refs[:2]
    sg_ref, wo_ref, x_ref, g_ref = refs[:4]
    del refs[:4]
    if final:
        fg_ref = refs.pop(0)
    out_ref, o_scr = refs[:2]
    tq = q_ref.shape[0] // n_seq
    lk = k_ref.shape[0] // n_seq
    low_half = lax.broadcasted_iota(jnp.int32, (tq, LANES), 1) < V_HEAD
    for sq, p in [(sq, p) for sq in range(n_seq) for p in range(n_pairs)]:
        qrows = pl.ds(sq * tq, tq)
        krows = pl.ds(sq * lk, lk)
        vcols = slice(p * LANES, (p + 1) * LANES)
        vp = v_ref[krows, vcols]
        outs = []
        for hh in range(2):
            cols = slice((2 * p + hh) * HEAD_PAD, (2 * p + hh + 1) * HEAD_PAD)
            q = q_ref[qrows, cols]
            s = lax.dot_general(q, k_ref[krows, cols], NT_DIMS, preferred_element_type=F32)
            mx = jnp.max(s, axis=-1, keepdims=True)
            if has_ctx:
                sc = lax.dot_general(q, kc_ref[:, cols], NT_DIMS, preferred_element_type=F32)
                mx = jnp.maximum(mx, jnp.max(sc, axis=-1, keepdims=True))
            e = jnp.exp2(s - mx)
            den = jnp.sum(e, axis=-1, keepdims=True)
            pv = jnp.dot(e.astype(BF16), vp, preferred_element_type=F32)
            if has_ctx:
                ec = jnp.exp2(sc - mx)
                den = den + jnp.sum(ec, axis=-1, keepdims=True)
                pv = pv + jnp.dot(ec.astype(BF16), vc_ref[:, vcols], preferred_element_type=F32)
            outs.append(pv / den)
        o = jnp.where(low_half, outs[0], outs[1])
        o_scr[qrows, vcols] = (o * sg_ref[qrows, vcols]).astype(BF16)

    u = jnp.dot(o_scr[...], wo_ref[...], preferred_element_type=F32)
    cond = 1 + pl.program_id(0) if has_ctx else 0

    def finish(u_all):
        xn = x_ref[...] + g_ref[pl.ds(cond, 1), :] * u_all
        out_ref[...] = _rms(xn, fg_ref[...]) if final else xn

    if n_groups == 1:
        finish(u)
    else:
        acc_scr = refs[2]
        grp = pl.program_id(2)

        @pl.when(grp == 0)
        def _():
            acc_scr[...] = u

        @pl.when((grp > 0) & (grp < n_groups - 1))
        def _():
            acc_scr[...] += u

        @pl.when(grp == n_groups - 1)
        def _():
            finish(acc_scr[...] + u)


def _attn_prompt(q, k, v, sg, w_o, x, mod, layer, final_g):
    hp = N_HEADS * HEAD_PAD
    nv = N_HEADS * V_HEAD
    xlo, _, _ = _row_pair(x)
    rows = PROMPT_SEQS * SEQ
    seq = lambda n: pl.BlockSpec((rows, n), lambda b: (b, 0))
    final = final_g is not None
    return pl.pallas_call(
        functools.partial(_attn_kernel, n_pairs=N_HEADS // 2, n_groups=1, has_ctx=False, final=final,
                          n_seq=PROMPT_SEQS),
        name="attn_prompt",
        grid=(BATCH // PROMPT_SEQS,),
        in_specs=[seq(hp), seq(hp), seq(nv), seq(nv), _entry(w_o, layer // 2, True), seq(D_MODEL),
                  _mod_spec(layer, 2)] + ([pl.BlockSpec((1, D_MODEL), lambda b: (0, 0))] if final else []),
        out_specs=seq(D_MODEL),
        out_shape=jax.ShapeDtypeStruct((ROWS_P, D_MODEL), F32),
        scratch_shapes=[pltpu.VMEM((rows, nv), BF16)],
        compiler_params=_cparams(("arbitrary",), 40 << 20),
    )(q, k, v, sg, w_o, xlo, mod, *([final_g] if final else []))


def _attn_latent(q, k, v, kc, vc, sg, w_o, x, mod, layer, final_g):
    tq = ATTN_TQ
    npair = ATTN_PAIRS
    ngrp = N_HEADS // 2 // npair
    tiles = DEC_SEQ // tq
    q0 = ROWS_P // tq
    s0 = ROWS_P // DEC_SEQ
    wide = npair * 2 * HEAD_PAD
    narrow = npair * LANES
    _, xhi, base = _row_pair(x)
    x0 = base // tq
    final = final_g is not None
    qrow = lambda b, t, p: (q0 + b * tiles + t, p)
    return pl.pallas_call(
        functools.partial(_attn_kernel, n_pairs=npair, n_groups=ngrp, has_ctx=True, final=final),
        name="attn_latent",
        grid=(DEC_BATCH, tiles, ngrp),
        in_specs=[pl.BlockSpec((tq, wide), qrow),
                  pl.BlockSpec((DEC_SEQ, wide), lambda b, t, p: (s0 + b, p)),
                  pl.BlockSpec((DEC_SEQ, narrow), lambda b, t, p: (s0 + b, p)),
                  pl.BlockSpec((PAST_LEN, wide), lambda b, t, p: (b, p)),
                  pl.BlockSpec((PAST_LEN, narrow), lambda b, t, p: (b, p)),
                  pl.BlockSpec((tq, narrow), qrow),
                  pl.BlockSpec((None, narrow, D_MODEL), lambda b, t, p: (layer // 2, p, 0)),
                  pl.BlockSpec((tq, D_MODEL), lambda b, t, p: (x0 + b * tiles + t, 0)),
                  _mod_spec(layer, 2)] + ([pl.BlockSpec((1, D_MODEL), lambda b, t, p: (0, 0))] if final else []),
        out_specs=pl.BlockSpec((tq, D_MODEL), lambda b, t, p: (b * tiles + t, 0)),
        out_shape=jax.ShapeDtypeStruct((ROWS_S, D_MODEL), F32),
        scratch_shapes=[pltpu.VMEM((tq, narrow), BF16), pltpu.VMEM((tq, D_MODEL), F32)],
        compiler_params=_cparams(("arbitrary", "arbitrary", "arbitrary"), 48 << 20),
    )(q, k, v, kc, vc, sg, w_o, xhi, mod, *([final_g] if final else []))


def _mla_weights(w_in, q_norm, w_qb, kv_norm, w_kvb, w_o):
    n = w_in.shape[0]
    o_pe = Q_LORA + KV_LORA
    o_gate = o_pe + QK_ROPE
    w_pe = jnp.pad(w_in[..., o_pe:o_gate], ((0, 0), (0, 0), (QK_NOPE, HEAD_PAD - QK_NOPE - QK_ROPE)))
    qb = w_qb.reshape(n, Q_LORA, N_HEADS, QK_NOPE + QK_ROPE)
    qb = jnp.pad(qb, ((0, 0), (0, 0), (0, 0), (0, HEAD_PAD - QK_NOPE - QK_ROPE)))
    kvb = w_kvb.reshape(n, KV_LORA, N_HEADS, QK_NOPE + V_HEAD)
    wk = jnp.pad(kvb[..., :QK_NOPE], ((0, 0), (0, 0), (0, 0), (0, HEAD_PAD - QK_NOPE)))
    wv = kvb[..., QK_NOPE:]
    return {
        "w_a": w_in[..., :o_pe].astype(BF16),
        "w_g": w_in[..., o_gate:].astype(BF16),
        "w_pe": w_pe.astype(BF16),
        "q_norm": q_norm[:, None, :],
        "kv_norm": kv_norm[:, None, :],
        "w_qb": qb.reshape(n, Q_LORA, N_HEADS * HEAD_PAD).astype(BF16),
        "w_k": wk.reshape(n, KV_LORA, N_HEADS * HEAD_PAD).astype(BF16),
        "w_v": wv.reshape(n, KV_LORA, N_HEADS * V_HEAD).astype(BF16),
        "w_o": w_o.astype(BF16),
    }


def kernel(x_prompt, x_sample, cache_ckv, cache_kpe, c, c_ctx, norm_w, ada_w, ada_b, hy_w_in, hy_conv_w, hy_conv_b, hy_f_w1, hy_f_b1, hy_f_freq, hy_f_w2, hy_f_b2, hy_f_w3, hy_f_bias, hy_w_out, mla_w_in, mla_q_norm, mla_w_qb, mla_kv_norm, mla_w_kvb, mla_w_o, final_norm):
    x = (x_prompt.reshape(ROWS_P, D_MODEL), x_sample.reshape(ROWS_S, D_MODEL))
    cond = jnp.concatenate([c_ctx[None, :], c, jnp.zeros((N_COND - 1 - DEC_BATCH, D_MODEL), F32)], axis=0)
    mod = _ada_all(cond, ada_w, ada_b)

    nw = norm_w[:, None, :]
    hy_w_in_b = hy_w_in.astype(BF16)
    hy_w_out_b = hy_w_out.astype(BF16)
    hy_conv_b3 = hy_conv_b[:, None, :]
    fh = FILTER_HIDDEN
    twice = lambda w: (jnp.pad(w, ((0, 0), (0, 2 * fh - w.shape[1]), (0, fh)))
                       + jnp.pad(w, ((0, 0), (fh, fh - w.shape[1]), (fh, 0))))
    f_w1, f_w2 = twice(hy_f_w1), twice(hy_f_w2)
    f_b1, f_freq, f_b2 = (jnp.tile(a, (1, 2))[:, None, :] for a in (hy_f_b1, hy_f_freq, hy_f_b2))
    f_bias = hy_f_bias[:, None, :]
    w = _mla_weights(mla_w_in, mla_q_norm, mla_w_qb, mla_kv_norm, mla_w_kvb, mla_w_o)
    kpe_ctx = jnp.pad(cache_kpe, ((0, 0), (0, 0), (0, 0), (QK_NOPE, HEAD_PAD - QK_NOPE - QK_ROPE)))

    new_ckv, new_kpe = [], []
    for layer in range(DEPTH):
        j = layer // 2
        if layer % 2 == 0:
            z, m = _hy_in(x, nw, mod, layer, hy_w_in_b, hy_conv_w, hy_conv_b3)
            t = []
            for L, blk0, nblk in ((DEC_SEQ, ROWS_P // SEQ_BLOCK, ROWS_S // SEQ_BLOCK), (SEQ, 0, ROWS_P // SEQ_BLOCK)):
                filt = _hy_filter(L, j, f_w1, f_b1, f_freq, f_w2, f_b2, hy_f_w3)
                cm, sm = _dft_mats(L)
                t.append(_hy_lconv(L, j, blk0, nblk, z, m, filt, f_bias, cm, sm))
            t = (t[1], t[0])
        else:
            x, q, k, v, sg, ckv, kpe = _mla_proj(t, hy_w_out_b, x, nw, mod, layer, w)
            kc, vc = _mla_ctx(j, cache_ckv, kpe_ctx, w)
            final_g = final_norm[None, :] if layer == DEPTH - 1 else None
            x = (_attn_prompt(q, k, v, sg, w["w_o"], x, mod, layer, final_g),
                 _attn_latent(q, k, v, kc, vc, sg, w["w_o"], x, mod, layer, final_g))
            new_ckv.append(ckv[:ROWS_P].reshape(BATCH, SEQ, KV_LORA))
            new_kpe.append(kpe[:ROWS_P, QK_NOPE:QK_NOPE + QK_ROPE].reshape(BATCH, SEQ, QK_ROPE))

    assert DEPTH % 2 == 0
    y_prompt, y_sample = x
    return (y_prompt.reshape(BATCH, SEQ, D_MODEL), y_sample.reshape(DEC_BATCH, DEC_SEQ, D_MODEL),
            jnp.stack(new_ckv, axis=1), jnp.stack(new_kpe, axis=1))
```

```python
import functools
import math

import numpy as np
import jax
import jax.numpy as jnp
from jax import lax
from jax.experimental import pallas as pl
from jax.experimental.pallas import tpu as pltpu

F32 = jnp.float32
BF16 = jnp.bfloat16

D_MODEL = 1024
BATCH = 16
SEQ = 256
DEPTH = 4
DEC_BATCH = 2
DEC_SEQ = 2048
PAST_LEN = 512
GRID_W = 64
EPS = 1e-6
HY_WIDTH = D_MODEL
FILTER_BANDS = 16
FILTER_EMB = 1 + 2 * FILTER_BANDS
FILTER_HIDDEN = 64
FAST_DECAY_PCT = 0.3
SLOW_DECAY_PCT = 1.5
DECAY_TARGET = 1e-2
N_HEADS = 16
Q_LORA = 384
KV_LORA = 256
QK_NOPE = 64
QK_ROPE = 32
V_HEAD = 64
ROPE_THETA = 10000.0

LANES = 128
MXU_TILE = 256
HEAD_PAD = LANES
ROWS_P = BATCH * SEQ
ROWS_S = DEC_BATCH * DEC_SEQ
ROWS = ROWS_P + ROWS_S
TM = 512
N_TILES = ROWS // TM
TILES_P = ROWS_P // TM
TILES_PER_DEC_SEQ = DEC_SEQ // TM
N_COND = 8
SEQ_BLOCK = 2048
TK = 512
LCONV_TC = 256
VMEM_CAP = 56 * 1024 * 1024


def _cparams(sem, vmem_bytes):
    return pltpu.CompilerParams(dimension_semantics=sem, vmem_limit_bytes=min(int(vmem_bytes), VMEM_CAP))


def _resident(shape, index_map):
    return pl.BlockSpec(shape, index_map, pipeline_mode=pl.Buffered(1))


def _entry(arr, idx, resident=False):
    zeros = (0,) * (arr.ndim - 1)
    return pl.BlockSpec((None,) + arr.shape[1:], lambda *_: (idx,) + zeros,
                        pipeline_mode=pl.Buffered(1) if resident else None)


def _cond_row(i, rows=TM):
    return jnp.where(i < ROWS_P // rows, 0, 1 + (i - ROWS_P // rows) // (DEC_SEQ // rows))


def _silu(x):
    return x * jax.nn.sigmoid(x)


def _rms(x, g):
    return x * lax.rsqrt(jnp.mean(x * x, axis=-1, keepdims=True) + EPS) * g


def _row_pair(x):
    if isinstance(x, tuple):
        return x[0], x[1], 0
    return x, x, ROWS_P


def _row_pair_specs(x, n, rows=TM):
    _, _, base = _row_pair(x)
    n_lo = ROWS_P // rows
    lo = pl.BlockSpec((rows, n), lambda i, *_: (jnp.minimum(i, n_lo - 1), 0))
    hi = pl.BlockSpec((rows, n), lambda i, *_: (jnp.maximum(i - n_lo, 0) + base // rows, 0))
    return lo, hi


def _pick_rows(i, lo_ref, hi_ref, rows=TM):
    return jnp.where(i < ROWS_P // rows, lo_ref[...], hi_ref[...])


def _ada_kernel(cond_ref, w_ref, b_ref, o_ref):
    s = _silu(cond_ref[...]).astype(BF16)
    o_ref[...] = jnp.dot(s, w_ref[...].astype(BF16), preferred_element_type=F32) + b_ref[...]


def _ada_all(cond, ada_w, ada_b):
    tn = 1024
    return pl.pallas_call(
        _ada_kernel,
        name="ada",
        grid=(DEPTH, 3 * D_MODEL // tn),
        in_specs=[
            pl.BlockSpec((N_COND, D_MODEL), lambda l, j: (0, 0)),
            pl.BlockSpec((None, D_MODEL, tn), lambda l, j: (l, 0, j)),
            pl.BlockSpec((None, 1, tn), lambda l, j: (l, 0, j)),
        ],
        out_specs=pl.BlockSpec((None, N_COND, tn), lambda l, j: (l, 0, j)),
        out_shape=jax.ShapeDtypeStruct((DEPTH, N_COND, 3 * D_MODEL), F32),
        compiler_params=_cparams(("arbitrary", "arbitrary"), 32 << 20),
    )(cond, ada_w, ada_b.reshape(DEPTH, 1, 3 * D_MODEL))


def _mod_spec(layer, part):
    return pl.BlockSpec((None, N_COND, D_MODEL), lambda i, *_: (layer, 0, part))


def _modnorm(i, x, nw_ref, sh_ref, sc_ref):
    c = _cond_row(i)
    return _rms(x, nw_ref[...]) * (1.0 + sc_ref[pl.ds(c, 1), :]) + sh_ref[pl.ds(c, 1), :]


HALO = 16
HY_ROWS = TM


def _halo_specs(x):
    _, _, base = _row_pair(x)
    r = HY_ROWS // HALO
    n_lo = ROWS_P // HY_ROWS
    lo_last = ROWS_P // HALO - 1
    hi_last = ROWS_S // HALO - 1
    lo_blk = lambda i: jnp.minimum(i, n_lo - 1)
    hi_blk = lambda i: jnp.maximum(i - n_lo, 0)
    blk = lambda f: pl.BlockSpec((HALO, D_MODEL), lambda i: (f(i), 0))
    return [
        blk(lambda i: jnp.maximum(lo_blk(i) * r - 1, 0)),
        blk(lambda i: jnp.maximum(hi_blk(i) * r - 1, 0) + base // HALO),
        blk(lambda i: jnp.minimum((lo_blk(i) + 1) * r, lo_last)),
        blk(lambda i: jnp.minimum((hi_blk(i) + 1) * r, hi_last) + base // HALO),
    ]


def _hy_in_kernel(xlo_ref, xhi_ref, plo_ref, phi_ref, nlo_ref, nhi_ref, nw_ref, sh_ref, sc_ref, w_ref,
                  cw_ref, cb_ref, z_ref, m_ref, h_scr, *u_scrs):
    i = pl.program_id(0)
    width = z_ref.shape[1]
    c = _cond_row(i, HY_ROWS)
    norm = lambda lo_ref, hi_ref: (_rms(_pick_rows(i, lo_ref, hi_ref, HY_ROWS), nw_ref[...])
                                   * (1.0 + sc_ref[pl.ds(c, 1), :]) + sh_ref[pl.ds(c, 1), :]).astype(BF16)
    h_scr[0:HALO, :] = norm(plo_ref, phi_ref)
    h_scr[HALO:HALO + HY_ROWS, :] = norm(xlo_ref, xhi_ref)
    h_scr[HALO + HY_ROWS:HALO + HY_ROWS + HALO, :] = norm(nlo_ref, nhi_ref)

    seq_mask = jnp.where(i < ROWS_P // HY_ROWS, SEQ - 1, DEC_SEQ - 1)
    row = lax.broadcasted_iota(jnp.int32, (TM, width), 0)
    for t in range(HY_ROWS // TM):
        r0 = t * TM
        pos = (row + (i * HY_ROWS + r0)) & seq_mask
        first = pos == 0
        last = pos == seq_mask
        rows_h = pl.ds(r0, TM + 2 * HALO)

        def conv(g, u_scr):
            cols = slice(g * width, (g + 1) * width)
            u_scr[...] = jnp.dot(h_scr[rows_h, :], w_ref[:, cols], preferred_element_type=F32)
            prev = jnp.where(first, 0.0, u_scr[HALO - 1:HALO - 1 + TM, :])
            nxt = jnp.where(last, 0.0, u_scr[HALO + 1:HALO + 1 + TM, :])
            return (cb_ref[:, cols] + prev * cw_ref[0:1, cols] + u_scr[HALO:HALO + TM, :] * cw_ref[1:2, cols]
                    + nxt * cw_ref[2:3, cols])

        u0, u1, u2 = u_scrs[3 * t:3 * t + 3]
        out_rows = pl.ds(r0, TM)
        x0 = conv(0, u0)
        z_ref[out_rows, :] = (conv(2, u2) * conv(1, u1)).astype(BF16)
        gate = jnp.dot(h_scr[pl.ds(r0 + HALO, TM), :], w_ref[:, 3 * width:4 * width], preferred_element_type=F32)
        m_ref[out_rows, :] = (x0 * _silu(gate)).astype(BF16)


def _hy_in(x, norm_w, mod, layer, w_in, conv_w, conv_b):
    j = layer // 2
    xlo, xhi, _ = _row_pair(x)
    out = pl.BlockSpec((HY_ROWS, HY_WIDTH), lambda i: (i, 0))
    n_tiles = HY_ROWS // TM
    return pl.pallas_call(
        _hy_in_kernel,
        name="hy_in",
        grid=(ROWS // HY_ROWS,),
        in_specs=[
            *_row_pair_specs(x, D_MODEL, HY_ROWS),
            *_halo_specs(x),
            _entry(norm_w, layer),
            _mod_spec(layer, 0),
            _mod_spec(layer, 1),
            _entry(w_in, j, resident=True),
            _entry(conv_w, j),
            _entry(conv_b, j),
        ],
        out_specs=[out, out],
        out_shape=[jax.ShapeDtypeStruct((ROWS, HY_WIDTH), BF16)] * 2,
        scratch_shapes=([pltpu.VMEM((HY_ROWS + 2 * HALO, D_MODEL), BF16)]
                        + [pltpu.VMEM((TM + 2 * HALO, HY_WIDTH), F32)] * (3 * n_tiles)),
        compiler_params=_cparams(("arbitrary",), 56 << 20),
    )(xlo, xhi, xlo, xhi, xlo, xhi, norm_w, mod, mod, w_in, conv_w, conv_b)


@functools.lru_cache(maxsize=None)
def _filter_consts(L):
    t = np.linspace(0.0, 1.0, L)[:, None]
    w = (2.0 * math.pi / L) * np.arange(L)[:, None]
    bands = np.linspace(1e-4, FILTER_BANDS - 1, FILTER_BANDS)[None, :]
    emb = np.concatenate([t, np.cos(bands * w), -np.sin(bands * w)], axis=-1)
    emb = np.pad(emb, ((0, 0), (0, FILTER_HIDDEN - FILTER_EMB)))
    emb = np.concatenate([emb[:L // 2], emb[L // 2:]], axis=1)
    max_decay = math.log(DECAY_TARGET) / FAST_DECAY_PCT
    min_decay = math.log(DECAY_TARGET) / SLOW_DECAY_PCT
    deltas = np.abs(np.linspace(min_decay, max_decay, HY_WIDTH))
    deltas = np.concatenate([deltas, deltas])[None, :]
    return emb.astype(np.float32), t.astype(np.float32), deltas.astype(np.float32)


def _filter_kernel(emb_ref, t_ref, w1_ref, b1_ref, fr_ref, w2_ref, b2_ref, w3_ref, dl_ref, o_ref, h_scr):
    @pl.when(pl.program_id(0) == 0)
    def _():
        hi = lax.Precision.HIGHEST
        fr = fr_ref[...]
        h = jnp.sin(fr * (jnp.dot(emb_ref[...], w1_ref[...], precision=hi, preferred_element_type=F32) + b1_ref[...]))
        h = jnp.sin(fr * (jnp.dot(h, w2_ref[...], precision=hi, preferred_element_type=F32) + b2_ref[...]))
        h_scr[...] = h.astype(BF16)

    w3 = w3_ref[...].astype(BF16)
    zero = jnp.zeros_like(w3)
    hid = h_scr[...]
    h = jnp.concatenate([jnp.dot(hid, jnp.concatenate([w3, zero], axis=0), preferred_element_type=F32),
                         jnp.dot(hid, jnp.concatenate([zero, w3], axis=0), preferred_element_type=F32)], axis=0)
    h = h * jnp.exp(-t_ref[...] * dl_ref[...])
    o_ref[...] = h / jnp.sum(jnp.abs(h), axis=0, keepdims=True)


def _hy_filter(L, hy, w1, b1, freq, w2, b2, w3):
    emb, t, deltas = _filter_consts(L)
    tcf = 512
    two = 2 * FILTER_HIDDEN
    full = lambda shape: pl.BlockSpec(shape, lambda c: (0, 0))
    return pl.pallas_call(
        _filter_kernel,
        name=f"hy_filter{L}",
        grid=(2 * HY_WIDTH // tcf,),
        in_specs=[
            full((L // 2, two)), full((L, 1)), _entry(w1, hy), _entry(b1, hy), _entry(freq, hy),
            _entry(w2, hy), _entry(b2, hy),
            pl.BlockSpec((None, FILTER_HIDDEN, tcf), lambda c: (hy, 0, c)),
            pl.BlockSpec((1, tcf), lambda c: (0, c)),
        ],
        out_specs=pl.BlockSpec((L, tcf), lambda c: (0, c)),
        out_shape=jax.ShapeDtypeStruct((L, 2 * HY_WIDTH), F32),
        scratch_shapes=[pltpu.VMEM((L // 2, two), BF16)],
        compiler_params=_cparams(("arbitrary",), 40 << 20),
    )(jnp.asarray(emb), jnp.asarray(t), w1, b1, freq, w2, b2, w3, jnp.asarray(deltas))


@functools.lru_cache(maxsize=None)
def _dft_consts(L):
    k = np.arange(L, dtype=np.int64)
    ang = (np.outer(k, k) % (2 * L)).astype(np.float64) * (math.pi / L)
    return np.cos(ang).astype(np.float32), np.sin(ang).astype(np.float32)


def _dft_mats(L):
    c, s = _dft_consts(L)
    return jnp.asarray(c).astype(BF16), jnp.asarray(s).astype(BF16)


def _alt_sign(shape):
    return (1 - 2 * (lax.broadcasted_iota(jnp.int32, shape, 0) & 1)).astype(F32)


def _lconv_kernel(z_ref, m_ref, ff_ref, fb_ref, bias_ref, c_ref, s_ref, t_ref, a_scr, b_scr, nq_scr, y_scr,
                  *, L, tk):
    rows, tc = y_scr.shape
    alt = _alt_sign((L, tc))
    bins = [slice(k0, k0 + tk) for k0 in range(0, L, tk)]

    @pl.when(pl.program_id(1) == 0)
    def _():
        inv_n = 1.0 / (2.0 * L)
        bias = bias_ref[...]
        ff = ff_ref[...]
        fb = fb_ref[...]
        f = ff + fb
        nq_scr[...] = jnp.broadcast_to((jnp.sum(f * alt, axis=0, keepdims=True) + bias) * inv_n, nq_scr.shape)
        f_b = f.astype(BF16)
        d_b = (fb - ff).astype(BF16)
        for kt in bins:
            hre = jnp.dot(c_ref[kt, :], f_b, preferred_element_type=F32) + bias
            him = jnp.dot(s_ref[kt, :], d_b, preferred_element_type=F32)
            a = hre * (2.0 * inv_n)
            if kt.start == 0:
                a = jnp.where(lax.broadcasted_iota(jnp.int32, hre.shape, 0) == 0, hre * inv_n, a)
            a_scr[kt, :] = a
            b_scr[kt, :] = him * (2.0 * inv_n)

    for s0 in range(0, rows, L):
        sq = pl.ds(s0, L)
        zb = z_ref[sq, :]
        z_nyq = jnp.sum(zb.astype(F32) * alt, axis=0, keepdims=True)
        y_scr[sq, :] = alt * (z_nyq * nq_scr[0:1, :])
        for kt in bins:
            zr = jnp.dot(c_ref[kt, :], zb, preferred_element_type=F32)
            zi = jnp.dot(s_ref[kt, :], zb, preferred_element_type=F32)
            a = a_scr[kt, :]
            b = b_scr[kt, :]
            yr = (zr * a + zi * b).astype(BF16)
            yw = (zi * a - zr * b).astype(BF16)
            y_scr[sq, :] += (jnp.dot(c_ref[:, kt], yr, preferred_element_type=F32)
                             + jnp.dot(s_ref[:, kt], yw, preferred_element_type=F32))
        t_ref[sq, :] = (y_scr[sq, :] * m_ref[sq, :].astype(F32)).astype(BF16)


def _hy_lconv(L, hy, row_block0, n_row_blocks, z, m, filt, f_bias, c, s):
    tk = min(TK, L)
    long_seq = L == SEQ_BLOCK
    tc = 2 * LCONV_TC if long_seq else LCONV_TC
    nc = HY_WIDTH // tc
    blk = pl.BlockSpec((SEQ_BLOCK, tc), lambda j, r: (r + row_block0, j))
    mat = _resident((L, L), lambda j, r: (0, 0))
    filt_mode = pl.Buffered(1) if long_seq else None
    return pl.pallas_call(
        functools.partial(_lconv_kernel, L=L, tk=tk),
        name=f"hy_lconv{L}",
        grid=(nc, n_row_blocks),
        in_specs=[blk, blk,
                  pl.BlockSpec((L, tc), lambda j, r: (0, j), pipeline_mode=filt_mode),
                  pl.BlockSpec((L, tc), lambda j, r: (0, j + nc), pipeline_mode=filt_mode),
                  pl.BlockSpec((None, 1, tc), lambda j, r: (hy, 0, j)),
                  mat, mat],
        out_specs=pl.BlockSpec((SEQ_BLOCK, tc), lambda j, r: (r, j)),
        out_shape=jax.ShapeDtypeStruct((n_row_blocks * SEQ_BLOCK, HY_WIDTH), BF16),
        scratch_shapes=[pltpu.VMEM((L, tc), F32), pltpu.VMEM((L, tc), F32), pltpu.VMEM((8, tc), F32),
                        pltpu.VMEM((SEQ_BLOCK, tc), F32)],
        compiler_params=_cparams(("arbitrary", "arbitrary"), 56 << 20),
    )(z, m, filt, filt, f_bias, c, s)


@functools.lru_cache(maxsize=None)
def _rope_consts():
    axis_dim = QK_ROPE // 2
    nf = axis_dim // 2
    inv = ROPE_THETA ** (-np.arange(0, axis_dim, 2, dtype=np.float64) / axis_dim)
    t = np.arange(DEC_SEQ)
    ang_r = (t // GRID_W)[:, None] * inv
    ang_c = (t % GRID_W)[:, None] * inv
    cos = np.ones((DEC_SEQ, HEAD_PAD))
    sin_up = np.zeros((DEC_SEQ, HEAD_PAD))
    sin_dn = np.zeros((DEC_SEQ, HEAD_PAD))
    for base, ang in ((QK_NOPE, ang_r), (QK_NOPE + axis_dim, ang_c)):
        cos[:, base:base + nf] = np.cos(ang)
        cos[:, base + nf:base + 2 * nf] = np.cos(ang)
        sin_up[:, base:base + nf] = -np.sin(ang)
        sin_dn[:, base + nf:base + 2 * nf] = np.sin(ang)
    return cos.astype(np.float32), sin_up.astype(np.float32), sin_dn.astype(np.float32)


ROPE_HALF = QK_ROPE // 4
Q_SCALE = math.log2(math.e) / math.sqrt(QK_NOPE + QK_ROPE)


def _mla_proj_kernel(tlo_ref, thi_ref, wout_ref, xlo_ref, xhi_ref, gprev_ref,
                     nw_ref, sh_ref, sc_ref, wa_ref, wg_ref, wpe_ref, qn_ref, kvn_ref, wqb_ref, wk_ref, wv_ref,
                     cos_ref, sup_ref, sdn_ref,
                     xn_ref, q_ref, k_ref, v_ref, sg_ref, ckv_ref, kpe_ref):
    i = pl.program_id(0)
    u = jnp.dot(_pick_rows(i, tlo_ref, thi_ref), wout_ref[...], preferred_element_type=F32)
    x = _pick_rows(i, xlo_ref, xhi_ref) + gprev_ref[pl.ds(_cond_row(i), 1), :] * u
    xn_ref[...] = x
    h = _modnorm(i, x, nw_ref, sh_ref, sc_ref).astype(BF16)
    lora = jnp.dot(h, wa_ref[...], preferred_element_type=F32)
    kpe = jnp.dot(h, wpe_ref[...], preferred_element_type=F32)
    qn = _rms(lora[:, 0:Q_LORA], qn_ref[...]).astype(BF16)
    ckv = _rms(lora[:, Q_LORA:Q_LORA + KV_LORA], kvn_ref[...])
    ckv_ref[...] = ckv
    kpe_ref[...] = kpe
    ckv_b = ckv.astype(BF16)
    q = jnp.dot(qn, wqb_ref[...], preferred_element_type=F32)
    kn = jnp.dot(ckv_b, wk_ref[...], preferred_element_type=F32)

    latent = i >= TILES_P
    cos = jnp.where(latent, cos_ref[...], 1.0)
    sup = jnp.where(latent, sup_ref[...], 0.0)
    sdn = jnp.where(latent, sdn_ref[...], 0.0)

    def rope(u, scale):
        return (u * (cos * scale) + pltpu.roll(u, HEAD_PAD - ROPE_HALF, axis=1) * (sup * scale)
                + pltpu.roll(u, ROPE_HALF, axis=1) * (sdn * scale))

    kpe_r = rope(kpe, 1.0)
    for hd in range(N_HEADS):
        cols = slice(hd * HEAD_PAD, (hd + 1) * HEAD_PAD)
        q_ref[:, cols] = rope(q[:, cols], Q_SCALE).astype(BF16)
        k_ref[:, cols] = (kn[:, cols] + kpe_r).astype(BF16)
    v_ref[...] = jnp.dot(ckv_b, wv_ref[...], preferred_element_type=F32).astype(BF16)
    sg_ref[...] = _silu(jnp.dot(h, wg_ref[...], preferred_element_type=F32))


def _mla_proj(t, w_out, x, norm_w, mod, layer, w):
    cos, sup, sdn = (jnp.asarray(c) for c in _rope_consts())
    j = layer // 2
    hp = N_HEADS * HEAD_PAD
    tlo, thi, _ = _row_pair(t)
    xlo, xhi, _ = _row_pair(x)
    rope_blk = pl.BlockSpec(
        (TM, HEAD_PAD), lambda i: (jnp.where(i >= TILES_P, (i - TILES_P) % TILES_PER_DEC_SEQ, 0), 0))
    tile = lambda n: pl.BlockSpec((TM, n), lambda i: (i, 0))
    return pl.pallas_call(
        _mla_proj_kernel,
        name="mla_proj",
        grid=(N_TILES,),
        in_specs=[
            *_row_pair_specs(t, HY_WIDTH),
            _entry(w_out, (layer - 1) // 2, True),
            *_row_pair_specs(x, D_MODEL),
            _mod_spec(layer - 1, 2),
            _entry(norm_w, layer),
            _mod_spec(layer, 0), _mod_spec(layer, 1),
            _entry(w["w_a"], j, True), _entry(w["w_g"], j, True), _entry(w["w_pe"], j, True),
            _entry(w["q_norm"], j), _entry(w["kv_norm"], j),
            _entry(w["w_qb"], j, True), _entry(w["w_k"], j, True), _entry(w["w_v"], j, True),
            rope_blk, rope_blk, rope_blk,
        ],
        out_specs=[tile(D_MODEL), tile(hp), tile(hp), tile(N_HEADS * V_HEAD),
                   tile(N_HEADS * V_HEAD), tile(KV_LORA), tile(HEAD_PAD)],
        out_shape=[
            jax.ShapeDtypeStruct((ROWS, D_MODEL), F32),
            jax.ShapeDtypeStruct((ROWS, hp), BF16),
            jax.ShapeDtypeStruct((ROWS, hp), BF16),
            jax.ShapeDtypeStruct((ROWS, N_HEADS * V_HEAD), BF16),
            jax.ShapeDtypeStruct((ROWS, N_HEADS * V_HEAD), F32),
            jax.ShapeDtypeStruct((ROWS, KV_LORA), F32),
            jax.ShapeDtypeStruct((ROWS, HEAD_PAD), F32),
        ],
        compiler_params=_cparams(("arbitrary",), 56 << 20),
    )(tlo, thi, w_out, xlo, xhi, mod, norm_w, mod, mod, w["w_a"], w["w_g"], w["w_pe"], w["q_norm"],
      w["kv_norm"], w["w_qb"], w["w_k"], w["w_v"], cos, sup, sdn)


def _mla_ctx_kernel(ckv_ref, kpe_ref, wk_ref, wv_ref, k_ref, v_ref):
    ckv_b = ckv_ref[...].astype(BF16)
    kn = jnp.dot(ckv_b, wk_ref[...], preferred_element_type=F32)
    v_ref[...] = jnp.dot(ckv_b, wv_ref[...], preferred_element_type=F32).astype(BF16)
    kpe = kpe_ref[...]
    for hd in range(N_HEADS):
        cols = slice(hd * HEAD_PAD, (hd + 1) * HEAD_PAD)
        k_ref[:, cols] = (kn[:, cols] + kpe).astype(BF16)


def _mla_ctx(j, ckv_ctx, kpe_ctx, w):
    hp = N_HEADS * HEAD_PAD
    rows = DEC_BATCH * PAST_LEN
    tile = lambda n: pl.BlockSpec((PAST_LEN, n), lambda i: (i, 0))
    cache = lambda n: pl.BlockSpec((None, None, PAST_LEN, n), lambda i: (i, j, 0, 0))
    return pl.pallas_call(
        _mla_ctx_kernel,
        name="mla_ctx",
        grid=(DEC_BATCH,),
        in_specs=[cache(KV_LORA), cache(HEAD_PAD), _entry(w["w_k"], j, True), _entry(w["w_v"], j, True)],
        out_specs=[tile(hp), tile(N_HEADS * V_HEAD)],
        out_shape=[jax.ShapeDtypeStruct((rows, hp), BF16),
                   jax.ShapeDtypeStruct((rows, N_HEADS * V_HEAD), BF16)],
        compiler_params=_cparams(("arbitrary",), 32 << 20),
    )(ckv_ctx, kpe_ctx, w["w_k"], w["w_v"])


NT_DIMS = (((1,), (1,)), ((), ()))
ATTN_PAIRS = 4
ATTN_TQ = 256
PROMPT_SEQS = 2


def _attn_kernel(*refs, n_pairs, n_groups, has_ctx, final, n_seq=1):
    refs = list(refs)
    q_ref, k_ref, v_ref = refs[:3]
    del refs[:3]
    if has_ctx:
        kc_ref, vc_ref = refs[:2]
        del refs[:2]
    sg_ref, wo_ref, x_ref, g_ref = refs[:4]
    del refs[:4]
    if final:
        fg_ref = refs.pop(0)
    out_ref, o_scr = refs[:2]
    tq = q_ref.shape[0] // n_seq
    lk = k_ref.shape[0] // n_seq
    low_half = lax.broadcasted_iota(jnp.int32, (tq, LANES), 1) < V_HEAD
    for sq, p in [(sq, p) for sq in range(n_seq) for p in range(n_pairs)]:
        qrows = pl.ds(sq * tq, tq)
        krows = pl.ds(sq * lk, lk)
        vcols = slice(p * LANES, (p + 1) * LANES)
        vp = v_ref[krows, vcols]
        outs = []
        scores = []
        for hh in range(2):
            cols = slice((2 * p + hh) * HEAD_PAD, (2 * p + hh + 1) * HEAD_PAD)
            q = q_ref[qrows, cols]
            s = lax.dot_general(q, k_ref[krows, cols], NT_DIMS, preferred_element_type=F32)
            sc = lax.dot_general(q, kc_ref[:, cols], NT_DIMS, preferred_element_type=F32) if has_ctx else None
            scores.append((s, sc))
        for s, sc in scores:
            mx = jnp.max(s, axis=-1, keepdims=True)
            if has_ctx:
                mx = jnp.maximum(mx, jnp.max(sc, axis=-1, keepdims=True))
            e = jnp.exp2(s - mx)
            den = jnp.sum(e, axis=-1, keepdims=True)
            pv = jnp.dot(e.astype(BF16), vp, preferred_element_type=F32)
            if has_ctx:
                ec = jnp.exp2(sc - mx)
                den = den + jnp.sum(ec, axis=-1, keepdims=True)
                pv = pv + jnp.dot(ec.astype(BF16), vc_ref[:, vcols], preferred_element_type=F32)
            outs.append(pv / den)
        o = jnp.where(low_half, outs[0], outs[1])
        o_scr[qrows, vcols] = (o * sg_ref[qrows, vcols]).astype(BF16)

    u = jnp.dot(o_scr[...], wo_ref[...], preferred_element_type=F32)
    cond = 1 + pl.program_id(0) if has_ctx else 0

    def finish(u_all):
        xn = x_ref[...] + g_ref[pl.ds(cond, 1), :] * u_all
        out_ref[...] = _rms(xn, fg_ref[...]) if final else xn

    if n_groups == 1:
        finish(u)
    else:
        acc_scr = refs[2]
        grp = pl.program_id(2)

        @pl.when(grp == 0)
        def _():
            acc_scr[...] = u

        @pl.when((grp > 0) & (grp < n_groups - 1))
        def _():
            acc_scr[...] += u

        @pl.when(grp == n_groups - 1)
        def _():
            finish(acc_scr[...] + u)


def _attn_prompt(q, k, v, sg, w_o, x, mod, layer, final_g):
    hp = N_HEADS * HEAD_PAD
    nv = N_HEADS * V_HEAD
    xlo, _, _ = _row_pair(x)
    rows = PROMPT_SEQS * SEQ
    seq = lambda n: pl.BlockSpec((rows, n), lambda b: (b, 0))
    final = final_g is not None
    return pl.pallas_call(
        functools.partial(_attn_kernel, n_pairs=N_HEADS // 2, n_groups=1, has_ctx=False, final=final,
                          n_seq=PROMPT_SEQS),
        name="attn_prompt",
        grid=(BATCH // PROMPT_SEQS,),
        in_specs=[seq(hp), seq(hp), seq(nv), seq(nv), _entry(w_o, layer // 2, True), seq(D_MODEL),
                  _mod_spec(layer, 2)] + ([pl.BlockSpec((1, D_MODEL), lambda b: (0, 0))] if final else []),
        out_specs=seq(D_MODEL),
        out_shape=jax.ShapeDtypeStruct((ROWS_P, D_MODEL), F32),
        scratch_shapes=[pltpu.VMEM((rows, nv), BF16)],
        compiler_params=_cparams(("arbitrary",), 40 << 20),
    )(q, k, v, sg, w_o, xlo, mod, *([final_g] if final else []))


def _attn_latent(q, k, v, kc, vc, sg, w_o, x, mod, layer, final_g):
    tq = ATTN_TQ
    npair = ATTN_PAIRS
    ngrp = N_HEADS // 2 // npair
    tiles = DEC_SEQ // tq
    q0 = ROWS_P // tq
    s0 = ROWS_P // DEC_SEQ
    wide = npair * 2 * HEAD_PAD
    narrow = npair * LANES
    _, xhi, base = _row_pair(x)
    x0 = base // tq
    final = final_g is not None
    qrow = lambda b, t, p: (q0 + b * tiles + t, p)
    return pl.pallas_call(
        functools.partial(_attn_kernel, n_pairs=npair, n_groups=ngrp, has_ctx=True, final=final),
        name="attn_latent",
        grid=(DEC_BATCH, tiles, ngrp),
        in_specs=[pl.BlockSpec((tq, wide), qrow),
                  pl.BlockSpec((DEC_SEQ, wide), lambda b, t, p: (s0 + b, p)),
                  pl.BlockSpec((DEC_SEQ, narrow), lambda b, t, p: (s0 + b, p)),
                  pl.BlockSpec((PAST_LEN, wide), lambda b, t, p: (b, p)),
                  pl.BlockSpec((PAST_LEN, narrow), lambda b, t, p: (b, p)),
                  pl.BlockSpec((tq, narrow), qrow),
                  pl.BlockSpec((None, narrow, D_MODEL), lambda b, t, p: (layer // 2, p, 0)),
                  pl.BlockSpec((tq, D_MODEL), lambda b, t, p: (x0 + b * tiles + t, 0)),
                  _mod_spec(layer, 2)] + ([pl.BlockSpec((1, D_MODEL), lambda b, t, p: (0, 0))] if final else []),
        out_specs=pl.BlockSpec((tq, D_MODEL), lambda b, t, p: (b * tiles + t, 0)),
        out_shape=jax.ShapeDtypeStruct((ROWS_S, D_MODEL), F32),
        scratch_shapes=[pltpu.VMEM((tq, narrow), BF16), pltpu.VMEM((tq, D_MODEL), F32)],
        compiler_params=_cparams(("arbitrary", "arbitrary", "arbitrary"), 48 << 20),
    )(q, k, v, kc, vc, sg, w_o, xhi, mod, *([final_g] if final else []))


def _mla_weights(w_in, q_norm, w_qb, kv_norm, w_kvb, w_o):
    n = w_in.shape[0]
    o_pe = Q_LORA + KV_LORA
    o_gate = o_pe + QK_ROPE
    w_pe = jnp.pad(w_in[..., o_pe:o_gate], ((0, 0), (0, 0), (QK_NOPE, HEAD_PAD - QK_NOPE - QK_ROPE)))
    qb = w_qb.reshape(n, Q_LORA, N_HEADS, QK_NOPE + QK_ROPE)
    qb = jnp.pad(qb, ((0, 0), (0, 0), (0, 0), (0, HEAD_PAD - QK_NOPE - QK_ROPE)))
    kvb = w_kvb.reshape(n, KV_LORA, N_HEADS, QK_NOPE + V_HEAD)
    wk = jnp.pad(kvb[..., :QK_NOPE], ((0, 0), (0, 0), (0, 0), (0, HEAD_PAD - QK_NOPE)))
    wv = kvb[..., QK_NOPE:]
    return {
        "w_a": w_in[..., :o_pe].astype(BF16),
        "w_g": w_in[..., o_gate:].astype(BF16),
        "w_pe": w_pe.astype(BF16),
        "q_norm": q_norm[:, None, :],
        "kv_norm": kv_norm[:, None, :],
        "w_qb": qb.reshape(n, Q_LORA, N_HEADS * HEAD_PAD).astype(BF16),
        "w_k": wk.reshape(n, KV_LORA, N_HEADS * HEAD_PAD).astype(BF16),
        "w_v": wv.reshape(n, KV_LORA, N_HEADS * V_HEAD).astype(BF16),
        "w_o": w_o.astype(BF16),
    }


def kernel(x_prompt, x_sample, cache_ckv, cache_kpe, c, c_ctx, norm_w, ada_w, ada_b, hy_w_in, hy_conv_w, hy_conv_b, hy_f_w1, hy_f_b1, hy_f_freq, hy_f_w2, hy_f_b2, hy_f_w3, hy_f_bias, hy_w_out, mla_w_in, mla_q_norm, mla_w_qb, mla_kv_norm, mla_w_kvb, mla_w_o, final_norm):
    x = (x_prompt.reshape(ROWS_P, D_MODEL), x_sample.reshape(ROWS_S, D_MODEL))
    cond = jnp.concatenate([c_ctx[None, :], c, jnp.zeros((N_COND - 1 - DEC_BATCH, D_MODEL), F32)], axis=0)
    mod = _ada_all(cond, ada_w, ada_b)

    nw = norm_w[:, None, :]
    hy_w_in_b = hy_w_in.astype(BF16)
    hy_w_out_b = hy_w_out.astype(BF16)
    hy_conv_b3 = hy_conv_b[:, None, :]
    fh = FILTER_HIDDEN
    twice = lambda w: (jnp.pad(w, ((0, 0), (0, 2 * fh - w.shape[1]), (0, fh)))
                       + jnp.pad(w, ((0, 0), (fh, fh - w.shape[1]), (fh, 0))))
    f_w1, f_w2 = twice(hy_f_w1), twice(hy_f_w2)
    f_b1, f_freq, f_b2 = (jnp.tile(a, (1, 2))[:, None, :] for a in (hy_f_b1, hy_f_freq, hy_f_b2))
    f_bias = hy_f_bias[:, None, :]
    w = _mla_weights(mla_w_in, mla_q_norm, mla_w_qb, mla_kv_norm, mla_w_kvb, mla_w_o)
    kpe_ctx = jnp.pad(cache_kpe, ((0, 0), (0, 0), (0, 0), (QK_NOPE, HEAD_PAD - QK_NOPE - QK_ROPE)))

    new_ckv, new_kpe = [], []
    for layer in range(DEPTH):
        j = layer // 2
        if layer % 2 == 0:
            z, m = _hy_in(x, nw, mod, layer, hy_w_in_b, hy_conv_w, hy_conv_b3)
            t = []
            for L, blk0, nblk in ((DEC_SEQ, ROWS_P // SEQ_BLOCK, ROWS_S // SEQ_BLOCK), (SEQ, 0, ROWS_P // SEQ_BLOCK)):
                filt = _hy_filter(L, j, f_w1, f_b1, f_freq, f_w2, f_b2, hy_f_w3)
                cm, sm = _dft_mats(L)
                t.append(_hy_lconv(L, j, blk0, nblk, z, m, filt, f_bias, cm, sm))
            t = (t[1], t[0])
        else:
            x, q, k, v, sg, ckv, kpe = _mla_proj(t, hy_w_out_b, x, nw, mod, layer, w)
            kc, vc = _mla_ctx(j, cache_ckv, kpe_ctx, w)
            final_g = final_norm[None, :] if layer == DEPTH - 1 else None
            x = (_attn_prompt(q, k, v, sg, w["w_o"], x, mod, layer, final_g),
                 _attn_latent(q, k, v, kc, vc, sg, w["w_o"], x, mod, layer, final_g))
            new_ckv.append(ckv[:ROWS_P].reshape(BATCH, SEQ, KV_LORA))
            new_kpe.append(kpe[:ROWS_P, QK_NOPE:QK_NOPE + QK_ROPE].reshape(BATCH, SEQ, QK_ROPE))

    assert DEPTH % 2 == 0
    y_prompt, y_sample = x
    return (y_prompt.reshape(BATCH, SEQ, D_MODEL), y_sample.reshape(DEC_BATCH, DEC_SEQ, D_MODEL),
            jnp.stack(new_ckv, axis=1), jnp.stack(new_kpe, axis=1))
```

```python
import functools
import math

import numpy as np
import jax
import jax.numpy as jnp
from jax import lax
from jax.experimental import pallas as pl
from jax.experimental.pallas import tpu as pltpu

F32 = jnp.float32
BF16 = jnp.bfloat16

D_MODEL = 1024
BATCH = 16
SEQ = 256
DEPTH = 4
DEC_BATCH = 2
DEC_SEQ = 2048
PAST_LEN = 512
GRID_W = 64
EPS = 1e-6
HY_WIDTH = D_MODEL
FILTER_BANDS = 16
FILTER_EMB = 1 + 2 * FILTER_BANDS
FILTER_HIDDEN = 64
FAST_DECAY_PCT = 0.3
SLOW_DECAY_PCT = 1.5
DECAY_TARGET = 1e-2
N_HEADS = 16
Q_LORA = 384
KV_LORA = 256
QK_NOPE = 64
QK_ROPE = 32
V_HEAD = 64
ROPE_THETA = 10000.0

LANES = 128
MXU_TILE = 256
HEAD_PAD = LANES
ROWS_P = BATCH * SEQ
ROWS_S = DEC_BATCH * DEC_SEQ
ROWS = ROWS_P + ROWS_S
TM = 512
N_TILES = ROWS // TM
TILES_P = ROWS_P // TM
TILES_PER_DEC_SEQ = DEC_SEQ // TM
N_COND = 8
SEQ_BLOCK = 2048
TK = 512
LCONV_TC = 256
VMEM_CAP = 56 * 1024 * 1024


def _cparams(sem, vmem_bytes):
    return pltpu.CompilerParams(dimension_semantics=sem, vmem_limit_bytes=min(int(vmem_bytes), VMEM_CAP))


def _resident(shape, index_map):
    return pl.BlockSpec(shape, index_map, pipeline_mode=pl.Buffered(1))


def _entry(arr, idx, resident=False):
    zeros = (0,) * (arr.ndim - 1)
    return pl.BlockSpec((None,) + arr.shape[1:], lambda *_: (idx,) + zeros,
                        pipeline_mode=pl.Buffered(1) if resident else None)


def _cond_row(i, rows=TM):
    return jnp.where(i < ROWS_P // rows, 0, 1 + (i - ROWS_P // rows) // (DEC_SEQ // rows))


def _silu(x):
    return x * jax.nn.sigmoid(x)


def _rms(x, g):
    return x * lax.rsqrt(jnp.mean(x * x, axis=-1, keepdims=True) + EPS) * g


def _row_pair(x):
    if isinstance(x, tuple):
        return x[0], x[1], 0
    return x, x, ROWS_P


def _row_pair_specs(x, n, rows=TM):
    _, _, base = _row_pair(x)
    n_lo = ROWS_P // rows
    lo = pl.BlockSpec((rows, n), lambda i, *_: (jnp.minimum(i, n_lo - 1), 0))
    hi = pl.BlockSpec((rows, n), lambda i, *_: (jnp.maximum(i - n_lo, 0) + base // rows, 0))
    return lo, hi


def _pick_rows(i, lo_ref, hi_ref, rows=TM):
    return jnp.where(i < ROWS_P // rows, lo_ref[...], hi_ref[...])


def _ada_kernel(cond_ref, w_ref, b_ref, o_ref):
    s = _silu(cond_ref[...]).astype(BF16)
    o_ref[...] = jnp.dot(s, w_ref[...].astype(BF16), preferred_element_type=F32) + b_ref[...]


def _ada_all(cond, ada_w, ada_b):
    tn = 1024
    return pl.pallas_call(
        _ada_kernel,
        name="ada",
        grid=(DEPTH, 3 * D_MODEL // tn),
        in_specs=[
            pl.BlockSpec((N_COND, D_MODEL), lambda l, j: (0, 0)),
            pl.BlockSpec((None, D_MODEL, tn), lambda l, j: (l, 0, j)),
            pl.BlockSpec((None, 1, tn), lambda l, j: (l, 0, j)),
        ],
        out_specs=pl.BlockSpec((None, N_COND, tn), lambda l, j: (l, 0, j)),
        out_shape=jax.ShapeDtypeStruct((DEPTH, N_COND, 3 * D_MODEL), F32),
        compiler_params=_cparams(("arbitrary", "arbitrary"), 32 << 20),
    )(cond, ada_w, ada_b.reshape(DEPTH, 1, 3 * D_MODEL))


def _mod_spec(layer, part):
    return pl.BlockSpec((None, N_COND, D_MODEL), lambda i, *_: (layer, 0, part))


def _modnorm(i, x, nw_ref, sh_ref, sc_ref):
    c = _cond_row(i)
    return _rms(x, nw_ref[...]) * (1.0 + sc_ref[pl.ds(c, 1), :]) + sh_ref[pl.ds(c, 1), :]


HALO = 16
HY_ROWS = TM


def _halo_specs(x):
    _, _, base = _row_pair(x)
    r = HY_ROWS // HALO
    n_lo = ROWS_P // HY_ROWS
    lo_last = ROWS_P // HALO - 1
    hi_last = ROWS_S // HALO - 1
    lo_blk = lambda i: jnp.minimum(i, n_lo - 1)
    hi_blk = lambda i: jnp.maximum(i - n_lo, 0)
    blk = lambda f: pl.BlockSpec((HALO, D_MODEL), lambda i: (f(i), 0))
    return [
        blk(lambda i: jnp.maximum(lo_blk(i) * r - 1, 0)),
        blk(lambda i: jnp.maximum(hi_blk(i) * r - 1, 0) + base // HALO),
        blk(lambda i: jnp.minimum((lo_blk(i) + 1) * r, lo_last)),
        blk(lambda i: jnp.minimum((hi_blk(i) + 1) * r, hi_last) + base // HALO),
    ]


def _hy_in_kernel(xlo_ref, xhi_ref, plo_ref, phi_ref, nlo_ref, nhi_ref, nw_ref, sh_ref, sc_ref, w_ref,
                  cw_ref, cb_ref, z_ref, m_ref, h_scr, *u_scrs):
    i = pl.program_id(0)
    width = z_ref.shape[1]
    c = _cond_row(i, HY_ROWS)
    norm = lambda lo_ref, hi_ref: (_rms(_pick_rows(i, lo_ref, hi_ref, HY_ROWS), nw_ref[...])
                                   * (1.0 + sc_ref[pl.ds(c, 1), :]) + sh_ref[pl.ds(c, 1), :]).astype(BF16)
    h_scr[0:HALO, :] = norm(plo_ref, phi_ref)
    h_scr[HALO:HALO + HY_ROWS, :] = norm(xlo_ref, xhi_ref)
    h_scr[HALO + HY_ROWS:HALO + HY_ROWS + HALO, :] = norm(nlo_ref, nhi_ref)

    seq_mask = jnp.where(i < ROWS_P // HY_ROWS, SEQ - 1, DEC_SEQ - 1)
    row = lax.broadcasted_iota(jnp.int32, (TM, width), 0)
    for t in range(HY_ROWS // TM):
        r0 = t * TM
        pos = (row + (i * HY_ROWS + r0)) & seq_mask
        first = pos == 0
        last = pos == seq_mask
        rows_h = pl.ds(r0, TM + 2 * HALO)

        def conv(g, u_scr):
            cols = slice(g * width, (g + 1) * width)
            u_scr[...] = jnp.dot(h_scr[rows_h, :], w_ref[:, cols], preferred_element_type=F32)
            prev = jnp.where(first, 0.0, u_scr[HALO - 1:HALO - 1 + TM, :])
            nxt = jnp.where(last, 0.0, u_scr[HALO + 1:HALO + 1 + TM, :])
            return (cb_ref[:, cols] + prev * cw_ref[0:1, cols] + u_scr[HALO:HALO + TM, :] * cw_ref[1:2, cols]
                    + nxt * cw_ref[2:3, cols])

        u0, u1, u2 = u_scrs[3 * t:3 * t + 3]
        out_rows = pl.ds(r0, TM)
        x0 = conv(0, u0)
        z_ref[out_rows, :] = (conv(2, u2) * conv(1, u1)).astype(BF16)
        gate = jnp.dot(h_scr[pl.ds(r0 + HALO, TM), :], w_ref[:, 3 * width:4 * width], preferred_element_type=F32)
        m_ref[out_rows, :] = (x0 * _silu(gate)).astype(BF16)


def _hy_in(x, norm_w, mod, layer, w_in, conv_w, conv_b):
    j = layer // 2
    xlo, xhi, _ = _row_pair(x)
    out = pl.BlockSpec((HY_ROWS, HY_WIDTH), lambda i: (i, 0))
    n_tiles = HY_ROWS // TM
    return pl.pallas_call(
        _hy_in_kernel,
        name="hy_in",
        grid=(ROWS // HY_ROWS,),
        in_specs=[
            *_row_pair_specs(x, D_MODEL, HY_ROWS),
            *_halo_specs(x),
            _entry(norm_w, layer),
            _mod_spec(layer, 0),
            _mod_spec(layer, 1),
            _entry(w_in, j, resident=True),
            _entry(conv_w, j),
            _entry(conv_b, j),
        ],
        out_specs=[out, out],
        out_shape=[jax.ShapeDtypeStruct((ROWS, HY_WIDTH), BF16)] * 2,
        scratch_shapes=([pltpu.VMEM((HY_ROWS + 2 * HALO, D_MODEL), BF16)]
                        + [pltpu.VMEM((TM + 2 * HALO, HY_WIDTH), F32)] * (3 * n_tiles)),
        compiler_params=_cparams(("arbitrary",), 56 << 20),
    )(xlo, xhi, xlo, xhi, xlo, xhi, norm_w, mod, mod, w_in, conv_w, conv_b)


@functools.lru_cache(maxsize=None)
def _filter_consts(L):
    t = np.linspace(0.0, 1.0, L)[:, None]
    w = (2.0 * math.pi / L) * np.arange(L)[:, None]
    bands = np.linspace(1e-4, FILTER_BANDS - 1, FILTER_BANDS)[None, :]
    emb = np.concatenate([t, np.cos(bands * w), -np.sin(bands * w)], axis=-1)
    emb = np.pad(emb, ((0, 0), (0, FILTER_HIDDEN - FILTER_EMB)))
    emb = np.concatenate([emb[:L // 2], emb[L // 2:]], axis=1)
    max_decay = math.log(DECAY_TARGET) / FAST_DECAY_PCT
    min_decay = math.log(DECAY_TARGET) / SLOW_DECAY_PCT
    deltas = np.abs(np.linspace(min_decay, max_decay, HY_WIDTH))
    deltas = np.concatenate([deltas, deltas])[None, :]
    return emb.astype(np.float32), t.astype(np.float32), deltas.astype(np.float32)


def _filter_kernel(emb_ref, t_ref, w1_ref, b1_ref, fr_ref, w2_ref, b2_ref, w3_ref, dl_ref, o_ref, h_scr):
    @pl.when(pl.program_id(0) == 0)
    def _():
        hi = lax.Precision.HIGHEST
        fr = fr_ref[...]
        h = jnp.sin(fr * (jnp.dot(emb_ref[...], w1_ref[...], precision=hi, preferred_element_type=F32) + b1_ref[...]))
        h = jnp.sin(fr * (jnp.dot(h, w2_ref[...], precision=hi, preferred_element_type=F32) + b2_ref[...]))
        h_scr[...] = h.astype(BF16)

    w3 = w3_ref[...].astype(BF16)
    zero = jnp.zeros_like(w3)
    hid = h_scr[...]
    h = jnp.concatenate([jnp.dot(hid, jnp.concatenate([w3, zero], axis=0), preferred_element_type=F32),
                         jnp.dot(hid, jnp.concatenate([zero, w3], axis=0), preferred_element_type=F32)], axis=0)
    h = h * jnp.exp(-t_ref[...] * dl_ref[...])
    o_ref[...] = h / jnp.sum(jnp.abs(h), axis=0, keepdims=True)


def _hy_filter(L, hy, w1, b1, freq, w2, b2, w3):
    emb, t, deltas = _filter_consts(L)
    tcf = 512
    two = 2 * FILTER_HIDDEN
    full = lambda shape: pl.BlockSpec(shape, lambda c: (0, 0))
    return pl.pallas_call(
        _filter_kernel,
        name=f"hy_filter{L}",
        grid=(2 * HY_WIDTH // tcf,),
        in_specs=[
            full((L // 2, two)), full((L, 1)), _entry(w1, hy), _entry(b1, hy), _entry(freq, hy),
            _entry(w2, hy), _entry(b2, hy),
            pl.BlockSpec((None, FILTER_HIDDEN, tcf), lambda c: (hy, 0, c)),
            pl.BlockSpec((1, tcf), lambda c: (0, c)),
        ],
        out_specs=pl.BlockSpec((L, tcf), lambda c: (0, c)),
        out_shape=jax.ShapeDtypeStruct((L, 2 * HY_WIDTH), F32),
        scratch_shapes=[pltpu.VMEM((L // 2, two), BF16)],
        compiler_params=_cparams(("arbitrary",), 40 << 20),
    )(jnp.asarray(emb), jnp.asarray(t), w1, b1, freq, w2, b2, w3, jnp.asarray(deltas))


@functools.lru_cache(maxsize=None)
def _dft_consts(L):
    k = np.arange(L, dtype=np.int64)
    ang = (np.outer(k, k) % (2 * L)).astype(np.float64) * (math.pi / L)
    return np.cos(ang).astype(np.float32), np.sin(ang).astype(np.float32)


def _dft_mats(L):
    c, s = _dft_consts(L)
    return jnp.asarray(c).astype(BF16), jnp.asarray(s).astype(BF16)


def _alt_sign(shape):
    return (1 - 2 * (lax.broadcasted_iota(jnp.int32, shape, 0) & 1)).astype(F32)


def _lconv_kernel(z_ref, m_ref, ff_ref, fb_ref, bias_ref, c_ref, s_ref, t_ref, a_scr, b_scr, nq_scr, y_scr,
                  *, L, tk):
    rows, tc = y_scr.shape
    alt = _alt_sign((L, tc))
    bins = [slice(k0, k0 + tk) for k0 in range(0, L, tk)]

    @pl.when(pl.program_id(1) == 0)
    def _():
        inv_n = 1.0 / (2.0 * L)
        bias = bias_ref[...]
        ff = ff_ref[...]
        fb = fb_ref[...]
        f = ff + fb
        nq_scr[...] = jnp.broadcast_to((jnp.sum(f * alt, axis=0, keepdims=True) + bias) * inv_n, nq_scr.shape)
        f_b = f.astype(BF16)
        d_b = (fb - ff).astype(BF16)
        for kt in bins:
            hre = jnp.dot(c_ref[kt, :], f_b, preferred_element_type=F32) + bias
            him = jnp.dot(s_ref[kt, :], d_b, preferred_element_type=F32)
            a = hre * (2.0 * inv_n)
            if kt.start == 0:
                a = jnp.where(lax.broadcasted_iota(jnp.int32, hre.shape, 0) == 0, hre * inv_n, a)
            a_scr[kt, :] = a
            b_scr[kt, :] = him * (2.0 * inv_n)

    for s0 in range(0, rows, L):
        sq = pl.ds(s0, L)
        zb = z_ref[sq, :]
        z_nyq = jnp.sum(zb.astype(F32) * alt, axis=0, keepdims=True)
        y_scr[sq, :] = alt * (z_nyq * nq_scr[0:1, :])
        for kt in bins:
            zr = jnp.dot(c_ref[kt, :], zb, preferred_element_type=F32)
            zi = jnp.dot(s_ref[kt, :], zb, preferred_element_type=F32)
            a = a_scr[kt, :]
            b = b_scr[kt, :]
            yr = (zr * a + zi * b).astype(BF16)
            yw = (zi * a - zr * b).astype(BF16)
            y_scr[sq, :] += (jnp.dot(c_ref[:, kt], yr, preferred_element_type=F32)
                             + jnp.dot(s_ref[:, kt], yw, preferred_element_type=F32))
        t_ref[sq, :] = (y_scr[sq, :] * m_ref[sq, :].astype(F32)).astype(BF16)


def _hy_lconv(L, hy, row_block0, n_row_blocks, z, m, filt, f_bias, c, s):
    tk = min(TK, L)
    long_seq = L == SEQ_BLOCK
    tc = 2 * LCONV_TC if long_seq else LCONV_TC
    nc = HY_WIDTH // tc
    blk = pl.BlockSpec((SEQ_BLOCK, tc), lambda j, r: (r + row_block0, j))
    mat = _resident((L, L), lambda j, r: (0, 0))
    filt_mode = pl.Buffered(1) if long_seq else None
    return pl.pallas_call(
        functools.partial(_lconv_kernel, L=L, tk=tk),
        name=f"hy_lconv{L}",
        grid=(nc, n_row_blocks),
        in_specs=[blk, blk,
                  pl.BlockSpec((L, tc), lambda j, r: (0, j), pipeline_mode=filt_mode),
                  pl.BlockSpec((L, tc), lambda j, r: (0, j + nc), pipeline_mode=filt_mode),
                  pl.BlockSpec((None, 1, tc), lambda j, r: (hy, 0, j)),
                  mat, mat],
        out_specs=pl.BlockSpec((SEQ_BLOCK, tc), lambda j, r: (r, j)),
        out_shape=jax.ShapeDtypeStruct((n_row_blocks * SEQ_BLOCK, HY_WIDTH), BF16),
        scratch_shapes=[pltpu.VMEM((L, tc), F32), pltpu.VMEM((L, tc), F32), pltpu.VMEM((8, tc), F32),
                        pltpu.VMEM((SEQ_BLOCK, tc), F32)],
        compiler_params=_cparams(("arbitrary", "arbitrary"), 56 << 20),
    )(z, m, filt, filt, f_bias, c, s)


@functools.lru_cache(maxsize=None)
def _rope_consts():
    axis_dim = QK_ROPE // 2
    nf = axis_dim // 2
    inv = ROPE_THETA ** (-np.arange(0, axis_dim, 2, dtype=np.float64) / axis_dim)
    t = np.arange(DEC_SEQ)
    ang_r = (t // GRID_W)[:, None] * inv
    ang_c = (t % GRID_W)[:, None] * inv
    cos = np.ones((DEC_SEQ, HEAD_PAD))
    sin_up = np.zeros((DEC_SEQ, HEAD_PAD))
    sin_dn = np.zeros((DEC_SEQ, HEAD_PAD))
    for base, ang in ((QK_NOPE, ang_r), (QK_NOPE + axis_dim, ang_c)):
        cos[:, base:base + nf] = np.cos(ang)
        cos[:, base + nf:base + 2 * nf] = np.cos(ang)
        sin_up[:, base:base + nf] = -np.sin(ang)
        sin_dn[:, base + nf:base + 2 * nf] = np.sin(ang)
    return cos.astype(np.float32), sin_up.astype(np.float32), sin_dn.astype(np.float32)


ROPE_HALF = QK_ROPE // 4
Q_SCALE = math.log2(math.e) / math.sqrt(QK_NOPE + QK_ROPE)


def _mla_proj_kernel(tlo_ref, thi_ref, wout_ref, xlo_ref, xhi_ref, gprev_ref,
                     nw_ref, sh_ref, sc_ref, wa_ref, wg_ref, wpe_ref, qn_ref, kvn_ref, wqb_ref, wk_ref, wv_ref,
                     cos_ref, sup_ref, sdn_ref,
                     xn_ref, q_ref, k_ref, v_ref, sg_ref, ckv_ref, kpe_ref):
    i = pl.program_id(0)
    u = jnp.dot(_pick_rows(i, tlo_ref, thi_ref), wout_ref[...], preferred_element_type=F32)
    x = _pick_rows(i, xlo_ref, xhi_ref) + gprev_ref[pl.ds(_cond_row(i), 1), :] * u
    xn_ref[...] = x
    h = _modnorm(i, x, nw_ref, sh_ref, sc_ref).astype(BF16)
    lora = jnp.dot(h, wa_ref[...], preferred_element_type=F32)
    kpe = jnp.dot(h, wpe_ref[...], preferred_element_type=F32)
    qn = _rms(lora[:, 0:Q_LORA], qn_ref[...]).astype(BF16)
    ckv = _rms(lora[:, Q_LORA:Q_LORA + KV_LORA], kvn_ref[...])
    ckv_ref[...] = ckv
    kpe_ref[...] = kpe
    ckv_b = ckv.astype(BF16)
    q = jnp.dot(qn, wqb_ref[...], preferred_element_type=F32)
    kn = jnp.dot(ckv_b, wk_ref[...], preferred_element_type=F32)

    latent = i >= TILES_P
    cos = jnp.where(latent, cos_ref[...], 1.0)
    sup = jnp.where(latent, sup_ref[...], 0.0)
    sdn = jnp.where(latent, sdn_ref[...], 0.0)

    def rope(u, scale):
        return (u * (cos * scale) + pltpu.roll(u, HEAD_PAD - ROPE_HALF, axis=1) * (sup * scale)
                + pltpu.roll(u, ROPE_HALF, axis=1) * (sdn * scale))

    kpe_r = rope(kpe, 1.0)
    for hd in range(N_HEADS):
        cols = slice(hd * HEAD_PAD, (hd + 1) * HEAD_PAD)
        q_ref[:, cols] = rope(q[:, cols], Q_SCALE).astype(BF16)
        k_ref[:, cols] = (kn[:, cols] + kpe_r).astype(BF16)
    v_ref[...] = jnp.dot(ckv_b, wv_ref[...], preferred_element_type=F32).astype(BF16)
    sg_ref[...] = _silu(jnp.dot(h, wg_ref[...], preferred_element_type=F32))


def _mla_proj(t, w_out, x, norm_w, mod, layer, w):
    cos, sup, sdn = (jnp.asarray(c) for c in _rope_consts())
    j = layer // 2
    hp = N_HEADS * HEAD_PAD
    tlo, thi, _ = _row_pair(t)
    xlo, xhi, _ = _row_pair(x)
    rope_blk = pl.BlockSpec(
        (TM, HEAD_PAD), lambda i: (jnp.where(i >= TILES_P, (i - TILES_P) % TILES_PER_DEC_SEQ, 0), 0))
    tile = lambda n: pl.BlockSpec((TM, n), lambda i: (i, 0))
    return pl.pallas_call(
        _mla_proj_kernel,
        name="mla_proj",
        grid=(N_TILES,),
        in_specs=[
            *_row_pair_specs(t, HY_WIDTH),
            _entry(w_out, (layer - 1) // 2, True),
            *_row_pair_specs(x, D_MODEL),
            _mod_spec(layer - 1, 2),
            _entry(norm_w, layer),
            _mod_spec(layer, 0), _mod_spec(layer, 1),
            _entry(w["w_a"], j, True), _entry(w["w_g"], j, True), _entry(w["w_pe"], j, True),
            _entry(w["q_norm"], j), _entry(w["kv_norm"], j),
            _entry(w["w_qb"], j, True), _entry(w["w_k"], j, True), _entry(w["w_v"], j, True),
            rope_blk, rope_blk, rope_blk,
        ],
        out_specs=[tile(D_MODEL), tile(hp), tile(hp), tile(N_HEADS * V_HEAD),
                   tile(N_HEADS * V_HEAD), tile(KV_LORA), tile(HEAD_PAD)],
        out_shape=[
            jax.ShapeDtypeStruct((ROWS, D_MODEL), F32),
            jax.ShapeDtypeStruct((ROWS, hp), BF16),
            jax.ShapeDtypeStruct((ROWS, hp), BF16),
            jax.ShapeDtypeStruct((ROWS, N_HEADS * V_HEAD), BF16),
            jax.ShapeDtypeStruct((ROWS, N_HEADS * V_HEAD), F32),
            jax.ShapeDtypeStruct((ROWS, KV_LORA), F32),
            jax.ShapeDtypeStruct((ROWS, HEAD_PAD), F32),
        ],
        compiler_params=_cparams(("arbitrary",), 56 << 20),
    )(tlo, thi, w_out, xlo, xhi, mod, norm_w, mod, mod, w["w_a"], w["w_g"], w["w_pe"], w["q_norm"],
      w["kv_norm"], w["w_qb"], w["w_k"], w["w_v"], cos, sup, sdn)


def _mla_ctx_kernel(ckv_ref, kpe_ref, wk_ref, wv_ref, k_ref, v_ref):
    ckv_b = ckv_ref[...].astype(BF16)
    kn = jnp.dot(ckv_b, wk_ref[...], preferred_element_type=F32)
    v_ref[...] = jnp.dot(ckv_b, wv_ref[...], preferred_element_type=F32).astype(BF16)
    kpe = kpe_ref[...]
    for hd in range(N_HEADS):
        cols = slice(hd * HEAD_PAD, (hd + 1) * HEAD_PAD)
        k_ref[:, cols] = (kn[:, cols] + kpe).astype(BF16)


def _mla_ctx(j, ckv_ctx, kpe_ctx, w):
    hp = N_HEADS * HEAD_PAD
    rows = DEC_BATCH * PAST_LEN
    tile = lambda n: pl.BlockSpec((PAST_LEN, n), lambda i: (i, 0))
    cache = lambda n: pl.BlockSpec((None, None, PAST_LEN, n), lambda i: (i, j, 0, 0))
    return pl.pallas_call(
        _mla_ctx_kernel,
        name="mla_ctx",
        grid=(DEC_BATCH,),
        in_specs=[cache(KV_LORA), cache(HEAD_PAD), _entry(w["w_k"], j, True), _entry(w["w_v"], j, True)],
        out_specs=[tile(hp), tile(N_HEADS * V_HEAD)],
        out_shape=[jax.ShapeDtypeStruct((rows, hp), BF16),
                   jax.ShapeDtypeStruct((rows, N_HEADS * V_HEAD), BF16)],
        compiler_params=_cparams(("arbitrary",), 32 << 20),
    )(ckv_ctx, kpe_ctx, w["w_k"], w["w_v"])


NT_DIMS = (((1,), (1,)), ((), ()))
ATTN_PAIRS = 4
ATTN_TQ = 256
PROMPT_SEQS = 2


def _attn_kernel(*refs, n_pairs, n_groups, has_ctx, final, n_seq=1):
    refs = list(refs)
    q_ref, k_ref, v_ref = refs[:3]
    del refs[:3]
    if has_ctx:
        kc_ref, vc_ref = refs[:2]
        del refs[:2]
    sg_ref, wo_ref, x_ref, g_ref = refs[:4]
    del refs[:4]
    if final:
        fg_ref = refs.pop(0)
    out_ref, o_scr = refs[:2]
    tq = q_ref.shape[0] // n_seq
    lk = k_ref.shape[0] // n_seq
    low_half = lax.broadcasted_iota(jnp.int32, (tq, LANES), 1) < V_HEAD
    for sq, p in [(sq, p) for sq in range(n_seq) for p in range(n_pairs)]:
        qrows = pl.ds(sq * tq, tq)
        krows = pl.ds(sq * lk, lk)
        vcols = slice(p * LANES, (p + 1) * LANES)
        vp = v_ref[krows, vcols]
        outs = []

        def head_scores(hh):
            cols = slice((2 * p + hh) * HEAD_PAD, (2 * p + hh + 1) * HEAD_PAD)
            q = q_ref[qrows, cols]
            s = lax.dot_general(q, k_ref[krows, cols], NT_DIMS, preferred_element_type=F32)
            sc = lax.dot_general(q, kc_ref[:, cols], NT_DIMS, preferred_element_type=F32) if has_ctx else None
            return s, sc

        scores = [head_scores(0), head_scores(1)] if has_ctx else None
        for hh in range(2):
            s, sc = scores[hh] if has_ctx else head_scores(hh)
            mx = jnp.max(s, axis=-1, keepdims=True)
            if has_ctx:
                mx = jnp.maximum(mx, jnp.max(sc, axis=-1, keepdims=True))
            e = jnp.exp2(s - mx)
            den = jnp.sum(e, axis=-1, keepdims=True)
            pv = jnp.dot(e.astype(BF16), vp, preferred_element_type=F32)
            if has_ctx:
                ec = jnp.exp2(sc - mx)
                den = den + jnp.sum(ec, axis=-1, keepdims=True)
                pv = pv + jnp.dot(ec.astype(BF16), vc_ref[:, vcols], preferred_element_type=F32)
            outs.append(pv / den)
        o = jnp.where(low_half, outs[0], outs[1])
        o_scr[qrows, vcols] = (o * sg_ref[qrows, vcols]).astype(BF16)

    u = jnp.dot(o_scr[...], wo_ref[...], preferred_element_type=F32)
    cond = 1 + pl.program_id(0) if has_ctx else 0

    def finish(u_all):
        xn = x_ref[...] + g_ref[pl.ds(cond, 1), :] * u_all
        out_ref[...] = _rms(xn, fg_ref[...]) if final else xn

    if n_groups == 1:
        finish(u)
    else:
        acc_scr = refs[2]
        grp = pl.program_id(2)

        @pl.when(grp == 0)
        def _():
            acc_scr[...] = u

        @pl.when((grp > 0) & (grp < n_groups - 1))
        def _():
            acc_scr[...] += u

        @pl.when(grp == n_groups - 1)
        def _():
            finish(acc_scr[...] + u)


def _attn_prompt(q, k, v, sg, w_o, x, mod, layer, final_g):
    hp = N_HEADS * HEAD_PAD
    nv = N_HEADS * V_HEAD
    xlo, _, _ = _row_pair(x)
    rows = PROMPT_SEQS * SEQ
    seq = lambda n: pl.BlockSpec((rows, n), lambda b: (b, 0))
    final = final_g is not None
    return pl.pallas_call(
        functools.partial(_attn_kernel, n_pairs=N_HEADS // 2, n_groups=1, has_ctx=False, final=final,
                          n_seq=PROMPT_SEQS),
        name="attn_prompt",
        grid=(BATCH // PROMPT_SEQS,),
        in_specs=[seq(hp), seq(hp), seq(nv), seq(nv), _entry(w_o, layer // 2, True), seq(D_MODEL),
                  _mod_spec(layer, 2)] + ([pl.BlockSpec((1, D_MODEL), lambda b: (0, 0))] if final else []),
        out_specs=seq(D_MODEL),
        out_shape=jax.ShapeDtypeStruct((ROWS_P, D_MODEL), F32),
        scratch_shapes=[pltpu.VMEM((rows, nv), BF16)],
        compiler_params=_cparams(("arbitrary",), 40 << 20),
    )(q, k, v, sg, w_o, xlo, mod, *([final_g] if final else []))


def _attn_latent(q, k, v, kc, vc, sg, w_o, x, mod, layer, final_g):
    tq = ATTN_TQ
    npair = ATTN_PAIRS
    ngrp = N_HEADS // 2 // npair
    tiles = DEC_SEQ // tq
    q0 = ROWS_P // tq
    s0 = ROWS_P // DEC_SEQ
    wide = npair * 2 * HEAD_PAD
    narrow = npair * LANES
    _, xhi, base = _row_pair(x)
    x0 = base // tq
    final = final_g is not None
    qrow = lambda b, t, p: (q0 + b * tiles + t, p)
    return pl.pallas_call(
        functools.partial(_attn_kernel, n_pairs=npair, n_groups=ngrp, has_ctx=True, final=final),
        name="attn_latent",
        grid=(DEC_BATCH, tiles, ngrp),
        in_specs=[pl.BlockSpec((tq, wide), qrow),
                  pl.BlockSpec((DEC_SEQ, wide), lambda b, t, p: (s0 + b, p)),
                  pl.BlockSpec((DEC_SEQ, narrow), lambda b, t, p: (s0 + b, p)),
                  pl.BlockSpec((PAST_LEN, wide), lambda b, t, p: (b, p)),
                  pl.BlockSpec((PAST_LEN, narrow), lambda b, t, p: (b, p)),
                  pl.BlockSpec((tq, narrow), qrow),
                  pl.BlockSpec((None, narrow, D_MODEL), lambda b, t, p: (layer // 2, p, 0)),
                  pl.BlockSpec((tq, D_MODEL), lambda b, t, p: (x0 + b * tiles + t, 0)),
                  _mod_spec(layer, 2)] + ([pl.BlockSpec((1, D_MODEL), lambda b, t, p: (0, 0))] if final else []),
        out_specs=pl.BlockSpec((tq, D_MODEL), lambda b, t, p: (b * tiles + t, 0)),
        out_shape=jax.ShapeDtypeStruct((ROWS_S, D_MODEL), F32),
        scratch_shapes=[pltpu.VMEM((tq, narrow), BF16), pltpu.VMEM((tq, D_MODEL), F32)],
        compiler_params=_cparams(("arbitrary", "arbitrary", "arbitrary"), 48 << 20),
    )(q, k, v, kc, vc, sg, w_o, xhi, mod, *([final_g] if final else []))


def _mla_weights(w_in, q_norm, w_qb, kv_norm, w_kvb, w_o):
    n = w_in.shape[0]
    o_pe = Q_LORA + KV_LORA
    o_gate = o_pe + QK_ROPE
    w_pe = jnp.pad(w_in[..., o_pe:o_gate], ((0, 0), (0, 0), (QK_NOPE, HEAD_PAD - QK_NOPE - QK_ROPE)))
    qb = w_qb.reshape(n, Q_LORA, N_HEADS, QK_NOPE + QK_ROPE)
    qb = jnp.pad(qb, ((0, 0), (0, 0), (0, 0), (0, HEAD_PAD - QK_NOPE - QK_ROPE)))
    kvb = w_kvb.reshape(n, KV_LORA, N_HEADS, QK_NOPE + V_HEAD)
    wk = jnp.pad(kvb[..., :QK_NOPE], ((0, 0), (0, 0), (0, 0), (0, HEAD_PAD - QK_NOPE)))
    wv = kvb[..., QK_NOPE:]
    return {
        "w_a": w_in[..., :o_pe].astype(BF16),
        "w_g": w_in[..., o_gate:].astype(BF16),
        "w_pe": w_pe.astype(BF16),
        "q_norm": q_norm[:, None, :],
        "kv_norm": kv_norm[:, None, :],
        "w_qb": qb.reshape(n, Q_LORA, N_HEADS * HEAD_PAD).astype(BF16),
        "w_k": wk.reshape(n, KV_LORA, N_HEADS * HEAD_PAD).astype(BF16),
        "w_v": wv.reshape(n, KV_LORA, N_HEADS * V_HEAD).astype(BF16),
        "w_o": w_o.astype(BF16),
    }


def kernel(x_prompt, x_sample, cache_ckv, cache_kpe, c, c_ctx, norm_w, ada_w, ada_b, hy_w_in, hy_conv_w, hy_conv_b, hy_f_w1, hy_f_b1, hy_f_freq, hy_f_w2, hy_f_b2, hy_f_w3, hy_f_bias, hy_w_out, mla_w_in, mla_q_norm, mla_w_qb, mla_kv_norm, mla_w_kvb, mla_w_o, final_norm):
    x = (x_prompt.reshape(ROWS_P, D_MODEL), x_sample.reshape(ROWS_S, D_MODEL))
    cond = jnp.concatenate([c_ctx[None, :], c, jnp.zeros((N_COND - 1 - DEC_BATCH, D_MODEL), F32)], axis=0)
    mod = _ada_all(cond, ada_w, ada_b)

    nw = norm_w[:, None, :]
    hy_w_in_b = hy_w_in.astype(BF16)
    hy_w_out_b = hy_w_out.astype(BF16)
    hy_conv_b3 = hy_conv_b[:, None, :]
    fh = FILTER_HIDDEN
    twice = lambda w: (jnp.pad(w, ((0, 0), (0, 2 * fh - w.shape[1]), (0, fh)))
                       + jnp.pad(w, ((0, 0), (fh, fh - w.shape[1]), (fh, 0))))
    f_w1, f_w2 = twice(hy_f_w1), twice(hy_f_w2)
    f_b1, f_freq, f_b2 = (jnp.tile(a, (1, 2))[:, None, :] for a in (hy_f_b1, hy_f_freq, hy_f_b2))
    f_bias = hy_f_bias[:, None, :]
    w = _mla_weights(mla_w_in, mla_q_norm, mla_w_qb, mla_kv_norm, mla_w_kvb, mla_w_o)
    kpe_ctx = jnp.pad(cache_kpe, ((0, 0), (0, 0), (0, 0), (QK_NOPE, HEAD_PAD - QK_NOPE - QK_ROPE)))

    new_ckv, new_kpe = [], []
    for layer in range(DEPTH):
        j = layer // 2
        if layer % 2 == 0:
            z, m = _hy_in(x, nw, mod, layer, hy_w_in_b, hy_conv_w, hy_conv_b3)
            t = []
            for L, blk0, nblk in ((DEC_SEQ, ROWS_P // SEQ_BLOCK, ROWS_S // SEQ_BLOCK), (SEQ, 0, ROWS_P // SEQ_BLOCK)):
                filt = _hy_filter(L, j, f_w1, f_b1, f_freq, f_w2, f_b2, hy_f_w3)
                cm, sm = _dft_mats(L)
                t.append(_hy_lconv(L, j, blk0, nblk, z, m, filt, f_bias, cm, sm))
            t = (t[1], t[0])
        else:
            x, q, k, v, sg, ckv, kpe = _mla_proj(t, hy_w_out_b, x, nw, mod, layer, w)
            kc, vc = _mla_ctx(j, cache_ckv, kpe_ctx, w)
            final_g = final_norm[None, :] if layer == DEPTH - 1 else None
            x = (_attn_prompt(q, k, v, sg, w["w_o"], x, mod, layer, final_g),
                 _attn_latent(q, k, v, kc, vc, sg, w["w_o"], x, mod, layer, final_g))
            new_ckv.append(ckv[:ROWS_P].reshape(BATCH, SEQ, KV_LORA))
            new_kpe.append(kpe[:ROWS_P, QK_NOPE:QK_NOPE + QK_ROPE].reshape(BATCH, SEQ, QK_ROPE))

    assert DEPTH % 2 == 0
    y_prompt, y_sample = x
    return (y_prompt.reshape(BATCH, SEQ, D_MODEL), y_sample.reshape(DEC_BATCH, DEC_SEQ, D_MODEL),
            jnp.stack(new_ckv, axis=1), jnp.stack(new_kpe, axis=1))
```

```python
import functools
import math

import numpy as np
import jax
import jax.numpy as jnp
from jax import lax
from jax.experimental import pallas as pl
from jax.experimental.pallas import tpu as pltpu

F32 = jnp.float32
BF16 = jnp.bfloat16

D_MODEL = 1024
BATCH = 16
SEQ = 256
DEPTH = 4
DEC_BATCH = 2
DEC_SEQ = 2048
PAST_LEN = 512
GRID_W = 64
EPS = 1e-6
HY_WIDTH = D_MODEL
FILTER_BANDS = 16
FILTER_EMB = 1 + 2 * FILTER_BANDS
FILTER_HIDDEN = 64
FAST_DECAY_PCT = 0.3
SLOW_DECAY_PCT = 1.5
DECAY_TARGET = 1e-2
N_HEADS = 16
Q_LORA = 384
KV_LORA = 256
QK_NOPE = 64
QK_ROPE = 32
V_HEAD = 64
ROPE_THETA = 10000.0

LANES = 128
MXU_TILE = 256
HEAD_PAD = LANES
ROWS_P = BATCH * SEQ
ROWS_S = DEC_BATCH * DEC_SEQ
ROWS = ROWS_P + ROWS_S
TM = 512
N_TILES = ROWS // TM
TILES_P = ROWS_P // TM
TILES_PER_DEC_SEQ = DEC_SEQ // TM
N_COND = 8
SEQ_BLOCK = 2048
TK = 512
LCONV_TC = 256
VMEM_CAP = 56 * 1024 * 1024


def _cparams(sem, vmem_bytes):
    return pltpu.CompilerParams(dimension_semantics=sem, vmem_limit_bytes=min(int(vmem_bytes), VMEM_CAP))


def _resident(shape, index_map):
    return pl.BlockSpec(shape, index_map, pipeline_mode=pl.Buffered(1))


def _entry(arr, idx, resident=False):
    zeros = (0,) * (arr.ndim - 1)
    return pl.BlockSpec((None,) + arr.shape[1:], lambda *_: (idx,) + zeros,
                        pipeline_mode=pl.Buffered(1) if resident else None)


def _cond_row(i, rows=TM):
    return jnp.where(i < ROWS_P // rows, 0, 1 + (i - ROWS_P // rows) // (DEC_SEQ // rows))


def _silu(x):
    return x * jax.nn.sigmoid(x)


def _rms(x, g):
    return x * lax.rsqrt(jnp.mean(x * x, axis=-1, keepdims=True) + EPS) * g


def _row_pair(x):
    if isinstance(x, tuple):
        return x[0], x[1], 0
    return x, x, ROWS_P


def _row_pair_specs(x, n, rows=TM):
    _, _, base = _row_pair(x)
    n_lo = ROWS_P // rows
    lo = pl.BlockSpec((rows, n), lambda i, *_: (jnp.minimum(i, n_lo - 1), 0))
    hi = pl.BlockSpec((rows, n), lambda i, *_: (jnp.maximum(i - n_lo, 0) + base // rows, 0))
    return lo, hi


def _pick_rows(i, lo_ref, hi_ref, rows=TM):
    return jnp.where(i < ROWS_P // rows, lo_ref[...], hi_ref[...])


def _ada_kernel(cond_ref, w_ref, b_ref, o_ref):
    s = _silu(cond_ref[...]).astype(BF16)
    o_ref[...] = jnp.dot(s, w_ref[...].astype(BF16), preferred_element_type=F32) + b_ref[...]


def _ada_all(cond, ada_w, ada_b):
    tn = 1024
    return pl.pallas_call(
        _ada_kernel,
        name="ada",
        grid=(DEPTH, 3 * D_MODEL // tn),
        in_specs=[
            pl.BlockSpec((N_COND, D_MODEL), lambda l, j: (0, 0)),
            pl.BlockSpec((None, D_MODEL, tn), lambda l, j: (l, 0, j)),
            pl.BlockSpec((None, 1, tn), lambda l, j: (l, 0, j)),
        ],
        out_specs=pl.BlockSpec((None, N_COND, tn), lambda l, j: (l, 0, j)),
        out_shape=jax.ShapeDtypeStruct((DEPTH, N_COND, 3 * D_MODEL), F32),
        compiler_params=_cparams(("arbitrary", "arbitrary"), 32 << 20),
    )(cond, ada_w, ada_b.reshape(DEPTH, 1, 3 * D_MODEL))


def _mod_spec(layer, part):
    return pl.BlockSpec((None, N_COND, D_MODEL), lambda i, *_: (layer, 0, part))


def _modnorm(i, x, nw_ref, sh_ref, sc_ref):
    c = _cond_row(i)
    return _rms(x, nw_ref[...]) * (1.0 + sc_ref[pl.ds(c, 1), :]) + sh_ref[pl.ds(c, 1), :]


HALO = 16
HY_ROWS = TM
CONV_CHUNK = MXU_TILE


def _halo_specs(x):
    _, _, base = _row_pair(x)
    r = HY_ROWS // HALO
    n_lo = ROWS_P // HY_ROWS
    lo_last = ROWS_P // HALO - 1
    hi_last = ROWS_S // HALO - 1
    lo_blk = lambda i: jnp.minimum(i, n_lo - 1)
    hi_blk = lambda i: jnp.maximum(i - n_lo, 0)
    blk = lambda f: pl.BlockSpec((HALO, D_MODEL), lambda i: (f(i), 0))
    return [
        blk(lambda i: jnp.maximum(lo_blk(i) * r - 1, 0)),
        blk(lambda i: jnp.maximum(hi_blk(i) * r - 1, 0) + base // HALO),
        blk(lambda i: jnp.minimum((lo_blk(i) + 1) * r, lo_last)),
        blk(lambda i: jnp.minimum((hi_blk(i) + 1) * r, hi_last) + base // HALO),
    ]


def _hy_in_kernel(xlo_ref, xhi_ref, plo_ref, phi_ref, nlo_ref, nhi_ref, nw_ref, sh_ref, sc_ref, w_ref,
                  cw_ref, cb_ref, z_ref, m_ref, h_scr, *u_scrs):
    i = pl.program_id(0)
    width = z_ref.shape[1]
    c = _cond_row(i, HY_ROWS)
    norm = lambda lo_ref, hi_ref: (_rms(_pick_rows(i, lo_ref, hi_ref, HY_ROWS), nw_ref[...])
                                   * (1.0 + sc_ref[pl.ds(c, 1), :]) + sh_ref[pl.ds(c, 1), :]).astype(BF16)
    h_scr[0:HALO, :] = norm(plo_ref, phi_ref)
    h_scr[HALO:HALO + HY_ROWS, :] = norm(xlo_ref, xhi_ref)
    h_scr[HALO + HY_ROWS:HALO + HY_ROWS + HALO, :] = norm(nlo_ref, nhi_ref)

    seq_mask = jnp.where(i < ROWS_P // HY_ROWS, SEQ - 1, DEC_SEQ - 1)
    for t in range(HY_ROWS // TM):
        r0 = t * TM
        rows_h = pl.ds(r0, TM + 2 * HALO)
        pos = (lax.broadcasted_iota(jnp.int32, (TM, CONV_CHUNK), 0) + (i * HY_ROWS + r0)) & seq_mask
        first_c = pos == 0
        last_c = pos == seq_mask

        def project(g, u_scr):
            u_scr[...] = jnp.dot(h_scr[rows_h, :], w_ref[:, g * width:(g + 1) * width],
                                 preferred_element_type=F32)

        def conv(g, u_scr, c0):
            lanes = slice(c0, c0 + CONV_CHUNK)
            wl = slice(g * width + c0, g * width + c0 + CONV_CHUNK)
            prev = jnp.where(first_c, 0.0, u_scr[HALO - 1:HALO - 1 + TM, lanes])
            nxt = jnp.where(last_c, 0.0, u_scr[HALO + 1:HALO + 1 + TM, lanes])
            return (cb_ref[:, wl] + prev * cw_ref[0:1, wl] + u_scr[HALO:HALO + TM, lanes] * cw_ref[1:2, wl]
                    + nxt * cw_ref[2:3, wl])

        u0, u1, u2 = u_scrs[3 * t:3 * t + 3]
        out_rows = pl.ds(r0, TM)
        chunks = range(0, width, CONV_CHUNK)
        project(0, u0)
        project(2, u2)
        project(1, u1)
        for c0 in chunks:
            z_ref[out_rows, c0:c0 + CONV_CHUNK] = (conv(2, u2, c0) * conv(1, u1, c0)).astype(BF16)
        for c0 in chunks:
            gate = jnp.dot(h_scr[pl.ds(r0 + HALO, TM), :],
                           w_ref[:, 3 * width + c0:3 * width + c0 + CONV_CHUNK], preferred_element_type=F32)
            m_ref[out_rows, c0:c0 + CONV_CHUNK] = (conv(0, u0, c0) * _silu(gate)).astype(BF16)


def _hy_in(x, norm_w, mod, layer, w_in, conv_w, conv_b):
    j = layer // 2
    xlo, xhi, _ = _row_pair(x)
    out = pl.BlockSpec((HY_ROWS, HY_WIDTH), lambda i: (i, 0))
    n_tiles = HY_ROWS // TM
    return pl.pallas_call(
        _hy_in_kernel,
        name="hy_in",
        grid=(ROWS // HY_ROWS,),
        in_specs=[
            *_row_pair_specs(x, D_MODEL, HY_ROWS),
            *_halo_specs(x),
            _entry(norm_w, layer),
            _mod_spec(layer, 0),
            _mod_spec(layer, 1),
            _entry(w_in, j, resident=True),
            _entry(conv_w, j),
            _entry(conv_b, j),
        ],
        out_specs=[out, out],
        out_shape=[jax.ShapeDtypeStruct((ROWS, HY_WIDTH), BF16)] * 2,
        scratch_shapes=([pltpu.VMEM((HY_ROWS + 2 * HALO, D_MODEL), BF16)]
                        + [pltpu.VMEM((TM + 2 * HALO, HY_WIDTH), F32)] * (3 * n_tiles)),
        compiler_params=_cparams(("arbitrary",), 56 << 20),
    )(xlo, xhi, xlo, xhi, xlo, xhi, norm_w, mod, mod, w_in, conv_w, conv_b)


@functools.lru_cache(maxsize=None)
def _filter_consts(L):
    t = np.linspace(0.0, 1.0, L)[:, None]
    w = (2.0 * math.pi / L) * np.arange(L)[:, None]
    bands = np.linspace(1e-4, FILTER_BANDS - 1, FILTER_BANDS)[None, :]
    emb = np.concatenate([t, np.cos(bands * w), -np.sin(bands * w)], axis=-1)
    emb = np.pad(emb, ((0, 0), (0, FILTER_HIDDEN - FILTER_EMB)))
    emb = np.concatenate([emb[:L // 2], emb[L // 2:]], axis=1)
    max_decay = math.log(DECAY_TARGET) / FAST_DECAY_PCT
    min_decay = math.log(DECAY_TARGET) / SLOW_DECAY_PCT
    deltas = np.abs(np.linspace(min_decay, max_decay, HY_WIDTH))
    deltas = np.concatenate([deltas, deltas])[None, :]
    return emb.astype(np.float32), t.astype(np.float32), deltas.astype(np.float32)


def _filter_kernel(emb_ref, t_ref, w1_ref, b1_ref, fr_ref, w2_ref, b2_ref, w3_ref, dl_ref, o_ref, h_scr):
    @pl.when(pl.program_id(0) == 0)
    def _():
        hi = lax.Precision.HIGHEST
        fr = fr_ref[...]
        h = jnp.sin(fr * (jnp.dot(emb_ref[...], w1_ref[...], precision=hi, preferred_element_type=F32) + b1_ref[...]))
        h = jnp.sin(fr * (jnp.dot(h, w2_ref[...], precision=hi, preferred_element_type=F32) + b2_ref[...]))
        h_scr[...] = h.astype(BF16)

    w3 = w3_ref[...].astype(BF16)
    zero = jnp.zeros_like(w3)
    hid = h_scr[...]
    h = jnp.concatenate([jnp.dot(hid, jnp.concatenate([w3, zero], axis=0), preferred_element_type=F32),
                         jnp.dot(hid, jnp.concatenate([zero, w3], axis=0), preferred_element_type=F32)], axis=0)
    h = h * jnp.exp(-t_ref[...] * dl_ref[...])
    o_ref[...] = h / jnp.sum(jnp.abs(h), axis=0, keepdims=True)


def _hy_filter(L, hy, w1, b1, freq, w2, b2, w3):
    emb, t, deltas = _filter_consts(L)
    tcf = 512
    two = 2 * FILTER_HIDDEN
    full = lambda shape: pl.BlockSpec(shape, lambda c: (0, 0))
    return pl.pallas_call(
        _filter_kernel,
        name=f"hy_filter{L}",
        grid=(2 * HY_WIDTH // tcf,),
        in_specs=[
            full((L // 2, two)), full((L, 1)), _entry(w1, hy), _entry(b1, hy), _entry(freq, hy),
            _entry(w2, hy), _entry(b2, hy),
            pl.BlockSpec((None, FILTER_HIDDEN, tcf), lambda c: (hy, 0, c)),
            pl.BlockSpec((1, tcf), lambda c: (0, c)),
        ],
        out_specs=pl.BlockSpec((L, tcf), lambda c: (0, c)),
        out_shape=jax.ShapeDtypeStruct((L, 2 * HY_WIDTH), F32),
        scratch_shapes=[pltpu.VMEM((L // 2, two), BF16)],
        compiler_params=_cparams(("arbitrary",), 40 << 20),
    )(jnp.asarray(emb), jnp.asarray(t), w1, b1, freq, w2, b2, w3, jnp.asarray(deltas))


@functools.lru_cache(maxsize=None)
def _dft_consts(L):
    k = np.arange(L, dtype=np.int64)
    ang = (np.outer(k, k) % (2 * L)).astype(np.float64) * (math.pi / L)
    return np.cos(ang).astype(np.float32), np.sin(ang).astype(np.float32)


def _dft_mats(L):
    c, s = _dft_consts(L)
    return jnp.asarray(c).astype(BF16), jnp.asarray(s).astype(BF16)


def _alt_sign(shape):
    return (1 - 2 * (lax.broadcasted_iota(jnp.int32, shape, 0) & 1)).astype(F32)


def _lconv_kernel(z_ref, m_ref, ff_ref, fb_ref, bias_ref, c_ref, s_ref, t_ref, a_scr, b_scr, nq_scr, y_scr,
                  *, L, tk):
    rows, tc = y_scr.shape
    alt = _alt_sign((L, tc))
    bins = [slice(k0, k0 + tk) for k0 in range(0, L, tk)]

    @pl.when(pl.program_id(1) == 0)
    def _():
        inv_n = 1.0 / (2.0 * L)
        bias = bias_ref[...]
        ff = ff_ref[...]
        fb = fb_ref[...]
        f = ff + fb
        nq_scr[...] = jnp.broadcast_to((jnp.sum(f * alt, axis=0, keepdims=True) + bias) * inv_n, nq_scr.shape)
        f_b = f.astype(BF16)
        d_b = (fb - ff).astype(BF16)
        for kt in bins:
            hre = jnp.dot(c_ref[kt, :], f_b, preferred_element_type=F32) + bias
            him = jnp.dot(s_ref[kt, :], d_b, preferred_element_type=F32)
            a = hre * (2.0 * inv_n)
            if kt.start == 0:
                a = jnp.where(lax.broadcasted_iota(jnp.int32, hre.shape, 0) == 0, hre * inv_n, a)
            a_scr[kt, :] = a
            b_scr[kt, :] = him * (2.0 * inv_n)

    for s0 in range(0, rows, L):
        sq = pl.ds(s0, L)
        zb = z_ref[sq, :]
        z_nyq = jnp.sum(zb.astype(F32) * alt, axis=0, keepdims=True)
        y_scr[sq, :] = alt * (z_nyq * nq_scr[0:1, :])
        for kt in bins:
            zr = jnp.dot(c_ref[kt, :], zb, preferred_element_type=F32)
            zi = jnp.dot(s_ref[kt, :], zb, preferred_element_type=F32)
            a = a_scr[kt, :]
            b = b_scr[kt, :]
            yr = (zr * a + zi * b).astype(BF16)
            yw = (zi * a - zr * b).astype(BF16)
            y_scr[sq, :] += (jnp.dot(c_ref[:, kt], yr, preferred_element_type=F32)
                             + jnp.dot(s_ref[:, kt], yw, preferred_element_type=F32))
        t_ref[sq, :] = (y_scr[sq, :] * m_ref[sq, :].astype(F32)).astype(BF16)


def _hy_lconv(L, hy, row_block0, n_row_blocks, z, m, filt, f_bias, c, s):
    tk = min(TK, L)
    long_seq = L == SEQ_BLOCK
    tc = 2 * LCONV_TC if long_seq else LCONV_TC
    nc = HY_WIDTH // tc
    blk = pl.BlockSpec((SEQ_BLOCK, tc), lambda j, r: (r + row_block0, j))
    mat = _resident((L, L), lambda j, r: (0, 0))
    filt_mode = pl.Buffered(1) if long_seq else None
    return pl.pallas_call(
        functools.partial(_lconv_kernel, L=L, tk=tk),
        name=f"hy_lconv{L}",
        grid=(nc, n_row_blocks),
        in_specs=[blk, blk,
                  pl.BlockSpec((L, tc), lambda j, r: (0, j), pipeline_mode=filt_mode),
                  pl.BlockSpec((L, tc), lambda j, r: (0, j + nc), pipeline_mode=filt_mode),
                  pl.BlockSpec((None, 1, tc), lambda j, r: (hy, 0, j)),
                  mat, mat],
        out_specs=pl.BlockSpec((SEQ_BLOCK, tc), lambda j, r: (r, j)),
        out_shape=jax.ShapeDtypeStruct((n_row_blocks * SEQ_BLOCK, HY_WIDTH), BF16),
        scratch_shapes=[pltpu.VMEM((L, tc), F32), pltpu.VMEM((L, tc), F32), pltpu.VMEM((8, tc), F32),
                        pltpu.VMEM((SEQ_BLOCK, tc), F32)],
        compiler_params=_cparams(("arbitrary", "arbitrary"), 56 << 20),
    )(z, m, filt, filt, f_bias, c, s)


@functools.lru_cache(maxsize=None)
def _rope_consts():
    axis_dim = QK_ROPE // 2
    nf = axis_dim // 2
    inv = ROPE_THETA ** (-np.arange(0, axis_dim, 2, dtype=np.float64) / axis_dim)
    t = np.arange(DEC_SEQ)
    ang_r = (t // GRID_W)[:, None] * inv
    ang_c = (t % GRID_W)[:, None] * inv
    cos = np.ones((DEC_SEQ, HEAD_PAD))
    sin_up = np.zeros((DEC_SEQ, HEAD_PAD))
    sin_dn = np.zeros((DEC_SEQ, HEAD_PAD))
    for base, ang in ((QK_NOPE, ang_r), (QK_NOPE + axis_dim, ang_c)):
        cos[:, base:base + nf] = np.cos(ang)
        cos[:, base + nf:base + 2 * nf] = np.cos(ang)
        sin_up[:, base:base + nf] = -np.sin(ang)
        sin_dn[:, base + nf:base + 2 * nf] = np.sin(ang)
    return cos.astype(np.float32), sin_up.astype(np.float32), sin_dn.astype(np.float32)


ROPE_HALF = QK_ROPE // 4
Q_SCALE = math.log2(math.e) / math.sqrt(QK_NOPE + QK_ROPE)


def _mla_proj_kernel(tlo_ref, thi_ref, wout_ref, xlo_ref, xhi_ref, gprev_ref,
                     nw_ref, sh_ref, sc_ref, wa_ref, wg_ref, wpe_ref, qn_ref, kvn_ref, wqb_ref, wk_ref, wv_ref,
                     cos_ref, sup_ref, sdn_ref,
                     xn_ref, q_ref, k_ref, v_ref, sg_ref, ckv_ref, kpe_ref):
    i = pl.program_id(0)
    u = jnp.dot(_pick_rows(i, tlo_ref, thi_ref), wout_ref[...], preferred_element_type=F32)
    x = _pick_rows(i, xlo_ref, xhi_ref) + gprev_ref[pl.ds(_cond_row(i), 1), :] * u
    xn_ref[...] = x
    h = _modnorm(i, x, nw_ref, sh_ref, sc_ref).astype(BF16)
    lora = jnp.dot(h, wa_ref[...], preferred_element_type=F32)
    kpe = jnp.dot(h, wpe_ref[...], preferred_element_type=F32)
    qn = _rms(lora[:, 0:Q_LORA], qn_ref[...]).astype(BF16)
    ckv = _rms(lora[:, Q_LORA:Q_LORA + KV_LORA], kvn_ref[...])
    ckv_ref[...] = ckv
    kpe_ref[...] = kpe
    ckv_b = ckv.astype(BF16)
    q = jnp.dot(qn, wqb_ref[...], preferred_element_type=F32)
    kn = jnp.dot(ckv_b, wk_ref[...], preferred_element_type=F32)

    latent = i >= TILES_P
    cos = jnp.where(latent, cos_ref[...], 1.0)
    sup = jnp.where(latent, sup_ref[...], 0.0)
    sdn = jnp.where(latent, sdn_ref[...], 0.0)

    def rope(u, scale):
        return (u * (cos * scale) + pltpu.roll(u, HEAD_PAD - ROPE_HALF, axis=1) * (sup * scale)
                + pltpu.roll(u, ROPE_HALF, axis=1) * (sdn * scale))

    kpe_r = rope(kpe, 1.0)
    for hd in range(N_HEADS):
        cols = slice(hd * HEAD_PAD, (hd + 1) * HEAD_PAD)
        q_ref[:, cols] = rope(q[:, cols], Q_SCALE).astype(BF16)
        k_ref[:, cols] = (kn[:, cols] + kpe_r).astype(BF16)
    v_ref[...] = jnp.dot(ckv_b, wv_ref[...], preferred_element_type=F32).astype(BF16)
    sg_ref[...] = _silu(jnp.dot(h, wg_ref[...], preferred_element_type=F32))


def _mla_proj(t, w_out, x, norm_w, mod, layer, w):
    cos, sup, sdn = (jnp.asarray(c) for c in _rope_consts())
    j = layer // 2
    hp = N_HEADS * HEAD_PAD
    tlo, thi, _ = _row_pair(t)
    xlo, xhi, _ = _row_pair(x)
    rope_blk = pl.BlockSpec(
        (TM, HEAD_PAD), lambda i: (jnp.where(i >= TILES_P, (i - TILES_P) % TILES_PER_DEC_SEQ, 0), 0))
    tile = lambda n: pl.BlockSpec((TM, n), lambda i: (i, 0))
    return pl.pallas_call(
        _mla_proj_kernel,
        name="mla_proj",
        grid=(N_TILES,),
        in_specs=[
            *_row_pair_specs(t, HY_WIDTH),
            _entry(w_out, (layer - 1) // 2, True),
            *_row_pair_specs(x, D_MODEL),
            _mod_spec(layer - 1, 2),
            _entry(norm_w, layer),
            _mod_spec(layer, 0), _mod_spec(layer, 1),
            _entry(w["w_a"], j, True), _entry(w["w_g"], j, True), _entry(w["w_pe"], j, True),
            _entry(w["q_norm"], j), _entry(w["kv_norm"], j),
            _entry(w["w_qb"], j, True), _entry(w["w_k"], j, True), _entry(w["w_v"], j, True),
            rope_blk, rope_blk, rope_blk,
        ],
        out_specs=[tile(D_MODEL), tile(hp), tile(hp), tile(N_HEADS * V_HEAD),
                   tile(N_HEADS * V_HEAD), tile(KV_LORA), tile(HEAD_PAD)],
        out_shape=[
            jax.ShapeDtypeStruct((ROWS, D_MODEL), F32),
            jax.ShapeDtypeStruct((ROWS, hp), BF16),
            jax.ShapeDtypeStruct((ROWS, hp), BF16),
            jax.ShapeDtypeStruct((ROWS, N_HEADS * V_HEAD), BF16),
            jax.ShapeDtypeStruct((ROWS, N_HEADS * V_HEAD), F32),
            jax.ShapeDtypeStruct((ROWS, KV_LORA), F32),
            jax.ShapeDtypeStruct((ROWS, HEAD_PAD), F32),
        ],
        compiler_params=_cparams(("arbitrary",), 56 << 20),
    )(tlo, thi, w_out, xlo, xhi, mod, norm_w, mod, mod, w["w_a"], w["w_g"], w["w_pe"], w["q_norm"],
      w["kv_norm"], w["w_qb"], w["w_k"], w["w_v"], cos, sup, sdn)


def _mla_ctx_kernel(ckv_ref, kpe_ref, wk_ref, wv_ref, k_ref, v_ref):
    ckv_b = ckv_ref[...].astype(BF16)
    kn = jnp.dot(ckv_b, wk_ref[...], preferred_element_type=F32)
    v_ref[...] = jnp.dot(ckv_b, wv_ref[...], preferred_element_type=F32).astype(BF16)
    kpe = kpe_ref[...]
    for hd in range(N_HEADS):
        cols = slice(hd * HEAD_PAD, (hd + 1) * HEAD_PAD)
        k_ref[:, cols] = (kn[:, cols] + kpe).astype(BF16)


def _mla_ctx(j, ckv_ctx, kpe_ctx, w):
    hp = N_HEADS * HEAD_PAD
    rows = DEC_BATCH * PAST_LEN
    tile = lambda n: pl.BlockSpec((PAST_LEN, n), lambda i: (i, 0))
    cache = lambda n: pl.BlockSpec((None, None, PAST_LEN, n), lambda i: (i, j, 0, 0))
    return pl.pallas_call(
        _mla_ctx_kernel,
        name="mla_ctx",
        grid=(DEC_BATCH,),
        in_specs=[cache(KV_LORA), cache(HEAD_PAD), _entry(w["w_k"], j, True), _entry(w["w_v"], j, True)],
        out_specs=[tile(hp), tile(N_HEADS * V_HEAD)],
        out_shape=[jax.ShapeDtypeStruct((rows, hp), BF16),
                   jax.ShapeDtypeStruct((rows, N_HEADS * V_HEAD), BF16)],
        compiler_params=_cparams(("arbitrary",), 32 << 20),
    )(ckv_ctx, kpe_ctx, w["w_k"], w["w_v"])


NT_DIMS = (((1,), (1,)), ((), ()))
ATTN_PAIRS = 4
ATTN_TQ = 256
PROMPT_SEQS = 2


def _attn_kernel(*refs, n_pairs, n_groups, has_ctx, final, n_seq=1):
    refs = list(refs)
    q_ref, k_ref, v_ref = refs[:3]
    del refs[:3]
    if has_ctx:
        kc_ref, vc_ref = refs[:2]
        del refs[:2]
    sg_ref, wo_ref, x_ref, g_ref = refs[:4]
    del refs[:4]
    if final:
        fg_ref = refs.pop(0)
    out_ref, o_scr = refs[:2]
    tq = q_ref.shape[0] // n_seq
    lk = k_ref.shape[0] // n_seq
    low_half = lax.broadcasted_iota(jnp.int32, (tq, LANES), 1) < V_HEAD
    for sq, p in [(sq, p) for sq in range(n_seq) for p in range(n_pairs)]:
        qrows = pl.ds(sq * tq, tq)
        krows = pl.ds(sq * lk, lk)
        vcols = slice(p * LANES, (p + 1) * LANES)
        vp = v_ref[krows, vcols]
        outs = []

        def head_scores(hh):
            cols = slice((2 * p + hh) * HEAD_PAD, (2 * p + hh + 1) * HEAD_PAD)
            q = q_ref[qrows, cols]
            s = lax.dot_general(q, k_ref[krows, cols], NT_DIMS, preferred_element_type=F32)
            sc = lax.dot_general(q, kc_ref[:, cols], NT_DIMS, preferred_element_type=F32) if has_ctx else None
            return s, sc

        scores = [head_scores(0), head_scores(1)] if has_ctx else None
        for hh in range(2):
            s, sc = scores[hh] if has_ctx else head_scores(hh)
            mx = jnp.max(s, axis=-1, keepdims=True)
            if has_ctx:
                mx = jnp.maximum(mx, jnp.max(sc, axis=-1, keepdims=True))
            e = jnp.exp2(s - mx)
            den = jnp.sum(e, axis=-1, keepdims=True)
            pv = jnp.dot(e.astype(BF16), vp, preferred_element_type=F32)
            if has_ctx:
                ec = jnp.exp2(sc - mx)
                den = den + jnp.sum(ec, axis=-1, keepdims=True)
                pv = pv + jnp.dot(ec.astype(BF16), vc_ref[:, vcols], preferred_element_type=F32)
            outs.append(pv / den)
        o = jnp.where(low_half, outs[0], outs[1])
        o_scr[qrows, vcols] = (o * sg_ref[qrows, vcols]).astype(BF16)

    u = jnp.dot(o_scr[...], wo_ref[...], preferred_element_type=F32)
    cond = 1 + pl.program_id(0) if has_ctx else 0

    def finish(u_all):
        xn = x_ref[...] + g_ref[pl.ds(cond, 1), :] * u_all
        out_ref[...] = _rms(xn, fg_ref[...]) if final else xn

    if n_groups == 1:
        finish(u)
    else:
        acc_scr = refs[2]
        grp = pl.program_id(2)

        @pl.when(grp == 0)
        def _():
            acc_scr[...] = u

        @pl.when((grp > 0) & (grp < n_groups - 1))
        def _():
            acc_scr[...] += u

        @pl.when(grp == n_groups - 1)
        def _():
            finish(acc_scr[...] + u)


def _attn_prompt(q, k, v, sg, w_o, x, mod, layer, final_g):
    hp = N_HEADS * HEAD_PAD
    nv = N_HEADS * V_HEAD
    xlo, _, _ = _row_pair(x)
    rows = PROMPT_SEQS * SEQ
    seq = lambda n: pl.BlockSpec((rows, n), lambda b: (b, 0))
    final = final_g is not None
    return pl.pallas_call(
        functools.partial(_attn_kernel, n_pairs=N_HEADS // 2, n_groups=1, has_ctx=False, final=final,
                          n_seq=PROMPT_SEQS),
        name="attn_prompt",
        grid=(BATCH // PROMPT_SEQS,),
        in_specs=[seq(hp), seq(hp), seq(nv), seq(nv), _entry(w_o, layer // 2, True), seq(D_MODEL),
                  _mod_spec(layer, 2)] + ([pl.BlockSpec((1, D_MODEL), lambda b: (0, 0))] if final else []),
        out_specs=seq(D_MODEL),
        out_shape=jax.ShapeDtypeStruct((ROWS_P, D_MODEL), F32),
        scratch_shapes=[pltpu.VMEM((rows, nv), BF16)],
        compiler_params=_cparams(("arbitrary",), 40 << 20),
    )(q, k, v, sg, w_o, xlo, mod, *([final_g] if final else []))


def _attn_latent(q, k, v, kc, vc, sg, w_o, x, mod, layer, final_g):
    tq = ATTN_TQ
    npair = ATTN_PAIRS
    ngrp = N_HEADS // 2 // npair
    tiles = DEC_SEQ // tq
    q0 = ROWS_P // tq
    s0 = ROWS_P // DEC_SEQ
    wide = npair * 2 * HEAD_PAD
    narrow = npair * LANES
    _, xhi, base = _row_pair(x)
    x0 = base // tq
    final = final_g is not None
    qrow = lambda b, t, p: (q0 + b * tiles + t, p)
    return pl.pallas_call(
        functools.partial(_attn_kernel, n_pairs=npair, n_groups=ngrp, has_ctx=True, final=final),
        name="attn_latent",
        grid=(DEC_BATCH, tiles, ngrp),
        in_specs=[pl.BlockSpec((tq, wide), qrow),
                  pl.BlockSpec((DEC_SEQ, wide), lambda b, t, p: (s0 + b, p)),
                  pl.BlockSpec((DEC_SEQ, narrow), lambda b, t, p: (s0 + b, p)),
                  pl.BlockSpec((PAST_LEN, wide), lambda b, t, p: (b, p)),
                  pl.BlockSpec((PAST_LEN, narrow), lambda b, t, p: (b, p)),
                  pl.BlockSpec((tq, narrow), qrow),
                  pl.BlockSpec((None, narrow, D_MODEL), lambda b, t, p: (layer // 2, p, 0)),
                  pl.BlockSpec((tq, D_MODEL), lambda b, t, p: (x0 + b * tiles + t, 0)),
                  _mod_spec(layer, 2)] + ([pl.BlockSpec((1, D_MODEL), lambda b, t, p: (0, 0))] if final else []),
        out_specs=pl.BlockSpec((tq, D_MODEL), lambda b, t, p: (b * tiles + t, 0)),
        out_shape=jax.ShapeDtypeStruct((ROWS_S, D_MODEL), F32),
        scratch_shapes=[pltpu.VMEM((tq, narrow), BF16), pltpu.VMEM((tq, D_MODEL), F32)],
        compiler_params=_cparams(("arbitrary", "arbitrary", "arbitrary"), 48 << 20),
    )(q, k, v, kc, vc, sg, w_o, xhi, mod, *([final_g] if final else []))


def _mla_weights(w_in, q_norm, w_qb, kv_norm, w_kvb, w_o):
    n = w_in.shape[0]
    o_pe = Q_LORA + KV_LORA
    o_gate = o_pe + QK_ROPE
    w_pe = jnp.pad(w_in[..., o_pe:o_gate], ((0, 0), (0, 0), (QK_NOPE, HEAD_PAD - QK_NOPE - QK_ROPE)))
    qb = w_qb.reshape(n, Q_LORA, N_HEADS, QK_NOPE + QK_ROPE)
    qb = jnp.pad(qb, ((0, 0), (0, 0), (0, 0), (0, HEAD_PAD - QK_NOPE - QK_ROPE)))
    kvb = w_kvb.reshape(n, KV_LORA, N_HEADS, QK_NOPE + V_HEAD)
    wk = jnp.pad(kvb[..., :QK_NOPE], ((0, 0), (0, 0), (0, 0), (0, HEAD_PAD - QK_NOPE)))
    wv = kvb[..., QK_NOPE:]
    return {
        "w_a": w_in[..., :o_pe].astype(BF16),
        "w_g": w_in[..., o_gate:].astype(BF16),
        "w_pe": w_pe.astype(BF16),
        "q_norm": q_norm[:, None, :],
        "kv_norm": kv_norm[:, None, :],
        "w_qb": qb.reshape(n, Q_LORA, N_HEADS * HEAD_PAD).astype(BF16),
        "w_k": wk.reshape(n, KV_LORA, N_HEADS * HEAD_PAD).astype(BF16),
        "w_v": wv.reshape(n, KV_LORA, N_HEADS * V_HEAD).astype(BF16),
        "w_o": w_o.astype(BF16),
    }


def kernel(x_prompt, x_sample, cache_ckv, cache_kpe, c, c_ctx, norm_w, ada_w, ada_b, hy_w_in, hy_conv_w, hy_conv_b, hy_f_w1, hy_f_b1, hy_f_freq, hy_f_w2, hy_f_b2, hy_f_w3, hy_f_bias, hy_w_out, mla_w_in, mla_q_norm, mla_w_qb, mla_kv_norm, mla_w_kvb, mla_w_o, final_norm):
    x = (x_prompt.reshape(ROWS_P, D_MODEL), x_sample.reshape(ROWS_S, D_MODEL))
    cond = jnp.concatenate([c_ctx[None, :], c, jnp.zeros((N_COND - 1 - DEC_BATCH, D_MODEL), F32)], axis=0)
    mod = _ada_all(cond, ada_w, ada_b)

    nw = norm_w[:, None, :]
    hy_w_in_b = hy_w_in.astype(BF16)
    hy_w_out_b = hy_w_out.astype(BF16)
    hy_conv_b3 = hy_conv_b[:, None, :]
    fh = FILTER_HIDDEN
    twice = lambda w: (jnp.pad(w, ((0, 0), (0, 2 * fh - w.shape[1]), (0, fh)))
                       + jnp.pad(w, ((0, 0), (fh, fh - w.shape[1]), (fh, 0))))
    f_w1, f_w2 = twice(hy_f_w1), twice(hy_f_w2)
    f_b1, f_freq, f_b2 = (jnp.tile(a, (1, 2))[:, None, :] for a in (hy_f_b1, hy_f_freq, hy_f_b2))
    f_bias = hy_f_bias[:, None, :]
    w = _mla_weights(mla_w_in, mla_q_norm, mla_w_qb, mla_kv_norm, mla_w_kvb, mla_w_o)
    kpe_ctx = jnp.pad(cache_kpe, ((0, 0), (0, 0), (0, 0), (QK_NOPE, HEAD_PAD - QK_NOPE - QK_ROPE)))

    new_ckv, new_kpe = [], []
    for layer in range(DEPTH):
        j = layer // 2
        if layer % 2 == 0:
            z, m = _hy_in(x, nw, mod, layer, hy_w_in_b, hy_conv_w, hy_conv_b3)
            t = []
            for L, blk0, nblk in ((DEC_SEQ, ROWS_P // SEQ_BLOCK, ROWS_S // SEQ_BLOCK), (SEQ, 0, ROWS_P // SEQ_BLOCK)):
                filt = _hy_filter(L, j, f_w1, f_b1, f_freq, f_w2, f_b2, hy_f_w3)
                cm, sm = _dft_mats(L)
                t.append(_hy_lconv(L, j, blk0, nblk, z, m, filt, f_bias, cm, sm))
            t = (t[1], t[0])
        else:
            x, q, k, v, sg, ckv, kpe = _mla_proj(t, hy_w_out_b, x, nw, mod, layer, w)
            kc, vc = _mla_ctx(j, cache_ckv, kpe_ctx, w)
            final_g = final_norm[None, :] if layer == DEPTH - 1 else None
            x = (_attn_prompt(q, k, v, sg, w["w_o"], x, mod, layer, final_g),
                 _attn_latent(q, k, v, kc, vc, sg, w["w_o"], x, mod, layer, final_g))
            new_ckv.append(ckv[:ROWS_P].reshape(BATCH, SEQ, KV_LORA))
            new_kpe.append(kpe[:ROWS_P, QK_NOPE:QK_NOPE + QK_ROPE].reshape(BATCH, SEQ, QK_ROPE))

    assert DEPTH % 2 == 0
    y_prompt, y_sample = x
    return (y_prompt.reshape(BATCH, SEQ, D_MODEL), y_sample.reshape(DEC_BATCH, DEC_SEQ, D_MODEL),
            jnp.stack(new_ckv, axis=1), jnp.stack(new_kpe, axis=1))
```

```python
import functools
import math

import numpy as np
import jax
import jax.numpy as jnp
from jax import lax
from jax.experimental import pallas as pl
from jax.experimental.pallas import tpu as pltpu

F32 = jnp.float32
BF16 = jnp.bfloat16

D_MODEL = 1024
BATCH = 16
SEQ = 256
DEPTH = 4
DEC_BATCH = 2
DEC_SEQ = 2048
PAST_LEN = 512
GRID_W = 64
EPS = 1e-6
HY_WIDTH = D_MODEL
FILTER_BANDS = 16
FILTER_EMB = 1 + 2 * FILTER_BANDS
FILTER_HIDDEN = 64
FAST_DECAY_PCT = 0.3
SLOW_DECAY_PCT = 1.5
DECAY_TARGET = 1e-2
N_HEADS = 16
Q_LORA = 384
KV_LORA = 256
QK_NOPE = 64
QK_ROPE = 32
V_HEAD = 64
ROPE_THETA = 10000.0

LANES = 128
MXU_TILE = 256
HEAD_PAD = LANES
ROWS_P = BATCH * SEQ
ROWS_S = DEC_BATCH * DEC_SEQ
ROWS = ROWS_P + ROWS_S
TM = 512
N_TILES = ROWS // TM
TILES_P = ROWS_P // TM
TILES_PER_DEC_SEQ = DEC_SEQ // TM
N_COND = 8
SEQ_BLOCK = 2048
TK = 512
LCONV_TC = 256
VMEM_CAP = 56 * 1024 * 1024


def _cparams(sem, vmem_bytes):
    return pltpu.CompilerParams(dimension_semantics=sem, vmem_limit_bytes=min(int(vmem_bytes), VMEM_CAP))


def _resident(shape, index_map):
    return pl.BlockSpec(shape, index_map, pipeline_mode=pl.Buffered(1))


def _entry(arr, idx, resident=False):
    zeros = (0,) * (arr.ndim - 1)
    return pl.BlockSpec((None,) + arr.shape[1:], lambda *_: (idx,) + zeros,
                        pipeline_mode=pl.Buffered(1) if resident else None)


def _cond_row(i, rows=TM):
    return jnp.where(i < ROWS_P // rows, 0, 1 + (i - ROWS_P // rows) // (DEC_SEQ // rows))


def _silu(x):
    return x * jax.nn.sigmoid(x)


def _rms(x, g):
    return x * lax.rsqrt(jnp.mean(x * x, axis=-1, keepdims=True) + EPS) * g


def _row_pair(x):
    if isinstance(x, tuple):
        return x[0], x[1], 0
    return x, x, ROWS_P


def _row_pair_specs(x, n, rows=TM):
    _, _, base = _row_pair(x)
    n_lo = ROWS_P // rows
    lo = pl.BlockSpec((rows, n), lambda i, *_: (jnp.minimum(i, n_lo - 1), 0))
    hi = pl.BlockSpec((rows, n), lambda i, *_: (jnp.maximum(i - n_lo, 0) + base // rows, 0))
    return lo, hi


def _pick_rows(i, lo_ref, hi_ref, rows=TM):
    return jnp.where(i < ROWS_P // rows, lo_ref[...], hi_ref[...])


def _ada_kernel(cond_ref, w_ref, b_ref, o_ref):
    s = _silu(cond_ref[...]).astype(BF16)
    o_ref[...] = jnp.dot(s, w_ref[...].astype(BF16), preferred_element_type=F32) + b_ref[...]


def _ada_all(cond, ada_w, ada_b):
    tn = 1536
    return pl.pallas_call(
        _ada_kernel,
        name="ada",
        grid=(DEPTH, 3 * D_MODEL // tn),
        in_specs=[
            pl.BlockSpec((N_COND, D_MODEL), lambda l, j: (0, 0)),
            pl.BlockSpec((None, D_MODEL, tn), lambda l, j: (l, 0, j)),
            pl.BlockSpec((None, 1, tn), lambda l, j: (l, 0, j)),
        ],
        out_specs=pl.BlockSpec((None, N_COND, tn), lambda l, j: (l, 0, j)),
        out_shape=jax.ShapeDtypeStruct((DEPTH, N_COND, 3 * D_MODEL), F32),
        compiler_params=_cparams(("arbitrary", "arbitrary"), 32 << 20),
    )(cond, ada_w, ada_b.reshape(DEPTH, 1, 3 * D_MODEL))


def _mod_spec(layer, part):
    return pl.BlockSpec((None, N_COND, D_MODEL), lambda i, *_: (layer, 0, part))


def _modnorm(i, x, nw_ref, sh_ref, sc_ref):
    c = _cond_row(i)
    return _rms(x, nw_ref[...]) * (1.0 + sc_ref[pl.ds(c, 1), :]) + sh_ref[pl.ds(c, 1), :]


HALO = 16
HY_ROWS = TM


def _halo_specs(x):
    _, _, base = _row_pair(x)
    r = HY_ROWS // HALO
    n_lo = ROWS_P // HY_ROWS
    lo_last = ROWS_P // HALO - 1
    hi_last = ROWS_S // HALO - 1
    lo_blk = lambda i: jnp.minimum(i, n_lo - 1)
    hi_blk = lambda i: jnp.maximum(i - n_lo, 0)
    blk = lambda f: pl.BlockSpec((HALO, D_MODEL), lambda i: (f(i), 0))
    return [
        blk(lambda i: jnp.maximum(lo_blk(i) * r - 1, 0)),
        blk(lambda i: jnp.maximum(hi_blk(i) * r - 1, 0) + base // HALO),
        blk(lambda i: jnp.minimum((lo_blk(i) + 1) * r, lo_last)),
        blk(lambda i: jnp.minimum((hi_blk(i) + 1) * r, hi_last) + base // HALO),
    ]


def _hy_in_kernel(xlo_ref, xhi_ref, plo_ref, phi_ref, nlo_ref, nhi_ref, nw_ref, sh_ref, sc_ref, w_ref,
                  cw_ref, cb_ref, z_ref, m_ref, h_scr, *u_scrs):
    i = pl.program_id(0)
    width = z_ref.shape[1]
    c = _cond_row(i, HY_ROWS)
    norm = lambda lo_ref, hi_ref: (_rms(_pick_rows(i, lo_ref, hi_ref, HY_ROWS), nw_ref[...])
                                   * (1.0 + sc_ref[pl.ds(c, 1), :]) + sh_ref[pl.ds(c, 1), :]).astype(BF16)
    h_scr[0:HALO, :] = norm(plo_ref, phi_ref)
    h_scr[HALO:HALO + HY_ROWS, :] = norm(xlo_ref, xhi_ref)
    h_scr[HALO + HY_ROWS:HALO + HY_ROWS + HALO, :] = norm(nlo_ref, nhi_ref)

    seq_mask = jnp.where(i < ROWS_P // HY_ROWS, SEQ - 1, DEC_SEQ - 1)
    row = lax.broadcasted_iota(jnp.int32, (TM, width), 0)
    for t in range(HY_ROWS // TM):
        r0 = t * TM
        pos = (row + (i * HY_ROWS + r0)) & seq_mask
        first = pos == 0
        last = pos == seq_mask
        rows_h = pl.ds(r0, TM + 2 * HALO)

        def conv(g, u_scr):
            cols = slice(g * width, (g + 1) * width)
            u_scr[...] = jnp.dot(h_scr[rows_h, :], w_ref[:, cols], preferred_element_type=F32)
            prev = jnp.where(first, 0.0, u_scr[HALO - 1:HALO - 1 + TM, :])
            nxt = jnp.where(last, 0.0, u_scr[HALO + 1:HALO + 1 + TM, :])
            return (cb_ref[:, cols] + prev * cw_ref[0:1, cols] + u_scr[HALO:HALO + TM, :] * cw_ref[1:2, cols]
                    + nxt * cw_ref[2:3, cols])

        u0, u1, u2 = u_scrs[3 * t:3 * t + 3]
        out_rows = pl.ds(r0, TM)
        x0 = conv(0, u0)
        z_ref[out_rows, :] = (conv(2, u2) * conv(1, u1)).astype(BF16)
        gate = jnp.dot(h_scr[pl.ds(r0 + HALO, TM), :], w_ref[:, 3 * width:4 * width], preferred_element_type=F32)
        m_ref[out_rows, :] = (x0 * _silu(gate)).astype(BF16)


def _hy_in(x, norm_w, mod, layer, w_in, conv_w, conv_b):
    j = layer // 2
    xlo, xhi, _ = _row_pair(x)
    out = pl.BlockSpec((HY_ROWS, HY_WIDTH), lambda i: (i, 0))
    n_tiles = HY_ROWS // TM
    return pl.pallas_call(
        _hy_in_kernel,
        name="hy_in",
        grid=(ROWS // HY_ROWS,),
        in_specs=[
            *_row_pair_specs(x, D_MODEL, HY_ROWS),
            *_halo_specs(x),
            _entry(norm_w, layer),
            _mod_spec(layer, 0),
            _mod_spec(layer, 1),
            _entry(w_in, j, resident=True),
            _entry(conv_w, j),
            _entry(conv_b, j),
        ],
        out_specs=[out, out],
        out_shape=[jax.ShapeDtypeStruct((ROWS, HY_WIDTH), BF16)] * 2,
        scratch_shapes=([pltpu.VMEM((HY_ROWS + 2 * HALO, D_MODEL), BF16)]
                        + [pltpu.VMEM((TM + 2 * HALO, HY_WIDTH), F32)] * (3 * n_tiles)),
        compiler_params=_cparams(("arbitrary",), 56 << 20),
    )(xlo, xhi, xlo, xhi, xlo, xhi, norm_w, mod, mod, w_in, conv_w, conv_b)


@functools.lru_cache(maxsize=None)
def _filter_consts(L):
    t = np.linspace(0.0, 1.0, L)[:, None]
    w = (2.0 * math.pi / L) * np.arange(L)[:, None]
    bands = np.linspace(1e-4, FILTER_BANDS - 1, FILTER_BANDS)[None, :]
    emb = np.concatenate([t, np.cos(bands * w), -np.sin(bands * w)], axis=-1)
    emb = np.pad(emb, ((0, 0), (0, FILTER_HIDDEN - FILTER_EMB)))
    emb = np.concatenate([emb[:L // 2], emb[L // 2:]], axis=1)
    max_decay = math.log(DECAY_TARGET) / FAST_DECAY_PCT
    min_decay = math.log(DECAY_TARGET) / SLOW_DECAY_PCT
    deltas = np.abs(np.linspace(min_decay, max_decay, HY_WIDTH))
    deltas = np.concatenate([deltas, deltas])[None, :]
    return emb.astype(np.float32), t.astype(np.float32), deltas.astype(np.float32)


def _filter_kernel(emb_ref, t_ref, w1_ref, b1_ref, fr_ref, w2_ref, b2_ref, w3_ref, dl_ref, o_ref, h_scr):
    @pl.when(pl.program_id(0) == 0)
    def _():
        hi = lax.Precision.HIGHEST
        fr = fr_ref[...]
        h = jnp.sin(fr * (jnp.dot(emb_ref[...], w1_ref[...], precision=hi, preferred_element_type=F32) + b1_ref[...]))
        h = jnp.sin(fr * (jnp.dot(h, w2_ref[...], precision=hi, preferred_element_type=F32) + b2_ref[...]))
        h_scr[...] = h.astype(BF16)

    w3 = w3_ref[...].astype(BF16)
    zero = jnp.zeros_like(w3)
    hid = h_scr[...]
    h = jnp.concatenate([jnp.dot(hid, jnp.concatenate([w3, zero], axis=0), preferred_element_type=F32),
                         jnp.dot(hid, jnp.concatenate([zero, w3], axis=0), preferred_element_type=F32)], axis=0)
    h = h * jnp.exp(-t_ref[...] * dl_ref[...])
    o_ref[...] = h / jnp.sum(jnp.abs(h), axis=0, keepdims=True)


def _hy_filter(L, hy, w1, b1, freq, w2, b2, w3):
    emb, t, deltas = _filter_consts(L)
    tcf = 512
    two = 2 * FILTER_HIDDEN
    full = lambda shape: pl.BlockSpec(shape, lambda c: (0, 0))
    return pl.pallas_call(
        _filter_kernel,
        name=f"hy_filter{L}",
        grid=(2 * HY_WIDTH // tcf,),
        in_specs=[
            full((L // 2, two)), full((L, 1)), _entry(w1, hy), _entry(b1, hy), _entry(freq, hy),
            _entry(w2, hy), _entry(b2, hy),
            pl.BlockSpec((None, FILTER_HIDDEN, tcf), lambda c: (hy, 0, c)),
            pl.BlockSpec((1, tcf), lambda c: (0, c)),
        ],
        out_specs=pl.BlockSpec((L, tcf), lambda c: (0, c)),
        out_shape=jax.ShapeDtypeStruct((L, 2 * HY_WIDTH), F32),
        scratch_shapes=[pltpu.VMEM((L // 2, two), BF16)],
        compiler_params=_cparams(("arbitrary",), 40 << 20),
    )(jnp.asarray(emb), jnp.asarray(t), w1, b1, freq, w2, b2, w3, jnp.asarray(deltas))


@functools.lru_cache(maxsize=None)
def _dft_consts(L):
    k = np.arange(L, dtype=np.int64)
    ang = (np.outer(k, k) % (2 * L)).astype(np.float64) * (math.pi / L)
    return np.cos(ang).astype(np.float32), np.sin(ang).astype(np.float32)


def _dft_mats(L):
    c, s = _dft_consts(L)
    return jnp.asarray(c).astype(BF16), jnp.asarray(s).astype(BF16)


def _alt_sign(shape):
    return (1 - 2 * (lax.broadcasted_iota(jnp.int32, shape, 0) & 1)).astype(F32)


def _lconv_kernel(z_ref, m_ref, ff_ref, fb_ref, bias_ref, c_ref, s_ref, t_ref, a_scr, b_scr, nq_scr, y_scr,
                  *, L, tk):
    rows, tc = y_scr.shape
    alt = _alt_sign((L, tc))
    bins = [slice(k0, k0 + tk) for k0 in range(0, L, tk)]

    @pl.when(pl.program_id(1) == 0)
    def _():
        inv_n = 1.0 / (2.0 * L)
        bias = bias_ref[...]
        ff = ff_ref[...]
        fb = fb_ref[...]
        f = ff + fb
        nq_scr[...] = jnp.broadcast_to((jnp.sum(f * alt, axis=0, keepdims=True) + bias) * inv_n, nq_scr.shape)
        f_b = f.astype(BF16)
        d_b = (fb - ff).astype(BF16)
        for kt in bins:
            hre = jnp.dot(c_ref[kt, :], f_b, preferred_element_type=F32) + bias
            him = jnp.dot(s_ref[kt, :], d_b, preferred_element_type=F32)
            a = hre * (2.0 * inv_n)
            if kt.start == 0:
                a = jnp.where(lax.broadcasted_iota(jnp.int32, hre.shape, 0) == 0, hre * inv_n, a)
            a_scr[kt, :] = a
            b_scr[kt, :] = him * (2.0 * inv_n)

    for s0 in range(0, rows, L):
        sq = pl.ds(s0, L)
        zb = z_ref[sq, :]
        z_nyq = jnp.sum(zb.astype(F32) * alt, axis=0, keepdims=True)
        y_scr[sq, :] = alt * (z_nyq * nq_scr[0:1, :])
        for kt in bins:
            zr = jnp.dot(c_ref[kt, :], zb, preferred_element_type=F32)
            zi = jnp.dot(s_ref[kt, :], zb, preferred_element_type=F32)
            a = a_scr[kt, :]
            b = b_scr[kt, :]
            yr = (zr * a + zi * b).astype(BF16)
            yw = (zi * a - zr * b).astype(BF16)
            y_scr[sq, :] += (jnp.dot(c_ref[:, kt], yr, preferred_element_type=F32)
                             + jnp.dot(s_ref[:, kt], yw, preferred_element_type=F32))
        t_ref[sq, :] = (y_scr[sq, :] * m_ref[sq, :].astype(F32)).astype(BF16)


def _hy_lconv(L, hy, row_block0, n_row_blocks, z, m, filt, f_bias, c, s):
    tk = min(TK, L)
    long_seq = L == SEQ_BLOCK
    tc = 2 * LCONV_TC if long_seq else LCONV_TC
    nc = HY_WIDTH // tc
    blk = pl.BlockSpec((SEQ_BLOCK, tc), lambda j, r: (r + row_block0, j))
    mat = _resident((L, L), lambda j, r: (0, 0))
    filt_mode = pl.Buffered(1) if long_seq else None
    return pl.pallas_call(
        functools.partial(_lconv_kernel, L=L, tk=tk),
        name=f"hy_lconv{L}",
        grid=(nc, n_row_blocks),
        in_specs=[blk, blk,
                  pl.BlockSpec((L, tc), lambda j, r: (0, j), pipeline_mode=filt_mode),
                  pl.BlockSpec((L, tc), lambda j, r: (0, j + nc), pipeline_mode=filt_mode),
                  pl.BlockSpec((None, 1, tc), lambda j, r: (hy, 0, j)),
                  mat, mat],
        out_specs=pl.BlockSpec((SEQ_BLOCK, tc), lambda j, r: (r, j)),
        out_shape=jax.ShapeDtypeStruct((n_row_blocks * SEQ_BLOCK, HY_WIDTH), BF16),
        scratch_shapes=[pltpu.VMEM((L, tc), F32), pltpu.VMEM((L, tc), F32), pltpu.VMEM((8, tc), F32),
                        pltpu.VMEM((SEQ_BLOCK, tc), F32)],
        compiler_params=_cparams(("arbitrary", "arbitrary"), 56 << 20),
    )(z, m, filt, filt, f_bias, c, s)


@functools.lru_cache(maxsize=None)
def _rope_consts():
    axis_dim = QK_ROPE // 2
    nf = axis_dim // 2
    inv = ROPE_THETA ** (-np.arange(0, axis_dim, 2, dtype=np.float64) / axis_dim)
    t = np.arange(DEC_SEQ)
    ang_r = (t // GRID_W)[:, None] * inv
    ang_c = (t % GRID_W)[:, None] * inv
    cos = np.ones((DEC_SEQ, HEAD_PAD))
    sin_up = np.zeros((DEC_SEQ, HEAD_PAD))
    sin_dn = np.zeros((DEC_SEQ, HEAD_PAD))
    for base, ang in ((QK_NOPE, ang_r), (QK_NOPE + axis_dim, ang_c)):
        cos[:, base:base + nf] = np.cos(ang)
        cos[:, base + nf:base + 2 * nf] = np.cos(ang)
        sin_up[:, base:base + nf] = -np.sin(ang)
        sin_dn[:, base + nf:base + 2 * nf] = np.sin(ang)
    return cos.astype(np.float32), sin_up.astype(np.float32), sin_dn.astype(np.float32)


ROPE_HALF = QK_ROPE // 4
Q_SCALE = math.log2(math.e) / math.sqrt(QK_NOPE + QK_ROPE)


def _mla_proj_kernel(tlo_ref, thi_ref, wout_ref, xlo_ref, xhi_ref, gprev_ref,
                     nw_ref, sh_ref, sc_ref, wa_ref, wg_ref, wpe_ref, qn_ref, kvn_ref, wqb_ref, wk_ref, wv_ref,
                     cos_ref, sup_ref, sdn_ref,
                     xn_ref, q_ref, k_ref, v_ref, sg_ref, ckv_ref, kpe_ref):
    i = pl.program_id(0)
    u = jnp.dot(_pick_rows(i, tlo_ref, thi_ref), wout_ref[...], preferred_element_type=F32)
    x = _pick_rows(i, xlo_ref, xhi_ref) + gprev_ref[pl.ds(_cond_row(i), 1), :] * u
    xn_ref[...] = x
    h = _modnorm(i, x, nw_ref, sh_ref, sc_ref).astype(BF16)
    lora = jnp.dot(h, wa_ref[...], preferred_element_type=F32)
    kpe = jnp.dot(h, wpe_ref[...], preferred_element_type=F32)
    qn = _rms(lora[:, 0:Q_LORA], qn_ref[...]).astype(BF16)
    ckv = _rms(lora[:, Q_LORA:Q_LORA + KV_LORA], kvn_ref[...])
    ckv_ref[...] = ckv
    kpe_ref[...] = kpe
    ckv_b = ckv.astype(BF16)
    q = jnp.dot(qn, wqb_ref[...], preferred_element_type=F32)
    kn = jnp.dot(ckv_b, wk_ref[...], preferred_element_type=F32)

    latent = i >= TILES_P
    cos = jnp.where(latent, cos_ref[...], 1.0)
    sup = jnp.where(latent, sup_ref[...], 0.0)
    sdn = jnp.where(latent, sdn_ref[...], 0.0)

    def rope(u, scale):
        return (u * (cos * scale) + pltpu.roll(u, HEAD_PAD - ROPE_HALF, axis=1) * (sup * scale)
                + pltpu.roll(u, ROPE_HALF, axis=1) * (sdn * scale))

    kpe_r = rope(kpe, 1.0)
    for hd in range(N_HEADS):
        cols = slice(hd * HEAD_PAD, (hd + 1) * HEAD_PAD)
        q_ref[:, cols] = rope(q[:, cols], Q_SCALE).astype(BF16)
        k_ref[:, cols] = (kn[:, cols] + kpe_r).astype(BF16)
    v_ref[...] = jnp.dot(ckv_b, wv_ref[...], preferred_element_type=F32).astype(BF16)
    sg_ref[...] = _silu(jnp.dot(h, wg_ref[...], preferred_element_type=F32))


def _mla_proj(t, w_out, x, norm_w, mod, layer, w):
    cos, sup, sdn = (jnp.asarray(c) for c in _rope_consts())
    j = layer // 2
    hp = N_HEADS * HEAD_PAD
    tlo, thi, _ = _row_pair(t)
    xlo, xhi, _ = _row_pair(x)
    rope_blk = pl.BlockSpec(
        (TM, HEAD_PAD), lambda i: (jnp.where(i >= TILES_P, (i - TILES_P) % TILES_PER_DEC_SEQ, 0), 0))
    tile = lambda n: pl.BlockSpec((TM, n), lambda i: (i, 0))
    return pl.pallas_call(
        _mla_proj_kernel,
        name="mla_proj",
        grid=(N_TILES,),
        in_specs=[
            *_row_pair_specs(t, HY_WIDTH),
            _entry(w_out, (layer - 1) // 2, True),
            *_row_pair_specs(x, D_MODEL),
            _mod_spec(layer - 1, 2),
            _entry(norm_w, layer),
            _mod_spec(layer, 0), _mod_spec(layer, 1),
            _entry(w["w_a"], j, True), _entry(w["w_g"], j, True), _entry(w["w_pe"], j, True),
            _entry(w["q_norm"], j), _entry(w["kv_norm"], j),
            _entry(w["w_qb"], j, True), _entry(w["w_k"], j, True), _entry(w["w_v"], j, True),
            rope_blk, rope_blk, rope_blk,
        ],
        out_specs=[tile(D_MODEL), tile(hp), tile(hp), tile(N_HEADS * V_HEAD),
                   tile(N_HEADS * V_HEAD), tile(KV_LORA), tile(HEAD_PAD)],
        out_shape=[
            jax.ShapeDtypeStruct((ROWS, D_MODEL), F32),
            jax.ShapeDtypeStruct((ROWS, hp), BF16),
            jax.ShapeDtypeStruct((ROWS, hp), BF16),
            jax.ShapeDtypeStruct((ROWS, N_HEADS * V_HEAD), BF16),
            jax.ShapeDtypeStruct((ROWS, N_HEADS * V_HEAD), F32),
            jax.ShapeDtypeStruct((ROWS, KV_LORA), F32),
            jax.ShapeDtypeStruct((ROWS, HEAD_PAD), F32),
        ],
        compiler_params=_cparams(("arbitrary",), 56 << 20),
    )(tlo, thi, w_out, xlo, xhi, mod, norm_w, mod, mod, w["w_a"], w["w_g"], w["w_pe"], w["q_norm"],
      w["kv_norm"], w["w_qb"], w["w_k"], w["w_v"], cos, sup, sdn)


def _mla_ctx_kernel(ckv_ref, kpe_ref, wk_ref, wv_ref, k_ref, v_ref):
    ckv_b = ckv_ref[...].astype(BF16)
    kn = jnp.dot(ckv_b, wk_ref[...], preferred_element_type=F32)
    v_ref[...] = jnp.dot(ckv_b, wv_ref[...], preferred_element_type=F32).astype(BF16)
    kpe = kpe_ref[...]
    for hd in range(N_HEADS):
        cols = slice(hd * HEAD_PAD, (hd + 1) * HEAD_PAD)
        k_ref[:, cols] = (kn[:, cols] + kpe).astype(BF16)


def _mla_ctx(j, ckv_ctx, kpe_ctx, w):
    hp = N_HEADS * HEAD_PAD
    rows = DEC_BATCH * PAST_LEN
    tile = lambda n: pl.BlockSpec((PAST_LEN, n), lambda i: (i, 0))
    cache = lambda n: pl.BlockSpec((None, None, PAST_LEN, n), lambda i: (i, j, 0, 0))
    return pl.pallas_call(
        _mla_ctx_kernel,
        name="mla_ctx",
        grid=(DEC_BATCH,),
        in_specs=[cache(KV_LORA), cache(HEAD_PAD), _entry(w["w_k"], j, True), _entry(w["w_v"], j, True)],
        out_specs=[tile(hp), tile(N_HEADS * V_HEAD)],
        out_shape=[jax.ShapeDtypeStruct((rows, hp), BF16),
                   jax.ShapeDtypeStruct((rows, N_HEADS * V_HEAD), BF16)],
        compiler_params=_cparams(("arbitrary",), 32 << 20),
    )(ckv_ctx, kpe_ctx, w["w_k"], w["w_v"])


NT_DIMS = (((1,), (1,)), ((), ()))
ATTN_PAIRS = 4
ATTN_TQ = 256
PROMPT_SEQS = 2


def _attn_kernel(*refs, n_pairs, n_groups, has_ctx, final, n_seq=1):
    refs = list(refs)
    q_ref, k_ref, v_ref = refs[:3]
    del refs[:3]
    if has_ctx:
        kc_ref, vc_ref = refs[:2]
        del refs[:2]
    sg_ref, wo_ref, x_ref, g_ref = refs[:4]
    del refs[:4]
    if final:
        fg_ref = refs.pop(0)
    out_ref, o_scr = refs[:2]
    tq = q_ref.shape[0] // n_seq
    lk = k_ref.shape[0] // n_seq
    low_half = lax.broadcasted_iota(jnp.int32, (tq, LANES), 1) < V_HEAD
    for sq, p in [(sq, p) for sq in range(n_seq) for p in range(n_pairs)]:
        qrows = pl.ds(sq * tq, tq)
        krows = pl.ds(sq * lk, lk)
        vcols = slice(p * LANES, (p + 1) * LANES)
        vp = v_ref[krows, vcols]
        outs = []

        def head_scores(hh):
            cols = slice((2 * p + hh) * HEAD_PAD, (2 * p + hh + 1) * HEAD_PAD)
            q = q_ref[qrows, cols]
            s = lax.dot_general(q, k_ref[krows, cols], NT_DIMS, preferred_element_type=F32)
            sc = lax.dot_general(q, kc_ref[:, cols], NT_DIMS, preferred_element_type=F32) if has_ctx else None
            return s, sc

        scores = [head_scores(0), head_scores(1)] if has_ctx else None
        for hh in range(2):
            s, sc = scores[hh] if has_ctx else head_scores(hh)
            mx = jnp.max(s, axis=-1, keepdims=True)
            if has_ctx:
                mx = jnp.maximum(mx, jnp.max(sc, axis=-1, keepdims=True))
            e = jnp.exp2(s - mx)
            den = jnp.sum(e, axis=-1, keepdims=True)
            pv = jnp.dot(e.astype(BF16), vp, preferred_element_type=F32)
            if has_ctx:
                ec = jnp.exp2(sc - mx)
                den = den + jnp.sum(ec, axis=-1, keepdims=True)
                pv = pv + jnp.dot(ec.astype(BF16), vc_ref[:, vcols], preferred_element_type=F32)
            outs.append(pv / den)
        o = jnp.where(low_half, outs[0], outs[1])
        o_scr[qrows, vcols] = (o * sg_ref[qrows, vcols]).astype(BF16)

    u = jnp.dot(o_scr[...], wo_ref[...], preferred_element_type=F32)
    cond = 1 + pl.program_id(0) if has_ctx else 0

    def finish(u_all):
        xn = x_ref[...] + g_ref[pl.ds(cond, 1), :] * u_all
        out_ref[...] = _rms(xn, fg_ref[...]) if final else xn

    if n_groups == 1:
        finish(u)
    else:
        acc_scr = refs[2]
        grp = pl.program_id(2)

        @pl.when(grp == 0)
        def _():
            acc_scr[...] = u

        @pl.when((grp > 0) & (grp < n_groups - 1))
        def _():
            acc_scr[...] += u

        @pl.when(grp == n_groups - 1)
        def _():
            finish(acc_scr[...] + u)


def _attn_prompt(q, k, v, sg, w_o, x, mod, layer, final_g):
    hp = N_HEADS * HEAD_PAD
    nv = N_HEADS * V_HEAD
    xlo, _, _ = _row_pair(x)
    rows = PROMPT_SEQS * SEQ
    seq = lambda n: pl.BlockSpec((rows, n), lambda b: (b, 0))
    final = final_g is not None
    return pl.pallas_call(
        functools.partial(_attn_kernel, n_pairs=N_HEADS // 2, n_groups=1, has_ctx=False, final=final,
                          n_seq=PROMPT_SEQS),
        name="attn_prompt",
        grid=(BATCH // PROMPT_SEQS,),
        in_specs=[seq(hp), seq(hp), seq(nv), seq(nv), _entry(w_o, layer // 2, True), seq(D_MODEL),
                  _mod_spec(layer, 2)] + ([pl.BlockSpec((1, D_MODEL), lambda b: (0, 0))] if final else []),
        out_specs=seq(D_MODEL),
        out_shape=jax.ShapeDtypeStruct((ROWS_P, D_MODEL), F32),
        scratch_shapes=[pltpu.VMEM((rows, nv), BF16)],
        compiler_params=_cparams(("arbitrary",), 40 << 20),
    )(q, k, v, sg, w_o, xlo, mod, *([final_g] if final else []))


def _attn_latent(q, k, v, kc, vc, sg, w_o, x, mod, layer, final_g):
    tq = ATTN_TQ
    npair = ATTN_PAIRS
    ngrp = N_HEADS // 2 // npair
    tiles = DEC_SEQ // tq
    q0 = ROWS_P // tq
    s0 = ROWS_P // DEC_SEQ
    wide = npair * 2 * HEAD_PAD
    narrow = npair * LANES
    _, xhi, base = _row_pair(x)
    x0 = base // tq
    final = final_g is not None
    qrow = lambda b, t, p: (q0 + b * tiles + t, p)
    return pl.pallas_call(
        functools.partial(_attn_kernel, n_pairs=npair, n_groups=ngrp, has_ctx=True, final=final),
        name="attn_latent",
        grid=(DEC_BATCH, tiles, ngrp),
        in_specs=[pl.BlockSpec((tq, wide), qrow),
                  pl.BlockSpec((DEC_SEQ, wide), lambda b, t, p: (s0 + b, p)),
                  pl.BlockSpec((DEC_SEQ, narrow), lambda b, t, p: (s0 + b, p)),
                  pl.BlockSpec((PAST_LEN, wide), lambda b, t, p: (b, p)),
                  pl.BlockSpec((PAST_LEN, narrow), lambda b, t, p: (b, p)),
                  pl.BlockSpec((tq, narrow), qrow),
                  pl.BlockSpec((None, narrow, D_MODEL), lambda b, t, p: (layer // 2, p, 0)),
                  pl.BlockSpec((tq, D_MODEL), lambda b, t, p: (x0 + b * tiles + t, 0)),
                  _mod_spec(layer, 2)] + ([pl.BlockSpec((1, D_MODEL), lambda b, t, p: (0, 0))] if final else []),
        out_specs=pl.BlockSpec((tq, D_MODEL), lambda b, t, p: (b * tiles + t, 0)),
        out_shape=jax.ShapeDtypeStruct((ROWS_S, D_MODEL), F32),
        scratch_shapes=[pltpu.VMEM((tq, narrow), BF16), pltpu.VMEM((tq, D_MODEL), F32)],
        compiler_params=_cparams(("arbitrary", "arbitrary", "arbitrary"), 48 << 20),
    )(q, k, v, kc, vc, sg, w_o, xhi, mod, *([final_g] if final else []))


def _mla_weights(w_in, q_norm, w_qb, kv_norm, w_kvb, w_o):
    n = w_in.shape[0]
    o_pe = Q_LORA + KV_LORA
    o_gate = o_pe + QK_ROPE
    w_pe = jnp.pad(w_in[..., o_pe:o_gate], ((0, 0), (0, 0), (QK_NOPE, HEAD_PAD - QK_NOPE - QK_ROPE)))
    qb = w_qb.reshape(n, Q_LORA, N_HEADS, QK_NOPE + QK_ROPE)
    qb = jnp.pad(qb, ((0, 0), (0, 0), (0, 0), (0, HEAD_PAD - QK_NOPE - QK_ROPE)))
    kvb = w_kvb.reshape(n, KV_LORA, N_HEADS, QK_NOPE + V_HEAD)
    wk = jnp.pad(kvb[..., :QK_NOPE], ((0, 0), (0, 0), (0, 0), (0, HEAD_PAD - QK_NOPE)))
    wv = kvb[..., QK_NOPE:]
    return {
        "w_a": w_in[..., :o_pe].astype(BF16),
        "w_g": w_in[..., o_gate:].astype(BF16),
        "w_pe": w_pe.astype(BF16),
        "q_norm": q_norm[:, None, :],
        "kv_norm": kv_norm[:, None, :],
        "w_qb": qb.reshape(n, Q_LORA, N_HEADS * HEAD_PAD).astype(BF16),
        "w_k": wk.reshape(n, KV_LORA, N_HEADS * HEAD_PAD).astype(BF16),
        "w_v": wv.reshape(n, KV_LORA, N_HEADS * V_HEAD).astype(BF16),
        "w_o": w_o.astype(BF16),
    }


def kernel(x_prompt, x_sample, cache_ckv, cache_kpe, c, c_ctx, norm_w, ada_w, ada_b, hy_w_in, hy_conv_w, hy_conv_b, hy_f_w1, hy_f_b1, hy_f_freq, hy_f_w2, hy_f_b2, hy_f_w3, hy_f_bias, hy_w_out, mla_w_in, mla_q_norm, mla_w_qb, mla_kv_norm, mla_w_kvb, mla_w_o, final_norm):
    x = (x_prompt.reshape(ROWS_P, D_MODEL), x_sample.reshape(ROWS_S, D_MODEL))
    cond = jnp.concatenate([c_ctx[None, :], c, jnp.zeros((N_COND - 1 - DEC_BATCH, D_MODEL), F32)], axis=0)
    mod = _ada_all(cond, ada_w, ada_b)

    nw = norm_w[:, None, :]
    hy_w_in_b = hy_w_in.astype(BF16)
    hy_w_out_b = hy_w_out.astype(BF16)
    hy_conv_b3 = hy_conv_b[:, None, :]
    fh = FILTER_HIDDEN
    twice = lambda w: (jnp.pad(w, ((0, 0), (0, 2 * fh - w.shape[1]), (0, fh)))
                       + jnp.pad(w, ((0, 0), (fh, fh - w.shape[1]), (fh, 0))))
    f_w1, f_w2 = twice(hy_f_w1), twice(hy_f_w2)
    f_b1, f_freq, f_b2 = (jnp.tile(a, (1, 2))[:, None, :] for a in (hy_f_b1, hy_f_freq, hy_f_b2))
    f_bias = hy_f_bias[:, None, :]
    w = _mla_weights(mla_w_in, mla_q_norm, mla_w_qb, mla_kv_norm, mla_w_kvb, mla_w_o)
    kpe_ctx = jnp.pad(cache_kpe, ((0, 0), (0, 0), (0, 0), (QK_NOPE, HEAD_PAD - QK_NOPE - QK_ROPE)))

    new_ckv, new_kpe = [], []
    for layer in range(DEPTH):
        j = layer // 2
        if layer % 2 == 0:
            z, m = _hy_in(x, nw, mod, layer, hy_w_in_b, hy_conv_w, hy_conv_b3)
            t = []
            for L, blk0, nblk in ((DEC_SEQ, ROWS_P // SEQ_BLOCK, ROWS_S // SEQ_BLOCK), (SEQ, 0, ROWS_P // SEQ_BLOCK)):
                filt = _hy_filter(L, j, f_w1, f_b1, f_freq, f_w2, f_b2, hy_f_w3)
                cm, sm = _dft_mats(L)
                t.append(_hy_lconv(L, j, blk0, nblk, z, m, filt, f_bias, cm, sm))
            t = (t[1], t[0])
        else:
            x, q, k, v, sg, ckv, kpe = _mla_proj(t, hy_w_out_b, x, nw, mod, layer, w)
            kc, vc = _mla_ctx(j, cache_ckv, kpe_ctx, w)
            final_g = final_norm[None, :] if layer == DEPTH - 1 else None
            x = (_attn_prompt(q, k, v, sg, w["w_o"], x, mod, layer, final_g),
                 _attn_latent(q, k, v, kc, vc, sg, w["w_o"], x, mod, layer, final_g))
            new_ckv.append(ckv[:ROWS_P].reshape(BATCH, SEQ, KV_LORA))
            new_kpe.append(kpe[:ROWS_P, QK_NOPE:QK_NOPE + QK_ROPE].reshape(BATCH, SEQ, QK_ROPE))

    assert DEPTH % 2 == 0
    y_prompt, y_sample = x
    return (y_prompt.reshape(BATCH, SEQ, D_MODEL), y_sample.reshape(DEC_BATCH, DEC_SEQ, D_MODEL),
            jnp.stack(new_ckv, axis=1), jnp.stack(new_kpe, axis=1))
```
